```python
import jax, jax.numpy as jnp
from jax import lax
import numpy as np

D_MODEL = 2048
BATCH = 2
SEQ = 4096
DEPTH = 1
DEC_BATCH = 16
DEC_SEQ = 16
PAST_LEN = 1024

CHUNK = 64
EPS = 1e-6
GLA_HEADS = 4
GLA_DK = D_MODEL // 2
GLA_DV = D_MODEL
GLA_HK = GLA_DK // GLA_HEADS
GLA_HV = GLA_DV // GLA_HEADS
GLA_GATE_RANK = 16
GLA_GATE_TAU = 16.0
MLA_HEADS = 16
MLA_Q_LORA = 512
MLA_KV_LORA = 512
MLA_NOPE = 128
MLA_ROPE = 64
MLA_VDIM = 128
MLA_SCALE = (MLA_NOPE + MLA_ROPE) ** -0.5
ROPE_THETA = 10000.0
Q_BLOCK = 128
D_FF = 5632
CONV_W = 3
IN_SIZES = (GLA_DK, GLA_DK, GLA_DV, GLA_GATE_RANK, GLA_DV,
            MLA_Q_LORA, MLA_KV_LORA, MLA_ROPE, D_MODEL, D_MODEL)
IN_COLS = 2 * GLA_DK + 2 * GLA_DV + GLA_GATE_RANK + MLA_Q_LORA + MLA_KV_LORA + MLA_ROPE + 2 * D_MODEL

kernel_name = "hybrid_gla_mla_convffn_stream_step"


def rms_norm(x, g):
    xf = x.astype(jnp.float32)
    y = xf * lax.rsqrt(jnp.mean(xf * xf, axis=-1, keepdims=True) + EPS)
    return (y * g.astype(jnp.float32)).astype(x.dtype)


def split_cols(z, sizes):
    offs = [int(o) for o in np.cumsum(sizes)[:-1]]
    return jnp.split(z, offs, axis=-1)


def rope(x, pos):
    half = MLA_ROPE // 2
    inv = ROPE_THETA ** (-jnp.arange(half, dtype=jnp.float32) * 2.0 / MLA_ROPE)
    ang = pos.astype(jnp.float32)[:, None] * inv[None, :]
    shape = (1, pos.shape[0]) + (1,) * (x.ndim - 3) + (half,)
    cos = jnp.cos(ang).reshape(shape)
    sin = jnp.sin(ang).reshape(shape)
    xf = x.astype(jnp.float32)
    x1, x2 = xf[..., :half], xf[..., half:]
    return jnp.concatenate([x1 * cos - x2 * sin, x2 * cos + x1 * sin], axis=-1).astype(x.dtype)


def gla_scan(q, k, v, log_a, S0, block):
    B, T = q.shape[:2]
    n = T // block

    def to_blocks(t):
        return t.reshape((B, n, block) + t.shape[2:]).swapaxes(0, 1)

    causal = jnp.tril(jnp.ones((block, block), dtype=bool))

    def step(S, inp):
        qc, kc, vc, ac = inp
        b = jnp.cumsum(ac, axis=1)
        diff = b[:, :, None] - b[:, None, :]
        decay = jnp.exp(jnp.where(causal[None, :, :, None, None], diff, -jnp.inf))
        A = jnp.einsum('bthd,bshd,btshd->bhts', qc, kc, decay)
        o = (jnp.einsum('bhts,bshv->bthv', A, vc)
             + jnp.einsum('bthd,bhdv->bthv', qc * jnp.exp(b), S))
        bl = b[:, -1]
        S_new = (jnp.exp(bl)[..., None] * S
                 + jnp.einsum('bshd,bshv->bhdv', kc * jnp.exp(bl[:, None] - b), vc))
        return S_new, o

    S_fin, o = lax.scan(step, S0, (to_blocks(q), to_blocks(k), to_blocks(v), to_blocks(log_a)))
    o = o.swapaxes(0, 1).reshape((B, T) + v.shape[2:])
    return o, S_fin


def mla_project(q_lat, kv_lat, kpe_raw, pos, lp):
    B, T = q_lat.shape[:2]
    q = (rms_norm(q_lat, lp["mla_g_qlat"]) @ lp["mla_w_uq"]).reshape(B, T, MLA_HEADS, MLA_NOPE + MLA_ROPE)
    q_nope = rms_norm(q[..., :MLA_NOPE], lp["mla_g_q"])
    q_pe = rope(rms_norm(q[..., MLA_NOPE:], lp["mla_g_qpe"]), pos)
    ckv = rms_norm(kv_lat, lp["mla_g_kvlat"])
    kpe = rope(rms_norm(kpe_raw, lp["mla_g_kpe"]), pos)
    return q_nope, q_pe, ckv, kpe


def mla_expand(ckv, lp):
    B, T = ckv.shape[:2]
    kv = (ckv @ lp["mla_w_ukv"]).reshape(B, T, MLA_HEADS, MLA_NOPE + MLA_VDIM)
    return rms_norm(kv[..., :MLA_NOPE], lp["mla_g_k"]), kv[..., MLA_NOPE:]


def mla_attend(q_nope, q_pe, k_nope, kpe, v, mask):
    s = (jnp.einsum('bqhd,bkhd->bhqk', q_nope, k_nope)
         + jnp.einsum('bqhr,bkr->bhqk', q_pe, kpe))
    s = s.astype(jnp.float32) * MLA_SCALE
    if mask is not None:
        s = jnp.where(mask, s, -jnp.inf)
    p = jax.nn.softmax(s, axis=-1)
    return jnp.einsum('bhqk,bkhv->bqhv', p.astype(v.dtype), v)


def mla_prompt_attention(q_nope, q_pe, k_nope, kpe, v):
    B, T = q_nope.shape[:2]
    nb = T // Q_BLOCK
    kchunk = jnp.arange(T) // CHUNK

    def blk(args):
        qn, qp, i = args
        qchunk = (i * Q_BLOCK + jnp.arange(Q_BLOCK)) // CHUNK
        mask = (kchunk[None, :] <= qchunk[:, None])[None, None]
        return mla_attend(qn, qp, k_nope, kpe, v, mask)

    qn_b = q_nope.reshape(B, nb, Q_BLOCK, MLA_HEADS, MLA_NOPE).swapaxes(0, 1)
    qp_b = q_pe.reshape(B, nb, Q_BLOCK, MLA_HEADS, MLA_ROPE).swapaxes(0, 1)
    out = lax.map(blk, (qn_b, qp_b, jnp.arange(nb)))
    return out.swapaxes(0, 1).reshape(B, T, MLA_HEADS * MLA_VDIM)


def trunk_layer(x, pos, gla_S0, gla_block, ckv_past, kpe_past, conv_hist, lp):
    B, T, _ = x.shape
    f32 = jnp.float32
    h = rms_norm(x, lp["g_norm1"])
    (a_q, a_k, a_v, a_lr, a_r, m_q, m_kv, m_kpe, gate_a, gate_b) = split_cols(h @ lp["w_in"], IN_SIZES)
    qh = a_q.reshape(B, T, GLA_HEADS, GLA_HK).astype(f32) * (GLA_HK ** -0.5)
    kh = a_k.reshape(B, T, GLA_HEADS, GLA_HK).astype(f32)
    vh = a_v.reshape(B, T, GLA_HEADS, GLA_HV).astype(f32)
    log_a = (jax.nn.log_sigmoid((a_lr @ lp["gla_w_gate2"] + lp["gla_b_gate"]).astype(f32))
             / GLA_GATE_TAU).reshape(B, T, GLA_HEADS, GLA_HK)
    o, S_new = gla_scan(qh, kh, vh, log_a, gla_S0.astype(f32), gla_block)
    o = rms_norm(o.astype(x.dtype), lp["gla_g_out"]).reshape(B, T, GLA_DV) * jax.nn.silu(a_r)
    y_a = o @ lp["w_br_gla"]
    q_nope, q_pe, ckv, kpe = mla_project(m_q, m_kv, m_kpe, pos, lp)
    if ckv_past is None:
        k_nope, v = mla_expand(ckv, lp)
        att = mla_prompt_attention(q_nope, q_pe, k_nope, kpe, v)
    else:
        ckv_all = jnp.concatenate([ckv_past.astype(ckv.dtype), ckv], axis=1)
        kpe_all = jnp.concatenate([kpe_past.astype(kpe.dtype), kpe], axis=1)
        k_nope, v = mla_expand(ckv_all, lp)
        att = mla_attend(q_nope, q_pe, k_nope, kpe_all, v, None).reshape(B, T, MLA_HEADS * MLA_VDIM)
    y_b = att @ lp["w_br_mla"]
    x = x + (jax.nn.sigmoid(gate_a) * y_a + jax.nn.sigmoid(gate_b) * y_b) @ lp["w_out"]
    h2 = rms_norm(x, lp["g_norm2"])
    a, gt = jnp.split(h2 @ lp["ffn_w_up"], [D_FF], axis=-1)
    a_ext = jnp.concatenate([conv_hist.astype(a.dtype), a], axis=1)
    cw = lp["ffn_conv_w"]
    c = lp["ffn_conv_b"] + cw[CONV_W - 1] * a_ext[:, CONV_W - 1:]
    for j in range(CONV_W - 1):
        c = c + cw[j] * a_ext[:, j:j + T]
    x = x + (jax.nn.gelu(c, approximate=False) * gt) @ lp["ffn_w_down"]
    return x, S_new.astype(x.dtype), ckv, kpe, a_ext[:, -(CONV_W - 1):]


def setup_inputs(seed: int = 0) -> dict:
    key = jax.random.key(seed)
    ks = jax.random.split(key, 32)
    f32 = jnp.float32

    def nrm(k, shape, scale):
        return jax.random.normal(k, shape, f32) * scale

    def gain(k, n):
        return 1.0 + nrm(k, (DEPTH, n), 0.01)

    return {
        "x_prompt": nrm(ks[0], (BATCH, SEQ, D_MODEL), 1.0),
        "x_sample": nrm(ks[1], (DEC_BATCH, DEC_SEQ, D_MODEL), 1.0),
        "state_gla": nrm(ks[2], (DEPTH, DEC_BATCH, GLA_HEADS, GLA_HK, GLA_HV), 0.5),
        "cache_mla_ckv": nrm(ks[3], (DEPTH, DEC_BATCH, PAST_LEN, MLA_KV_LORA), 1.0),
        "cache_mla_kpe": nrm(ks[4], (DEPTH, DEC_BATCH, PAST_LEN, MLA_ROPE), 1.0),
        "cache_ffn_conv": nrm(ks[5], (DEPTH, DEC_BATCH, CONV_W - 1, D_FF), 1.0),
        "w_in": nrm(ks[6], (DEPTH, D_MODEL, IN_COLS), D_MODEL ** -0.5),
        "g_norm1": gain(ks[7], D_MODEL),
        "gla_w_gate2": nrm(ks[8], (DEPTH, GLA_GATE_RANK, GLA_DK), GLA_GATE_RANK ** -0.5),
        "gla_b_gate": nrm(ks[9], (DEPTH, GLA_DK), 0.1),
        "gla_g_out": gain(ks[10], GLA_HV),
        "w_br_gla": nrm(ks[11], (DEPTH, GLA_DV, D_MODEL), GLA_DV ** -0.5),
        "mla_g_qlat": gain(ks[12], MLA_Q_LORA),
        "mla_w_uq": nrm(ks[13], (DEPTH, MLA_Q_LORA, MLA_HEADS * (MLA_NOPE + MLA_ROPE)), MLA_Q_LORA ** -0.5),
        "mla_g_kvlat": gain(ks[14], MLA_KV_LORA),
        "mla_w_ukv": nrm(ks[15], (DEPTH, MLA_KV_LORA, MLA_HEADS * (MLA_NOPE + MLA_VDIM)), MLA_KV_LORA ** -0.5),
        "mla_g_q": gain(ks[16], MLA_NOPE),
        "mla_g_k": gain(ks[17], MLA_NOPE),
        "mla_g_qpe": gain(ks[18], MLA_ROPE),
        "mla_g_kpe": gain(ks[19], MLA_ROPE),
        "w_br_mla": nrm(ks[20], (DEPTH, MLA_HEADS * MLA_VDIM, D_MODEL), (MLA_HEADS * MLA_VDIM) ** -0.5),
        "w_out": nrm(ks[21], (DEPTH, D_MODEL, D_MODEL), D_MODEL ** -0.5),
        "g_norm2": gain(ks[22], D_MODEL),
        "ffn_w_up": nrm(ks[23], (DEPTH, D_MODEL, 2 * D_FF), D_MODEL ** -0.5),
        "ffn_conv_w": nrm(ks[24], (DEPTH, CONV_W, D_FF), CONV_W ** -0.5),
        "ffn_conv_b": nrm(ks[25], (DEPTH, D_FF), 0.01),
        "ffn_w_down": nrm(ks[26], (DEPTH, D_FF, D_MODEL), D_FF ** -0.5),
    }


def reference(x_prompt, x_sample, state_gla, cache_mla_ckv, cache_mla_kpe, cache_ffn_conv,
              w_in, g_norm1, gla_w_gate2, gla_b_gate, gla_g_out, w_br_gla,
              mla_g_qlat, mla_w_uq, mla_g_kvlat, mla_w_ukv, mla_g_q, mla_g_k, mla_g_qpe, mla_g_kpe,
              w_br_mla, w_out, g_norm2, ffn_w_up, ffn_conv_w, ffn_conv_b, ffn_w_down):
    B, T = x_prompt.shape[:2]
    DB, DT = x_sample.shape[:2]
    past = cache_mla_ckv.shape[2]
    pos_p = jnp.arange(T)
    pos_s = past + jnp.arange(DT)
    xp, xs = x_prompt, x_sample
    sp_l, ss_l, cp_l, cs_l, kp_l, ks_l, fp_l, fs_l = [], [], [], [], [], [], [], []
    for l in range(DEPTH):
        lp = {
            "w_in": w_in[l], "g_norm1": g_norm1[l],
            "gla_w_gate2": gla_w_gate2[l], "gla_b_gate": gla_b_gate[l], "gla_g_out": gla_g_out[l],
            "w_br_gla": w_br_gla[l],
            "mla_g_qlat": mla_g_qlat[l], "mla_w_uq": mla_w_uq[l], "mla_g_kvlat": mla_g_kvlat[l],
            "mla_w_ukv": mla_w_ukv[l], "mla_g_q": mla_g_q[l], "mla_g_k": mla_g_k[l],
            "mla_g_qpe": mla_g_qpe[l], "mla_g_kpe": mla_g_kpe[l], "w_br_mla": w_br_mla[l],
            "w_out": w_out[l], "g_norm2": g_norm2[l],
            "ffn_w_up": ffn_w_up[l], "ffn_conv_w": ffn_conv_w[l], "ffn_conv_b": ffn_conv_b[l],
            "ffn_w_down": ffn_w_down[l],
        }
        S0 = jnp.zeros((B, GLA_HEADS, GLA_HK, GLA_HV), jnp.float32)
        hist0 = jnp.zeros((B, CONV_W - 1, D_FF), xp.dtype)
        xp, sp, cp, kp, fp = trunk_layer(xp, pos_p, S0, CHUNK, None, None, hist0, lp)
        xs, ss, cs, kss, fs = trunk_layer(xs, pos_s, state_gla[l], DT, cache_mla_ckv[l],
                                          cache_mla_kpe[l], cache_ffn_conv[l], lp)
        sp_l.append(sp); ss_l.append(ss); cp_l.append(cp); cs_l.append(cs)
        kp_l.append(kp); ks_l.append(kss); fp_l.append(fp); fs_l.append(fs)
    return (xp, xs,
            jnp.stack(sp_l, 0), jnp.stack(ss_l, 0),
            jnp.stack(cp_l, 0), jnp.stack(cs_l, 0),
            jnp.stack(kp_l, 0), jnp.stack(ks_l, 0),
            jnp.stack(fp_l, 0), jnp.stack(fs_l, 0))
```

```python
import functools

import numpy as np
import jax
import jax.numpy as jnp
from jax import lax
from jax.experimental import pallas as pl
from jax.experimental.pallas import tpu as pltpu

F32 = jnp.float32
BF16 = jnp.bfloat16

D_MODEL = 2048
CHUNK = 64
EPS = 1e-6
GLA_HEADS = 4
GLA_HK = 256
GLA_HV = 512
GLA_GATE_RANK = 16
GLA_GATE_TAU = 16.0
MLA_HEADS = 16
MLA_LORA = 512
MLA_NOPE = 128
MLA_ROPE = 64
MLA_VDIM = 128
MLA_SCALE = (MLA_NOPE + MLA_ROPE) ** -0.5
ROPE_THETA = 10000.0
D_FF = 5632
CONV_W = 3
LANES = 128
NEG_BIG = -1e30

COL_Q, COL_K, COL_V, COL_R, COL_MQ, COL_MKV, COL_GA, COL_GB, MAIN_COLS = (
    0, 1024, 2048, 4096, 6144, 6656, 7168, 9216, 11264)
SMALL_COLS = 256
SMALL_LR = 128

MIB = 1024 * 1024


def _params(semantics, vmem_mib):
    return pltpu.CompilerParams(dimension_semantics=semantics, vmem_limit_bytes=vmem_mib * MIB)


def _dot(a, b):
    return jnp.dot(a, b, preferred_element_type=F32)


def _dot_nt(a, b):
    return lax.dot_general(a, b, (((1,), (1,)), ((), ())), preferred_element_type=F32)


def _dot_tn(a, b):
    return lax.dot_general(a, b, (((0,), (0,)), ((), ())), preferred_element_type=F32)


def _sigmoid(x):
    return 1.0 / (1.0 + jnp.exp(-x))


def _row_rms(x):
    return x * lax.rsqrt(jnp.mean(x * x, axis=-1, keepdims=True) + EPS)


def _slab_rms(x, p_ref):
    ms = _dot((x * x).astype(BF16), p_ref[...])
    return x * lax.rsqrt(ms + EPS)


def _in_proj_body(x_ref, g_ref, wm_ref, ws_ref, zm_ref, zs_ref, h_ref):
    @pl.when(pl.program_id(1) == 0)
    def _():
        h_ref[...] = (_row_rms(x_ref[...]) * g_ref[...]).astype(BF16)
        zs_ref[...] = _dot(h_ref[...], ws_ref[...])

    zm_ref[...] = _dot(h_ref[...], wm_ref[...]).astype(BF16)


def _in_proj(x, g, wm, ws, tm):
    n = x.shape[0]
    tn = 1024
    return pl.pallas_call(
        _in_proj_body,
        grid=(n // tm, MAIN_COLS // tn),
        in_specs=[
            pl.BlockSpec((tm, D_MODEL), lambda i, j: (i, 0)),
            pl.BlockSpec((1, D_MODEL), lambda i, j: (0, 0)),
            pl.BlockSpec((D_MODEL, tn), lambda i, j: (0, j)),
            pl.BlockSpec((D_MODEL, SMALL_COLS), lambda i, j: (0, 0)),
        ],
        out_specs=[
            pl.BlockSpec((tm, tn), lambda i, j: (i, j)),
            pl.BlockSpec((tm, SMALL_COLS), lambda i, j: (i, 0)),
        ],
        out_shape=[
            jax.ShapeDtypeStruct((n, MAIN_COLS), BF16),
            jax.ShapeDtypeStruct((n, SMALL_COLS), F32),
        ],
        scratch_shapes=[pltpu.VMEM((tm, D_MODEL), BF16)],
        compiler_params=_params(("parallel", "arbitrary"), 48),
        name="in_proj",
    )(x, g, wm, ws)


def _gla_tables(c):
    levels = int(np.log2(c))
    t = np.arange(c)[:, None]
    u = np.arange(c)[None, :]
    mats = [(u <= t), (u > t)]
    masks = [(u == t)]
    for l in range(levels):
        m = c >> (l + 1)
        mid_t = (t // (2 * m)) * 2 * m + m
        upper = t >= mid_t
        mats.append(np.where(upper, (u >= mid_t) & (u <= t), (u > t) & (u < mid_t)))
        mid_u = (u // (2 * m)) * 2 * m + m
        masks.append((t // (2 * m) == u // (2 * m)) & upper & (u < mid_u))
    gmat = np.concatenate(mats, axis=0).astype(np.float32)
    return jnp.asarray(gmat, BF16), jnp.asarray(np.stack(masks).astype(np.float32)), levels


def _gla_body(q_ref, k_ref, v_ref, r_ref, zs_ref, w2_ref, bg_ref, go_ref, s0_ref, gmat_ref, mask_ref,
              og_ref, sout_ref, s_ref, *, c, nchunk, levels):
    t = pl.program_id(2)

    @pl.when(t == 0)
    def _():
        s_ref[...] = s0_ref[0, 0]

    ones_c = jnp.ones((c, LANES), BF16)

    def chunk(ci, carry):
        r0 = pl.multiple_of(ci * c, c)
        rows = pl.ds(r0, c)
        q = q_ref[rows, :].astype(F32) * (GLA_HK ** -0.5)
        k = k_ref[rows, :].astype(F32)
        v = v_ref[rows, :]
        x = _dot(zs_ref[rows, :].astype(BF16), w2_ref[...]) + bg_ref[...]
        log_a = (jnp.minimum(x, 0.0) - jnp.log1p(jnp.exp(-jnp.abs(x)))) * (1.0 / GLA_GATE_TAU)
        hi = log_a.astype(BF16)
        rem = log_a - hi.astype(F32)
        mid = rem.astype(BF16)
        lo = (rem - mid.astype(F32)).astype(BF16)
        gm = gmat_ref[...]
        dec = jnp.exp(_dot(gm, hi) + _dot(gm, mid) + _dot(gm, lo))
        q_in = q * dec[0:c]
        k_out = k * dec[c:2 * c]
        att = mask_ref[0] * _dot_nt(q.astype(BF16), k.astype(BF16))
        for l in range(levels):
            d = dec[(2 + l) * c:(3 + l) * c]
            att = att + mask_ref[1 + l] * _dot_nt((q * d).astype(BF16), (k * d).astype(BF16))
        s = s_ref[...]
        o = _dot(att.astype(BF16), v) + _dot(q_in.astype(BF16), s.astype(BF16))
        b_last = _dot_tn(hi, ones_c) + _dot_tn(mid, ones_c) + _dot_tn(lo, ones_c)
        decay = jnp.exp(b_last)
        s_ref[...] = (s * jnp.concatenate([decay] * (GLA_HV // LANES), axis=1)
                      + _dot_tn(k_out.astype(BF16), v))
        gate = r_ref[rows, :].astype(F32)
        og = _row_rms(o) * go_ref[...] * (gate * _sigmoid(gate))
        og_ref[rows, :] = og.astype(BF16)
        return carry

    lax.fori_loop(0, nchunk, chunk, 0)

    @pl.when(t == pl.num_programs(2) - 1)
    def _():
        sout_ref[0, 0] = s_ref[...]


def _gla(zm, zs, w2ext, bg, go, s0, batch, seq, c, tb):
    n = zm.shape[0]
    nt = seq // tb
    gmat, masks, levels = _gla_tables(c)
    row = lambda b, h, t: b * nt + t
    return pl.pallas_call(
        functools.partial(_gla_body, c=c, nchunk=tb // c, levels=levels),
        grid=(batch, GLA_HEADS, nt),
        in_specs=[
            pl.BlockSpec((tb, GLA_HK), lambda b, h, t: (row(b, h, t), COL_Q // GLA_HK + h)),
            pl.BlockSpec((tb, GLA_HK), lambda b, h, t: (row(b, h, t), COL_K // GLA_HK + h)),
            pl.BlockSpec((tb, GLA_HV), lambda b, h, t: (row(b, h, t), COL_V // GLA_HV + h)),
            pl.BlockSpec((tb, GLA_HV), lambda b, h, t: (row(b, h, t), COL_R // GLA_HV + h)),
            pl.BlockSpec((tb, SMALL_COLS), lambda b, h, t: (row(b, h, t), 0)),
            pl.BlockSpec((SMALL_COLS, GLA_HK), lambda b, h, t: (0, h)),
            pl.BlockSpec((1, GLA_HK), lambda b, h, t: (0, h)),
            pl.BlockSpec((1, GLA_HV), lambda b, h, t: (0, 0)),
            pl.BlockSpec((1, 1, GLA_HK, GLA_HV), lambda b, h, t: (b, h, 0, 0)),
            pl.BlockSpec(gmat.shape, lambda b, h, t: (0, 0)),
            pl.BlockSpec(masks.shape, lambda b, h, t: (0, 0, 0)),
        ],
        out_specs=[
            pl.BlockSpec((tb, GLA_HV), lambda b, h, t: (row(b, h, t), h)),
            pl.BlockSpec((1, 1, GLA_HK, GLA_HV), lambda b, h, t: (b, h, 0, 0)),
        ],
        out_shape=[
            jax.ShapeDtypeStruct((n, GLA_HEADS * GLA_HV), BF16),
            jax.ShapeDtypeStruct((batch, GLA_HEADS, GLA_HK, GLA_HV), F32),
        ],
        scratch_shapes=[pltpu.VMEM((GLA_HK, GLA_HV), F32)],
        compiler_params=_params(("parallel", "parallel", "arbitrary"), 32),
        name="gla",
    )(zm, zm, zm, zm, zs, w2ext, bg, go, s0, gmat, masks)


def _mla_proj_body(mq_ref, mkv_ref, zs_ref, wqn_ref, wqp_ref, gql_ref, gkl_ref, gq_ref, gqp_ref, gkp_ref,
                   tab_ref, p_ref, q_ref, ckv_ref, kpe_ref, kp2_ref):
    qlat = (_row_rms(mq_ref[...].astype(F32)) * gql_ref[...]).astype(BF16)
    qn = _dot(qlat, wqn_ref[...])
    qp = _dot(qlat, wqp_ref[...])
    tab = tab_ref[...]
    for h in range(MLA_HEADS):
        lo = h * LANES
        nope = _slab_rms(qn[:, lo:lo + LANES], p_ref) * (gq_ref[...] * MLA_SCALE)
        pe = _slab_rms(qp[:, lo:lo + LANES], p_ref) * (gqp_ref[...] * MLA_SCALE) * tab
        q_ref[:, 2 * lo:2 * lo + LANES] = nope.astype(BF16)
        q_ref[:, 2 * lo + LANES:2 * lo + 2 * LANES] = pe.astype(BF16)
    ckv_ref[...] = _row_rms(mkv_ref[...].astype(F32)) * gkl_ref[...]
    slab = zs_ref[:, 0:LANES]
    rot = _row_rms(slab) * gkp_ref[...] * tab
    kp2 = rot + pltpu.roll(rot, MLA_ROPE, axis=1)
    kpe_ref[...] = kp2[:, 0:MLA_ROPE]
    kp2_ref[...] = kp2.astype(BF16)


def _mla_proj(zm, zs, wqn, wqp, gql, gkl, gq, gqp, gkp, tab, p128, tm, tab_blocks):
    n = zm.shape[0]
    const = lambda i: (0, 0)
    return pl.pallas_call(
        _mla_proj_body,
        grid=(n // tm,),
        in_specs=[
            pl.BlockSpec((tm, MLA_LORA), lambda i: (i, COL_MQ // MLA_LORA)),
            pl.BlockSpec((tm, MLA_LORA), lambda i: (i, COL_MKV // MLA_LORA)),
            pl.BlockSpec((tm, SMALL_COLS), lambda i: (i, 0)),
            pl.BlockSpec(wqn.shape, const),
            pl.BlockSpec(wqp.shape, const),
            pl.BlockSpec((1, MLA_LORA), const),
            pl.BlockSpec((1, MLA_LORA), const),
            pl.BlockSpec((1, LANES), const),
            pl.BlockSpec((1, LANES), const),
            pl.BlockSpec((1, LANES), const),
            pl.BlockSpec((tm, LANES), lambda i: (i % tab_blocks, 0)),
            pl.BlockSpec((LANES, LANES), const),
        ],
        out_specs=[
            pl.BlockSpec((tm, 2 * LANES * MLA_HEADS), lambda i: (i, 0)),
            pl.BlockSpec((tm, MLA_LORA), lambda i: (i, 0)),
            pl.BlockSpec((tm, MLA_ROPE), lambda i: (i, 0)),
            pl.BlockSpec((tm, LANES), lambda i: (i, 0)),
        ],
        out_shape=[
            jax.ShapeDtypeStruct((n, 2 * LANES * MLA_HEADS), BF16),
            jax.ShapeDtypeStruct((n, MLA_LORA), F32),
            jax.ShapeDtypeStruct((n, MLA_ROPE), F32),
            jax.ShapeDtypeStruct((n, LANES), BF16),
        ],
        compiler_params=_params(("parallel",), 48),
        name="mla_proj",
    )(zm, zm, zs, wqn, wqp, gql, gkl, gq, gqp, gkp, tab, p128)


def _mla_expand_body(c_ref, wk_ref, wv_ref, gk_ref, p_ref, kn_ref, v_ref):
    cb = c_ref[...].astype(BF16)
    kn = _dot(cb, wk_ref[...])
    for h in range(MLA_HEADS):
        lo = h * LANES
        kn_ref[:, lo:lo + LANES] = (_slab_rms(kn[:, lo:lo + LANES], p_ref) * gk_ref[...]).astype(BF16)
    v_ref[...] = _dot(cb, wv_ref[...]).astype(BF16)


def _mla_expand(ckv, wk, wv, gk, p128, tm):
    n = ckv.shape[0]
    const = lambda i: (0, 0)
    width = MLA_HEADS * LANES
    return pl.pallas_call(
        _mla_expand_body,
        grid=(n // tm,),
        in_specs=[
            pl.BlockSpec((tm, MLA_LORA), lambda i: (i, 0)),
            pl.BlockSpec(wk.shape, const),
            pl.BlockSpec(wv.shape, const),
            pl.BlockSpec((1, LANES), const),
            pl.BlockSpec((LANES, LANES), const),
        ],
        out_specs=[pl.BlockSpec((tm, width), lambda i: (i, 0))] * 2,
        out_shape=[jax.ShapeDtypeStruct((n, width), BF16)] * 2,
        compiler_params=_params(("parallel",), 40),
        name="mla_expand",
    )(ckv, wk, wv, gk, p128)


def _attn_prompt_body(q_ref, kn_ref, kp_ref, v_ref, o_ref, kext_ref, *, tq):
    qi = pl.program_id(2)

    @pl.when(qi == 0)
    def _():
        kext_ref[:, 0:LANES] = kn_ref[...]
        kext_ref[:, LANES:2 * LANES] = kp_ref[...]

    q = q_ref[...]

    def block(kb, carry, diagonal):
        m, l, acc = carry
        rows = pl.ds(pl.multiple_of(kb * tq, tq), tq)
        s = _dot_nt(q, kext_ref[rows, :])
        if diagonal:
            qc = lax.broadcasted_iota(jnp.int32, (tq, tq), 0) // CHUNK
            kc = lax.broadcasted_iota(jnp.int32, (tq, tq), 1) // CHUNK
            s = jnp.where(kc <= qc, s, NEG_BIG)
        m_new = jnp.maximum(m, jnp.max(s, axis=-1, keepdims=True))
        p = jnp.exp(s - m_new)
        alpha = jnp.exp(m - m_new)
        l = alpha * l + jnp.sum(p, axis=-1, keepdims=True)
        acc = alpha * acc + _dot(p.astype(BF16), v_ref[rows, :])
        return m_new, l, acc

    init = (jnp.full((tq, 1), NEG_BIG, F32), jnp.zeros((tq, 1), F32), jnp.zeros((tq, MLA_VDIM), F32))
    carry = lax.fori_loop(0, qi, lambda kb, cr: block(kb, cr, False), init)
    _, l, acc = block(qi, carry, True)
    o_ref[...] = (acc / l).astype(BF16)


def _attn_prompt(qext, kn, kp2, v, batch, seq, tq):
    n = qext.shape[0]
    nq = seq // tq
    return pl.pallas_call(
        functools.partial(_attn_prompt_body, tq=tq),
        grid=(batch, MLA_HEADS, nq),
        in_specs=[
            pl.BlockSpec((tq, 2 * LANES), lambda b, h, i: (b * nq + i, h)),
            pl.BlockSpec((seq, LANES), lambda b, h, i: (b, h)),
            pl.BlockSpec((seq, LANES), lambda b, h, i: (b, 0)),
            pl.BlockSpec((seq, LANES), lambda b, h, i: (b, h)),
        ],
        out_specs=pl.BlockSpec((tq, MLA_VDIM), lambda b, h, i: (b * nq + i, h)),
        out_shape=jax.ShapeDtypeStruct((n, MLA_HEADS * MLA_VDIM), BF16),
        scratch_shapes=[pltpu.VMEM((seq, 2 * LANES), BF16)],
        compiler_params=_params(("parallel", "parallel", "arbitrary"), 40),
        name="attn_prompt",
    )(qext, kn, kp2, v)


def _attn_sample_body(q_ref, knp_ref, kpp_ref, vp_ref, knn_ref, kpn_ref, vn_ref, o_ref):
    qn = q_ref[:, 0:LANES]
    qp = q_ref[:, LANES:2 * LANES]
    s_past = _dot_nt(qn, knp_ref[...]) + _dot_nt(qp, kpp_ref[...])
    s_new = _dot_nt(qn, knn_ref[...]) + _dot_nt(qp, kpn_ref[...])
    m = jnp.maximum(jnp.max(s_past, axis=-1, keepdims=True), jnp.max(s_new, axis=-1, keepdims=True))
    p_past = jnp.exp(s_past - m)
    p_new = jnp.exp(s_new - m)
    l = jnp.sum(p_past, axis=-1, keepdims=True) + jnp.sum(p_new, axis=-1, keepdims=True)
    acc = _dot(p_past.astype(BF16), vp_ref[...]) + _dot(p_new.astype(BF16), vn_ref[...])
    o_ref[...] = (acc / l).astype(BF16)


def _attn_sample(qext, kn_past, kp2_past, v_past, kn_new, kp2_new, v_new, batch, past, seq):
    n = qext.shape[0]
    return pl.pallas_call(
        _attn_sample_body,
        grid=(batch, MLA_HEADS),
        in_specs=[
            pl.BlockSpec((seq, 2 * LANES), lambda b, h: (b, h)),
            pl.BlockSpec((past, LANES), lambda b, h: (b, h)),
            pl.BlockSpec((past, LANES), lambda b, h: (b, 0)),
            pl.BlockSpec((past, LANES), lambda b, h: (b, h)),
            pl.BlockSpec((seq, LANES), lambda b, h: (b, h)),
            pl.BlockSpec((seq, LANES), lambda b, h: (b, 0)),
            pl.BlockSpec((seq, LANES), lambda b, h: (b, h)),
        ],
        out_specs=pl.BlockSpec((seq, MLA_VDIM), lambda b, h: (b, h)),
        out_shape=jax.ShapeDtypeStruct((n, MLA_HEADS * MLA_VDIM), BF16),
        compiler_params=_params(("parallel", "parallel"), 32),
        name="attn_sample",
    )(qext, kn_past, kp2_past, v_past, kn_new, kp2_new, v_new)


def _merge_body(og_ref, at_ref, wg_ref, wm_ref, ga_ref, gb_ref, wo_ref, x_ref, g2_ref, x1_ref, h2_ref):
    j = pl.program_id(1)

    @pl.when(j == 0)
    def _():
        x1_ref[...] = x_ref[...]

    u = (_sigmoid(ga_ref[...].astype(F32)) * _dot(og_ref[...], wg_ref[...])
         + _sigmoid(gb_ref[...].astype(F32)) * _dot(at_ref[...], wm_ref[...]))
    x1_ref[...] += _dot(u.astype(BF16), wo_ref[...])

    @pl.when(j == pl.num_programs(1) - 1)
    def _():
        h2_ref[...] = (_row_rms(x1_ref[...]) * g2_ref[...]).astype(BF16)


def _merge(og, att, wg, wm, zm, wo, x, g2, tm):
    n = x.shape[0]
    tn = 512
    return pl.pallas_call(
        _merge_body,
        grid=(n // tm, D_MODEL // tn),
        in_specs=[
            pl.BlockSpec((tm, D_MODEL), lambda i, j: (i, 0)),
            pl.BlockSpec((tm, D_MODEL), lambda i, j: (i, 0)),
            pl.BlockSpec((D_MODEL, tn), lambda i, j: (0, j)),
            pl.BlockSpec((D_MODEL, tn), lambda i, j: (0, j)),
            pl.BlockSpec((tm, tn), lambda i, j: (i, COL_GA // tn + j)),
            pl.BlockSpec((tm, tn), lambda i, j: (i, COL_GB // tn + j)),
            pl.BlockSpec((tn, D_MODEL), lambda i, j: (j, 0)),
            pl.BlockSpec((tm, D_MODEL), lambda i, j: (i, 0)),
            pl.BlockSpec((1, D_MODEL), lambda i, j: (0, 0)),
        ],
        out_specs=[
            pl.BlockSpec((tm, D_MODEL), lambda i, j: (i, 0)),
            pl.BlockSpec((tm, D_MODEL), lambda i, j: (i, 0)),
        ],
        out_shape=[
            jax.ShapeDtypeStruct((n, D_MODEL), F32),
            jax.ShapeDtypeStruct((n, D_MODEL), BF16),
        ],
        compiler_params=_params(("parallel", "arbitrary"), 56),
        name="merge",
    )(og, att, wg, wm, zm, zm, wo, x, g2)


def _gelu_gate(a, a1, a2, cw_ref, gt):
    c = cw_ref[3:4, :] + cw_ref[2:3, :] * a + cw_ref[0:1, :] * a2 + cw_ref[1:2, :] * a1
    return 0.5 * c * (1.0 + lax.erf(c * (2.0 ** -0.5))) * gt


def _ffn_up_seq_body(h_ref, wa_ref, wg_ref, cw_ref, hist_ref, act_ref, tail_ref, carry_ref, *, tm, tiles_per_seq):
    i = pl.program_id(0)
    j = pl.program_id(1)
    a = _dot(h_ref[...], wa_ref[...])
    gt = _dot(h_ref[...], wg_ref[...])

    @pl.when(i % tiles_per_seq == 0)
    def _():
        carry_ref[j] = hist_ref[0]

    prev = carry_ref[j]
    row = lax.broadcasted_iota(jnp.int32, (tm, 1), 0)
    a1 = jnp.where(row == 0, prev[7:8, :], pltpu.roll(a, 1, axis=0))
    a2 = jnp.where(row == 0, prev[6:7, :], jnp.where(row == 1, prev[7:8, :], pltpu.roll(a, 2, axis=0)))
    act_ref[...] = _gelu_gate(a, a1, a2, cw_ref, gt).astype(BF16)
    carry_ref[j] = a[tm - 8:tm, :]
    tail_ref[0] = a[tm - (CONV_W - 1):tm, :]


def _ffn_up_seq(h2, wup, cw, hist8, batch, seq, tm):
    n = h2.shape[0]
    tn = 512
    nj = D_FF // tn
    tps = seq // tm
    return pl.pallas_call(
        functools.partial(_ffn_up_seq_body, tm=tm, tiles_per_seq=tps),
        grid=(n // tm, nj),
        in_specs=[
            pl.BlockSpec((tm, D_MODEL), lambda i, j: (i, 0)),
            pl.BlockSpec((D_MODEL, tn), lambda i, j: (0, j)),
            pl.BlockSpec((D_MODEL, tn), lambda i, j: (0, nj + j)),
            pl.BlockSpec((8, tn), lambda i, j: (0, j)),
            pl.BlockSpec((1, 8, tn), lambda i, j: (i // tps, 0, j)),
        ],
        out_specs=[
            pl.BlockSpec((tm, tn), lambda i, j: (i, j)),
            pl.BlockSpec((1, CONV_W - 1, tn), lambda i, j: (i, 0, j)),
        ],
        out_shape=[
            jax.ShapeDtypeStruct((n, D_FF), BF16),
            jax.ShapeDtypeStruct((n // tm, CONV_W - 1, D_FF), F32),
        ],
        scratch_shapes=[pltpu.VMEM((nj, 8, tn), F32)],
        compiler_params=_params(("arbitrary", "arbitrary"), 48),
        name="ffn_up_seq",
    )(h2, wup, wup, cw, hist8)


def _ffn_up_multi_body(h_ref, wa_ref, wg_ref, cw_ref, p1_ref, p2_ref, act_ref, a_ref, *, tm, seq):
    a = _dot(h_ref[...], wa_ref[...])
    gt = _dot(h_ref[...], wg_ref[...])
    pos = lax.broadcasted_iota(jnp.int32, (tm, 1), 0) % seq
    a1 = jnp.where(pos == 0, p1_ref[...], pltpu.roll(a, 1, axis=0))
    a2 = jnp.where(pos <= 1, p2_ref[...], pltpu.roll(a, 2, axis=0))
    act_ref[...] = _gelu_gate(a, a1, a2, cw_ref, gt).astype(BF16)
    a_ref[...] = a


def _ffn_up_multi(h2, wup, cw, p1, p2, seq):
    n = h2.shape[0]
    tn = 512
    nj = D_FF // tn
    return pl.pallas_call(
        functools.partial(_ffn_up_multi_body, tm=n, seq=seq),
        grid=(nj,),
        in_specs=[
            pl.BlockSpec((n, D_MODEL), lambda j: (0, 0)),
            pl.BlockSpec((D_MODEL, tn), lambda j: (0, j)),
            pl.BlockSpec((D_MODEL, tn), lambda j: (0, nj + j)),
            pl.BlockSpec((8, tn), lambda j: (0, j)),
            pl.BlockSpec((n, tn), lambda j: (0, j)),
            pl.BlockSpec((n, tn), lambda j: (0, j)),
        ],
        out_specs=[pl.BlockSpec((n, tn), lambda j: (0, j))] * 2,
        out_shape=[jax.ShapeDtypeStruct((n, D_FF), BF16), jax.ShapeDtypeStruct((n, D_FF), F32)],
        compiler_params=_params(("parallel",), 32),
        name="ffn_up_multi",
    )(h2, wup, wup, cw, p1, p2)


def _ffn_down_body(act_ref, wd_ref, x1_ref, o_ref):
    o_ref[...] = x1_ref[...] + _dot(act_ref[...], wd_ref[...])


def _ffn_down(act, wd, x1, tm):
    n = act.shape[0]
    tn = 512
    return pl.pallas_call(
        _ffn_down_body,
        grid=(n // tm, D_MODEL // tn),
        in_specs=[
            pl.BlockSpec((tm, D_FF), lambda i, j: (i, 0)),
            pl.BlockSpec((D_FF, tn), lambda i, j: (0, j)),
            pl.BlockSpec((tm, tn), lambda i, j: (i, j)),
        ],
        out_specs=pl.BlockSpec((tm, tn), lambda i, j: (i, j)),
        out_shape=jax.ShapeDtypeStruct((n, D_MODEL), F32),
        compiler_params=_params(("parallel", "parallel"), 48),
        name="ffn_down",
    )(act, wd, x1)


def _swap_halves(w):
    half = w.shape[-1] // 2
    return jnp.concatenate([w[..., half:], w[..., :half]], axis=-1)


def _rope_table(pos):
    half = MLA_ROPE // 2
    inv = ROPE_THETA ** (-jnp.arange(half, dtype=F32) * 2.0 / MLA_ROPE)
    ang = pos.astype(F32)[:, None] * inv[None, :]
    cos, sin = jnp.cos(ang), jnp.sin(ang)
    return jnp.concatenate([cos, cos, -sin, sin], axis=-1)


def _layer_weights(w_in, g_norm1, gla_w_gate2, gla_b_gate, gla_g_out, w_br_gla, mla_g_qlat, mla_w_uq,
                   mla_g_kvlat, mla_w_ukv, mla_g_q, mla_g_k, mla_g_qpe, mla_g_kpe, w_br_mla, w_out,
                   g_norm2, ffn_w_up, ffn_conv_w, ffn_conv_b, ffn_w_down):
    o_lr = 2 * 1024 + 2048
    o_r = o_lr + GLA_GATE_RANK
    o_kpe = o_r + 2048 + 2 * MLA_LORA
    o_ga = o_kpe + MLA_ROPE
    w_main = jnp.concatenate([w_in[:, :o_lr], w_in[:, o_r:o_kpe], w_in[:, o_ga:]], axis=1).astype(BF16)
    w_kpe = w_in[:, o_kpe:o_ga]
    w_small = jnp.concatenate(
        [w_kpe, _swap_halves(w_kpe), w_in[:, o_lr:o_r],
         jnp.zeros((D_MODEL, SMALL_COLS - SMALL_LR - GLA_GATE_RANK), F32)], axis=1).astype(BF16)
    w2ext = jnp.zeros((SMALL_COLS, GLA_HEADS * GLA_HK), F32).at[SMALL_LR:SMALL_LR + GLA_GATE_RANK].set(
        gla_w_gate2).astype(BF16)
    wq = mla_w_uq.reshape(MLA_LORA, MLA_HEADS, MLA_NOPE + MLA_ROPE)
    wqn = wq[:, :, :MLA_NOPE].reshape(MLA_LORA, MLA_HEADS * MLA_NOPE).astype(BF16)
    wq_pe = wq[:, :, MLA_NOPE:]
    wqp = jnp.concatenate([wq_pe, _swap_halves(wq_pe)], axis=-1).reshape(MLA_LORA, MLA_HEADS * LANES).astype(BF16)
    wkv = mla_w_ukv.reshape(MLA_LORA, MLA_HEADS, MLA_NOPE + MLA_VDIM)
    wk = wkv[:, :, :MLA_NOPE].reshape(MLA_LORA, MLA_HEADS * MLA_NOPE).astype(BF16)
    wv = wkv[:, :, MLA_NOPE:].reshape(MLA_LORA, MLA_HEADS * MLA_VDIM).astype(BF16)
    cw = jnp.concatenate([ffn_conv_w, ffn_conv_b[None, :], jnp.zeros((8 - CONV_W - 1, D_FF), F32)], axis=0)
    return dict(
        w_main=w_main, w_small=w_small, g1=g_norm1[None, :], w2ext=w2ext, bg=gla_b_gate[None, :],
        go=gla_g_out[None, :], wg=w_br_gla.astype(BF16), wqn=wqn, wqp=wqp, gql=mla_g_qlat[None, :],
        gkl=mla_g_kvlat[None, :], gq=mla_g_q[None, :],
        gqp=jnp.concatenate([mla_g_qpe, _swap_halves(mla_g_qpe)])[None, :],
        gkp=jnp.concatenate([mla_g_kpe, _swap_halves(mla_g_kpe)])[None, :],
        wk=wk, wv=wv, gk=mla_g_k[None, :], wm=w_br_mla.astype(BF16), wo=w_out.astype(BF16),
        g2=g_norm2[None, :], wup=ffn_w_up.astype(BF16), cw=cw, wd=ffn_w_down.astype(BF16),
        p128=jnp.full((LANES, LANES), 1.0 / LANES, BF16),
    )


def _trunk_front(x, w, s0, batch, seq, chunk, gla_tb, tm, tab, tab_blocks):
    zm, zs = _in_proj(x, w["g1"], w["w_main"], w["w_small"], tm)
    og, s_new = _gla(zm, zs, w["w2ext"], w["bg"], w["go"], s0, batch, seq, chunk, gla_tb)
    qext, ckv, kpe, kp2 = _mla_proj(zm, zs, w["wqn"], w["wqp"], w["gql"], w["gkl"], w["gq"], w["gqp"],
                                    w["gkp"], tab, w["p128"], tm, tab_blocks)
    kn, v = _mla_expand(ckv, w["wk"], w["wv"], w["gk"], w["p128"], tm)
    return zm, og, s_new, qext, ckv, kpe, kp2, kn, v


def kernel(x_prompt, x_sample, state_gla, cache_mla_ckv, cache_mla_kpe, cache_ffn_conv, w_in, g_norm1, gla_w_gate2, gla_b_gate, gla_g_out, w_br_gla, mla_g_qlat, mla_w_uq, mla_g_kvlat, mla_w_ukv, mla_g_q, mla_g_k, mla_g_qpe, mla_g_kpe, w_br_mla, w_out, g_norm2, ffn_w_up, ffn_conv_w, ffn_conv_b, ffn_w_down):
    bp, tp, _ = x_prompt.shape
    bs, ts, _ = x_sample.shape
    depth = w_in.shape[0]
    past = cache_mla_ckv.shape[2]
    np_rows, ns_rows = bp * tp, bs * ts
    tm_p = 512
    tab_p = _rope_table(jnp.arange(tp))
    tab_s = jnp.tile(_rope_table(past + jnp.arange(ts)), (bs, 1))
    xp = x_prompt.reshape(np_rows, D_MODEL)
    xs = x_sample.reshape(ns_rows, D_MODEL)
    outs = [[] for _ in range(8)]
    layer_weights = (w_in, g_norm1, gla_w_gate2, gla_b_gate, gla_g_out, w_br_gla, mla_g_qlat, mla_w_uq,
                     mla_g_kvlat, mla_w_ukv, mla_g_q, mla_g_k, mla_g_qpe, mla_g_kpe, w_br_mla, w_out,
                     g_norm2, ffn_w_up, ffn_conv_w, ffn_conv_b, ffn_w_down)
    for l in range(depth):
        w = _layer_weights(*[a[l] for a in layer_weights])

        s0 = jnp.zeros((bp, GLA_HEADS, GLA_HK, GLA_HV), F32)
        zm, og, sp, qext, ckv_p, kpe_p, kp2, kn, v = _trunk_front(
            xp, w, s0, bp, tp, CHUNK, 512, tm_p, tab_p, tp // tm_p)
        att = _attn_prompt(qext, kn, kp2, v, bp, tp, 512)
        x1, h2 = _merge(og, att, w["wg"], w["wm"], zm, w["wo"], xp, w["g2"], tm_p)
        hist8 = jnp.zeros((bp, 8, D_FF), F32)
        act, tails = _ffn_up_seq(h2, w["wup"], w["cw"], hist8, bp, tp, tm_p)
        fp = tails.reshape(bp, tp // tm_p, CONV_W - 1, D_FF)[:, -1]
        xp = _ffn_down(act, w["wd"], x1, tm_p)

        zm, og, ss, qext, ckv_s, kpe_s, kp2, kn, v = _trunk_front(
            xs, w, state_gla[l], bs, ts, ts, ts, ns_rows, tab_s, 1)
        kn_past, v_past = _mla_expand(cache_mla_ckv[l].reshape(bs * past, MLA_LORA), w["wk"], w["wv"],
                                      w["gk"], w["p128"], 512)
        kpe_past = cache_mla_kpe[l].reshape(bs * past, MLA_ROPE)
        kp2_past = jnp.concatenate([kpe_past, kpe_past], axis=-1).astype(BF16)
        att = _attn_sample(qext, kn_past, kp2_past, v_past, kn, kp2, v, bs, past, ts)
        x1, h2 = _merge(og, att, w["wg"], w["wm"], zm, w["wo"], xs, w["g2"], ns_rows)
        hist = cache_ffn_conv[l]
        zrow = jnp.zeros((bs, ts - 1, D_FF), F32)
        p1 = jnp.concatenate([hist[:, 1:2], zrow], axis=1).reshape(ns_rows, D_FF)
        p2 = jnp.concatenate([hist, zrow[:, 1:]], axis=1).reshape(ns_rows, D_FF)
        act, a_full = _ffn_up_multi(h2, w["wup"], w["cw"], p1, p2, ts)
        fs = a_full.reshape(bs, ts, D_FF)[:, ts - (CONV_W - 1):]
        xs = _ffn_down(act, w["wd"], x1, ns_rows)

        for lst, val in zip(outs, (sp, ss, ckv_p.reshape(bp, tp, MLA_LORA), ckv_s.reshape(bs, ts, MLA_LORA),
                                   kpe_p.reshape(bp, tp, MLA_ROPE), kpe_s.reshape(bs, ts, MLA_ROPE), fp, fs)):
            lst.append(val)
    return (xp.reshape(bp, tp, D_MODEL), xs.reshape(bs, ts, D_MODEL)) + tuple(jnp.stack(o, 0) for o in outs)
```

```python
import functools

import numpy as np
import jax
import jax.numpy as jnp
from jax import lax
from jax.experimental import pallas as pl
from jax.experimental.pallas import tpu as pltpu

F32 = jnp.float32
BF16 = jnp.bfloat16

D_MODEL = 2048
CHUNK = 64
EPS = 1e-6
GLA_HEADS = 4
GLA_HK = 256
GLA_HV = 512
GLA_GATE_RANK = 16
GLA_GATE_TAU = 16.0
MLA_HEADS = 16
MLA_LORA = 512
MLA_NOPE = 128
MLA_ROPE = 64
MLA_VDIM = 128
MLA_SCALE = (MLA_NOPE + MLA_ROPE) ** -0.5
Q_SCALE = MLA_SCALE * float(np.log2(np.e))
ROPE_THETA = 10000.0
D_FF = 5632
CONV_W = 3
LANES = 128
NEG_BIG = -1e30
ATTN_UNROLL = 4

COL_Q, COL_K, COL_V, COL_R, COL_MQ, COL_MKV, COL_GA, COL_GB, MAIN_COLS = (
    0, 1024, 2048, 4096, 6144, 6656, 7168, 9216, 11264)
SMALL_COLS = 256
SMALL_LR = 128

MIB = 1024 * 1024


def _params(semantics, vmem_mib):
    return pltpu.CompilerParams(dimension_semantics=semantics, vmem_limit_bytes=vmem_mib * MIB)


def _dot(a, b):
    return jnp.dot(a, b, preferred_element_type=F32)


def _dot_nt(a, b):
    return lax.dot_general(a, b, (((1,), (1,)), ((), ())), preferred_element_type=F32)


def _dot_tn(a, b):
    return lax.dot_general(a, b, (((0,), (0,)), ((), ())), preferred_element_type=F32)


def _sigmoid(x):
    return 1.0 / (1.0 + jnp.exp(-x))


def _row_rms(x):
    return x * lax.rsqrt(jnp.mean(x * x, axis=-1, keepdims=True) + EPS)


def _slab_rms(x, p_ref):
    ms = _dot((x * x).astype(BF16), p_ref[...])
    return x * lax.rsqrt(ms + EPS)


def _in_proj_body(x_ref, g_ref, wm_ref, ws_ref, zm_ref, zs_ref, h_ref):
    @pl.when(pl.program_id(1) == 0)
    def _():
        h_ref[...] = (_row_rms(x_ref[...]) * g_ref[...]).astype(BF16)
        zs_ref[...] = _dot(h_ref[...], ws_ref[...])

    zm_ref[...] = _dot(h_ref[...], wm_ref[...]).astype(BF16)


def _in_proj(x, g, wm, ws, tm):
    n = x.shape[0]
    tn = 1024
    return pl.pallas_call(
        _in_proj_body,
        grid=(n // tm, MAIN_COLS // tn),
        in_specs=[
            pl.BlockSpec((tm, D_MODEL), lambda i, j: (i, 0)),
            pl.BlockSpec((1, D_MODEL), lambda i, j: (0, 0)),
            pl.BlockSpec((D_MODEL, tn), lambda i, j: (0, j)),
            pl.BlockSpec((D_MODEL, SMALL_COLS), lambda i, j: (0, 0)),
        ],
        out_specs=[
            pl.BlockSpec((tm, tn), lambda i, j: (i, j)),
            pl.BlockSpec((tm, SMALL_COLS), lambda i, j: (i, 0)),
        ],
        out_shape=[
            jax.ShapeDtypeStruct((n, MAIN_COLS), BF16),
            jax.ShapeDtypeStruct((n, SMALL_COLS), F32),
        ],
        scratch_shapes=[pltpu.VMEM((tm, D_MODEL), BF16)],
        compiler_params=_params(("parallel", "arbitrary"), 48),
        name="in_proj",
    )(x, g, wm, ws)


def _gla_tables(c):
    levels = int(np.log2(c))
    t = np.arange(c)[:, None]
    u = np.arange(c)[None, :]
    mats = [(u <= t), (u > t)]
    masks = [(u == t)]
    for l in range(levels):
        m = c >> (l + 1)
        mid_t = (t // (2 * m)) * 2 * m + m
        upper = t >= mid_t
        mats.append(np.where(upper, (u >= mid_t) & (u <= t), (u > t) & (u < mid_t)))
        mid_u = (u // (2 * m)) * 2 * m + m
        masks.append((t // (2 * m) == u // (2 * m)) & upper & (u < mid_u))
    gmat = np.concatenate(mats, axis=0).astype(np.float32)
    return jnp.asarray(gmat, BF16), jnp.asarray(np.stack(masks).astype(np.float32)), levels


def _gla_body(q_ref, k_ref, v_ref, r_ref, zs_ref, w2_ref, bg_ref, go_ref, s0_ref, gmat_ref, mask_ref,
              og_ref, sout_ref, s_ref, *, c, nchunk, levels):
    t = pl.program_id(2)

    @pl.when(t == 0)
    def _():
        s_ref[...] = s0_ref[0, 0]

    ones_c = jnp.ones((c, LANES), BF16)

    def chunk(ci, carry):
        r0 = pl.multiple_of(ci * c, c)
        rows = pl.ds(r0, c)
        q = q_ref[rows, :].astype(F32) * (GLA_HK ** -0.5)
        k = k_ref[rows, :].astype(F32)
        v = v_ref[rows, :]
        x = _dot(zs_ref[rows, :].astype(BF16), w2_ref[...]) + bg_ref[...]
        log_a = (jnp.minimum(x, 0.0) - jnp.log1p(jnp.exp(-jnp.abs(x)))) * (1.0 / GLA_GATE_TAU)
        hi = log_a.astype(BF16)
        rem = log_a - hi.astype(F32)
        mid = rem.astype(BF16)
        lo = (rem - mid.astype(F32)).astype(BF16)
        gm = gmat_ref[...]
        dec = jnp.exp(_dot(gm, hi) + _dot(gm, mid) + _dot(gm, lo))
        q_in = q * dec[0:c]
        k_out = k * dec[c:2 * c]
        att = mask_ref[0] * _dot_nt(q.astype(BF16), k.astype(BF16))
        for l in range(levels):
            d = dec[(2 + l) * c:(3 + l) * c]
            att = att + mask_ref[1 + l] * _dot_nt((q * d).astype(BF16), (k * d).astype(BF16))
        s = s_ref[...]
        o = _dot(att.astype(BF16), v) + _dot(q_in.astype(BF16), s.astype(BF16))
        b_last = _dot_tn(hi, ones_c) + _dot_tn(mid, ones_c) + _dot_tn(lo, ones_c)
        decay = jnp.exp(b_last)
        s_ref[...] = (s * jnp.concatenate([decay] * (GLA_HV // LANES), axis=1)
                      + _dot_tn(k_out.astype(BF16), v))
        gate = r_ref[rows, :].astype(F32)
        og = _row_rms(o) * go_ref[...] * (gate * _sigmoid(gate))
        og_ref[rows, :] = og.astype(BF16)
        return carry

    lax.fori_loop(0, nchunk, chunk, 0)

    @pl.when(t == pl.num_programs(2) - 1)
    def _():
        sout_ref[0, 0] = s_ref[...]


def _gla(zm, zs, w2ext, bg, go, s0, batch, seq, c, tb):
    n = zm.shape[0]
    nt = seq // tb
    gmat, masks, levels = _gla_tables(c)
    row = lambda b, h, t: b * nt + t
    return pl.pallas_call(
        functools.partial(_gla_body, c=c, nchunk=tb // c, levels=levels),
        grid=(batch, GLA_HEADS, nt),
        in_specs=[
            pl.BlockSpec((tb, GLA_HK), lambda b, h, t: (row(b, h, t), COL_Q // GLA_HK + h)),
            pl.BlockSpec((tb, GLA_HK), lambda b, h, t: (row(b, h, t), COL_K // GLA_HK + h)),
            pl.BlockSpec((tb, GLA_HV), lambda b, h, t: (row(b, h, t), COL_V // GLA_HV + h)),
            pl.BlockSpec((tb, GLA_HV), lambda b, h, t: (row(b, h, t), COL_R // GLA_HV + h)),
            pl.BlockSpec((tb, SMALL_COLS), lambda b, h, t: (row(b, h, t), 0)),
            pl.BlockSpec((SMALL_COLS, GLA_HK), lambda b, h, t: (0, h)),
            pl.BlockSpec((1, GLA_HK), lambda b, h, t: (0, h)),
            pl.BlockSpec((1, GLA_HV), lambda b, h, t: (0, 0)),
            pl.BlockSpec((1, 1, GLA_HK, GLA_HV), lambda b, h, t: (b, h, 0, 0)),
            pl.BlockSpec(gmat.shape, lambda b, h, t: (0, 0)),
            pl.BlockSpec(masks.shape, lambda b, h, t: (0, 0, 0)),
        ],
        out_specs=[
            pl.BlockSpec((tb, GLA_HV), lambda b, h, t: (row(b, h, t), h)),
            pl.BlockSpec((1, 1, GLA_HK, GLA_HV), lambda b, h, t: (b, h, 0, 0)),
        ],
        out_shape=[
            jax.ShapeDtypeStruct((n, GLA_HEADS * GLA_HV), BF16),
            jax.ShapeDtypeStruct((batch, GLA_HEADS, GLA_HK, GLA_HV), F32),
        ],
        scratch_shapes=[pltpu.VMEM((GLA_HK, GLA_HV), F32)],
        compiler_params=_params(("parallel", "parallel", "arbitrary"), 32),
        name="gla",
    )(zm, zm, zm, zm, zs, w2ext, bg, go, s0, gmat, masks)


def _mla_proj_body(mq_ref, mkv_ref, zs_ref, wqn_ref, wqp_ref, gql_ref, gkl_ref, gq_ref, gqp_ref, gkp_ref,
                   tab_ref, p_ref, q_ref, ckv_ref, kpe_ref, kp2_ref):
    qlat = (_row_rms(mq_ref[...].astype(F32)) * gql_ref[...]).astype(BF16)
    qn = _dot(qlat, wqn_ref[...])
    qp = _dot(qlat, wqp_ref[...])
    tab = tab_ref[...]
    for h in range(MLA_HEADS):
        lo = h * LANES
        nope = _slab_rms(qn[:, lo:lo + LANES], p_ref) * (gq_ref[...] * Q_SCALE)
        pe = _slab_rms(qp[:, lo:lo + LANES], p_ref) * (gqp_ref[...] * Q_SCALE) * tab
        q_ref[:, 2 * lo:2 * lo + LANES] = nope.astype(BF16)
        q_ref[:, 2 * lo + LANES:2 * lo + 2 * LANES] = pe.astype(BF16)
    ckv_ref[...] = _row_rms(mkv_ref[...].astype(F32)) * gkl_ref[...]
    slab = zs_ref[:, 0:LANES]
    rot = _row_rms(slab) * gkp_ref[...] * tab
    kp2 = rot + pltpu.roll(rot, MLA_ROPE, axis=1)
    kpe_ref[...] = kp2[:, 0:MLA_ROPE]
    kp2_ref[...] = kp2.astype(BF16)


def _mla_proj(zm, zs, wqn, wqp, gql, gkl, gq, gqp, gkp, tab, p128, tm, tab_blocks):
    n = zm.shape[0]
    const = lambda i: (0, 0)
    return pl.pallas_call(
        _mla_proj_body,
        grid=(n // tm,),
        in_specs=[
            pl.BlockSpec((tm, MLA_LORA), lambda i: (i, COL_MQ // MLA_LORA)),
            pl.BlockSpec((tm, MLA_LORA), lambda i: (i, COL_MKV // MLA_LORA)),
            pl.BlockSpec((tm, SMALL_COLS), lambda i: (i, 0)),
            pl.BlockSpec(wqn.shape, const),
            pl.BlockSpec(wqp.shape, const),
            pl.BlockSpec((1, MLA_LORA), const),
            pl.BlockSpec((1, MLA_LORA), const),
            pl.BlockSpec((1, LANES), const),
            pl.BlockSpec((1, LANES), const),
            pl.BlockSpec((1, LANES), const),
            pl.BlockSpec((tm, LANES), lambda i: (i % tab_blocks, 0)),
            pl.BlockSpec((LANES, LANES), const),
        ],
        out_specs=[
            pl.BlockSpec((tm, 2 * LANES * MLA_HEADS), lambda i: (i, 0)),
            pl.BlockSpec((tm, MLA_LORA), lambda i: (i, 0)),
            pl.BlockSpec((tm, MLA_ROPE), lambda i: (i, 0)),
            pl.BlockSpec((tm, LANES), lambda i: (i, 0)),
        ],
        out_shape=[
            jax.ShapeDtypeStruct((n, 2 * LANES * MLA_HEADS), BF16),
            jax.ShapeDtypeStruct((n, MLA_LORA), F32),
            jax.ShapeDtypeStruct((n, MLA_ROPE), F32),
            jax.ShapeDtypeStruct((n, LANES), BF16),
        ],
        compiler_params=_params(("parallel",), 48),
        name="mla_proj",
    )(zm, zm, zs, wqn, wqp, gql, gkl, gq, gqp, gkp, tab, p128)


def _mla_expand_body(c_ref, wk_ref, wv_ref, gk_ref, p_ref, kn_ref, v_ref, *, transposed_v):
    cb = c_ref[...].astype(BF16)
    kn = _dot(cb, wk_ref[...])
    for h in range(MLA_HEADS):
        lo = h * LANES
        kn_ref[:, lo:lo + LANES] = (_slab_rms(kn[:, lo:lo + LANES], p_ref) * gk_ref[...]).astype(BF16)
    if transposed_v:
        vt = _dot_nt(wv_ref[...], cb).astype(BF16)
        for h in range(MLA_HEADS):
            v_ref[h, 0] = vt[h * MLA_VDIM:(h + 1) * MLA_VDIM, :]
    else:
        v_ref[...] = _dot(cb, wv_ref[...]).astype(BF16)


def _mla_expand(ckv, wk, wv, gk, p128, tm, transposed_v=False):
    n = ckv.shape[0]
    const = lambda i: (0, 0)
    width = MLA_HEADS * LANES
    if transposed_v:
        v_spec = pl.BlockSpec((MLA_HEADS, 1, MLA_VDIM, tm), lambda i: (0, i, 0, 0))
        v_shape = jax.ShapeDtypeStruct((MLA_HEADS, n // tm, MLA_VDIM, tm), BF16)
    else:
        v_spec = pl.BlockSpec((tm, width), lambda i: (i, 0))
        v_shape = jax.ShapeDtypeStruct((n, width), BF16)
    return pl.pallas_call(
        functools.partial(_mla_expand_body, transposed_v=transposed_v),
        grid=(n // tm,),
        in_specs=[
            pl.BlockSpec((tm, MLA_LORA), lambda i: (i, 0)),
            pl.BlockSpec(wk.shape, const),
            pl.BlockSpec(wv.shape, const),
            pl.BlockSpec((1, LANES), const),
            pl.BlockSpec((LANES, LANES), const),
        ],
        out_specs=[pl.BlockSpec((tm, width), lambda i: (i, 0)), v_spec],
        out_shape=[jax.ShapeDtypeStruct((n, width), BF16), v_shape],
        compiler_params=_params(("parallel",), 40),
        name="mla_expand",
    )(ckv, wk, wv, gk, p128)


def _attn_prompt_body(qi_tab, kb_tab, q_ref, kn_ref, kp_ref, vt_ref, o_ref,
                      kext_ref, vx_ref, bias_ref, m_ref, acc_ref, s0, s1, p0, p1, a0, a1, *, tq, nq):
    @pl.when((pl.program_id(0) == 0) & (pl.program_id(1) == 0))
    def _():
        kc = lax.broadcasted_iota(jnp.int32, (tq, tq), 0) // CHUNK
        qc = lax.broadcasted_iota(jnp.int32, (tq, tq), 1) // CHUNK
        bias_ref[...] = jnp.where(kc <= qc, 0.0, NEG_BIG)

    kext_ref[:, 0:LANES] = kn_ref[...]
    kext_ref[:, LANES:2 * LANES] = kp_ref[...]
    vx_ref[:, 0:MLA_VDIM, :] = vt_ref[0]
    vx_ref[:, MLA_VDIM:, :] = jnp.ones((nq, vx_ref.shape[1] - MLA_VDIM, tq), BF16)
    m_ref[...] = jnp.full(m_ref.shape, NEG_BIG, F32)
    acc_ref[...] = jnp.zeros(acc_ref.shape, F32)
    nblk = nq * (nq + 1) // 2
    s_bufs, p_bufs, a_bufs = (s0, s1), (p0, p1), (a0, a1)

    def rows(i):
        return pl.ds(pl.multiple_of(i * tq, tq), tq)

    def scores(t, par):
        s_bufs[par][...] = _dot_nt(kext_ref[rows(kb_tab[t]), :], q_ref[rows(qi_tab[t]), :])

    def softmax(t, par, diagonal):
        qi = qi_tab[t]
        s = s_bufs[par][...]
        if diagonal:
            s = s + bias_ref[...]
        m_old = m_ref[qi]
        m_new = jnp.maximum(m_old, jnp.max(s, axis=0, keepdims=True))
        a_bufs[par][...] = jnp.exp2(m_old - m_new)
        m_ref[qi] = m_new
        p_bufs[par][...] = jnp.exp2(s - m_new).astype(BF16)

    def values(t, par):
        qi = qi_tab[t]
        acc_ref[qi] = a_bufs[par][...] * acc_ref[qi] + _dot(vx_ref[kb_tab[t]], p_bufs[par][...])

    def step(u, par, diagonal):
        scores(u, par)
        softmax(u - 1, 1 - par, diagonal)
        values(u - 2, par)

    def sweep(first, stop, diagonal):
        trips = (stop - first) // ATTN_UNROLL

        def trip(j, carry):
            for i in range(ATTN_UNROLL):
                step(first + ATTN_UNROLL * j + i, i % 2, diagonal)
            return carry

        lax.fori_loop(0, trips, trip, 0)
        for u in range(first + trips * ATTN_UNROLL, stop):
            step(u, u % 2, diagonal)

    scores(0, 0)
    scores(1, 1)
    softmax(0, 0, True)
    sweep(2, nq + 1, True)
    step(nq + 1, 1, False)
    sweep(nq + 2, nblk, False)
    softmax(nblk - 1, 1, False)
    values(nblk - 2, 0)
    values(nblk - 1, 1)
    for qi in range(nq):
        acc = acc_ref[qi]
        o_ref[qi * tq:(qi + 1) * tq, :] = jnp.transpose(
            acc[0:MLA_VDIM] / acc[MLA_VDIM:MLA_VDIM + 1]).astype(BF16)


def _attn_prompt(qext, kn, kp2, vt, batch, seq, tq):
    n = qext.shape[0]
    nq = seq // tq
    assert nq % 2 == 0
    pairs = [(i, i) for i in range(nq)] + [(qi, kb) for qi in range(nq) for kb in range(qi)]
    qi_tab = jnp.asarray([p[0] for p in pairs], jnp.int32)
    kb_tab = jnp.asarray([p[1] for p in pairs], jnp.int32)
    ones_rows = 16
    grid_spec = pltpu.PrefetchScalarGridSpec(
        num_scalar_prefetch=2,
        grid=(batch, MLA_HEADS),
        in_specs=[
            pl.BlockSpec((seq, 2 * LANES), lambda b, h, *_: (b, h)),
            pl.BlockSpec((seq, LANES), lambda b, h, *_: (b, h)),
            pl.BlockSpec((seq, LANES), lambda b, h, *_: (b, 0)),
            pl.BlockSpec((1, nq, MLA_VDIM, tq), lambda b, h, *_: (h, b, 0, 0)),
        ],
        out_specs=pl.BlockSpec((seq, MLA_VDIM), lambda b, h, *_: (b, h)),
        scratch_shapes=[
            pltpu.VMEM((seq, 2 * LANES), BF16),
            pltpu.VMEM((nq, MLA_VDIM + ones_rows, tq), BF16),
            pltpu.VMEM((tq, tq), F32),
            pltpu.VMEM((nq, 1, tq), F32),
            pltpu.VMEM((nq, MLA_VDIM + ones_rows, tq), F32),
            pltpu.VMEM((tq, tq), F32), pltpu.VMEM((tq, tq), F32),
            pltpu.VMEM((tq, tq), BF16), pltpu.VMEM((tq, tq), BF16),
            pltpu.VMEM((1, tq), F32), pltpu.VMEM((1, tq), F32),
        ],
    )
    return pl.pallas_call(
        functools.partial(_attn_prompt_body, tq=tq, nq=nq),
        grid_spec=grid_spec,
        out_shape=jax.ShapeDtypeStruct((n, MLA_HEADS * MLA_VDIM), BF16),
        compiler_params=_params(("arbitrary", "arbitrary"), 40),
        name="attn_prompt",
    )(qi_tab, kb_tab, qext, kn, kp2, vt)


def _attn_sample_body(q_ref, knp_ref, kpp_ref, vp_ref, knn_ref, kpn_ref, vn_ref, o_ref):
    qn = q_ref[:, 0:LANES]
    qp = q_ref[:, LANES:2 * LANES]
    s_past = _dot_nt(qn, knp_ref[...]) + _dot_nt(qp, kpp_ref[...])
    s_new = _dot_nt(qn, knn_ref[...]) + _dot_nt(qp, kpn_ref[...])
    m = jnp.maximum(jnp.max(s_past, axis=-1, keepdims=True), jnp.max(s_new, axis=-1, keepdims=True))
    p_past = jnp.exp2(s_past - m)
    p_new = jnp.exp2(s_new - m)
    l = jnp.sum(p_past, axis=-1, keepdims=True) + jnp.sum(p_new, axis=-1, keepdims=True)
    acc = _dot(p_past.astype(BF16), vp_ref[...]) + _dot(p_new.astype(BF16), vn_ref[...])
    o_ref[...] = (acc / l).astype(BF16)


def _attn_sample(qext, kn_past, kp2_past, v_past, kn_new, kp2_new, v_new, batch, past, seq):
    n = qext.shape[0]
    return pl.pallas_call(
        _attn_sample_body,
        grid=(batch, MLA_HEADS),
        in_specs=[
            pl.BlockSpec((seq, 2 * LANES), lambda b, h: (b, h)),
            pl.BlockSpec((past, LANES), lambda b, h: (b, h)),
            pl.BlockSpec((past, LANES), lambda b, h: (b, 0)),
            pl.BlockSpec((past, LANES), lambda b, h: (b, h)),
            pl.BlockSpec((seq, LANES), lambda b, h: (b, h)),
            pl.BlockSpec((seq, LANES), lambda b, h: (b, 0)),
            pl.BlockSpec((seq, LANES), lambda b, h: (b, h)),
        ],
        out_specs=pl.BlockSpec((seq, MLA_VDIM), lambda b, h: (b, h)),
        out_shape=jax.ShapeDtypeStruct((n, MLA_HEADS * MLA_VDIM), BF16),
        compiler_params=_params(("parallel", "parallel"), 32),
        name="attn_sample",
    )(qext, kn_past, kp2_past, v_past, kn_new, kp2_new, v_new)


def _merge_body(og_ref, at_ref, wg_ref, wm_ref, ga_ref, gb_ref, wo_ref, x_ref, g2_ref, x1_ref, h2_ref):
    j = pl.program_id(1)

    @pl.when(j == 0)
    def _():
        x1_ref[...] = x_ref[...]

    u = (_sigmoid(ga_ref[...].astype(F32)) * _dot(og_ref[...], wg_ref[...])
         + _sigmoid(gb_ref[...].astype(F32)) * _dot(at_ref[...], wm_ref[...]))
    x1_ref[...] += _dot(u.astype(BF16), wo_ref[...])

    @pl.when(j == pl.num_programs(1) - 1)
    def _():
        h2_ref[...] = (_row_rms(x1_ref[...]) * g2_ref[...]).astype(BF16)


def _merge(og, att, wg, wm, zm, wo, x, g2, tm):
    n = x.shape[0]
    tn = 512
    return pl.pallas_call(
        _merge_body,
        grid=(n // tm, D_MODEL // tn),
        in_specs=[
            pl.BlockSpec((tm, D_MODEL), lambda i, j: (i, 0)),
            pl.BlockSpec((tm, D_MODEL), lambda i, j: (i, 0)),
            pl.BlockSpec((D_MODEL, tn), lambda i, j: (0, j)),
            pl.BlockSpec((D_MODEL, tn), lambda i, j: (0, j)),
            pl.BlockSpec((tm, tn), lambda i, j: (i, COL_GA // tn + j)),
            pl.BlockSpec((tm, tn), lambda i, j: (i, COL_GB // tn + j)),
            pl.BlockSpec((tn, D_MODEL), lambda i, j: (j, 0)),
            pl.BlockSpec((tm, D_MODEL), lambda i, j: (i, 0)),
            pl.BlockSpec((1, D_MODEL), lambda i, j: (0, 0)),
        ],
        out_specs=[
            pl.BlockSpec((tm, D_MODEL), lambda i, j: (i, 0)),
            pl.BlockSpec((tm, D_MODEL), lambda i, j: (i, 0)),
        ],
        out_shape=[
            jax.ShapeDtypeStruct((n, D_MODEL), F32),
            jax.ShapeDtypeStruct((n, D_MODEL), BF16),
        ],
        compiler_params=_params(("parallel", "arbitrary"), 56),
        name="merge",
    )(og, att, wg, wm, zm, zm, wo, x, g2)


def _gelu_gate(a, a1, a2, cw_ref, gt):
    c = cw_ref[3:4, :] + cw_ref[2:3, :] * a + cw_ref[0:1, :] * a2 + cw_ref[1:2, :] * a1
    return 0.5 * c * (1.0 + lax.erf(c * (2.0 ** -0.5))) * gt


def _ffn_up_seq_body(h_ref, wa_ref, wg_ref, cw_ref, hist_ref, act_ref, tail_ref, carry_ref, *, tm, tiles_per_seq):
    i = pl.program_id(0)
    j = pl.program_id(1)
    a = _dot(h_ref[...], wa_ref[...])
    gt = _dot(h_ref[...], wg_ref[...])

    @pl.when(i % tiles_per_seq == 0)
    def _():
        carry_ref[j] = hist_ref[0]

    prev = carry_ref[j]
    row = lax.broadcasted_iota(jnp.int32, (tm, 1), 0)
    a1 = jnp.where(row == 0, prev[7:8, :], pltpu.roll(a, 1, axis=0))
    a2 = jnp.where(row == 0, prev[6:7, :], jnp.where(row == 1, prev[7:8, :], pltpu.roll(a, 2, axis=0)))
    act_ref[...] = _gelu_gate(a, a1, a2, cw_ref, gt).astype(BF16)
    carry_ref[j] = a[tm - 8:tm, :]
    tail_ref[0] = a[tm - (CONV_W - 1):tm, :]


def _ffn_up_seq(h2, wup, cw, hist8, batch, seq, tm):
    n = h2.shape[0]
    tn = 512
    nj = D_FF // tn
    tps = seq // tm
    return pl.pallas_call(
        functools.partial(_ffn_up_seq_body, tm=tm, tiles_per_seq=tps),
        grid=(n // tm, nj),
        in_specs=[
            pl.BlockSpec((tm, D_MODEL), lambda i, j: (i, 0)),
            pl.BlockSpec((D_MODEL, tn), lambda i, j: (0, j)),
            pl.BlockSpec((D_MODEL, tn), lambda i, j: (0, nj + j)),
            pl.BlockSpec((8, tn), lambda i, j: (0, j)),
            pl.BlockSpec((1, 8, tn), lambda i, j: (i // tps, 0, j)),
        ],
        out_specs=[
            pl.BlockSpec((tm, tn), lambda i, j: (i, j)),
            pl.BlockSpec((1, CONV_W - 1, tn), lambda i, j: (i, 0, j)),
        ],
        out_shape=[
            jax.ShapeDtypeStruct((n, D_FF), BF16),
            jax.ShapeDtypeStruct((n // tm, CONV_W - 1, D_FF), F32),
        ],
        scratch_shapes=[pltpu.VMEM((nj, 8, tn), F32)],
        compiler_params=_params(("arbitrary", "arbitrary"), 48),
        name="ffn_up_seq",
    )(h2, wup, wup, cw, hist8)


def _ffn_up_multi_body(h_ref, wa_ref, wg_ref, cw_ref, p1_ref, p2_ref, act_ref, a_ref, *, tm, seq):
    a = _dot(h_ref[...], wa_ref[...])
    gt = _dot(h_ref[...], wg_ref[...])
    pos = lax.broadcasted_iota(jnp.int32, (tm, 1), 0) % seq
    a1 = jnp.where(pos == 0, p1_ref[...], pltpu.roll(a, 1, axis=0))
    a2 = jnp.where(pos <= 1, p2_ref[...], pltpu.roll(a, 2, axis=0))
    act_ref[...] = _gelu_gate(a, a1, a2, cw_ref, gt).astype(BF16)
    a_ref[...] = a


def _ffn_up_multi(h2, wup, cw, p1, p2, seq):
    n = h2.shape[0]
    tn = 512
    nj = D_FF // tn
    return pl.pallas_call(
        functools.partial(_ffn_up_multi_body, tm=n, seq=seq),
        grid=(nj,),
        in_specs=[
            pl.BlockSpec((n, D_MODEL), lambda j: (0, 0)),
            pl.BlockSpec((D_MODEL, tn), lambda j: (0, j)),
            pl.BlockSpec((D_MODEL, tn), lambda j: (0, nj + j)),
            pl.BlockSpec((8, tn), lambda j: (0, j)),
            pl.BlockSpec((n, tn), lambda j: (0, j)),
            pl.BlockSpec((n, tn), lambda j: (0, j)),
        ],
        out_specs=[pl.BlockSpec((n, tn), lambda j: (0, j))] * 2,
        out_shape=[jax.ShapeDtypeStruct((n, D_FF), BF16), jax.ShapeDtypeStruct((n, D_FF), F32)],
        compiler_params=_params(("parallel",), 32),
        name="ffn_up_multi",
    )(h2, wup, wup, cw, p1, p2)


def _ffn_down_body(act_ref, wd_ref, x1_ref, o_ref):
    o_ref[...] = x1_ref[...] + _dot(act_ref[...], wd_ref[...])


def _ffn_down(act, wd, x1, tm):
    n = act.shape[0]
    tn = 512
    return pl.pallas_call(
        _ffn_down_body,
        grid=(n // tm, D_MODEL // tn),
        in_specs=[
            pl.BlockSpec((tm, D_FF), lambda i, j: (i, 0)),
            pl.BlockSpec((D_FF, tn), lambda i, j: (0, j)),
            pl.BlockSpec((tm, tn), lambda i, j: (i, j)),
        ],
        out_specs=pl.BlockSpec((tm, tn), lambda i, j: (i, j)),
        out_shape=jax.ShapeDtypeStruct((n, D_MODEL), F32),
        compiler_params=_params(("parallel", "parallel"), 48),
        name="ffn_down",
    )(act, wd, x1)


def _swap_halves(w):
    half = w.shape[-1] // 2
    return jnp.concatenate([w[..., half:], w[..., :half]], axis=-1)


def _rope_table(pos):
    half = MLA_ROPE // 2
    inv = ROPE_THETA ** (-jnp.arange(half, dtype=F32) * 2.0 / MLA_ROPE)
    ang = pos.astype(F32)[:, None] * inv[None, :]
    cos, sin = jnp.cos(ang), jnp.sin(ang)
    return jnp.concatenate([cos, cos, -sin, sin], axis=-1)


def _layer_weights(w_in, g_norm1, gla_w_gate2, gla_b_gate, gla_g_out, w_br_gla, mla_g_qlat, mla_w_uq,
                   mla_g_kvlat, mla_w_ukv, mla_g_q, mla_g_k, mla_g_qpe, mla_g_kpe, w_br_mla, w_out,
                   g_norm2, ffn_w_up, ffn_conv_w, ffn_conv_b, ffn_w_down):
    o_lr = 2 * 1024 + 2048
    o_r = o_lr + GLA_GATE_RANK
    o_kpe = o_r + 2048 + 2 * MLA_LORA
    o_ga = o_kpe + MLA_ROPE
    w_main = jnp.concatenate([w_in[:, :o_lr], w_in[:, o_r:o_kpe], w_in[:, o_ga:]], axis=1).astype(BF16)
    w_kpe = w_in[:, o_kpe:o_ga]
    w_small = jnp.concatenate(
        [w_kpe, _swap_halves(w_kpe), w_in[:, o_lr:o_r],
         jnp.zeros((D_MODEL, SMALL_COLS - SMALL_LR - GLA_GATE_RANK), F32)], axis=1).astype(BF16)
    w2ext = jnp.zeros((SMALL_COLS, GLA_HEADS * GLA_HK), F32).at[SMALL_LR:SMALL_LR + GLA_GATE_RANK].set(
        gla_w_gate2).astype(BF16)
    wq = mla_w_uq.reshape(MLA_LORA, MLA_HEADS, MLA_NOPE + MLA_ROPE)
    wqn = wq[:, :, :MLA_NOPE].reshape(MLA_LORA, MLA_HEADS * MLA_NOPE).astype(BF16)
    wq_pe = wq[:, :, MLA_NOPE:]
    wqp = jnp.concatenate([wq_pe, _swap_halves(wq_pe)], axis=-1).reshape(MLA_LORA, MLA_HEADS * LANES).astype(BF16)
    wkv = mla_w_ukv.reshape(MLA_LORA, MLA_HEADS, MLA_NOPE + MLA_VDIM)
    wk = wkv[:, :, :MLA_NOPE].reshape(MLA_LORA, MLA_HEADS * MLA_NOPE).astype(BF16)
    wv = wkv[:, :, MLA_NOPE:].reshape(MLA_LORA, MLA_HEADS * MLA_VDIM).astype(BF16)
    cw = jnp.concatenate([ffn_conv_w, ffn_conv_b[None, :], jnp.zeros((8 - CONV_W - 1, D_FF), F32)], axis=0)
    return dict(
        w_main=w_main, w_small=w_small, g1=g_norm1[None, :], w2ext=w2ext, bg=gla_b_gate[None, :],
        go=gla_g_out[None, :], wg=w_br_gla.astype(BF16), wqn=wqn, wqp=wqp, gql=mla_g_qlat[None, :],
        gkl=mla_g_kvlat[None, :], gq=mla_g_q[None, :],
        gqp=jnp.concatenate([mla_g_qpe, _swap_halves(mla_g_qpe)])[None, :],
        gkp=jnp.concatenate([mla_g_kpe, _swap_halves(mla_g_kpe)])[None, :],
        wk=wk, wv=wv, wvt=wv.T, gk=mla_g_k[None, :], wm=w_br_mla.astype(BF16), wo=w_out.astype(BF16),
        g2=g_norm2[None, :], wup=ffn_w_up.astype(BF16), cw=cw, wd=ffn_w_down.astype(BF16),
        p128=jnp.full((LANES, LANES), 1.0 / LANES, BF16),
    )


def _trunk_front(x, w, s0, batch, seq, chunk, gla_tb, tm, tab, tab_blocks, transposed_v):
    zm, zs = _in_proj(x, w["g1"], w["w_main"], w["w_small"], tm)
    og, s_new = _gla(zm, zs, w["w2ext"], w["bg"], w["go"], s0, batch, seq, chunk, gla_tb)
    qext, ckv, kpe, kp2 = _mla_proj(zm, zs, w["wqn"], w["wqp"], w["gql"], w["gkl"], w["gq"], w["gqp"],
                                    w["gkp"], tab, w["p128"], tm, tab_blocks)
    kn, v = _mla_expand(ckv, w["wk"], w["wvt"] if transposed_v else w["wv"], w["gk"], w["p128"], tm,
                        transposed_v)
    return zm, og, s_new, qext, ckv, kpe, kp2, kn, v


def kernel(x_prompt, x_sample, state_gla, cache_mla_ckv, cache_mla_kpe, cache_ffn_conv, w_in, g_norm1, gla_w_gate2, gla_b_gate, gla_g_out, w_br_gla, mla_g_qlat, mla_w_uq, mla_g_kvlat, mla_w_ukv, mla_g_q, mla_g_k, mla_g_qpe, mla_g_kpe, w_br_mla, w_out, g_norm2, ffn_w_up, ffn_conv_w, ffn_conv_b, ffn_w_down):
    bp, tp, _ = x_prompt.shape
    bs, ts, _ = x_sample.shape
    depth = w_in.shape[0]
    past = cache_mla_ckv.shape[2]
    np_rows, ns_rows = bp * tp, bs * ts
    tm_p = 512
    tab_p = _rope_table(jnp.arange(tp))
    tab_s = jnp.tile(_rope_table(past + jnp.arange(ts)), (bs, 1))
    xp = x_prompt.reshape(np_rows, D_MODEL)
    xs = x_sample.reshape(ns_rows, D_MODEL)
    outs = [[] for _ in range(8)]
    layer_weights = (w_in, g_norm1, gla_w_gate2, gla_b_gate, gla_g_out, w_br_gla, mla_g_qlat, mla_w_uq,
                     mla_g_kvlat, mla_w_ukv, mla_g_q, mla_g_k, mla_g_qpe, mla_g_kpe, w_br_mla, w_out,
                     g_norm2, ffn_w_up, ffn_conv_w, ffn_conv_b, ffn_w_down)
    for l in range(depth):
        w = _layer_weights(*[a[l] for a in layer_weights])

        s0 = jnp.zeros((bp, GLA_HEADS, GLA_HK, GLA_HV), F32)
        zm, og, sp, qext, ckv_p, kpe_p, kp2, kn, v = _trunk_front(
            xp, w, s0, bp, tp, CHUNK, 512, tm_p, tab_p, tp // tm_p, True)
        att = _attn_prompt(qext, kn, kp2, v, bp, tp, 512)
        x1, h2 = _merge(og, att, w["wg"], w["wm"], zm, w["wo"], xp, w["g2"], tm_p)
        hist8 = jnp.zeros((bp, 8, D_FF), F32)
        act, tails = _ffn_up_seq(h2, w["wup"], w["cw"], hist8, bp, tp, tm_p)
        fp = tails.reshape(bp, tp // tm_p, CONV_W - 1, D_FF)[:, -1]
        xp = _ffn_down(act, w["wd"], x1, tm_p)

        zm, og, ss, qext, ckv_s, kpe_s, kp2, kn, v = _trunk_front(
            xs, w, state_gla[l], bs, ts, ts, ts, ns_rows, tab_s, 1, False)
        kn_past, v_past = _mla_expand(cache_mla_ckv[l].reshape(bs * past, MLA_LORA), w["wk"], w["wv"],
                                      w["gk"], w["p128"], 512)
        kpe_past = cache_mla_kpe[l].reshape(bs * past, MLA_ROPE)
        kp2_past = jnp.concatenate([kpe_past, kpe_past], axis=-1).astype(BF16)
        att = _attn_sample(qext, kn_past, kp2_past, v_past, kn, kp2, v, bs, past, ts)
        x1, h2 = _merge(og, att, w["wg"], w["wm"], zm, w["wo"], xs, w["g2"], ns_rows)
        hist = cache_ffn_conv[l]
        zrow = jnp.zeros((bs, ts - 1, D_FF), F32)
        p1 = jnp.concatenate([hist[:, 1:2], zrow], axis=1).reshape(ns_rows, D_FF)
        p2 = jnp.concatenate([hist, zrow[:, 1:]], axis=1).reshape(ns_rows, D_FF)
        act, a_full = _ffn_up_multi(h2, w["wup"], w["cw"], p1, p2, ts)
        fs = a_full.reshape(bs, ts, D_FF)[:, ts - (CONV_W - 1):]
        xs = _ffn_down(act, w["wd"], x1, ns_rows)

        for lst, val in zip(outs, (sp, ss, ckv_p.reshape(bp, tp, MLA_LORA), ckv_s.reshape(bs, ts, MLA_LORA),
                                   kpe_p.reshape(bp, tp, MLA_ROPE), kpe_s.reshape(bs, ts, MLA_ROPE), fp, fs)):
            lst.append(val)
    return (xp.reshape(bp, tp, D_MODEL), xs.reshape(bs, ts, D_MODEL)) + tuple(jnp.stack(o, 0) for o in outs)
```

```python
import functools

import numpy as np
import jax
import jax.numpy as jnp
from jax import lax
from jax.experimental import pallas as pl
from jax.experimental.pallas import tpu as pltpu

F32 = jnp.float32
BF16 = jnp.bfloat16

D_MODEL = 2048
CHUNK = 64
EPS = 1e-6
GLA_HEADS = 4
GLA_HK = 256
GLA_HV = 512
GLA_GATE_RANK = 16
GLA_GATE_TAU = 16.0
MLA_HEADS = 16
MLA_LORA = 512
MLA_NOPE = 128
MLA_ROPE = 64
MLA_VDIM = 128
MLA_SCALE = (MLA_NOPE + MLA_ROPE) ** -0.5
LOG2E = float(np.log2(np.e))
Q_SCALE = MLA_SCALE * LOG2E
ROPE_THETA = 10000.0
D_FF = 5632
CONV_W = 3
LANES = 128
NEG_BIG = -1e30
ATTN_UNROLL = 4

COL_Q, COL_K, COL_V, COL_R, COL_MQ, COL_MKV, COL_GA, COL_GB, MAIN_COLS = (
    0, 1024, 2048, 4096, 6144, 6656, 7168, 9216, 11264)
SMALL_COLS = 256
SMALL_LR = 128

MIB = 1024 * 1024


def _params(semantics, vmem_mib):
    return pltpu.CompilerParams(dimension_semantics=semantics, vmem_limit_bytes=vmem_mib * MIB)


def _dot(a, b):
    return jnp.dot(a, b, preferred_element_type=F32)


def _dot_nt(a, b):
    return lax.dot_general(a, b, (((1,), (1,)), ((), ())), preferred_element_type=F32)


def _dot_tn(a, b):
    return lax.dot_general(a, b, (((0,), (0,)), ((), ())), preferred_element_type=F32)


def _sigmoid(x):
    return 1.0 / (1.0 + jnp.exp(-x))


def _row_rms(x):
    return x * lax.rsqrt(jnp.mean(x * x, axis=-1, keepdims=True) + EPS)


def _slab_rms(x, p_ref):
    ms = _dot((x * x).astype(BF16), p_ref[...])
    return x * lax.rsqrt(ms + EPS)


def _in_proj_body(x_ref, g_ref, wm_ref, ws_ref, zm_ref, zs_ref, h_ref):
    @pl.when(pl.program_id(1) == 0)
    def _():
        h_ref[...] = (_row_rms(x_ref[...]) * g_ref[...]).astype(BF16)
        zs_ref[...] = _dot(h_ref[...], ws_ref[...])

    zm_ref[...] = _dot(h_ref[...], wm_ref[...]).astype(BF16)


def _in_proj(x, g, wm, ws, tm):
    n = x.shape[0]
    tn = 1024
    return pl.pallas_call(
        _in_proj_body,
        grid=(n // tm, MAIN_COLS // tn),
        in_specs=[
            pl.BlockSpec((tm, D_MODEL), lambda i, j: (i, 0)),
            pl.BlockSpec((1, D_MODEL), lambda i, j: (0, 0)),
            pl.BlockSpec((D_MODEL, tn), lambda i, j: (0, j)),
            pl.BlockSpec((D_MODEL, SMALL_COLS), lambda i, j: (0, 0)),
        ],
        out_specs=[
            pl.BlockSpec((tm, tn), lambda i, j: (i, j)),
            pl.BlockSpec((tm, SMALL_COLS), lambda i, j: (i, 0)),
        ],
        out_shape=[
            jax.ShapeDtypeStruct((n, MAIN_COLS), BF16),
            jax.ShapeDtypeStruct((n, SMALL_COLS), F32),
        ],
        scratch_shapes=[pltpu.VMEM((tm, D_MODEL), BF16)],
        compiler_params=_params(("parallel", "arbitrary"), 48),
        name="in_proj",
    )(x, g, wm, ws)


def _gla_tables(c):
    levels = int(np.log2(c))
    t = np.arange(c)[:, None]
    u = np.arange(c)[None, :]
    masks = [(u == t)]
    level2 = None
    for l in range(levels):
        m = c >> (l + 1)
        mid_t = (t // (2 * m)) * 2 * m + m
        upper = t >= mid_t
        if m == 2:
            level2 = np.where(upper, (u >= mid_t) & (u <= t), (u > t) & (u < mid_t))
        mid_u = (u // (2 * m)) * 2 * m + m
        masks.append((t // (2 * m) == u // (2 * m)) & upper & (u < mid_u))
    eye_h = np.eye(GLA_HEADS, dtype=np.float32)
    gmat = np.concatenate([np.kron(eye_h, (u <= t).astype(np.float32)),
                           np.kron(eye_h, level2.astype(np.float32))], axis=0)
    group = _gla_group_heads(c)
    masks = np.stack([np.kron(np.eye(group, dtype=np.float32), mk.astype(np.float32)) for mk in masks])
    return jnp.asarray(gmat, BF16), jnp.asarray(masks), levels


def _gla_group_heads(c):
    return min(GLA_HEADS, max(1, LANES // c))


def _gla_level_exponent(b, log_a, level2, m, c):
    if m == 1:
        row = lax.broadcasted_iota(jnp.int32, (c, 1), 0)
        return jnp.where(row % 2 == 1, log_a, 0.0)
    if m == 2:
        return level2
    parts = [jnp.broadcast_to(b[i + m - 1:i + m, :], (2 * m, GLA_HK)) for i in range(0, c, 2 * m)]
    ref = parts[0] if len(parts) == 1 else jnp.concatenate(parts, axis=0)
    return -jnp.abs(b - ref)


def _gla_body(q_ref, k_ref, v_ref, r_ref, zs_ref, w2_ref, bg_ref, go_ref, s0_ref, gmat_ref, mask_ref,
              og_ref, sout_ref, st_ref, *, c, nchunk, levels):
    t = pl.program_id(1)

    @pl.when(t == 0)
    def _():
        for h in range(GLA_HEADS):
            st_ref[h] = jnp.transpose(s0_ref[0, h])

    nrow = GLA_HEADS * c
    gw = _gla_group_heads(c) * c

    def stack(x, width):
        return jnp.concatenate([x[:, h * width:(h + 1) * width] for h in range(GLA_HEADS)], axis=0)

    def chunk(ci, carry):
        rows = pl.ds(pl.multiple_of(ci * c, c), c)
        q = stack(q_ref[rows, :], GLA_HK).astype(F32) * (GLA_HK ** -0.5)
        k = stack(k_ref[rows, :], GLA_HK).astype(F32)
        v = stack(v_ref[rows, :], GLA_HV)
        x = stack(_dot(zs_ref[rows, :].astype(BF16), w2_ref[...]) + bg_ref[...], GLA_HK) * LOG2E
        log_a = (jnp.minimum(x, 0.0) - jnp.log2(1.0 + jnp.exp2(-jnp.abs(x)))) * (1.0 / GLA_GATE_TAU)
        hi = log_a.astype(BF16)
        lo = (log_a - hi.astype(F32)).astype(BF16)
        gm = gmat_ref[...]
        pre = _dot(gm, hi) + _dot(gm, lo)
        b = pre[0:nrow]
        b_last = [b[h * c + c - 1:h * c + c, :] for h in range(GLA_HEADS)]
        q_in = (q * jnp.exp2(b)).astype(BF16)
        k_out = (k * jnp.exp2(jnp.concatenate([jnp.broadcast_to(r, (c, GLA_HK)) for r in b_last], axis=0)
                              - b)).astype(BF16)

        def diag_blocks(prod):
            return [prod[g:g + gw, g:g + gw] for g in range(0, nrow, gw)]

        att = [mask_ref[0] * blk for blk in diag_blocks(_dot_nt(q.astype(BF16), k.astype(BF16)))]
        for l in range(levels):
            d = jnp.exp2(_gla_level_exponent(b, log_a, pre[nrow:2 * nrow], c >> (l + 1), nrow))
            prod = _dot_nt((q * d).astype(BF16), (k * d).astype(BF16))
            att = [a + mask_ref[1 + l] * blk for a, blk in zip(att, diag_blocks(prod))]
        if len(att) == 1:
            att_full = att[0].astype(BF16)
        else:
            zero = jnp.zeros((gw, gw), BF16)
            att_full = jnp.concatenate(
                [jnp.concatenate([a.astype(BF16) if i == j else zero for j in range(len(att))], axis=1)
                 for i, a in enumerate(att)], axis=0)
        o_intra = _dot(att_full, v)
        for h in range(GLA_HEADS):
            hr = slice(h * c, (h + 1) * c)
            cv = slice(h * GLA_HV, (h + 1) * GLA_HV)
            st = st_ref[h]
            o = o_intra[hr] + _dot_nt(q_in[hr], st.astype(BF16))
            st_ref[h] = st * jnp.exp2(b_last[h]) + _dot_tn(v[hr], k_out[hr])
            gate = r_ref[rows, cv].astype(F32)
            og = _row_rms(o) * go_ref[...] * (gate * _sigmoid(gate))
            og_ref[rows, cv] = og.astype(BF16)
        return carry

    lax.fori_loop(0, nchunk, chunk, 0)

    @pl.when(t == pl.num_programs(1) - 1)
    def _():
        for h in range(GLA_HEADS):
            sout_ref[0, h] = jnp.transpose(st_ref[h])


def _gla(zm, zs, w2ext, bg, go, s0, batch, seq, c, tb):
    n = zm.shape[0]
    nt = seq // tb
    gmat, masks, levels = _gla_tables(c)
    dk, dv = GLA_HEADS * GLA_HK, GLA_HEADS * GLA_HV
    const2 = lambda b, t: (0, 0)
    return pl.pallas_call(
        functools.partial(_gla_body, c=c, nchunk=tb // c, levels=levels),
        grid=(batch, nt),
        in_specs=[
            pl.BlockSpec((tb, dk), lambda b, t: (b * nt + t, COL_Q // dk)),
            pl.BlockSpec((tb, dk), lambda b, t: (b * nt + t, COL_K // dk)),
            pl.BlockSpec((tb, dv), lambda b, t: (b * nt + t, COL_V // dv)),
            pl.BlockSpec((tb, dv), lambda b, t: (b * nt + t, COL_R // dv)),
            pl.BlockSpec((tb, SMALL_COLS), lambda b, t: (b * nt + t, 0)),
            pl.BlockSpec((SMALL_COLS, dk), const2),
            pl.BlockSpec((1, dk), const2),
            pl.BlockSpec((1, GLA_HV), const2),
            pl.BlockSpec((1, GLA_HEADS, GLA_HK, GLA_HV), lambda b, t: (b, 0, 0, 0)),
            pl.BlockSpec(gmat.shape, const2),
            pl.BlockSpec(masks.shape, lambda b, t: (0, 0, 0)),
        ],
        out_specs=[
            pl.BlockSpec((tb, dv), lambda b, t: (b * nt + t, 0)),
            pl.BlockSpec((1, GLA_HEADS, GLA_HK, GLA_HV), lambda b, t: (b, 0, 0, 0)),
        ],
        out_shape=[
            jax.ShapeDtypeStruct((n, dv), BF16),
            jax.ShapeDtypeStruct((batch, GLA_HEADS, GLA_HK, GLA_HV), F32),
        ],
        scratch_shapes=[pltpu.VMEM((GLA_HEADS, GLA_HV, GLA_HK), F32)],
        compiler_params=_params(("parallel", "arbitrary"), 40),
        name="gla",
    )(zm, zm, zm, zm, zs, w2ext, bg, go, s0, gmat, masks)


def _mla_proj_body(mq_ref, mkv_ref, zs_ref, wqn_ref, wqp_ref, gql_ref, gkl_ref, gq_ref, gqp_ref, gkp_ref,
                   tab_ref, p_ref, q_ref, ckv_ref, kpe_ref, kp2_ref):
    qlat = (_row_rms(mq_ref[...].astype(F32)) * gql_ref[...]).astype(BF16)
    qn = _dot(qlat, wqn_ref[...])
    qp = _dot(qlat, wqp_ref[...])
    tab = tab_ref[...]
    for h in range(MLA_HEADS):
        lo = h * LANES
        nope = _slab_rms(qn[:, lo:lo + LANES], p_ref) * (gq_ref[...] * Q_SCALE)
        pe = _slab_rms(qp[:, lo:lo + LANES], p_ref) * (gqp_ref[...] * Q_SCALE) * tab
        q_ref[:, 2 * lo:2 * lo + LANES] = nope.astype(BF16)
        q_ref[:, 2 * lo + LANES:2 * lo + 2 * LANES] = pe.astype(BF16)
    ckv_ref[...] = _row_rms(mkv_ref[...].astype(F32)) * gkl_ref[...]
    slab = zs_ref[:, 0:LANES]
    rot = _row_rms(slab) * gkp_ref[...] * tab
    kp2 = rot + pltpu.roll(rot, MLA_ROPE, axis=1)
    kpe_ref[...] = kp2[:, 0:MLA_ROPE]
    kp2_ref[...] = kp2.astype(BF16)


def _mla_proj(zm, zs, wqn, wqp, gql, gkl, gq, gqp, gkp, tab, p128, tm, tab_blocks):
    n = zm.shape[0]
    const = lambda i: (0, 0)
    return pl.pallas_call(
        _mla_proj_body,
        grid=(n // tm,),
        in_specs=[
            pl.BlockSpec((tm, MLA_LORA), lambda i: (i, COL_MQ // MLA_LORA)),
            pl.BlockSpec((tm, MLA_LORA), lambda i: (i, COL_MKV // MLA_LORA)),
            pl.BlockSpec((tm, SMALL_COLS), lambda i: (i, 0)),
            pl.BlockSpec(wqn.shape, const),
            pl.BlockSpec(wqp.shape, const),
            pl.BlockSpec((1, MLA_LORA), const),
            pl.BlockSpec((1, MLA_LORA), const),
            pl.BlockSpec((1, LANES), const),
            pl.BlockSpec((1, LANES), const),
            pl.BlockSpec((1, LANES), const),
            pl.BlockSpec((tm, LANES), lambda i: (i % tab_blocks, 0)),
            pl.BlockSpec((LANES, LANES), const),
        ],
        out_specs=[
            pl.BlockSpec((tm, 2 * LANES * MLA_HEADS), lambda i: (i, 0)),
            pl.BlockSpec((tm, MLA_LORA), lambda i: (i, 0)),
            pl.BlockSpec((tm, MLA_ROPE), lambda i: (i, 0)),
            pl.BlockSpec((tm, LANES), lambda i: (i, 0)),
        ],
        out_shape=[
            jax.ShapeDtypeStruct((n, 2 * LANES * MLA_HEADS), BF16),
            jax.ShapeDtypeStruct((n, MLA_LORA), F32),
            jax.ShapeDtypeStruct((n, MLA_ROPE), F32),
            jax.ShapeDtypeStruct((n, LANES), BF16),
        ],
        compiler_params=_params(("parallel",), 48),
        name="mla_proj",
    )(zm, zm, zs, wqn, wqp, gql, gkl, gq, gqp, gkp, tab, p128)


def _mla_expand_body(c_ref, wk_ref, wv_ref, gk_ref, p_ref, kn_ref, v_ref, *, transposed_v):
    cb = c_ref[...].astype(BF16)
    kn = _dot(cb, wk_ref[...])
    for h in range(MLA_HEADS):
        lo = h * LANES
        kn_ref[:, lo:lo + LANES] = (_slab_rms(kn[:, lo:lo + LANES], p_ref) * gk_ref[...]).astype(BF16)
    if transposed_v:
        vt = _dot_nt(wv_ref[...], cb).astype(BF16)
        for h in range(MLA_HEADS):
            v_ref[h, 0] = vt[h * MLA_VDIM:(h + 1) * MLA_VDIM, :]
    else:
        v_ref[...] = _dot(cb, wv_ref[...]).astype(BF16)


def _mla_expand(ckv, wk, wv, gk, p128, tm, transposed_v=False):
    n = ckv.shape[0]
    const = lambda i: (0, 0)
    width = MLA_HEADS * LANES
    if transposed_v:
        v_spec = pl.BlockSpec((MLA_HEADS, 1, MLA_VDIM, tm), lambda i: (0, i, 0, 0))
        v_shape = jax.ShapeDtypeStruct((MLA_HEADS, n // tm, MLA_VDIM, tm), BF16)
    else:
        v_spec = pl.BlockSpec((tm, width), lambda i: (i, 0))
        v_shape = jax.ShapeDtypeStruct((n, width), BF16)
    return pl.pallas_call(
        functools.partial(_mla_expand_body, transposed_v=transposed_v),
        grid=(n // tm,),
        in_specs=[
            pl.BlockSpec((tm, MLA_LORA), lambda i: (i, 0)),
            pl.BlockSpec(wk.shape, const),
            pl.BlockSpec(wv.shape, const),
            pl.BlockSpec((1, LANES), const),
            pl.BlockSpec((LANES, LANES), const),
        ],
        out_specs=[pl.BlockSpec((tm, width), lambda i: (i, 0)), v_spec],
        out_shape=[jax.ShapeDtypeStruct((n, width), BF16), v_shape],
        compiler_params=_params(("parallel",), 40),
        name="mla_expand",
    )(ckv, wk, wv, gk, p128)


def _attn_prompt_body(qi_tab, kb_tab, q_ref, kn_ref, kp_ref, vt_ref, o_ref,
                      kext_ref, vx_ref, bias_ref, m_ref, acc_ref, s0, s1, p0, p1, a0, a1, *, tq, nq):
    @pl.when((pl.program_id(0) == 0) & (pl.program_id(1) == 0))
    def _():
        kc = lax.broadcasted_iota(jnp.int32, (tq, tq), 0) // CHUNK
        qc = lax.broadcasted_iota(jnp.int32, (tq, tq), 1) // CHUNK
        bias_ref[...] = jnp.where(kc <= qc, 0.0, NEG_BIG)

    kext_ref[:, 0:LANES] = kn_ref[...]
    kext_ref[:, LANES:2 * LANES] = kp_ref[...]
    vx_ref[:, 0:MLA_VDIM, :] = vt_ref[0]
    vx_ref[:, MLA_VDIM:, :] = jnp.ones((nq, vx_ref.shape[1] - MLA_VDIM, tq), BF16)
    m_ref[...] = jnp.full(m_ref.shape, NEG_BIG, F32)
    acc_ref[...] = jnp.zeros(acc_ref.shape, F32)
    nblk = nq * (nq + 1) // 2
    s_bufs, p_bufs, a_bufs = (s0, s1), (p0, p1), (a0, a1)

    def rows(i):
        return pl.ds(pl.multiple_of(i * tq, tq), tq)

    def scores(t, par):
        s_bufs[par][...] = _dot_nt(kext_ref[rows(kb_tab[t]), :], q_ref[rows(qi_tab[t]), :])

    def softmax(t, par, diagonal):
        qi = qi_tab[t]
        s = s_bufs[par][...]
        if diagonal:
            s = s + bias_ref[...]
        m_old = m_ref[qi]
        m_new = jnp.maximum(m_old, jnp.max(s, axis=0, keepdims=True))
        a_bufs[par][...] = jnp.exp2(m_old - m_new)
        m_ref[qi] = m_new
        p_bufs[par][...] = jnp.exp2(s - m_new).astype(BF16)

    def values(t, par):
        qi = qi_tab[t]
        acc_ref[qi] = a_bufs[par][...] * acc_ref[qi] + _dot(vx_ref[kb_tab[t]], p_bufs[par][...])

    def step(u, par, diagonal):
        scores(u, par)
        softmax(u - 1, 1 - par, diagonal)
        values(u - 2, par)

    def sweep(first, stop, diagonal):
        trips = (stop - first) // ATTN_UNROLL

        def trip(j, carry):
            for i in range(ATTN_UNROLL):
                step(first + ATTN_UNROLL * j + i, i % 2, diagonal)
            return carry

        lax.fori_loop(0, trips, trip, 0)
        for u in range(first + trips * ATTN_UNROLL, stop):
            step(u, u % 2, diagonal)

    scores(0, 0)
    scores(1, 1)
    softmax(0, 0, True)
    sweep(2, nq + 1, True)
    step(nq + 1, 1, False)
    sweep(nq + 2, nblk, False)
    softmax(nblk - 1, 1, False)
    values(nblk - 2, 0)
    values(nblk - 1, 1)
    for qi in range(nq):
        acc = acc_ref[qi]
        o_ref[qi * tq:(qi + 1) * tq, :] = jnp.transpose(
            acc[0:MLA_VDIM] / acc[MLA_VDIM:MLA_VDIM + 1]).astype(BF16)


def _attn_prompt(qext, kn, kp2, vt, batch, seq, tq):
    n = qext.shape[0]
    nq = seq // tq
    assert nq % 2 == 0
    pairs = [(i, i) for i in range(nq)] + [(qi, kb) for qi in range(nq) for kb in range(qi)]
    qi_tab = jnp.asarray([p[0] for p in pairs], jnp.int32)
    kb_tab = jnp.asarray([p[1] for p in pairs], jnp.int32)
    ones_rows = 16
    grid_spec = pltpu.PrefetchScalarGridSpec(
        num_scalar_prefetch=2,
        grid=(batch, MLA_HEADS),
        in_specs=[
            pl.BlockSpec((seq, 2 * LANES), lambda b, h, *_: (b, h)),
            pl.BlockSpec((seq, LANES), lambda b, h, *_: (b, h)),
            pl.BlockSpec((seq, LANES), lambda b, h, *_: (b, 0)),
            pl.BlockSpec((1, nq, MLA_VDIM, tq), lambda b, h, *_: (h, b, 0, 0)),
        ],
        out_specs=pl.BlockSpec((seq, MLA_VDIM), lambda b, h, *_: (b, h)),
        scratch_shapes=[
            pltpu.VMEM((seq, 2 * LANES), BF16),
            pltpu.VMEM((nq, MLA_VDIM + ones_rows, tq), BF16),
            pltpu.VMEM((tq, tq), F32),
            pltpu.VMEM((nq, 1, tq), F32),
            pltpu.VMEM((nq, MLA_VDIM + ones_rows, tq), F32),
            pltpu.VMEM((tq, tq), F32), pltpu.VMEM((tq, tq), F32),
            pltpu.VMEM((tq, tq), BF16), pltpu.VMEM((tq, tq), BF16),
            pltpu.VMEM((1, tq), F32), pltpu.VMEM((1, tq), F32),
        ],
    )
    return pl.pallas_call(
        functools.partial(_attn_prompt_body, tq=tq, nq=nq),
        grid_spec=grid_spec,
        out_shape=jax.ShapeDtypeStruct((n, MLA_HEADS * MLA_VDIM), BF16),
        compiler_params=_params(("arbitrary", "arbitrary"), 40),
        name="attn_prompt",
    )(qi_tab, kb_tab, qext, kn, kp2, vt)


def _attn_sample_body(q_ref, knp_ref, kpp_ref, vp_ref, knn_ref, kpn_ref, vn_ref, o_ref):
    qn = q_ref[:, 0:LANES]
    qp = q_ref[:, LANES:2 * LANES]
    s_past = _dot_nt(qn, knp_ref[...]) + _dot_nt(qp, kpp_ref[...])
    s_new = _dot_nt(qn, knn_ref[...]) + _dot_nt(qp, kpn_ref[...])
    m = jnp.maximum(jnp.max(s_past, axis=-1, keepdims=True), jnp.max(s_new, axis=-1, keepdims=True))
    p_past = jnp.exp2(s_past - m)
    p_new = jnp.exp2(s_new - m)
    l = jnp.sum(p_past, axis=-1, keepdims=True) + jnp.sum(p_new, axis=-1, keepdims=True)
    acc = _dot(p_past.astype(BF16), vp_ref[...]) + _dot(p_new.astype(BF16), vn_ref[...])
    o_ref[...] = (acc / l).astype(BF16)


def _attn_sample(qext, kn_past, kp2_past, v_past, kn_new, kp2_new, v_new, batch, past, seq):
    n = qext.shape[0]
    return pl.pallas_call(
        _attn_sample_body,
        grid=(batch, MLA_HEADS),
        in_specs=[
            pl.BlockSpec((seq, 2 * LANES), lambda b, h: (b, h)),
            pl.BlockSpec((past, LANES), lambda b, h: (b, h)),
            pl.BlockSpec((past, LANES), lambda b, h: (b, 0)),
            pl.BlockSpec((past, LANES), lambda b, h: (b, h)),
            pl.BlockSpec((seq, LANES), lambda b, h: (b, h)),
            pl.BlockSpec((seq, LANES), lambda b, h: (b, 0)),
            pl.BlockSpec((seq, LANES), lambda b, h: (b, h)),
        ],
        out_specs=pl.BlockSpec((seq, MLA_VDIM), lambda b, h: (b, h)),
        out_shape=jax.ShapeDtypeStruct((n, MLA_HEADS * MLA_VDIM), BF16),
        compiler_params=_params(("parallel", "parallel"), 32),
        name="attn_sample",
    )(qext, kn_past, kp2_past, v_past, kn_new, kp2_new, v_new)


def _merge_body(og_ref, at_ref, wg_ref, wm_ref, ga_ref, gb_ref, wo_ref, x_ref, g2_ref, x1_ref, h2_ref):
    j = pl.program_id(1)

    @pl.when(j == 0)
    def _():
        x1_ref[...] = x_ref[...]

    u = (_sigmoid(ga_ref[...].astype(F32)) * _dot(og_ref[...], wg_ref[...])
         + _sigmoid(gb_ref[...].astype(F32)) * _dot(at_ref[...], wm_ref[...]))
    x1_ref[...] += _dot(u.astype(BF16), wo_ref[...])

    @pl.when(j == pl.num_programs(1) - 1)
    def _():
        h2_ref[...] = (_row_rms(x1_ref[...]) * g2_ref[...]).astype(BF16)


def _merge(og, att, wg, wm, zm, wo, x, g2, tm):
    n = x.shape[0]
    tn = 512
    return pl.pallas_call(
        _merge_body,
        grid=(n // tm, D_MODEL // tn),
        in_specs=[
            pl.BlockSpec((tm, D_MODEL), lambda i, j: (i, 0)),
            pl.BlockSpec((tm, D_MODEL), lambda i, j: (i, 0)),
            pl.BlockSpec((D_MODEL, tn), lambda i, j: (0, j)),
            pl.BlockSpec((D_MODEL, tn), lambda i, j: (0, j)),
            pl.BlockSpec((tm, tn), lambda i, j: (i, COL_GA // tn + j)),
            pl.BlockSpec((tm, tn), lambda i, j: (i, COL_GB // tn + j)),
            pl.BlockSpec((tn, D_MODEL), lambda i, j: (j, 0)),
            pl.BlockSpec((tm, D_MODEL), lambda i, j: (i, 0)),
            pl.BlockSpec((1, D_MODEL), lambda i, j: (0, 0)),
        ],
        out_specs=[
            pl.BlockSpec((tm, D_MODEL), lambda i, j: (i, 0)),
            pl.BlockSpec((tm, D_MODEL), lambda i, j: (i, 0)),
        ],
        out_shape=[
            jax.ShapeDtypeStruct((n, D_MODEL), F32),
            jax.ShapeDtypeStruct((n, D_MODEL), BF16),
        ],
        compiler_params=_params(("parallel", "arbitrary"), 56),
        name="merge",
    )(og, att, wg, wm, zm, zm, wo, x, g2)


def _gelu_gate(a, a1, a2, cw_ref, gt):
    c = cw_ref[3:4, :] + cw_ref[2:3, :] * a + cw_ref[0:1, :] * a2 + cw_ref[1:2, :] * a1
    return 0.5 * c * (1.0 + lax.erf(c * (2.0 ** -0.5))) * gt


def _ffn_up_seq_body(h_ref, wa_ref, wg_ref, cw_ref, hist_ref, act_ref, tail_ref, carry_ref, *, tm, tiles_per_seq):
    i = pl.program_id(0)
    j = pl.program_id(1)
    a = _dot(h_ref[...], wa_ref[...])
    gt = _dot(h_ref[...], wg_ref[...])

    @pl.when(i % tiles_per_seq == 0)
    def _():
        carry_ref[j] = hist_ref[0]

    prev = carry_ref[j]
    row = lax.broadcasted_iota(jnp.int32, (tm, 1), 0)
    a1 = jnp.where(row == 0, prev[7:8, :], pltpu.roll(a, 1, axis=0))
    a2 = jnp.where(row == 0, prev[6:7, :], jnp.where(row == 1, prev[7:8, :], pltpu.roll(a, 2, axis=0)))
    act_ref[...] = _gelu_gate(a, a1, a2, cw_ref, gt).astype(BF16)
    carry_ref[j] = a[tm - 8:tm, :]
    tail_ref[0] = a[tm - (CONV_W - 1):tm, :]


def _ffn_up_seq(h2, wup, cw, hist8, batch, seq, tm):
    n = h2.shape[0]
    tn = 512
    nj = D_FF // tn
    tps = seq // tm
    return pl.pallas_call(
        functools.partial(_ffn_up_seq_body, tm=tm, tiles_per_seq=tps),
        grid=(n // tm, nj),
        in_specs=[
            pl.BlockSpec((tm, D_MODEL), lambda i, j: (i, 0)),
            pl.BlockSpec((D_MODEL, tn), lambda i, j: (0, j)),
            pl.BlockSpec((D_MODEL, tn), lambda i, j: (0, nj + j)),
            pl.BlockSpec((8, tn), lambda i, j: (0, j)),
            pl.BlockSpec((1, 8, tn), lambda i, j: (i // tps, 0, j)),
        ],
        out_specs=[
            pl.BlockSpec((tm, tn), lambda i, j: (i, j)),
            pl.BlockSpec((1, CONV_W - 1, tn), lambda i, j: (i, 0, j)),
        ],
        out_shape=[
            jax.ShapeDtypeStruct((n, D_FF), BF16),
            jax.ShapeDtypeStruct((n // tm, CONV_W - 1, D_FF), F32),
        ],
        scratch_shapes=[pltpu.VMEM((nj, 8, tn), F32)],
        compiler_params=_params(("arbitrary", "arbitrary"), 48),
        name="ffn_up_seq",
    )(h2, wup, wup, cw, hist8)


def _ffn_up_multi_body(h_ref, wa_ref, wg_ref, cw_ref, p1_ref, p2_ref, act_ref, a_ref, *, tm, seq):
    a = _dot(h_ref[...], wa_ref[...])
    gt = _dot(h_ref[...], wg_ref[...])
    pos = lax.broadcasted_iota(jnp.int32, (tm, 1), 0) % seq
    a1 = jnp.where(pos == 0, p1_ref[...], pltpu.roll(a, 1, axis=0))
    a2 = jnp.where(pos <= 1, p2_ref[...], pltpu.roll(a, 2, axis=0))
    act_ref[...] = _gelu_gate(a, a1, a2, cw_ref, gt).astype(BF16)
    a_ref[...] = a


def _ffn_up_multi(h2, wup, cw, p1, p2, seq):
    n = h2.shape[0]
    tn = 512
    nj = D_FF // tn
    return pl.pallas_call(
        functools.partial(_ffn_up_multi_body, tm=n, seq=seq),
        grid=(nj,),
        in_specs=[
            pl.BlockSpec((n, D_MODEL), lambda j: (0, 0)),
            pl.BlockSpec((D_MODEL, tn), lambda j: (0, j)),
            pl.BlockSpec((D_MODEL, tn), lambda j: (0, nj + j)),
            pl.BlockSpec((8, tn), lambda j: (0, j)),
            pl.BlockSpec((n, tn), lambda j: (0, j)),
            pl.BlockSpec((n, tn), lambda j: (0, j)),
        ],
        out_specs=[pl.BlockSpec((n, tn), lambda j: (0, j))] * 2,
        out_shape=[jax.ShapeDtypeStruct((n, D_FF), BF16), jax.ShapeDtypeStruct((n, D_FF), F32)],
        compiler_params=_params(("parallel",), 32),
        name="ffn_up_multi",
    )(h2, wup, wup, cw, p1, p2)


def _ffn_down_body(act_ref, wd_ref, x1_ref, o_ref):
    o_ref[...] = x1_ref[...] + _dot(act_ref[...], wd_ref[...])


def _ffn_down(act, wd, x1, tm):
    n = act.shape[0]
    tn = 512
    return pl.pallas_call(
        _ffn_down_body,
        grid=(n // tm, D_MODEL // tn),
        in_specs=[
            pl.BlockSpec((tm, D_FF), lambda i, j: (i, 0)),
            pl.BlockSpec((D_FF, tn), lambda i, j: (0, j)),
            pl.BlockSpec((tm, tn), lambda i, j: (i, j)),
        ],
        out_specs=pl.BlockSpec((tm, tn), lambda i, j: (i, j)),
        out_shape=jax.ShapeDtypeStruct((n, D_MODEL), F32),
        compiler_params=_params(("parallel", "parallel"), 48),
        name="ffn_down",
    )(act, wd, x1)


def _swap_halves(w):
    half = w.shape[-1] // 2
    return jnp.concatenate([w[..., half:], w[..., :half]], axis=-1)


def _rope_table(pos):
    half = MLA_ROPE // 2
    inv = ROPE_THETA ** (-jnp.arange(half, dtype=F32) * 2.0 / MLA_ROPE)
    ang = pos.astype(F32)[:, None] * inv[None, :]
    cos, sin = jnp.cos(ang), jnp.sin(ang)
    return jnp.concatenate([cos, cos, -sin, sin], axis=-1)


def _layer_weights(w_in, g_norm1, gla_w_gate2, gla_b_gate, gla_g_out, w_br_gla, mla_g_qlat, mla_w_uq,
                   mla_g_kvlat, mla_w_ukv, mla_g_q, mla_g_k, mla_g_qpe, mla_g_kpe, w_br_mla, w_out,
                   g_norm2, ffn_w_up, ffn_conv_w, ffn_conv_b, ffn_w_down):
    o_lr = 2 * 1024 + 2048
    o_r = o_lr + GLA_GATE_RANK
    o_kpe = o_r + 2048 + 2 * MLA_LORA
    o_ga = o_kpe + MLA_ROPE
    w_main = jnp.concatenate([w_in[:, :o_lr], w_in[:, o_r:o_kpe], w_in[:, o_ga:]], axis=1).astype(BF16)
    w_kpe = w_in[:, o_kpe:o_ga]
    w_small = jnp.concatenate(
        [w_kpe, _swap_halves(w_kpe), w_in[:, o_lr:o_r],
         jnp.zeros((D_MODEL, SMALL_COLS - SMALL_LR - GLA_GATE_RANK), F32)], axis=1).astype(BF16)
    w2ext = jnp.zeros((SMALL_COLS, GLA_HEADS * GLA_HK), F32).at[SMALL_LR:SMALL_LR + GLA_GATE_RANK].set(
        gla_w_gate2).astype(BF16)
    wq = mla_w_uq.reshape(MLA_LORA, MLA_HEADS, MLA_NOPE + MLA_ROPE)
    wqn = wq[:, :, :MLA_NOPE].reshape(MLA_LORA, MLA_HEADS * MLA_NOPE).astype(BF16)
    wq_pe = wq[:, :, MLA_NOPE:]
    wqp = jnp.concatenate([wq_pe, _swap_halves(wq_pe)], axis=-1).reshape(MLA_LORA, MLA_HEADS * LANES).astype(BF16)
    wkv = mla_w_ukv.reshape(MLA_LORA, MLA_HEADS, MLA_NOPE + MLA_VDIM)
    wk = wkv[:, :, :MLA_NOPE].reshape(MLA_LORA, MLA_HEADS * MLA_NOPE).astype(BF16)
    wv = wkv[:, :, MLA_NOPE:].reshape(MLA_LORA, MLA_HEADS * MLA_VDIM).astype(BF16)
    cw = jnp.concatenate([ffn_conv_w, ffn_conv_b[None, :], jnp.zeros((8 - CONV_W - 1, D_FF), F32)], axis=0)
    return dict(
        w_main=w_main, w_small=w_small, g1=g_norm1[None, :], w2ext=w2ext, bg=gla_b_gate[None, :],
        go=gla_g_out[None, :], wg=w_br_gla.astype(BF16), wqn=wqn, wqp=wqp, gql=mla_g_qlat[None, :],
        gkl=mla_g_kvlat[None, :], gq=mla_g_q[None, :],
        gqp=jnp.concatenate([mla_g_qpe, _swap_halves(mla_g_qpe)])[None, :],
        gkp=jnp.concatenate([mla_g_kpe, _swap_halves(mla_g_kpe)])[None, :],
        wk=wk, wv=wv, wvt=wv.T, gk=mla_g_k[None, :], wm=w_br_mla.astype(BF16), wo=w_out.astype(BF16),
        g2=g_norm2[None, :], wup=ffn_w_up.astype(BF16), cw=cw, wd=ffn_w_down.astype(BF16),
        p128=jnp.full((LANES, LANES), 1.0 / LANES, BF16),
    )


def _trunk_front(x, w, s0, batch, seq, chunk, gla_tb, tm, tab, tab_blocks, transposed_v):
    zm, zs = _in_proj(x, w["g1"], w["w_main"], w["w_small"], tm)
    og, s_new = _gla(zm, zs, w["w2ext"], w["bg"], w["go"], s0, batch, seq, chunk, gla_tb)
    qext, ckv, kpe, kp2 = _mla_proj(zm, zs, w["wqn"], w["wqp"], w["gql"], w["gkl"], w["gq"], w["gqp"],
                                    w["gkp"], tab, w["p128"], tm, tab_blocks)
    kn, v = _mla_expand(ckv, w["wk"], w["wvt"] if transposed_v else w["wv"], w["gk"], w["p128"], tm,
                        transposed_v)
    return zm, og, s_new, qext, ckv, kpe, kp2, kn, v


def kernel(x_prompt, x_sample, state_gla, cache_mla_ckv, cache_mla_kpe, cache_ffn_conv, w_in, g_norm1, gla_w_gate2, gla_b_gate, gla_g_out, w_br_gla, mla_g_qlat, mla_w_uq, mla_g_kvlat, mla_w_ukv, mla_g_q, mla_g_k, mla_g_qpe, mla_g_kpe, w_br_mla, w_out, g_norm2, ffn_w_up, ffn_conv_w, ffn_conv_b, ffn_w_down):
    bp, tp, _ = x_prompt.shape
    bs, ts, _ = x_sample.shape
    depth = w_in.shape[0]
    past = cache_mla_ckv.shape[2]
    np_rows, ns_rows = bp * tp, bs * ts
    tm_p = 512
    tab_p = _rope_table(jnp.arange(tp))
    tab_s = jnp.tile(_rope_table(past + jnp.arange(ts)), (bs, 1))
    xp = x_prompt.reshape(np_rows, D_MODEL)
    xs = x_sample.reshape(ns_rows, D_MODEL)
    outs = [[] for _ in range(8)]
    layer_weights = (w_in, g_norm1, gla_w_gate2, gla_b_gate, gla_g_out, w_br_gla, mla_g_qlat, mla_w_uq,
                     mla_g_kvlat, mla_w_ukv, mla_g_q, mla_g_k, mla_g_qpe, mla_g_kpe, w_br_mla, w_out,
                     g_norm2, ffn_w_up, ffn_conv_w, ffn_conv_b, ffn_w_down)
    for l in range(depth):
        w = _layer_weights(*[a[l] for a in layer_weights])

        s0 = jnp.zeros((bp, GLA_HEADS, GLA_HK, GLA_HV), F32)
        zm, og, sp, qext, ckv_p, kpe_p, kp2, kn, v = _trunk_front(
            xp, w, s0, bp, tp, CHUNK, 512, tm_p, tab_p, tp // tm_p, True)
        att = _attn_prompt(qext, kn, kp2, v, bp, tp, 512)
        x1, h2 = _merge(og, att, w["wg"], w["wm"], zm, w["wo"], xp, w["g2"], tm_p)
        hist8 = jnp.zeros((bp, 8, D_FF), F32)
        act, tails = _ffn_up_seq(h2, w["wup"], w["cw"], hist8, bp, tp, tm_p)
        fp = tails.reshape(bp, tp // tm_p, CONV_W - 1, D_FF)[:, -1]
        xp = _ffn_down(act, w["wd"], x1, tm_p)

        zm, og, ss, qext, ckv_s, kpe_s, kp2, kn, v = _trunk_front(
            xs, w, state_gla[l], bs, ts, ts, ts, ns_rows, tab_s, 1, False)
        kn_past, v_past = _mla_expand(cache_mla_ckv[l].reshape(bs * past, MLA_LORA), w["wk"], w["wv"],
                                      w["gk"], w["p128"], 512)
        kpe_past = cache_mla_kpe[l].reshape(bs * past, MLA_ROPE)
        kp2_past = jnp.concatenate([kpe_past, kpe_past], axis=-1).astype(BF16)
        att = _attn_sample(qext, kn_past, kp2_past, v_past, kn, kp2, v, bs, past, ts)
        x1, h2 = _merge(og, att, w["wg"], w["wm"], zm, w["wo"], xs, w["g2"], ns_rows)
        hist = cache_ffn_conv[l]
        zrow = jnp.zeros((bs, ts - 1, D_FF), F32)
        p1 = jnp.concatenate([hist[:, 1:2], zrow], axis=1).reshape(ns_rows, D_FF)
        p2 = jnp.concatenate([hist, zrow[:, 1:]], axis=1).reshape(ns_rows, D_FF)
        act, a_full = _ffn_up_multi(h2, w["wup"], w["cw"], p1, p2, ts)
        fs = a_full.reshape(bs, ts, D_FF)[:, ts - (CONV_W - 1):]
        xs = _ffn_down(act, w["wd"], x1, ns_rows)

        for lst, val in zip(outs, (sp, ss, ckv_p.reshape(bp, tp, MLA_LORA), ckv_s.reshape(bs, ts, MLA_LORA),
                                   kpe_p.reshape(bp, tp, MLA_ROPE), kpe_s.reshape(bs, ts, MLA_ROPE), fp, fs)):
            lst.append(val)
    return (xp.reshape(bp, tp, D_MODEL), xs.reshape(bs, ts, D_MODEL)) + tuple(jnp.stack(o, 0) for o in outs)
```

```python
import functools

import numpy as np
import jax
import jax.numpy as jnp
from jax import lax
from jax.experimental import pallas as pl
from jax.experimental.pallas import tpu as pltpu

F32 = jnp.float32
BF16 = jnp.bfloat16

D_MODEL = 2048
CHUNK = 64
EPS = 1e-6
GLA_HEADS = 4
GLA_HK = 256
GLA_HV = 512
GLA_GATE_RANK = 16
GLA_GATE_TAU = 16.0
MLA_HEADS = 16
MLA_LORA = 512
MLA_NOPE = 128
MLA_ROPE = 64
MLA_VDIM = 128
MLA_SCALE = (MLA_NOPE + MLA_ROPE) ** -0.5
LOG2E = float(np.log2(np.e))
Q_SCALE = MLA_SCALE * LOG2E
ROPE_THETA = 10000.0
D_FF = 5632
CONV_W = 3
LANES = 128
NEG_BIG = -1e30
ATTN_UNROLL = 4

COL_Q, COL_K, COL_V, COL_R, COL_MQ, COL_MKV, COL_GA, COL_GB, MAIN_COLS = (
    0, 1024, 2048, 4096, 6144, 6656, 7168, 9216, 11264)
SMALL_COLS = 256
SMALL_LR = 128

MIB = 1024 * 1024


def _params(semantics, vmem_mib):
    return pltpu.CompilerParams(dimension_semantics=semantics, vmem_limit_bytes=vmem_mib * MIB)


def _dot(a, b):
    return jnp.dot(a, b, preferred_element_type=F32)


def _dot_nt(a, b):
    return lax.dot_general(a, b, (((1,), (1,)), ((), ())), preferred_element_type=F32)


def _dot_tn(a, b):
    return lax.dot_general(a, b, (((0,), (0,)), ((), ())), preferred_element_type=F32)


def _sigmoid(x):
    return 1.0 / (1.0 + jnp.exp(-x))


def _row_rms(x):
    return x * lax.rsqrt(jnp.mean(x * x, axis=-1, keepdims=True) + EPS)


def _slab_rms(x, p_ref):
    ms = _dot((x * x).astype(BF16), p_ref[...])
    return x * lax.rsqrt(ms + EPS)


W_IN_SEGMENTS = ((0, COL_R, 0), (COL_R, COL_GA, GLA_GATE_RANK), (COL_GA, MAIN_COLS, GLA_GATE_RANK + MLA_ROPE))


def _regroup_body(main_ref, next_ref, o_ref, *, tn):
    j = pl.program_id(0)
    for lo, hi, shift in W_IN_SEGMENTS:
        @pl.when((j >= lo // tn) & (j < hi // tn))
        def _(shift=shift):
            if shift == 0:
                o_ref[...] = main_ref[...].astype(BF16)
            else:
                both = jnp.concatenate([main_ref[...], next_ref[...]], axis=1)
                o_ref[...] = both[:, shift:shift + tn].astype(BF16)


def _regroup_w_in(w_in):
    tn, tr = 512, 512
    assert all(lo % tn == 0 and hi % tn == 0 and shift < LANES for lo, hi, shift in W_IN_SEGMENTS)
    return pl.pallas_call(
        functools.partial(_regroup_body, tn=tn),
        grid=(MAIN_COLS // tn, D_MODEL // tr),
        in_specs=[
            pl.BlockSpec((tr, tn), lambda j, r: (r, j)),
            pl.BlockSpec((tr, LANES), lambda j, r: (r, (j + 1) * (tn // LANES))),
        ],
        out_specs=pl.BlockSpec((tr, tn), lambda j, r: (r, j)),
        out_shape=jax.ShapeDtypeStruct((D_MODEL, MAIN_COLS), BF16),
        compiler_params=_params(("parallel", "parallel"), 32),
        name="regroup_w_in",
    )(w_in, w_in)


def _in_proj_body(x_ref, g_ref, wm_ref, ws_ref, zm_ref, zs_ref, h_ref):
    @pl.when(pl.program_id(1) == 0)
    def _():
        h_ref[...] = (_row_rms(x_ref[...]) * g_ref[...]).astype(BF16)
        zs_ref[...] = _dot(h_ref[...], ws_ref[...])

    zm_ref[...] = _dot(h_ref[...], wm_ref[...]).astype(BF16)


def _in_proj(x, g, wm, ws, tm):
    n = x.shape[0]
    tn = 1024
    return pl.pallas_call(
        _in_proj_body,
        grid=(n // tm, MAIN_COLS // tn),
        in_specs=[
            pl.BlockSpec((tm, D_MODEL), lambda i, j: (i, 0)),
            pl.BlockSpec((1, D_MODEL), lambda i, j: (0, 0)),
            pl.BlockSpec((D_MODEL, tn), lambda i, j: (0, j)),
            pl.BlockSpec((D_MODEL, SMALL_COLS), lambda i, j: (0, 0)),
        ],
        out_specs=[
            pl.BlockSpec((tm, tn), lambda i, j: (i, j)),
            pl.BlockSpec((tm, SMALL_COLS), lambda i, j: (i, 0)),
        ],
        out_shape=[
            jax.ShapeDtypeStruct((n, MAIN_COLS), BF16),
            jax.ShapeDtypeStruct((n, SMALL_COLS), F32),
        ],
        scratch_shapes=[pltpu.VMEM((tm, D_MODEL), BF16)],
        compiler_params=_params(("parallel", "arbitrary"), 56),
        name="in_proj",
    )(x, g, wm, ws)


def _gla_tables(c):
    levels = int(np.log2(c))
    t = np.arange(c)[:, None]
    u = np.arange(c)[None, :]
    masks = [(u == t)]
    level2 = None
    for l in range(levels):
        m = c >> (l + 1)
        mid_t = (t // (2 * m)) * 2 * m + m
        upper = t >= mid_t
        if m == 2:
            level2 = np.where(upper, (u >= mid_t) & (u <= t), (u > t) & (u < mid_t))
        mid_u = (u // (2 * m)) * 2 * m + m
        masks.append((t // (2 * m) == u // (2 * m)) & upper & (u < mid_u))
    eye_h = np.eye(GLA_HEADS, dtype=np.float32)
    gmat = np.concatenate([np.kron(eye_h, (u <= t).astype(np.float32)),
                           np.kron(eye_h, level2.astype(np.float32))], axis=0)
    group = _gla_group_heads(c)
    masks = np.stack([np.kron(np.eye(group, dtype=np.float32), mk.astype(np.float32)) for mk in masks])
    return jnp.asarray(gmat, BF16), jnp.asarray(masks), levels


def _gla_group_heads(c):
    return min(GLA_HEADS, max(1, LANES // c))


def _gla_level_exponent(b, log_a, level2, m, c):
    if m == 1:
        row = lax.broadcasted_iota(jnp.int32, (c, 1), 0)
        return jnp.where(row % 2 == 1, log_a, 0.0)
    if m == 2:
        return level2
    parts = [jnp.broadcast_to(b[i + m - 1:i + m, :], (2 * m, GLA_HK)) for i in range(0, c, 2 * m)]
    ref = parts[0] if len(parts) == 1 else jnp.concatenate(parts, axis=0)
    return -jnp.abs(b - ref)


def _gla_body(q_ref, k_ref, v_ref, r_ref, zs_ref, w2_ref, bg_ref, go_ref, s0_ref, gmat_ref, mask_ref,
              og_ref, sout_ref, st_ref, *, c, nchunk, levels):
    t = pl.program_id(1)

    @pl.when(t == 0)
    def _():
        for h in range(GLA_HEADS):
            st_ref[h] = jnp.transpose(s0_ref[0, h])

    nrow = GLA_HEADS * c
    gw = _gla_group_heads(c) * c

    def stack(x, width):
        return jnp.concatenate([x[:, h * width:(h + 1) * width] for h in range(GLA_HEADS)], axis=0)

    def chunk(ci, carry):
        rows = pl.ds(pl.multiple_of(ci * c, c), c)
        q = stack(q_ref[rows, :], GLA_HK).astype(F32) * (GLA_HK ** -0.5)
        k = stack(k_ref[rows, :], GLA_HK).astype(F32)
        v = stack(v_ref[rows, :], GLA_HV)
        x = stack(_dot(zs_ref[rows, :].astype(BF16), w2_ref[...]) + bg_ref[...], GLA_HK) * LOG2E
        log_a = (jnp.minimum(x, 0.0) - jnp.log2(1.0 + jnp.exp2(-jnp.abs(x)))) * (1.0 / GLA_GATE_TAU)
        hi = log_a.astype(BF16)
        lo = (log_a - hi.astype(F32)).astype(BF16)
        gm = gmat_ref[...]
        pre = _dot(gm, hi) + _dot(gm, lo)
        b = pre[0:nrow]
        b_last = [b[h * c + c - 1:h * c + c, :] for h in range(GLA_HEADS)]
        q_in = (q * jnp.exp2(b)).astype(BF16)
        k_out = (k * jnp.exp2(jnp.concatenate([jnp.broadcast_to(r, (c, GLA_HK)) for r in b_last], axis=0)
                              - b)).astype(BF16)

        def diag_blocks(prod):
            return [prod[g:g + gw, g:g + gw] for g in range(0, nrow, gw)]

        att = [mask_ref[0] * blk for blk in diag_blocks(_dot_nt(q.astype(BF16), k.astype(BF16)))]
        for l in range(levels):
            d = jnp.exp2(_gla_level_exponent(b, log_a, pre[nrow:2 * nrow], c >> (l + 1), nrow))
            prod = _dot_nt((q * d).astype(BF16), (k * d).astype(BF16))
            att = [a + mask_ref[1 + l] * blk for a, blk in zip(att, diag_blocks(prod))]
        if len(att) == 1:
            att_full = att[0].astype(BF16)
        else:
            zero = jnp.zeros((gw, gw), BF16)
            att_full = jnp.concatenate(
                [jnp.concatenate([a.astype(BF16) if i == j else zero for j in range(len(att))], axis=1)
                 for i, a in enumerate(att)], axis=0)
        o_intra = _dot(att_full, v)
        for h in range(GLA_HEADS):
            hr = slice(h * c, (h + 1) * c)
            cv = slice(h * GLA_HV, (h + 1) * GLA_HV)
            st = st_ref[h]
            o = o_intra[hr] + _dot_nt(q_in[hr], st.astype(BF16))
            st_ref[h] = st * jnp.exp2(b_last[h]) + _dot_tn(v[hr], k_out[hr])
            gate = r_ref[rows, cv].astype(F32)
            og = _row_rms(o) * go_ref[...] * (gate * _sigmoid(gate))
            og_ref[rows, cv] = og.astype(BF16)
        return carry

    lax.fori_loop(0, nchunk, chunk, 0)

    @pl.when(t == pl.num_programs(1) - 1)
    def _():
        for h in range(GLA_HEADS):
            sout_ref[0, h] = jnp.transpose(st_ref[h])


def _gla(zm, zs, w2ext, bg, go, s0, batch, seq, c, tb):
    n = zm.shape[0]
    nt = seq // tb
    gmat, masks, levels = _gla_tables(c)
    dk, dv = GLA_HEADS * GLA_HK, GLA_HEADS * GLA_HV
    const2 = lambda b, t: (0, 0)
    return pl.pallas_call(
        functools.partial(_gla_body, c=c, nchunk=tb // c, levels=levels),
        grid=(batch, nt),
        in_specs=[
            pl.BlockSpec((tb, dk), lambda b, t: (b * nt + t, COL_Q // dk)),
            pl.BlockSpec((tb, dk), lambda b, t: (b * nt + t, COL_K // dk)),
            pl.BlockSpec((tb, dv), lambda b, t: (b * nt + t, COL_V // dv)),
            pl.BlockSpec((tb, dv), lambda b, t: (b * nt + t, COL_R // dv)),
            pl.BlockSpec((tb, SMALL_COLS), lambda b, t: (b * nt + t, 0)),
            pl.BlockSpec((SMALL_COLS, dk), const2),
            pl.BlockSpec((1, dk), const2),
            pl.BlockSpec((1, GLA_HV), const2),
            pl.BlockSpec((1, GLA_HEADS, GLA_HK, GLA_HV), lambda b, t: (b, 0, 0, 0)),
            pl.BlockSpec(gmat.shape, const2),
            pl.BlockSpec(masks.shape, lambda b, t: (0, 0, 0)),
        ],
        out_specs=[
            pl.BlockSpec((tb, dv), lambda b, t: (b * nt + t, 0)),
            pl.BlockSpec((1, GLA_HEADS, GLA_HK, GLA_HV), lambda b, t: (b, 0, 0, 0)),
        ],
        out_shape=[
            jax.ShapeDtypeStruct((n, dv), BF16),
            jax.ShapeDtypeStruct((batch, GLA_HEADS, GLA_HK, GLA_HV), F32),
        ],
        scratch_shapes=[pltpu.VMEM((GLA_HEADS, GLA_HV, GLA_HK), F32)],
        compiler_params=_params(("parallel", "arbitrary"), 40),
        name="gla",
    )(zm, zm, zm, zm, zs, w2ext, bg, go, s0, gmat, masks)


def _mla_proj_body(mq_ref, mkv_ref, zs_ref, wqn_ref, wqp_ref, gql_ref, gkl_ref, gq_ref, gqp_ref, gkp_ref,
                   tab_ref, p_ref, q_ref, ckv_ref, kpe_ref, kp2_ref):
    qlat = (_row_rms(mq_ref[...].astype(F32)) * gql_ref[...]).astype(BF16)
    qn = _dot(qlat, wqn_ref[...])
    qp = _dot(qlat, wqp_ref[...])
    tab = tab_ref[...]
    tab2 = jnp.concatenate([tab, tab], axis=1)
    for lo in range(0, MLA_HEADS * LANES, 2 * LANES):
        cols = slice(lo, lo + 2 * LANES)
        nope = (_slab_rms(qn[:, cols], p_ref) * (gq_ref[...] * Q_SCALE)).astype(BF16)
        pe = (_slab_rms(qp[:, cols], p_ref) * (gqp_ref[...] * Q_SCALE) * tab2).astype(BF16)
        for i in range(2):
            dst = 2 * lo + i * 2 * LANES
            q_ref[:, dst:dst + LANES] = nope[:, i * LANES:(i + 1) * LANES]
            q_ref[:, dst + LANES:dst + 2 * LANES] = pe[:, i * LANES:(i + 1) * LANES]
    ckv_ref[...] = _row_rms(mkv_ref[...].astype(F32)) * gkl_ref[...]
    slab = zs_ref[:, 0:LANES]
    rot = _row_rms(slab) * gkp_ref[...] * tab
    kp2 = rot + pltpu.roll(rot, MLA_ROPE, axis=1)
    kpe_ref[...] = kp2[:, 0:MLA_ROPE]
    kp2_ref[...] = kp2.astype(BF16)


def _mla_proj(zm, zs, wqn, wqp, gql, gkl, gq, gqp, gkp, tab, p128, tm, tab_blocks):
    n = zm.shape[0]
    const = lambda i: (0, 0)
    return pl.pallas_call(
        _mla_proj_body,
        grid=(n // tm,),
        in_specs=[
            pl.BlockSpec((tm, MLA_LORA), lambda i: (i, COL_MQ // MLA_LORA)),
            pl.BlockSpec((tm, MLA_LORA), lambda i: (i, COL_MKV // MLA_LORA)),
            pl.BlockSpec((tm, SMALL_COLS), lambda i: (i, 0)),
            pl.BlockSpec(wqn.shape, const),
            pl.BlockSpec(wqp.shape, const),
            pl.BlockSpec((1, MLA_LORA), const),
            pl.BlockSpec((1, MLA_LORA), const),
            pl.BlockSpec((1, 2 * LANES), const),
            pl.BlockSpec((1, 2 * LANES), const),
            pl.BlockSpec((1, LANES), const),
            pl.BlockSpec((tm, LANES), lambda i: (i % tab_blocks, 0)),
            pl.BlockSpec((2 * LANES, 2 * LANES), const),
        ],
        out_specs=[
            pl.BlockSpec((tm, 2 * LANES * MLA_HEADS), lambda i: (i, 0)),
            pl.BlockSpec((tm, MLA_LORA), lambda i: (i, 0)),
            pl.BlockSpec((tm, MLA_ROPE), lambda i: (i, 0)),
            pl.BlockSpec((tm, LANES), lambda i: (i, 0)),
        ],
        out_shape=[
            jax.ShapeDtypeStruct((n, 2 * LANES * MLA_HEADS), BF16),
            jax.ShapeDtypeStruct((n, MLA_LORA), F32),
            jax.ShapeDtypeStruct((n, MLA_ROPE), F32),
            jax.ShapeDtypeStruct((n, LANES), BF16),
        ],
        compiler_params=_params(("parallel",), 48),
        name="mla_proj",
    )(zm, zm, zs, wqn, wqp, gql, gkl, gq, gqp, gkp, tab, p128)


def _mla_expand_keys(c_ref, wk_ref, gk_ref, p_ref, kn_ref):
    cb = c_ref[...].astype(BF16)
    kn = _dot(cb, wk_ref[...])
    for lo in range(0, MLA_HEADS * LANES, 2 * LANES):
        cols = slice(lo, lo + 2 * LANES)
        kn_ref[:, cols] = (_slab_rms(kn[:, cols], p_ref) * gk_ref[...]).astype(BF16)
    return cb


def _mla_expand_k_body(c_ref, wk_ref, gk_ref, p_ref, kn_ref):
    _mla_expand_keys(c_ref, wk_ref, gk_ref, p_ref, kn_ref)


def _mla_expand_kv_body(c_ref, wk_ref, gk_ref, p_ref, wvt_ref, kn_ref, vt_ref):
    cb = _mla_expand_keys(c_ref, wk_ref, gk_ref, p_ref, kn_ref)
    vt = _dot_nt(wvt_ref[...], cb).astype(BF16)
    for h in range(MLA_HEADS):
        vt_ref[h, 0] = vt[h * MLA_VDIM:(h + 1) * MLA_VDIM, :]


def _mla_expand(ckv, wk, gk, p128, tm, wvt=None):
    n = ckv.shape[0]
    const = lambda i: (0, 0)
    width = MLA_HEADS * LANES
    in_specs = [
        pl.BlockSpec((tm, MLA_LORA), lambda i: (i, 0)),
        pl.BlockSpec(wk.shape, const),
        pl.BlockSpec((1, 2 * LANES), const),
        pl.BlockSpec((2 * LANES, 2 * LANES), const),
    ]
    out_specs = [pl.BlockSpec((tm, width), lambda i: (i, 0))]
    out_shape = [jax.ShapeDtypeStruct((n, width), BF16)]
    args = [ckv, wk, gk, p128]
    body = _mla_expand_k_body
    if wvt is not None:
        body = _mla_expand_kv_body
        in_specs.append(pl.BlockSpec(wvt.shape, const))
        out_specs.append(pl.BlockSpec((MLA_HEADS, 1, MLA_VDIM, tm), lambda i: (0, i, 0, 0)))
        out_shape.append(jax.ShapeDtypeStruct((MLA_HEADS, n // tm, MLA_VDIM, tm), BF16))
        args.append(wvt)
    return pl.pallas_call(
        body,
        grid=(n // tm,),
        in_specs=in_specs,
        out_specs=out_specs,
        out_shape=out_shape,
        compiler_params=_params(("parallel",), 40),
        name="mla_expand",
    )(*args)


def _attn_prompt_body(qi_tab, kb_tab, q_ref, kn_ref, kp_ref, vt_ref, o_ref,
                      kext_ref, vx_ref, bias_ref, m_ref, acc_ref, s0, s1, p0, p1, a0, a1, *, tq, nq):
    @pl.when((pl.program_id(0) == 0) & (pl.program_id(1) == 0))
    def _():
        kc = lax.broadcasted_iota(jnp.int32, (tq, tq), 0) // CHUNK
        qc = lax.broadcasted_iota(jnp.int32, (tq, tq), 1) // CHUNK
        bias_ref[...] = jnp.where(kc <= qc, 0.0, NEG_BIG)

    kext_ref[:, 0:LANES] = kn_ref[...]
    kext_ref[:, LANES:2 * LANES] = kp_ref[...]
    vx_ref[:, 0:MLA_VDIM, :] = vt_ref[0]
    vx_ref[:, MLA_VDIM:, :] = jnp.ones((nq, vx_ref.shape[1] - MLA_VDIM, tq), BF16)
    m_ref[...] = jnp.full(m_ref.shape, NEG_BIG, F32)
    acc_ref[...] = jnp.zeros(acc_ref.shape, F32)
    nblk = nq * (nq + 1) // 2
    s_bufs, p_bufs, a_bufs = (s0, s1), (p0, p1), (a0, a1)

    def rows(i):
        return pl.ds(pl.multiple_of(i * tq, tq), tq)

    def scores(t, par):
        s_bufs[par][...] = _dot_nt(kext_ref[rows(kb_tab[t]), :], q_ref[rows(qi_tab[t]), :])

    def softmax(t, par, diagonal):
        qi = qi_tab[t]
        s = s_bufs[par][...]
        if diagonal:
            s = s + bias_ref[...]
        m_old = m_ref[qi]
        m_new = jnp.maximum(m_old, jnp.max(s, axis=0, keepdims=True))
        a_bufs[par][...] = jnp.exp2(m_old - m_new)
        m_ref[qi] = m_new
        p_bufs[par][...] = jnp.exp2(s - m_new).astype(BF16)

    def values(t, par):
        qi = qi_tab[t]
        acc_ref[qi] = a_bufs[par][...] * acc_ref[qi] + _dot(vx_ref[kb_tab[t]], p_bufs[par][...])

    def step(u, par, diagonal):
        scores(u, par)
        softmax(u - 1, 1 - par, diagonal)
        values(u - 2, par)

    def sweep(first, stop, diagonal):
        trips = (stop - first) // ATTN_UNROLL

        def trip(j, carry):
            for i in range(ATTN_UNROLL):
                step(first + ATTN_UNROLL * j + i, i % 2, diagonal)
            return carry

        lax.fori_loop(0, trips, trip, 0)
        for u in range(first + trips * ATTN_UNROLL, stop):
            step(u, u % 2, diagonal)

    scores(0, 0)
    scores(1, 1)
    softmax(0, 0, True)
    sweep(2, nq + 1, True)
    step(nq + 1, 1, False)
    sweep(nq + 2, nblk, False)
    softmax(nblk - 1, 1, False)
    values(nblk - 2, 0)
    values(nblk - 1, 1)
    for qi in range(nq):
        acc = acc_ref[qi]
        o_ref[qi * tq:(qi + 1) * tq, :] = jnp.transpose(
            acc[0:MLA_VDIM] / acc[MLA_VDIM:MLA_VDIM + 1]).astype(BF16)


def _attn_prompt(qext, kn, kp2, vt, batch, seq, tq):
    n = qext.shape[0]
    nq = seq // tq
    assert nq % 2 == 0
    pairs = [(i, i) for i in range(nq)] + [(qi, kb) for qi in range(nq) for kb in range(qi)]
    qi_tab = jnp.asarray([p[0] for p in pairs], jnp.int32)
    kb_tab = jnp.asarray([p[1] for p in pairs], jnp.int32)
    ones_rows = 16
    grid_spec = pltpu.PrefetchScalarGridSpec(
        num_scalar_prefetch=2,
        grid=(batch, MLA_HEADS),
        in_specs=[
            pl.BlockSpec((seq, 2 * LANES), lambda b, h, *_: (b, h)),
            pl.BlockSpec((seq, LANES), lambda b, h, *_: (b, h)),
            pl.BlockSpec((seq, LANES), lambda b, h, *_: (b, 0)),
            pl.BlockSpec((1, nq, MLA_VDIM, tq), lambda b, h, *_: (h, b, 0, 0)),
        ],
        out_specs=pl.BlockSpec((seq, MLA_VDIM), lambda b, h, *_: (b, h)),
        scratch_shapes=[
            pltpu.VMEM((seq, 2 * LANES), BF16),
            pltpu.VMEM((nq, MLA_VDIM + ones_rows, tq), BF16),
            pltpu.VMEM((tq, tq), F32),
            pltpu.VMEM((nq, 1, tq), F32),
            pltpu.VMEM((nq, MLA_VDIM + ones_rows, tq), F32),
            pltpu.VMEM((tq, tq), F32), pltpu.VMEM((tq, tq), F32),
            pltpu.VMEM((tq, tq), BF16), pltpu.VMEM((tq, tq), BF16),
            pltpu.VMEM((1, tq), F32), pltpu.VMEM((1, tq), F32),
        ],
    )
    return pl.pallas_call(
        functools.partial(_attn_prompt_body, tq=tq, nq=nq),
        grid_spec=grid_spec,
        out_shape=jax.ShapeDtypeStruct((n, MLA_HEADS * MLA_VDIM), BF16),
        compiler_params=_params(("arbitrary", "arbitrary"), 40),
        name="attn_prompt",
    )(qi_tab, kb_tab, qext, kn, kp2, vt)


def _attn_sample_body(q_ref, hm_ref, knp_ref, kpp_ref, cp_ref, knn_ref, kpn_ref, cn_ref, wv_ref, o_ref, *, seq):
    q = q_ref[...]
    qn = jnp.concatenate([q[:, 2 * h * LANES:(2 * h + 1) * LANES] for h in range(MLA_HEADS)], axis=1)
    qbd = jnp.concatenate([qn] * MLA_HEADS, axis=0) * hm_ref[...]
    qpe = jnp.concatenate([q[:, (2 * h + 1) * LANES:(2 * h + 2) * LANES] for h in range(MLA_HEADS)], axis=0)
    s_past = _dot_nt(knp_ref[...], qbd) + _dot_nt(kpp_ref[...], qpe)
    s_new = _dot_nt(knn_ref[...], qbd) + _dot_nt(kpn_ref[...], qpe)
    m = jnp.maximum(jnp.max(s_past, axis=0, keepdims=True), jnp.max(s_new, axis=0, keepdims=True))
    p_past = jnp.exp2(s_past - m).astype(BF16)
    p_new = jnp.exp2(s_new - m).astype(BF16)

    def with_ones(c_ref):
        c = c_ref[...].astype(BF16)
        return jnp.concatenate([c, jnp.ones((c.shape[0], LANES), BF16)], axis=1)

    acc = _dot_tn(p_past, with_ones(cp_ref)) + _dot_tn(p_new, with_ones(cn_ref))
    lat = (acc[:, 0:MLA_LORA] / acc[:, MLA_LORA:MLA_LORA + 1]).astype(BF16)
    for h in range(MLA_HEADS):
        cols = slice(h * MLA_VDIM, (h + 1) * MLA_VDIM)
        o_ref[:, cols] = _dot(lat[h * seq:(h + 1) * seq], wv_ref[:, cols]).astype(BF16)


def _attn_sample(qext, kn_past, kp2_past, ckv_past, kn_new, kp2_new, ckv_new, wv, batch, past, seq):
    n = qext.shape[0]
    width = MLA_HEADS * LANES
    head_mask = jnp.asarray(np.kron(np.eye(MLA_HEADS, dtype=np.float32), np.ones((seq, LANES), np.float32)), BF16)
    stream = lambda b: (b, 0)
    const = lambda b: (0, 0)
    return pl.pallas_call(
        functools.partial(_attn_sample_body, seq=seq),
        grid=(batch,),
        in_specs=[
            pl.BlockSpec((seq, 2 * width), stream),
            pl.BlockSpec(head_mask.shape, const),
            pl.BlockSpec((past, width), stream),
            pl.BlockSpec((past, LANES), stream),
            pl.BlockSpec((past, MLA_LORA), stream),
            pl.BlockSpec((seq, width), stream),
            pl.BlockSpec((seq, LANES), stream),
            pl.BlockSpec((seq, MLA_LORA), stream),
            pl.BlockSpec(wv.shape, const),
        ],
        out_specs=pl.BlockSpec((seq, MLA_HEADS * MLA_VDIM), stream),
        out_shape=jax.ShapeDtypeStruct((n, MLA_HEADS * MLA_VDIM), BF16),
        compiler_params=_params(("parallel",), 48),
        name="attn_sample",
    )(qext, head_mask, kn_past, kp2_past, ckv_past, kn_new, kp2_new, ckv_new, wv)


MERGE_COL_CHUNK = 256


def _merge_body(og_ref, at_ref, wg_ref, wm_ref, ga_ref, gb_ref, wo_ref, x_ref, g2_ref, x1_ref, h2_ref):
    j = pl.program_id(1)

    @pl.when(j == 0)
    def _():
        x1_ref[...] = x_ref[...]

    og = og_ref[...]
    at = at_ref[...]
    upd = None
    for lo in range(0, ga_ref.shape[1], MERGE_COL_CHUNK):
        cols = slice(lo, lo + MERGE_COL_CHUNK)
        u = (_sigmoid(ga_ref[:, cols].astype(F32)) * _dot(og, wg_ref[:, cols])
             + _sigmoid(gb_ref[:, cols].astype(F32)) * _dot(at, wm_ref[:, cols]))
        part = _dot(u.astype(BF16), wo_ref[cols, :])
        upd = part if upd is None else upd + part
    x1_ref[...] += upd

    @pl.when(j == pl.num_programs(1) - 1)
    def _():
        h2_ref[...] = (_row_rms(x1_ref[...]) * g2_ref[...]).astype(BF16)


def _merge(og, att, wg, wm, zm, wo, x, g2, tm):
    n = x.shape[0]
    tn = 512
    return pl.pallas_call(
        _merge_body,
        grid=(n // tm, D_MODEL // tn),
        in_specs=[
            pl.BlockSpec((tm, D_MODEL), lambda i, j: (i, 0)),
            pl.BlockSpec((tm, D_MODEL), lambda i, j: (i, 0)),
            pl.BlockSpec((D_MODEL, tn), lambda i, j: (0, j)),
            pl.BlockSpec((D_MODEL, tn), lambda i, j: (0, j)),
            pl.BlockSpec((tm, tn), lambda i, j: (i, COL_GA // tn + j)),
            pl.BlockSpec((tm, tn), lambda i, j: (i, COL_GB // tn + j)),
            pl.BlockSpec((tn, D_MODEL), lambda i, j: (j, 0)),
            pl.BlockSpec((tm, D_MODEL), lambda i, j: (i, 0)),
            pl.BlockSpec((1, D_MODEL), lambda i, j: (0, 0)),
        ],
        out_specs=[
            pl.BlockSpec((tm, D_MODEL), lambda i, j: (i, 0)),
            pl.BlockSpec((tm, D_MODEL), lambda i, j: (i, 0)),
        ],
        out_shape=[
            jax.ShapeDtypeStruct((n, D_MODEL), F32),
            jax.ShapeDtypeStruct((n, D_MODEL), BF16),
        ],
        compiler_params=_params(("parallel", "arbitrary"), 56),
        name="merge",
    )(og, att, wg, wm, zm, zm, wo, x, g2)


FFN_COL_CHUNK = 256


def _gelu_gate(a, a1, a2, cw, gt):
    c = cw[3:4, :] + cw[2:3, :] * a + cw[0:1, :] * a2 + cw[1:2, :] * a1
    return 0.5 * c * (1.0 + lax.erf(c * (2.0 ** -0.5))) * gt


def _ffn_up_seq_body(h_ref, wa_ref, wg_ref, cw_ref, hist_ref, act_ref, tail_ref, carry_ref, *, tm, tiles_per_seq):
    i = pl.program_id(0)
    j = pl.program_id(1)

    @pl.when(i % tiles_per_seq == 0)
    def _():
        carry_ref[j] = hist_ref[0]

    row = lax.broadcasted_iota(jnp.int32, (tm, 1), 0)
    h = h_ref[...]
    for lo in range(0, act_ref.shape[1], FFN_COL_CHUNK):
        cols = slice(lo, lo + FFN_COL_CHUNK)
        a = _dot(h, wa_ref[:, cols])
        gt = _dot(h, wg_ref[:, cols])
        prev = carry_ref[j, :, cols]
        a1 = jnp.where(row == 0, prev[7:8, :], pltpu.roll(a, 1, axis=0))
        a2 = jnp.where(row == 0, prev[6:7, :], jnp.where(row == 1, prev[7:8, :], pltpu.roll(a, 2, axis=0)))
        act_ref[:, cols] = _gelu_gate(a, a1, a2, cw_ref[:, cols], gt).astype(BF16)
        carry_ref[j, :, cols] = a[tm - 8:tm, :]
        tail_ref[0, :, cols] = a[tm - (CONV_W - 1):tm, :]


def _ffn_up_seq(h2, wup, cw, hist8, batch, seq, tm):
    n = h2.shape[0]
    tn = 512
    nj = D_FF // tn
    tps = seq // tm
    return pl.pallas_call(
        functools.partial(_ffn_up_seq_body, tm=tm, tiles_per_seq=tps),
        grid=(n // tm, nj),
        in_specs=[
            pl.BlockSpec((tm, D_MODEL), lambda i, j: (i, 0)),
            pl.BlockSpec((D_MODEL, tn), lambda i, j: (0, j)),
            pl.BlockSpec((D_MODEL, tn), lambda i, j: (0, nj + j)),
            pl.BlockSpec((8, tn), lambda i, j: (0, j)),
            pl.BlockSpec((1, 8, tn), lambda i, j: (i // tps, 0, j)),
        ],
        out_specs=[
            pl.BlockSpec((tm, tn), lambda i, j: (i, j)),
            pl.BlockSpec((1, CONV_W - 1, tn), lambda i, j: (i, 0, j)),
        ],
        out_shape=[
            jax.ShapeDtypeStruct((n, D_FF), BF16),
            jax.ShapeDtypeStruct((n // tm, CONV_W - 1, D_FF), F32),
        ],
        scratch_shapes=[pltpu.VMEM((nj, 8, tn), F32)],
        compiler_params=_params(("arbitrary", "arbitrary"), 48),
        name="ffn_up_seq",
    )(h2, wup, wup, cw, hist8)


def _ffn_up_multi_body(h_ref, wa_ref, wg_ref, cw_ref, p1_ref, p2_ref, act_ref, a_ref, *, tm, seq):
    a = _dot(h_ref[...], wa_ref[...])
    gt = _dot(h_ref[...], wg_ref[...])
    pos = lax.broadcasted_iota(jnp.int32, (tm, 1), 0) % seq
    a1 = jnp.where(pos == 0, p1_ref[...], pltpu.roll(a, 1, axis=0))
    a2 = jnp.where(pos <= 1, p2_ref[...], pltpu.roll(a, 2, axis=0))
    act_ref[...] = _gelu_gate(a, a1, a2, cw_ref[...], gt).astype(BF16)
    a_ref[...] = a


def _ffn_up_multi(h2, wup, cw, p1, p2, seq):
    n = h2.shape[0]
    tn = 512
    nj = D_FF // tn
    return pl.pallas_call(
        functools.partial(_ffn_up_multi_body, tm=n, seq=seq),
        grid=(nj,),
        in_specs=[
            pl.BlockSpec((n, D_MODEL), lambda j: (0, 0)),
            pl.BlockSpec((D_MODEL, tn), lambda j: (0, j)),
            pl.BlockSpec((D_MODEL, tn), lambda j: (0, nj + j)),
            pl.BlockSpec((8, tn), lambda j: (0, j)),
            pl.BlockSpec((n, tn), lambda j: (0, j)),
            pl.BlockSpec((n, tn), lambda j: (0, j)),
        ],
        out_specs=[pl.BlockSpec((n, tn), lambda j: (0, j))] * 2,
        out_shape=[jax.ShapeDtypeStruct((n, D_FF), BF16), jax.ShapeDtypeStruct((n, D_FF), F32)],
        compiler_params=_params(("parallel",), 32),
        name="ffn_up_multi",
    )(h2, wup, wup, cw, p1, p2)


def _ffn_down_body(act_ref, wd_ref, x1_ref, o_ref):
    o_ref[...] = x1_ref[...] + _dot(act_ref[...], wd_ref[...])


def _ffn_down(act, wd, x1, tm):
    n = act.shape[0]
    tn = 512
    return pl.pallas_call(
        _ffn_down_body,
        grid=(n // tm, D_MODEL // tn),
        in_specs=[
            pl.BlockSpec((tm, D_FF), lambda i, j: (i, 0)),
            pl.BlockSpec((D_FF, tn), lambda i, j: (0, j)),
            pl.BlockSpec((tm, tn), lambda i, j: (i, j)),
        ],
        out_specs=pl.BlockSpec((tm, tn), lambda i, j: (i, j)),
        out_shape=jax.ShapeDtypeStruct((n, D_MODEL), F32),
        compiler_params=_params(("parallel", "parallel"), 48),
        name="ffn_down",
    )(act, wd, x1)


def _swap_halves(w):
    half = w.shape[-1] // 2
    return jnp.concatenate([w[..., half:], w[..., :half]], axis=-1)


def _rope_table(pos):
    half = MLA_ROPE // 2
    inv = ROPE_THETA ** (-jnp.arange(half, dtype=F32) * 2.0 / MLA_ROPE)
    ang = pos.astype(F32)[:, None] * inv[None, :]
    cos, sin = jnp.cos(ang), jnp.sin(ang)
    return jnp.concatenate([cos, cos, -sin, sin], axis=-1)


def _layer_weights(w_in, g_norm1, gla_w_gate2, gla_b_gate, gla_g_out, w_br_gla, mla_g_qlat, mla_w_uq,
                   mla_g_kvlat, mla_w_ukv, mla_g_q, mla_g_k, mla_g_qpe, mla_g_kpe, w_br_mla, w_out,
                   g_norm2, ffn_w_up, ffn_conv_w, ffn_conv_b, ffn_w_down):
    o_lr = 2 * 1024 + 2048
    o_r = o_lr + GLA_GATE_RANK
    o_kpe = o_r + 2048 + 2 * MLA_LORA
    o_ga = o_kpe + MLA_ROPE
    w_main = _regroup_w_in(w_in)
    w_kpe = w_in[:, o_kpe:o_ga]
    w_small = jnp.concatenate(
        [w_kpe, _swap_halves(w_kpe), w_in[:, o_lr:o_r],
         jnp.zeros((D_MODEL, SMALL_COLS - SMALL_LR - GLA_GATE_RANK), F32)], axis=1).astype(BF16)
    w2ext = jnp.zeros((SMALL_COLS, GLA_HEADS * GLA_HK), F32).at[SMALL_LR:SMALL_LR + GLA_GATE_RANK].set(
        gla_w_gate2).astype(BF16)
    wq = mla_w_uq.reshape(MLA_LORA, MLA_HEADS, MLA_NOPE + MLA_ROPE)
    wqn = wq[:, :, :MLA_NOPE].reshape(MLA_LORA, MLA_HEADS * MLA_NOPE).astype(BF16)
    wq_pe = wq[:, :, MLA_NOPE:]
    wqp = jnp.concatenate([wq_pe, _swap_halves(wq_pe)], axis=-1).reshape(MLA_LORA, MLA_HEADS * LANES).astype(BF16)
    wkv = mla_w_ukv.reshape(MLA_LORA, MLA_HEADS, MLA_NOPE + MLA_VDIM)
    wk = wkv[:, :, :MLA_NOPE].reshape(MLA_LORA, MLA_HEADS * MLA_NOPE).astype(BF16)
    wv = wkv[:, :, MLA_NOPE:].reshape(MLA_LORA, MLA_HEADS * MLA_VDIM).astype(BF16)
    cw = jnp.concatenate([ffn_conv_w, ffn_conv_b[None, :], jnp.zeros((8 - CONV_W - 1, D_FF), F32)], axis=0)
    return dict(
        w_main=w_main, w_small=w_small, g1=g_norm1[None, :], w2ext=w2ext, bg=gla_b_gate[None, :],
        go=gla_g_out[None, :], wg=w_br_gla.astype(BF16), wqn=wqn, wqp=wqp, gql=mla_g_qlat[None, :],
        gkl=mla_g_kvlat[None, :], gq=jnp.tile(mla_g_q, 2)[None, :],
        gqp=jnp.tile(jnp.concatenate([mla_g_qpe, _swap_halves(mla_g_qpe)]), 2)[None, :],
        gkp=jnp.concatenate([mla_g_kpe, _swap_halves(mla_g_kpe)])[None, :],
        wk=wk, wv=wv, wvt=wv.T, gk=jnp.tile(mla_g_k, 2)[None, :], wm=w_br_mla.astype(BF16),
        wo=w_out.astype(BF16),
        g2=g_norm2[None, :], wup=ffn_w_up.astype(BF16), cw=cw, wd=ffn_w_down.astype(BF16),
        p128=jnp.asarray(np.kron(np.eye(2, dtype=np.float32), np.full((LANES, LANES), 1.0 / LANES, np.float32)), BF16),
    )


def _trunk_front(x, w, s0, batch, seq, chunk, gla_tb, tm_in, tm, tab, tab_blocks, with_values):
    zm, zs = _in_proj(x, w["g1"], w["w_main"], w["w_small"], tm_in)
    og, s_new = _gla(zm, zs, w["w2ext"], w["bg"], w["go"], s0, batch, seq, chunk, gla_tb)
    qext, ckv, kpe, kp2 = _mla_proj(zm, zs, w["wqn"], w["wqp"], w["gql"], w["gkl"], w["gq"], w["gqp"],
                                    w["gkp"], tab, w["p128"], tm, tab_blocks)
    kv = _mla_expand(ckv, w["wk"], w["gk"], w["p128"], tm, w["wvt"] if with_values else None)
    return zm, og, s_new, qext, ckv, kpe, kp2, kv


def kernel(x_prompt, x_sample, state_gla, cache_mla_ckv, cache_mla_kpe, cache_ffn_conv, w_in, g_norm1, gla_w_gate2, gla_b_gate, gla_g_out, w_br_gla, mla_g_qlat, mla_w_uq, mla_g_kvlat, mla_w_ukv, mla_g_q, mla_g_k, mla_g_qpe, mla_g_kpe, w_br_mla, w_out, g_norm2, ffn_w_up, ffn_conv_w, ffn_conv_b, ffn_w_down):
    bp, tp, _ = x_prompt.shape
    bs, ts, _ = x_sample.shape
    depth = w_in.shape[0]
    past = cache_mla_ckv.shape[2]
    np_rows, ns_rows = bp * tp, bs * ts
    tm_p = 512
    tm_big = 1024
    tab_p = _rope_table(jnp.arange(tp))
    tab_s = jnp.tile(_rope_table(past + jnp.arange(ts)), (bs, 1))
    xp = x_prompt.reshape(np_rows, D_MODEL)
    xs = x_sample.reshape(ns_rows, D_MODEL)
    outs = [[] for _ in range(8)]
    layer_weights = (w_in, g_norm1, gla_w_gate2, gla_b_gate, gla_g_out, w_br_gla, mla_g_qlat, mla_w_uq,
                     mla_g_kvlat, mla_w_ukv, mla_g_q, mla_g_k, mla_g_qpe, mla_g_kpe, w_br_mla, w_out,
                     g_norm2, ffn_w_up, ffn_conv_w, ffn_conv_b, ffn_w_down)
    for l in range(depth):
        w = _layer_weights(*[a[l] for a in layer_weights])

        s0 = jnp.zeros((bp, GLA_HEADS, GLA_HK, GLA_HV), F32)
        zm, og, sp, qext, ckv_p, kpe_p, kp2, (kn, vt) = _trunk_front(
            xp, w, s0, bp, tp, CHUNK, 512, tm_big, tm_p, tab_p, tp // tm_p, True)
        att = _attn_prompt(qext, kn, kp2, vt, bp, tp, 512)
        x1, h2 = _merge(og, att, w["wg"], w["wm"], zm, w["wo"], xp, w["g2"], tm_p)
        hist8 = jnp.zeros((bp, 8, D_FF), F32)
        act, tails = _ffn_up_seq(h2, w["wup"], w["cw"], hist8, bp, tp, tm_big)
        fp = tails.reshape(bp, tp // tm_big, CONV_W - 1, D_FF)[:, -1]
        xp = _ffn_down(act, w["wd"], x1, tm_p)

        zm, og, ss, qext, ckv_s, kpe_s, kp2, (kn,) = _trunk_front(
            xs, w, state_gla[l], bs, ts, ts, ts, ns_rows, ns_rows, tab_s, 1, False)
        ckv_past = cache_mla_ckv[l].reshape(bs * past, MLA_LORA)
        (kn_past,) = _mla_expand(ckv_past, w["wk"], w["gk"], w["p128"], 512)
        kpe_past = cache_mla_kpe[l].reshape(bs * past, MLA_ROPE)
        kp2_past = jnp.concatenate([kpe_past, kpe_past], axis=-1).astype(BF16)
        att = _attn_sample(qext, kn_past, kp2_past, ckv_past, kn, kp2, ckv_s, w["wv"], bs, past, ts)
        x1, h2 = _merge(og, att, w["wg"], w["wm"], zm, w["wo"], xs, w["g2"], ns_rows)
        hist = cache_ffn_conv[l]
        zrow = jnp.zeros((bs, ts - 1, D_FF), F32)
        p1 = jnp.concatenate([hist[:, 1:2], zrow], axis=1).reshape(ns_rows, D_FF)
        p2 = jnp.concatenate([hist, zrow[:, 1:]], axis=1).reshape(ns_rows, D_FF)
        act, a_full = _ffn_up_multi(h2, w["wup"], w["cw"], p1, p2, ts)
        fs = a_full.reshape(bs, ts, D_FF)[:, ts - (CONV_W - 1):]
        xs = _ffn_down(act, w["wd"], x1, ns_rows)

        for lst, val in zip(outs, (sp, ss, ckv_p.reshape(bp, tp, MLA_LORA), ckv_s.reshape(bs, ts, MLA_LORA),
                                   kpe_p.reshape(bp, tp, MLA_ROPE), kpe_s.reshape(bs, ts, MLA_ROPE), fp, fs)):
            lst.append(val)
    return (xp.reshape(bp, tp, D_MODEL), xs.reshape(bs, ts, D_MODEL)) + tuple(jnp.stack(o, 0) for o in outs)
```

```python
import functools

import numpy as np
import jax
import jax.numpy as jnp
from jax import lax
from jax.experimental import pallas as pl
from jax.experimental.pallas import tpu as pltpu

F32 = jnp.float32
BF16 = jnp.bfloat16

D_MODEL = 2048
CHUNK = 64
EPS = 1e-6
GLA_HEADS = 4
GLA_HK = 256
GLA_HV = 512
GLA_GATE_RANK = 16
GLA_GATE_TAU = 16.0
MLA_HEADS = 16
MLA_LORA = 512
MLA_NOPE = 128
MLA_ROPE = 64
MLA_VDIM = 128
MLA_SCALE = (MLA_NOPE + MLA_ROPE) ** -0.5
LOG2E = float(np.log2(np.e))
Q_SCALE = MLA_SCALE * LOG2E
ROPE_THETA = 10000.0
D_FF = 5632
CONV_W = 3
LANES = 128
NEG_BIG = -1e30
ATTN_UNROLL = 4

COL_Q, COL_K, COL_V, COL_R, COL_MQ, COL_MKV, COL_GA, COL_GB, MAIN_COLS = (
    0, 1024, 2048, 4096, 6144, 6656, 7168, 9216, 11264)
SMALL_COLS = 256
SMALL_LR = 128

MIB = 1024 * 1024


def _params(semantics, vmem_mib):
    return pltpu.CompilerParams(dimension_semantics=semantics, vmem_limit_bytes=vmem_mib * MIB)


def _dot(a, b):
    return jnp.dot(a, b, preferred_element_type=F32)


def _dot_nt(a, b):
    return lax.dot_general(a, b, (((1,), (1,)), ((), ())), preferred_element_type=F32)


def _dot_tn(a, b):
    return lax.dot_general(a, b, (((0,), (0,)), ((), ())), preferred_element_type=F32)


def _sigmoid(x):
    return 1.0 / (1.0 + jnp.exp(-x))


def _row_rms(x):
    return x * lax.rsqrt(jnp.mean(x * x, axis=-1, keepdims=True) + EPS)


def _slab_rms(x, p_ref):
    ms = _dot((x * x).astype(BF16), p_ref[...])
    return x * lax.rsqrt(ms + EPS)


W_IN_SEGMENTS = ((0, COL_R, 0), (COL_R, COL_GA, GLA_GATE_RANK), (COL_GA, MAIN_COLS, GLA_GATE_RANK + MLA_ROPE))


def _regroup_body(main_ref, next_ref, o_ref, narrow_ref, *, tr):
    j = pl.program_id(0)

    @pl.when(j == 0)
    def _():
        narrow_ref[...] = jnp.zeros(narrow_ref.shape, F32)

    prev = 0
    for lo, hi, shift in W_IN_SEGMENTS:
        if shift > prev:
            @pl.when(j == lo // tr)
            def _(prev=prev, shift=shift):
                narrow_ref[prev:shift, :] = main_ref[prev:shift, :]

        @pl.when((j >= lo // tr) & (j < hi // tr))
        def _(shift=shift):
            if shift == 0:
                o_ref[...] = main_ref[...].astype(BF16)
            else:
                o_ref[0:tr - shift, :] = main_ref[shift:tr, :].astype(BF16)
                o_ref[tr - shift:tr, :] = next_ref[0:shift, :].astype(BF16)
        prev = shift


def _regroup_w_in(w_in_t):
    tr = 1024
    assert all(lo % tr == 0 and hi % tr == 0 and shift % 16 == 0 and shift < LANES
               for lo, hi, shift in W_IN_SEGMENTS)
    return pl.pallas_call(
        functools.partial(_regroup_body, tr=tr),
        grid=(MAIN_COLS // tr,),
        in_specs=[
            pl.BlockSpec((tr, D_MODEL), lambda j: (j, 0)),
            pl.BlockSpec((LANES, D_MODEL), lambda j: ((j + 1) * (tr // LANES), 0)),
        ],
        out_specs=[
            pl.BlockSpec((tr, D_MODEL), lambda j: (j, 0)),
            pl.BlockSpec((LANES, D_MODEL), lambda j: (0, 0)),
        ],
        out_shape=[
            jax.ShapeDtypeStruct((MAIN_COLS, D_MODEL), BF16),
            jax.ShapeDtypeStruct((LANES, D_MODEL), F32),
        ],
        compiler_params=_params(("arbitrary",), 48),
        name="regroup_w_in",
    )(w_in_t, w_in_t)


def _in_proj_body(x_ref, g_ref, wm_ref, ws_ref, zm_ref, zs_ref, h_ref):
    @pl.when(pl.program_id(1) == 0)
    def _():
        h_ref[...] = (_row_rms(x_ref[...]) * g_ref[...]).astype(BF16)
        zs_ref[...] = _dot_nt(h_ref[...], ws_ref[...])

    zm_ref[...] = _dot_nt(h_ref[...], wm_ref[...]).astype(BF16)


def _in_proj(x, g, wm, ws, tm):
    n = x.shape[0]
    tn = 1024
    return pl.pallas_call(
        _in_proj_body,
        grid=(n // tm, MAIN_COLS // tn),
        in_specs=[
            pl.BlockSpec((tm, D_MODEL), lambda i, j: (i, 0)),
            pl.BlockSpec((1, D_MODEL), lambda i, j: (0, 0)),
            pl.BlockSpec((tn, D_MODEL), lambda i, j: (j, 0)),
            pl.BlockSpec((SMALL_COLS, D_MODEL), lambda i, j: (0, 0)),
        ],
        out_specs=[
            pl.BlockSpec((tm, tn), lambda i, j: (i, j)),
            pl.BlockSpec((tm, SMALL_COLS), lambda i, j: (i, 0)),
        ],
        out_shape=[
            jax.ShapeDtypeStruct((n, MAIN_COLS), BF16),
            jax.ShapeDtypeStruct((n, SMALL_COLS), F32),
        ],
        scratch_shapes=[pltpu.VMEM((tm, D_MODEL), BF16)],
        compiler_params=_params(("parallel", "arbitrary"), 56),
        name="in_proj",
    )(x, g, wm, ws)


def _gla_tables(c):
    levels = int(np.log2(c))
    t = np.arange(c)[:, None]
    u = np.arange(c)[None, :]
    masks = [(u == t)]
    level2 = None
    for l in range(levels):
        m = c >> (l + 1)
        mid_t = (t // (2 * m)) * 2 * m + m
        upper = t >= mid_t
        if m == 2:
            level2 = np.where(upper, (u >= mid_t) & (u <= t), (u > t) & (u < mid_t))
        mid_u = (u // (2 * m)) * 2 * m + m
        masks.append((t // (2 * m) == u // (2 * m)) & upper & (u < mid_u))
    eye_h = np.eye(GLA_HEADS, dtype=np.float32)
    gmat = np.concatenate([np.kron(eye_h, (u <= t).astype(np.float32)),
                           np.kron(eye_h, level2.astype(np.float32))], axis=0)
    group = _gla_group_heads(c)
    masks = np.stack([np.kron(np.eye(group, dtype=np.float32), mk.astype(np.float32)) for mk in masks])
    return jnp.asarray(gmat, BF16), jnp.asarray(masks), levels


def _gla_group_heads(c):
    return min(GLA_HEADS, max(1, LANES // c))


def _gla_level_exponent(b, log_a, level2, m, c):
    if m == 1:
        row = lax.broadcasted_iota(jnp.int32, (c, 1), 0)
        return jnp.where(row % 2 == 1, log_a, 0.0)
    if m == 2:
        return level2
    parts = [jnp.broadcast_to(b[i + m - 1:i + m, :], (2 * m, GLA_HK)) for i in range(0, c, 2 * m)]
    ref = parts[0] if len(parts) == 1 else jnp.concatenate(parts, axis=0)
    return -jnp.abs(b - ref)


def _gla_body(q_ref, k_ref, v_ref, r_ref, zs_ref, w2_ref, bg_ref, go_ref, s0_ref, gmat_ref, mask_ref,
              og_ref, sout_ref, st_ref, *, c, nchunk, levels, single):
    t = pl.program_id(1)

    if not single:
        @pl.when(t == 0)
        def _():
            for h in range(GLA_HEADS):
                st_ref[h] = jnp.transpose(s0_ref[0, h])

    nrow = GLA_HEADS * c
    gw = _gla_group_heads(c) * c

    def stack(x, width):
        return jnp.concatenate([x[:, h * width:(h + 1) * width] for h in range(GLA_HEADS)], axis=0)

    def chunk(ci, carry):
        rows = pl.ds(0, c) if single else pl.ds(pl.multiple_of(ci * c, c), c)
        q = stack(q_ref[rows, :], GLA_HK).astype(F32) * (GLA_HK ** -0.5)
        k = stack(k_ref[rows, :], GLA_HK).astype(F32)
        v = stack(v_ref[rows, :], GLA_HV)
        x = stack(_dot(zs_ref[rows, :].astype(BF16), w2_ref[...]) + bg_ref[...], GLA_HK) * LOG2E
        log_a = (jnp.minimum(x, 0.0) - jnp.log2(1.0 + jnp.exp2(-jnp.abs(x)))) * (1.0 / GLA_GATE_TAU)
        hi = log_a.astype(BF16)
        lo = (log_a - hi.astype(F32)).astype(BF16)
        gm = gmat_ref[...]
        pre = _dot(gm, hi) + _dot(gm, lo)
        b = pre[0:nrow]
        b_last = [b[h * c + c - 1:h * c + c, :] for h in range(GLA_HEADS)]
        q_in = (q * jnp.exp2(b)).astype(BF16)
        k_out = (k * jnp.exp2(jnp.concatenate([jnp.broadcast_to(r, (c, GLA_HK)) for r in b_last], axis=0)
                              - b)).astype(BF16)

        def diag_blocks(prod):
            return [prod[g:g + gw, g:g + gw] for g in range(0, nrow, gw)]

        att = [mask_ref[0] * blk for blk in diag_blocks(_dot_nt(q.astype(BF16), k.astype(BF16)))]
        for l in range(levels):
            d = jnp.exp2(_gla_level_exponent(b, log_a, pre[nrow:2 * nrow], c >> (l + 1), nrow))
            prod = _dot_nt((q * d).astype(BF16), (k * d).astype(BF16))
            att = [a + mask_ref[1 + l] * blk for a, blk in zip(att, diag_blocks(prod))]
        if len(att) == 1:
            att_full = att[0].astype(BF16)
        else:
            zero = jnp.zeros((gw, gw), BF16)
            att_full = jnp.concatenate(
                [jnp.concatenate([a.astype(BF16) if i == j else zero for j in range(len(att))], axis=1)
                 for i, a in enumerate(att)], axis=0)
        o_intra = _dot(att_full, v)
        for h in range(GLA_HEADS):
            hr = slice(h * c, (h + 1) * c)
            cv = slice(h * GLA_HV, (h + 1) * GLA_HV)
            if single:
                s0 = s0_ref[0, h]
                o = o_intra[hr] + _dot(q_in[hr], s0.astype(BF16))
                ones = jnp.ones((c, LANES), BF16)
                decay = jnp.exp2(_dot_tn(hi[hr], ones) + _dot_tn(lo[hr], ones))
                sout_ref[0, h] = (s0 * jnp.concatenate([decay] * (GLA_HV // LANES), axis=1)
                                  + _dot_tn(k_out[hr], v[hr]))
            else:
                st = st_ref[h]
                o = o_intra[hr] + _dot_nt(q_in[hr], st.astype(BF16))
                st_ref[h] = st * jnp.exp2(b_last[h]) + _dot_tn(v[hr], k_out[hr])
            gate = r_ref[rows, cv].astype(F32)
            og = _row_rms(o) * go_ref[...] * (gate * _sigmoid(gate))
            og_ref[rows, cv] = og.astype(BF16)
        return carry

    if single:
        chunk(0, 0)
        return
    lax.fori_loop(0, nchunk, chunk, 0)

    @pl.when(t == pl.num_programs(1) - 1)
    def _():
        for h in range(GLA_HEADS):
            sout_ref[0, h] = jnp.transpose(st_ref[h])


def _gla(zm, zs, w2ext, bg, go, s0, batch, seq, c, tb):
    n = zm.shape[0]
    nt = seq // tb
    gmat, masks, levels = _gla_tables(c)
    dk, dv = GLA_HEADS * GLA_HK, GLA_HEADS * GLA_HV
    const2 = lambda b, t: (0, 0)
    return pl.pallas_call(
        functools.partial(_gla_body, c=c, nchunk=tb // c, levels=levels, single=(seq == c)),
        grid=(batch, nt),
        in_specs=[
            pl.BlockSpec((tb, dk), lambda b, t: (b * nt + t, COL_Q // dk)),
            pl.BlockSpec((tb, dk), lambda b, t: (b * nt + t, COL_K // dk)),
            pl.BlockSpec((tb, dv), lambda b, t: (b * nt + t, COL_V // dv)),
            pl.BlockSpec((tb, dv), lambda b, t: (b * nt + t, COL_R // dv)),
            pl.BlockSpec((tb, SMALL_COLS), lambda b, t: (b * nt + t, 0)),
            pl.BlockSpec((SMALL_COLS, dk), const2),
            pl.BlockSpec((1, dk), const2),
            pl.BlockSpec((1, GLA_HV), const2),
            pl.BlockSpec((1, GLA_HEADS, GLA_HK, GLA_HV), lambda b, t: (b, 0, 0, 0)),
            pl.BlockSpec(gmat.shape, const2),
            pl.BlockSpec(masks.shape, lambda b, t: (0, 0, 0)),
        ],
        out_specs=[
            pl.BlockSpec((tb, dv), lambda b, t: (b * nt + t, 0)),
            pl.BlockSpec((1, GLA_HEADS, GLA_HK, GLA_HV), lambda b, t: (b, 0, 0, 0)),
        ],
        out_shape=[
            jax.ShapeDtypeStruct((n, dv), BF16),
            jax.ShapeDtypeStruct((batch, GLA_HEADS, GLA_HK, GLA_HV), F32),
        ],
        scratch_shapes=[pltpu.VMEM((GLA_HEADS, GLA_HV, GLA_HK), F32)],
        compiler_params=_params(("parallel", "arbitrary"), 40),
        name="gla",
    )(zm, zm, zm, zm, zs, w2ext, bg, go, s0, gmat, masks)


def _mla_proj_body(mq_ref, mkv_ref, zs_ref, wqn_ref, wqp_ref, gql_ref, gkl_ref, gq_ref, gqp_ref, gkp_ref,
                   tab_ref, p_ref, q_ref, ckv_ref, kpe_ref, kp2_ref):
    qlat = (_row_rms(mq_ref[...].astype(F32)) * gql_ref[...]).astype(BF16)
    qn = _dot(qlat, wqn_ref[...])
    qp = _dot(qlat, wqp_ref[...])
    tab = tab_ref[...]
    tab2 = jnp.concatenate([tab, tab], axis=1)
    for lo in range(0, MLA_HEADS * LANES, 2 * LANES):
        cols = slice(lo, lo + 2 * LANES)
        nope = (_slab_rms(qn[:, cols], p_ref) * (gq_ref[...] * Q_SCALE)).astype(BF16)
        pe = (_slab_rms(qp[:, cols], p_ref) * (gqp_ref[...] * Q_SCALE) * tab2).astype(BF16)
        for i in range(2):
            dst = 2 * lo + i * 2 * LANES
            q_ref[:, dst:dst + LANES] = nope[:, i * LANES:(i + 1) * LANES]
            q_ref[:, dst + LANES:dst + 2 * LANES] = pe[:, i * LANES:(i + 1) * LANES]
    ckv_ref[...] = _row_rms(mkv_ref[...].astype(F32)) * gkl_ref[...]
    slab = zs_ref[:, 0:LANES]
    rot = _row_rms(slab) * gkp_ref[...] * tab
    kp2 = rot + pltpu.roll(rot, MLA_ROPE, axis=1)
    kpe_ref[...] = kp2[:, 0:MLA_ROPE]
    kp2_ref[...] = kp2.astype(BF16)


def _mla_proj(zm, zs, wqn, wqp, gql, gkl, gq, gqp, gkp, tab, p128, tm, tab_blocks):
    n = zm.shape[0]
    const = lambda i: (0, 0)
    return pl.pallas_call(
        _mla_proj_body,
        grid=(n // tm,),
        in_specs=[
            pl.BlockSpec((tm, MLA_LORA), lambda i: (i, COL_MQ // MLA_LORA)),
            pl.BlockSpec((tm, MLA_LORA), lambda i: (i, COL_MKV // MLA_LORA)),
            pl.BlockSpec((tm, SMALL_COLS), lambda i: (i, 0)),
            pl.BlockSpec(wqn.shape, const),
            pl.BlockSpec(wqp.shape, const),
            pl.BlockSpec((1, MLA_LORA), const),
            pl.BlockSpec((1, MLA_LORA), const),
            pl.BlockSpec((1, 2 * LANES), const),
            pl.BlockSpec((1, 2 * LANES), const),
            pl.BlockSpec((1, LANES), const),
            pl.BlockSpec((tm, LANES), lambda i: (i % tab_blocks, 0)),
            pl.BlockSpec((2 * LANES, 2 * LANES), const),
        ],
        out_specs=[
            pl.BlockSpec((tm, 2 * LANES * MLA_HEADS), lambda i: (i, 0)),
            pl.BlockSpec((tm, MLA_LORA), lambda i: (i, 0)),
            pl.BlockSpec((tm, MLA_ROPE), lambda i: (i, 0)),
            pl.BlockSpec((tm, LANES), lambda i: (i, 0)),
        ],
        out_shape=[
            jax.ShapeDtypeStruct((n, 2 * LANES * MLA_HEADS), BF16),
            jax.ShapeDtypeStruct((n, MLA_LORA), F32),
            jax.ShapeDtypeStruct((n, MLA_ROPE), F32),
            jax.ShapeDtypeStruct((n, LANES), BF16),
        ],
        compiler_params=_params(("parallel",), 48),
        name="mla_proj",
    )(zm, zm, zs, wqn, wqp, gql, gkl, gq, gqp, gkp, tab, p128)


def _mla_expand_keys(c_ref, wk_ref, gk_ref, p_ref, kn_ref):
    cb = c_ref[...].astype(BF16)
    kn = _dot(cb, wk_ref[...])
    for lo in range(0, MLA_HEADS * LANES, 2 * LANES):
        cols = slice(lo, lo + 2 * LANES)
        kn_ref[:, cols] = (_slab_rms(kn[:, cols], p_ref) * gk_ref[...]).astype(BF16)
    return cb


def _mla_expand_k_body(c_ref, wk_ref, gk_ref, p_ref, kn_ref):
    _mla_expand_keys(c_ref, wk_ref, gk_ref, p_ref, kn_ref)


def _mla_expand_kv_body(c_ref, wk_ref, gk_ref, p_ref, wvt_ref, kn_ref, vt_ref):
    cb = _mla_expand_keys(c_ref, wk_ref, gk_ref, p_ref, kn_ref)
    vt = _dot_nt(wvt_ref[...], cb).astype(BF16)
    for h in range(MLA_HEADS):
        vt_ref[h, 0] = vt[h * MLA_VDIM:(h + 1) * MLA_VDIM, :]


def _mla_expand(ckv, wk, gk, p128, tm, wvt=None):
    n = ckv.shape[0]
    const = lambda i: (0, 0)
    width = MLA_HEADS * LANES
    in_specs = [
        pl.BlockSpec((tm, MLA_LORA), lambda i: (i, 0)),
        pl.BlockSpec(wk.shape, const),
        pl.BlockSpec((1, 2 * LANES), const),
        pl.BlockSpec((2 * LANES, 2 * LANES), const),
    ]
    out_specs = [pl.BlockSpec((tm, width), lambda i: (i, 0))]
    out_shape = [jax.ShapeDtypeStruct((n, width), BF16)]
    args = [ckv, wk, gk, p128]
    body = _mla_expand_k_body
    if wvt is not None:
        body = _mla_expand_kv_body
        in_specs.append(pl.BlockSpec(wvt.shape, const))
        out_specs.append(pl.BlockSpec((MLA_HEADS, 1, MLA_VDIM, tm), lambda i: (0, i, 0, 0)))
        out_shape.append(jax.ShapeDtypeStruct((MLA_HEADS, n // tm, MLA_VDIM, tm), BF16))
        args.append(wvt)
    return pl.pallas_call(
        body,
        grid=(n // tm,),
        in_specs=in_specs,
        out_specs=out_specs,
        out_shape=out_shape,
        compiler_params=_params(("parallel",), 40),
        name="mla_expand",
    )(*args)


def _attn_prompt_body(qi_tab, kb_tab, q_ref, kn_ref, kp_ref, vt_ref, o_ref,
                      kext_ref, vx_ref, bias_ref, m_ref, acc_ref, s0, s1, p0, p1, a0, a1, *, tq, nq):
    @pl.when((pl.program_id(0) == 0) & (pl.program_id(1) == 0))
    def _():
        kc = lax.broadcasted_iota(jnp.int32, (tq, tq), 0) // CHUNK
        qc = lax.broadcasted_iota(jnp.int32, (tq, tq), 1) // CHUNK
        bias_ref[...] = jnp.where(kc <= qc, 0.0, NEG_BIG)

    kext_ref[:, 0:LANES] = kn_ref[...]
    kext_ref[:, LANES:2 * LANES] = kp_ref[...]
    vx_ref[:, 0:MLA_VDIM, :] = vt_ref[0]
    vx_ref[:, MLA_VDIM:, :] = jnp.ones((nq, vx_ref.shape[1] - MLA_VDIM, tq), BF16)
    m_ref[...] = jnp.full(m_ref.shape, NEG_BIG, F32)
    acc_ref[...] = jnp.zeros(acc_ref.shape, F32)
    nblk = nq * (nq + 1) // 2
    s_bufs, p_bufs, a_bufs = (s0, s1), (p0, p1), (a0, a1)

    def rows(i):
        return pl.ds(pl.multiple_of(i * tq, tq), tq)

    def scores(t, par):
        s_bufs[par][...] = _dot_nt(kext_ref[rows(kb_tab[t]), :], q_ref[rows(qi_tab[t]), :])

    def softmax(t, par, diagonal):
        qi = qi_tab[t]
        s = s_bufs[par][...]
        if diagonal:
            s = s + bias_ref[...]
        m_old = m_ref[qi]
        m_new = jnp.maximum(m_old, jnp.max(s, axis=0, keepdims=True))
        a_bufs[par][...] = jnp.exp2(m_old - m_new)
        m_ref[qi] = m_new
        p_bufs[par][...] = jnp.exp2(s - m_new).astype(BF16)

    def values(t, par):
        qi = qi_tab[t]
        acc_ref[qi] = a_bufs[par][...] * acc_ref[qi] + _dot(vx_ref[kb_tab[t]], p_bufs[par][...])

    def step(u, par, diagonal):
        scores(u, par)
        softmax(u - 1, 1 - par, diagonal)
        values(u - 2, par)

    def sweep(first, stop, diagonal):
        trips = (stop - first) // ATTN_UNROLL

        def trip(j, carry):
            for i in range(ATTN_UNROLL):
                step(first + ATTN_UNROLL * j + i, i % 2, diagonal)
            return carry

        lax.fori_loop(0, trips, trip, 0)
        for u in range(first + trips * ATTN_UNROLL, stop):
            step(u, u % 2, diagonal)

    scores(0, 0)
    scores(1, 1)
    softmax(0, 0, True)
    sweep(2, nq + 1, True)
    step(nq + 1, 1, False)
    sweep(nq + 2, nblk, False)
    softmax(nblk - 1, 1, False)
    values(nblk - 2, 0)
    values(nblk - 1, 1)
    for qi in range(nq):
        acc = acc_ref[qi]
        o_ref[qi * tq:(qi + 1) * tq, :] = jnp.transpose(
            acc[0:MLA_VDIM] / acc[MLA_VDIM:MLA_VDIM + 1]).astype(BF16)


def _attn_prompt(qext, kn, kp2, vt, batch, seq, tq):
    n = qext.shape[0]
    nq = seq // tq
    assert nq % 2 == 0
    pairs = [(i, i) for i in range(nq)] + [(qi, kb) for qi in range(nq) for kb in range(qi)]
    qi_tab = jnp.asarray([p[0] for p in pairs], jnp.int32)
    kb_tab = jnp.asarray([p[1] for p in pairs], jnp.int32)
    ones_rows = 16
    grid_spec = pltpu.PrefetchScalarGridSpec(
        num_scalar_prefetch=2,
        grid=(batch, MLA_HEADS),
        in_specs=[
            pl.BlockSpec((seq, 2 * LANES), lambda b, h, *_: (b, h)),
            pl.BlockSpec((seq, LANES), lambda b, h, *_: (b, h)),
            pl.BlockSpec((seq, LANES), lambda b, h, *_: (b, 0)),
            pl.BlockSpec((1, nq, MLA_VDIM, tq), lambda b, h, *_: (h, b, 0, 0)),
        ],
        out_specs=pl.BlockSpec((seq, MLA_VDIM), lambda b, h, *_: (b, h)),
        scratch_shapes=[
            pltpu.VMEM((seq, 2 * LANES), BF16),
            pltpu.VMEM((nq, MLA_VDIM + ones_rows, tq), BF16),
            pltpu.VMEM((tq, tq), F32),
            pltpu.VMEM((nq, 1, tq), F32),
            pltpu.VMEM((nq, MLA_VDIM + ones_rows, tq), F32),
            pltpu.VMEM((tq, tq), F32), pltpu.VMEM((tq, tq), F32),
            pltpu.VMEM((tq, tq), BF16), pltpu.VMEM((tq, tq), BF16),
            pltpu.VMEM((1, tq), F32), pltpu.VMEM((1, tq), F32),
        ],
    )
    return pl.pallas_call(
        functools.partial(_attn_prompt_body, tq=tq, nq=nq),
        grid_spec=grid_spec,
        out_shape=jax.ShapeDtypeStruct((n, MLA_HEADS * MLA_VDIM), BF16),
        compiler_params=_params(("arbitrary", "arbitrary"), 40),
        name="attn_prompt",
    )(qi_tab, kb_tab, qext, kn, kp2, vt)


def _attn_sample_body(q_ref, hm_ref, knp_ref, kpp_ref, cp_ref, knn_ref, kpn_ref, cn_ref, wv_ref, o_ref, *, seq):
    q = q_ref[...]
    qn = jnp.concatenate([q[:, 2 * h * LANES:(2 * h + 1) * LANES] for h in range(MLA_HEADS)], axis=1)
    qbd = jnp.concatenate([qn] * MLA_HEADS, axis=0) * hm_ref[...]
    qpe = jnp.concatenate([q[:, (2 * h + 1) * LANES:(2 * h + 2) * LANES] for h in range(MLA_HEADS)], axis=0)
    s_past = _dot_nt(knp_ref[...], qbd) + _dot_nt(kpp_ref[...], qpe)
    s_new = _dot_nt(knn_ref[...], qbd) + _dot_nt(kpn_ref[...], qpe)
    m = jnp.maximum(jnp.max(s_past, axis=0, keepdims=True), jnp.max(s_new, axis=0, keepdims=True))
    p_past = jnp.exp2(s_past - m).astype(BF16)
    p_new = jnp.exp2(s_new - m).astype(BF16)

    def with_ones(c_ref):
        c = c_ref[...].astype(BF16)
        return jnp.concatenate([c, jnp.ones((c.shape[0], LANES), BF16)], axis=1)

    acc = _dot_tn(p_past, with_ones(cp_ref)) + _dot_tn(p_new, with_ones(cn_ref))
    lat = (acc[:, 0:MLA_LORA] / acc[:, MLA_LORA:MLA_LORA + 1]).astype(BF16)
    for h in range(MLA_HEADS):
        cols = slice(h * MLA_VDIM, (h + 1) * MLA_VDIM)
        o_ref[:, cols] = _dot(lat[h * seq:(h + 1) * seq], wv_ref[:, cols]).astype(BF16)


def _attn_sample(qext, kn_past, kp2_past, ckv_past, kn_new, kp2_new, ckv_new, wv, batch, past, seq):
    n = qext.shape[0]
    width = MLA_HEADS * LANES
    head_mask = jnp.asarray(np.kron(np.eye(MLA_HEADS, dtype=np.float32), np.ones((seq, LANES), np.float32)), BF16)
    stream = lambda b: (b, 0)
    const = lambda b: (0, 0)
    return pl.pallas_call(
        functools.partial(_attn_sample_body, seq=seq),
        grid=(batch,),
        in_specs=[
            pl.BlockSpec((seq, 2 * width), stream),
            pl.BlockSpec(head_mask.shape, const),
            pl.BlockSpec((past, width), stream),
            pl.BlockSpec((past, LANES), stream),
            pl.BlockSpec((past, MLA_LORA), stream),
            pl.BlockSpec((seq, width), stream),
            pl.BlockSpec((seq, LANES), stream),
            pl.BlockSpec((seq, MLA_LORA), stream),
            pl.BlockSpec(wv.shape, const),
        ],
        out_specs=pl.BlockSpec((seq, MLA_HEADS * MLA_VDIM), stream),
        out_shape=jax.ShapeDtypeStruct((n, MLA_HEADS * MLA_VDIM), BF16),
        compiler_params=_params(("parallel",), 48),
        name="attn_sample",
    )(qext, head_mask, kn_past, kp2_past, ckv_past, kn_new, kp2_new, ckv_new, wv)


MERGE_COL_CHUNK = 256


def _merge_body(og_ref, at_ref, wg_ref, wm_ref, ga_ref, gb_ref, wo_ref, x_ref, g2_ref, x1_ref, h2_ref):
    j = pl.program_id(1)

    @pl.when(j == 0)
    def _():
        x1_ref[...] = x_ref[...]

    og = og_ref[...]
    at = at_ref[...]
    upd = None
    for lo in range(0, ga_ref.shape[1], MERGE_COL_CHUNK):
        cols = slice(lo, lo + MERGE_COL_CHUNK)
        u = (_sigmoid(ga_ref[:, cols].astype(F32)) * _dot(og, wg_ref[:, cols])
             + _sigmoid(gb_ref[:, cols].astype(F32)) * _dot(at, wm_ref[:, cols]))
        part = _dot(u.astype(BF16), wo_ref[cols, :])
        upd = part if upd is None else upd + part
    x1_ref[...] += upd

    @pl.when(j == pl.num_programs(1) - 1)
    def _():
        h2_ref[...] = (_row_rms(x1_ref[...]) * g2_ref[...]).astype(BF16)


def _merge(og, att, wg, wm, zm, wo, x, g2, tm):
    n = x.shape[0]
    tn = 512
    return pl.pallas_call(
        _merge_body,
        grid=(n // tm, D_MODEL // tn),
        in_specs=[
            pl.BlockSpec((tm, D_MODEL), lambda i, j: (i, 0)),
            pl.BlockSpec((tm, D_MODEL), lambda i, j: (i, 0)),
            pl.BlockSpec((D_MODEL, tn), lambda i, j: (0, j)),
            pl.BlockSpec((D_MODEL, tn), lambda i, j: (0, j)),
            pl.BlockSpec((tm, tn), lambda i, j: (i, COL_GA // tn + j)),
            pl.BlockSpec((tm, tn), lambda i, j: (i, COL_GB // tn + j)),
            pl.BlockSpec((tn, D_MODEL), lambda i, j: (j, 0)),
            pl.BlockSpec((tm, D_MODEL), lambda i, j: (i, 0)),
            pl.BlockSpec((1, D_MODEL), lambda i, j: (0, 0)),
        ],
        out_specs=[
            pl.BlockSpec((tm, D_MODEL), lambda i, j: (i, 0)),
            pl.BlockSpec((tm, D_MODEL), lambda i, j: (i, 0)),
        ],
        out_shape=[
            jax.ShapeDtypeStruct((n, D_MODEL), F32),
            jax.ShapeDtypeStruct((n, D_MODEL), BF16),
        ],
        compiler_params=_params(("parallel", "arbitrary"), 56),
        name="merge",
    )(og, att, wg, wm, zm, zm, wo, x, g2)


FFN_COL_CHUNK = 256


def _gelu_gate(a, a1, a2, cw, gt):
    c = cw[3:4, :] + cw[2:3, :] * a + cw[0:1, :] * a2 + cw[1:2, :] * a1
    return 0.5 * c * (1.0 + lax.erf(c * (2.0 ** -0.5))) * gt


def _ffn_up_seq_body(h_ref, wa_ref, wg_ref, cw_ref, hist_ref, act_ref, tail_ref, carry_ref, *, tm, tiles_per_seq):
    i = pl.program_id(0)
    j = pl.program_id(1)

    @pl.when(i % tiles_per_seq == 0)
    def _():
        carry_ref[j] = hist_ref[0]

    row = lax.broadcasted_iota(jnp.int32, (tm, 1), 0)
    h = h_ref[...]
    for lo in range(0, act_ref.shape[1], FFN_COL_CHUNK):
        cols = slice(lo, lo + FFN_COL_CHUNK)
        a = _dot(h, wa_ref[:, cols])
        gt = _dot(h, wg_ref[:, cols])
        prev = carry_ref[j, :, cols]
        a1 = jnp.where(row == 0, prev[7:8, :], pltpu.roll(a, 1, axis=0))
        a2 = jnp.where(row == 0, prev[6:7, :], jnp.where(row == 1, prev[7:8, :], pltpu.roll(a, 2, axis=0)))
        act_ref[:, cols] = _gelu_gate(a, a1, a2, cw_ref[:, cols], gt).astype(BF16)
        carry_ref[j, :, cols] = a[tm - 8:tm, :]
        tail_ref[0, :, cols] = a[tm - (CONV_W - 1):tm, :]


def _ffn_up_seq(h2, wup, cw, hist8, batch, seq, tm):
    n = h2.shape[0]
    tn = 512
    nj = D_FF // tn
    tps = seq // tm
    return pl.pallas_call(
        functools.partial(_ffn_up_seq_body, tm=tm, tiles_per_seq=tps),
        grid=(n // tm, nj),
        in_specs=[
            pl.BlockSpec((tm, D_MODEL), lambda i, j: (i, 0)),
            pl.BlockSpec((D_MODEL, tn), lambda i, j: (0, j)),
            pl.BlockSpec((D_MODEL, tn), lambda i, j: (0, nj + j)),
            pl.BlockSpec((8, tn), lambda i, j: (0, j)),
            pl.BlockSpec((1, 8, tn), lambda i, j: (i // tps, 0, j)),
        ],
        out_specs=[
            pl.BlockSpec((tm, tn), lambda i, j: (i, j)),
            pl.BlockSpec((1, CONV_W - 1, tn), lambda i, j: (i, 0, j)),
        ],
        out_shape=[
            jax.ShapeDtypeStruct((n, D_FF), BF16),
            jax.ShapeDtypeStruct((n // tm, CONV_W - 1, D_FF), F32),
        ],
        scratch_shapes=[pltpu.VMEM((nj, 8, tn), F32)],
        compiler_params=_params(("arbitrary", "arbitrary"), 48),
        name="ffn_up_seq",
    )(h2, wup, wup, cw, hist8)


def _ffn_up_multi_body(h_ref, wa_ref, wg_ref, cw_ref, p1_ref, p2_ref, act_ref, a_ref, *, tm, seq):
    a = _dot(h_ref[...], wa_ref[...])
    gt = _dot(h_ref[...], wg_ref[...])
    pos = lax.broadcasted_iota(jnp.int32, (tm, 1), 0) % seq
    a1 = jnp.where(pos == 0, p1_ref[...], pltpu.roll(a, 1, axis=0))
    a2 = jnp.where(pos <= 1, p2_ref[...], pltpu.roll(a, 2, axis=0))
    act_ref[...] = _gelu_gate(a, a1, a2, cw_ref[...], gt).astype(BF16)
    a_ref[...] = a


def _ffn_up_multi(h2, wup, cw, p1, p2, seq):
    n = h2.shape[0]
    tn = 512
    nj = D_FF // tn
    return pl.pallas_call(
        functools.partial(_ffn_up_multi_body, tm=n, seq=seq),
        grid=(nj,),
        in_specs=[
            pl.BlockSpec((n, D_MODEL), lambda j: (0, 0)),
            pl.BlockSpec((D_MODEL, tn), lambda j: (0, j)),
            pl.BlockSpec((D_MODEL, tn), lambda j: (0, nj + j)),
            pl.BlockSpec((8, tn), lambda j: (0, j)),
            pl.BlockSpec((n, tn), lambda j: (0, j)),
            pl.BlockSpec((n, tn), lambda j: (0, j)),
        ],
        out_specs=[pl.BlockSpec((n, tn), lambda j: (0, j))] * 2,
        out_shape=[jax.ShapeDtypeStruct((n, D_FF), BF16), jax.ShapeDtypeStruct((n, D_FF), F32)],
        compiler_params=_params(("parallel",), 32),
        name="ffn_up_multi",
    )(h2, wup, wup, cw, p1, p2)


def _ffn_down_body(act_ref, wd_ref, x1_ref, o_ref):
    o_ref[...] = x1_ref[...] + _dot(act_ref[...], wd_ref[...])


def _ffn_down(act, wd, x1, tm):
    n = act.shape[0]
    tn = 512
    return pl.pallas_call(
        _ffn_down_body,
        grid=(n // tm, D_MODEL // tn),
        in_specs=[
            pl.BlockSpec((tm, D_FF), lambda i, j: (i, 0)),
            pl.BlockSpec((D_FF, tn), lambda i, j: (0, j)),
            pl.BlockSpec((tm, tn), lambda i, j: (i, j)),
        ],
        out_specs=pl.BlockSpec((tm, tn), lambda i, j: (i, j)),
        out_shape=jax.ShapeDtypeStruct((n, D_MODEL), F32),
        compiler_params=_params(("parallel", "parallel"), 56),
        name="ffn_down",
    )(act, wd, x1)


def _swap_halves(w):
    half = w.shape[-1] // 2
    return jnp.concatenate([w[..., half:], w[..., :half]], axis=-1)


def _rope_table(pos):
    half = MLA_ROPE // 2
    inv = ROPE_THETA ** (-jnp.arange(half, dtype=F32) * 2.0 / MLA_ROPE)
    ang = pos.astype(F32)[:, None] * inv[None, :]
    cos, sin = jnp.cos(ang), jnp.sin(ang)
    return jnp.concatenate([cos, cos, -sin, sin], axis=-1)


def _layer_weights(w_in, g_norm1, gla_w_gate2, gla_b_gate, gla_g_out, w_br_gla, mla_g_qlat, mla_w_uq,
                   mla_g_kvlat, mla_w_ukv, mla_g_q, mla_g_k, mla_g_qpe, mla_g_kpe, w_br_mla, w_out,
                   g_norm2, ffn_w_up, ffn_conv_w, ffn_conv_b, ffn_w_down):
    w_main, w_narrow = _regroup_w_in(w_in.T)
    w_lr = w_narrow[0:GLA_GATE_RANK]
    w_kpe = w_narrow[GLA_GATE_RANK:GLA_GATE_RANK + MLA_ROPE]
    half = MLA_ROPE // 2
    w_small = jnp.concatenate(
        [w_kpe, w_kpe[half:], w_kpe[:half], w_lr,
         jnp.zeros((SMALL_COLS - SMALL_LR - GLA_GATE_RANK, D_MODEL), F32)], axis=0).astype(BF16)
    w2ext = jnp.zeros((SMALL_COLS, GLA_HEADS * GLA_HK), F32).at[SMALL_LR:SMALL_LR + GLA_GATE_RANK].set(
        gla_w_gate2).astype(BF16)
    wq = mla_w_uq.reshape(MLA_LORA, MLA_HEADS, MLA_NOPE + MLA_ROPE)
    wqn = wq[:, :, :MLA_NOPE].reshape(MLA_LORA, MLA_HEADS * MLA_NOPE).astype(BF16)
    wq_pe = wq[:, :, MLA_NOPE:]
    wqp = jnp.concatenate([wq_pe, _swap_halves(wq_pe)], axis=-1).reshape(MLA_LORA, MLA_HEADS * LANES).astype(BF16)
    wkv = mla_w_ukv.reshape(MLA_LORA, MLA_HEADS, MLA_NOPE + MLA_VDIM)
    wk = wkv[:, :, :MLA_NOPE].reshape(MLA_LORA, MLA_HEADS * MLA_NOPE).astype(BF16)
    wv = wkv[:, :, MLA_NOPE:].reshape(MLA_LORA, MLA_HEADS * MLA_VDIM).astype(BF16)
    cw = jnp.concatenate([ffn_conv_w, ffn_conv_b[None, :], jnp.zeros((8 - CONV_W - 1, D_FF), F32)], axis=0)
    return dict(
        w_main=w_main, w_small=w_small, g1=g_norm1[None, :], w2ext=w2ext, bg=gla_b_gate[None, :],
        go=gla_g_out[None, :], wg=w_br_gla.astype(BF16), wqn=wqn, wqp=wqp, gql=mla_g_qlat[None, :],
        gkl=mla_g_kvlat[None, :], gq=jnp.tile(mla_g_q, 2)[None, :],
        gqp=jnp.tile(jnp.concatenate([mla_g_qpe, _swap_halves(mla_g_qpe)]), 2)[None, :],
        gkp=jnp.concatenate([mla_g_kpe, _swap_halves(mla_g_kpe)])[None, :],
        wk=wk, wv=wv, wvt=wv.T, gk=jnp.tile(mla_g_k, 2)[None, :], wm=w_br_mla.astype(BF16),
        wo=w_out.astype(BF16),
        g2=g_norm2[None, :], wup=ffn_w_up.astype(BF16), cw=cw, wd=ffn_w_down.astype(BF16),
        p128=jnp.asarray(np.kron(np.eye(2, dtype=np.float32), np.full((LANES, LANES), 1.0 / LANES, np.float32)), BF16),
    )


def _trunk_front(x, w, s0, batch, seq, chunk, gla_tb, tm_in, tm, tab, tab_blocks, with_values):
    zm, zs = _in_proj(x, w["g1"], w["w_main"], w["w_small"], tm_in)
    og, s_new = _gla(zm, zs, w["w2ext"], w["bg"], w["go"], s0, batch, seq, chunk, gla_tb)
    qext, ckv, kpe, kp2 = _mla_proj(zm, zs, w["wqn"], w["wqp"], w["gql"], w["gkl"], w["gq"], w["gqp"],
                                    w["gkp"], tab, w["p128"], tm, tab_blocks)
    kv = _mla_expand(ckv, w["wk"], w["gk"], w["p128"], tm, w["wvt"] if with_values else None)
    return zm, og, s_new, qext, ckv, kpe, kp2, kv


def kernel(x_prompt, x_sample, state_gla, cache_mla_ckv, cache_mla_kpe, cache_ffn_conv, w_in, g_norm1, gla_w_gate2, gla_b_gate, gla_g_out, w_br_gla, mla_g_qlat, mla_w_uq, mla_g_kvlat, mla_w_ukv, mla_g_q, mla_g_k, mla_g_qpe, mla_g_kpe, w_br_mla, w_out, g_norm2, ffn_w_up, ffn_conv_w, ffn_conv_b, ffn_w_down):
    bp, tp, _ = x_prompt.shape
    bs, ts, _ = x_sample.shape
    depth = w_in.shape[0]
    past = cache_mla_ckv.shape[2]
    np_rows, ns_rows = bp * tp, bs * ts
    tm_p = 512
    tm_big = 1024
    tab_p = _rope_table(jnp.arange(tp))
    tab_s = jnp.tile(_rope_table(past + jnp.arange(ts)), (bs, 1))
    xp = x_prompt.reshape(np_rows, D_MODEL)
    xs = x_sample.reshape(ns_rows, D_MODEL)
    outs = [[] for _ in range(8)]
    layer_weights = (w_in, g_norm1, gla_w_gate2, gla_b_gate, gla_g_out, w_br_gla, mla_g_qlat, mla_w_uq,
                     mla_g_kvlat, mla_w_ukv, mla_g_q, mla_g_k, mla_g_qpe, mla_g_kpe, w_br_mla, w_out,
                     g_norm2, ffn_w_up, ffn_conv_w, ffn_conv_b, ffn_w_down)
    for l in range(depth):
        w = _layer_weights(*[a[l] for a in layer_weights])

        s0 = jnp.zeros((bp, GLA_HEADS, GLA_HK, GLA_HV), F32)
        zm, og, sp, qext, ckv_p, kpe_p, kp2, (kn, vt) = _trunk_front(
            xp, w, s0, bp, tp, CHUNK, 512, tm_big, tm_p, tab_p, tp // tm_p, True)
        att = _attn_prompt(qext, kn, kp2, vt, bp, tp, 512)
        x1, h2 = _merge(og, att, w["wg"], w["wm"], zm, w["wo"], xp, w["g2"], tm_p)
        hist8 = jnp.zeros((bp, 8, D_FF), F32)
        act, tails = _ffn_up_seq(h2, w["wup"], w["cw"], hist8, bp, tp, tm_big)
        fp = tails.reshape(bp, tp // tm_big, CONV_W - 1, D_FF)[:, -1]
        xp = _ffn_down(act, w["wd"], x1, tm_big)

        zm, og, ss, qext, ckv_s, kpe_s, kp2, (kn,) = _trunk_front(
            xs, w, state_gla[l], bs, ts, ts, ts, ns_rows, ns_rows, tab_s, 1, False)
        ckv_past = cache_mla_ckv[l].reshape(bs * past, MLA_LORA)
        (kn_past,) = _mla_expand(ckv_past, w["wk"], w["gk"], w["p128"], 512)
        kpe_past = cache_mla_kpe[l].reshape(bs * past, MLA_ROPE)
        kp2_past = jnp.concatenate([kpe_past, kpe_past], axis=-1).astype(BF16)
        att = _attn_sample(qext, kn_past, kp2_past, ckv_past, kn, kp2, ckv_s, w["wv"], bs, past, ts)
        x1, h2 = _merge(og, att, w["wg"], w["wm"], zm, w["wo"], xs, w["g2"], ns_rows)
        hist = cache_ffn_conv[l]
        zrow = jnp.zeros((bs, ts - 1, D_FF), F32)
        p1 = jnp.concatenate([hist[:, 1:2], zrow], axis=1).reshape(ns_rows, D_FF)
        p2 = jnp.concatenate([hist, zrow[:, 1:]], axis=1).reshape(ns_rows, D_FF)
        act, a_full = _ffn_up_multi(h2, w["wup"], w["cw"], p1, p2, ts)
        fs = a_full.reshape(bs, ts, D_FF)[:, ts - (CONV_W - 1):]
        xs = _ffn_down(act, w["wd"], x1, ns_rows)

        for lst, val in zip(outs, (sp, ss, ckv_p.reshape(bp, tp, MLA_LORA), ckv_s.reshape(bs, ts, MLA_LORA),
                                   kpe_p.reshape(bp, tp, MLA_ROPE), kpe_s.reshape(bs, ts, MLA_ROPE), fp, fs)):
            lst.append(val)
    return (xp.reshape(bp, tp, D_MODEL), xs.reshape(bs, ts, D_MODEL)) + tuple(jnp.stack(o, 0) for o in outs)
```

```python
import functools

import numpy as np
import jax
import jax.numpy as jnp
from jax import lax
from jax.experimental import pallas as pl
from jax.experimental.pallas import tpu as pltpu

F32 = jnp.float32
BF16 = jnp.bfloat16

D_MODEL = 2048
CHUNK = 64
EPS = 1e-6
GLA_HEADS = 4
GLA_HK = 256
GLA_HV = 512
GLA_GATE_RANK = 16
GLA_GATE_TAU = 16.0
MLA_HEADS = 16
MLA_LORA = 512
MLA_NOPE = 128
MLA_ROPE = 64
MLA_VDIM = 128
MLA_SCALE = (MLA_NOPE + MLA_ROPE) ** -0.5
LOG2E = float(np.log2(np.e))
Q_SCALE = MLA_SCALE * LOG2E
ROPE_THETA = 10000.0
D_FF = 5632
CONV_W = 3
LANES = 128
NEG_BIG = -1e30
ATTN_UNROLL = 4

COL_Q, COL_K, COL_V, COL_R, COL_MQ, COL_MKV, COL_GA, COL_GB, MAIN_COLS = (
    0, 1024, 2048, 4096, 6144, 6656, 7168, 9216, 11264)
SMALL_COLS = 256
SMALL_LR = 128

MIB = 1024 * 1024


def _params(semantics, vmem_mib):
    return pltpu.CompilerParams(dimension_semantics=semantics, vmem_limit_bytes=vmem_mib * MIB)


def _dot(a, b):
    return jnp.dot(a, b, preferred_element_type=F32)


def _dot_nt(a, b):
    return lax.dot_general(a, b, (((1,), (1,)), ((), ())), preferred_element_type=F32)


def _dot_tn(a, b):
    return lax.dot_general(a, b, (((0,), (0,)), ((), ())), preferred_element_type=F32)


def _sigmoid(x):
    return 1.0 / (1.0 + jnp.exp(-x))


def _row_rms(x):
    return x * lax.rsqrt(jnp.mean(x * x, axis=-1, keepdims=True) + EPS)


def _slab_rms(x, p_ref):
    ms = _dot((x * x).astype(BF16), p_ref[...])
    return x * lax.rsqrt(ms + EPS)


W_IN_SEGMENTS = ((0, COL_R, 0), (COL_R, COL_GA, GLA_GATE_RANK), (COL_GA, MAIN_COLS, GLA_GATE_RANK + MLA_ROPE))


def _regroup_body(main_ref, next_ref, o_ref, narrow_ref, *, tr):
    j = pl.program_id(0)

    @pl.when(j == 0)
    def _():
        narrow_ref[...] = jnp.zeros(narrow_ref.shape, F32)

    prev = 0
    for lo, hi, shift in W_IN_SEGMENTS:
        if shift > prev:
            @pl.when(j == lo // tr)
            def _(prev=prev, shift=shift):
                narrow_ref[prev:shift, :] = main_ref[prev:shift, :]

        @pl.when((j >= lo // tr) & (j < hi // tr))
        def _(shift=shift):
            if shift == 0:
                o_ref[...] = main_ref[...].astype(BF16)
            else:
                o_ref[0:tr - shift, :] = main_ref[shift:tr, :].astype(BF16)
                o_ref[tr - shift:tr, :] = next_ref[0:shift, :].astype(BF16)
        prev = shift


def _regroup_w_in(w_in_t):
    tr = 1024
    assert all(lo % tr == 0 and hi % tr == 0 and shift % 16 == 0 and shift < LANES
               for lo, hi, shift in W_IN_SEGMENTS)
    return pl.pallas_call(
        functools.partial(_regroup_body, tr=tr),
        grid=(MAIN_COLS // tr,),
        in_specs=[
            pl.BlockSpec((tr, D_MODEL), lambda j: (j, 0)),
            pl.BlockSpec((LANES, D_MODEL), lambda j: ((j + 1) * (tr // LANES), 0)),
        ],
        out_specs=[
            pl.BlockSpec((tr, D_MODEL), lambda j: (j, 0)),
            pl.BlockSpec((LANES, D_MODEL), lambda j: (0, 0)),
        ],
        out_shape=[
            jax.ShapeDtypeStruct((MAIN_COLS, D_MODEL), BF16),
            jax.ShapeDtypeStruct((LANES, D_MODEL), F32),
        ],
        compiler_params=_params(("arbitrary",), 48),
        name="regroup_w_in",
    )(w_in_t, w_in_t)


def _in_proj_body(x_ref, g_ref, wm_ref, ws_ref, zm_ref, zs_ref, h_ref):
    @pl.when(pl.program_id(1) == 0)
    def _():
        h_ref[...] = (_row_rms(x_ref[...]) * g_ref[...]).astype(BF16)
        zs_ref[...] = _dot_nt(h_ref[...], ws_ref[...])

    zm_ref[...] = _dot_nt(h_ref[...], wm_ref[...]).astype(BF16)


def _in_proj(x, g, wm, ws, tm):
    n = x.shape[0]
    tn = 1024
    return pl.pallas_call(
        _in_proj_body,
        grid=(n // tm, MAIN_COLS // tn),
        in_specs=[
            pl.BlockSpec((tm, D_MODEL), lambda i, j: (i, 0)),
            pl.BlockSpec((1, D_MODEL), lambda i, j: (0, 0)),
            pl.BlockSpec((tn, D_MODEL), lambda i, j: (j, 0)),
            pl.BlockSpec((SMALL_COLS, D_MODEL), lambda i, j: (0, 0)),
        ],
        out_specs=[
            pl.BlockSpec((tm, tn), lambda i, j: (i, j)),
            pl.BlockSpec((tm, SMALL_COLS), lambda i, j: (i, 0)),
        ],
        out_shape=[
            jax.ShapeDtypeStruct((n, MAIN_COLS), BF16),
            jax.ShapeDtypeStruct((n, SMALL_COLS), F32),
        ],
        scratch_shapes=[pltpu.VMEM((tm, D_MODEL), BF16)],
        compiler_params=_params(("parallel", "arbitrary"), 56),
        name="in_proj",
    )(x, g, wm, ws)


def _gla_tables(c):
    levels = int(np.log2(c))
    t = np.arange(c)[:, None]
    u = np.arange(c)[None, :]
    masks = [(u == t)]
    level2 = None
    for l in range(levels):
        m = c >> (l + 1)
        mid_t = (t // (2 * m)) * 2 * m + m
        upper = t >= mid_t
        if m == 2:
            level2 = np.where(upper, (u >= mid_t) & (u <= t), (u > t) & (u < mid_t))
        mid_u = (u // (2 * m)) * 2 * m + m
        masks.append((t // (2 * m) == u // (2 * m)) & upper & (u < mid_u))
    eye_h = np.eye(GLA_HEADS, dtype=np.float32)
    gmat = np.concatenate([np.kron(eye_h, (u <= t).astype(np.float32)),
                           np.kron(eye_h, level2.astype(np.float32))], axis=0)
    group = _gla_group_heads(c)
    masks = np.stack([np.kron(np.eye(group, dtype=np.float32), mk.astype(np.float32)) for mk in masks])
    return jnp.asarray(gmat, BF16), jnp.asarray(masks), levels


def _gla_group_heads(c):
    return min(GLA_HEADS, max(1, LANES // c))


def _gla_level_exponent(b, log_a, level2, m, c):
    if m == 1:
        row = lax.broadcasted_iota(jnp.int32, (c, 1), 0)
        return jnp.where(row % 2 == 1, log_a, 0.0)
    if m == 2:
        return level2
    parts = [jnp.broadcast_to(b[i + m - 1:i + m, :], (2 * m, GLA_HK)) for i in range(0, c, 2 * m)]
    ref = parts[0] if len(parts) == 1 else jnp.concatenate(parts, axis=0)
    return -jnp.abs(b - ref)


def _gla_body(q_ref, k_ref, v_ref, r_ref, zs_ref, w2_ref, bg_ref, go_ref, s0_ref, gmat_ref, mask_ref,
              og_ref, sout_ref, st_ref, *, c, nchunk, levels, single):
    t = pl.program_id(1)

    if not single:
        @pl.when(t == 0)
        def _():
            for h in range(GLA_HEADS):
                st_ref[h] = jnp.transpose(s0_ref[0, h])

    nrow = GLA_HEADS * c
    gw = _gla_group_heads(c) * c

    def stack(x, width):
        return jnp.concatenate([x[:, h * width:(h + 1) * width] for h in range(GLA_HEADS)], axis=0)

    def chunk(ci, carry):
        rows = pl.ds(0, c) if single else pl.ds(pl.multiple_of(ci * c, c), c)
        q = stack(q_ref[rows, :], GLA_HK).astype(F32) * (GLA_HK ** -0.5)
        k = stack(k_ref[rows, :], GLA_HK).astype(F32)
        v = stack(v_ref[rows, :], GLA_HV)
        x = stack(_dot(zs_ref[rows, :].astype(BF16), w2_ref[...]) + bg_ref[...], GLA_HK) * LOG2E
        log_a = (jnp.minimum(x, 0.0) - jnp.log2(1.0 + jnp.exp2(-jnp.abs(x)))) * (1.0 / GLA_GATE_TAU)
        hi = log_a.astype(BF16)
        lo = (log_a - hi.astype(F32)).astype(BF16)
        gm = gmat_ref[...]
        pre = _dot(gm, hi) + _dot(gm, lo)
        b = pre[0:nrow]
        b_last = [b[h * c + c - 1:h * c + c, :] for h in range(GLA_HEADS)]
        q_in = (q * jnp.exp2(b)).astype(BF16)
        k_out = (k * jnp.exp2(jnp.concatenate([jnp.broadcast_to(r, (c, GLA_HK)) for r in b_last], axis=0)
                              - b)).astype(BF16)

        def diag_blocks(prod):
            return [prod[g:g + gw, g:g + gw] for g in range(0, nrow, gw)]

        att = [mask_ref[0] * blk for blk in diag_blocks(_dot_nt(q.astype(BF16), k.astype(BF16)))]
        for l in range(levels):
            d = jnp.exp2(_gla_level_exponent(b, log_a, pre[nrow:2 * nrow], c >> (l + 1), nrow))
            prod = _dot_nt((q * d).astype(BF16), (k * d).astype(BF16))
            att = [a + mask_ref[1 + l] * blk for a, blk in zip(att, diag_blocks(prod))]
        if len(att) == 1:
            att_full = att[0].astype(BF16)
        else:
            zero = jnp.zeros((gw, gw), BF16)
            att_full = jnp.concatenate(
                [jnp.concatenate([a.astype(BF16) if i == j else zero for j in range(len(att))], axis=1)
                 for i, a in enumerate(att)], axis=0)
        o_intra = _dot(att_full, v)
        for h in range(GLA_HEADS):
            hr = slice(h * c, (h + 1) * c)
            cv = slice(h * GLA_HV, (h + 1) * GLA_HV)
            if single:
                s0 = s0_ref[0, h]
                o = o_intra[hr] + _dot(q_in[hr], s0.astype(BF16))
                ones = jnp.ones((c, LANES), BF16)
                decay = jnp.exp2(_dot_tn(hi[hr], ones) + _dot_tn(lo[hr], ones))
                sout_ref[0, h] = (s0 * jnp.concatenate([decay] * (GLA_HV // LANES), axis=1)
                                  + _dot_tn(k_out[hr], v[hr]))
            else:
                st = st_ref[h]
                o = o_intra[hr] + _dot_nt(q_in[hr], st.astype(BF16))
                st_ref[h] = st * jnp.exp2(b_last[h]) + _dot_tn(v[hr], k_out[hr])
            gate = r_ref[rows, cv].astype(F32)
            og = _row_rms(o) * go_ref[...] * (gate * _sigmoid(gate))
            og_ref[rows, cv] = og.astype(BF16)
        return carry

    if single:
        chunk(0, 0)
        return
    lax.fori_loop(0, nchunk, chunk, 0)

    @pl.when(t == pl.num_programs(1) - 1)
    def _():
        for h in range(GLA_HEADS):
            sout_ref[0, h] = jnp.transpose(st_ref[h])


def _gla(zm, zs, w2ext, bg, go, s0, batch, seq, c, tb):
    n = zm.shape[0]
    nt = seq // tb
    gmat, masks, levels = _gla_tables(c)
    dk, dv = GLA_HEADS * GLA_HK, GLA_HEADS * GLA_HV
    const2 = lambda b, t: (0, 0)
    return pl.pallas_call(
        functools.partial(_gla_body, c=c, nchunk=tb // c, levels=levels, single=(seq == c)),
        grid=(batch, nt),
        in_specs=[
            pl.BlockSpec((tb, dk), lambda b, t: (b * nt + t, COL_Q // dk)),
            pl.BlockSpec((tb, dk), lambda b, t: (b * nt + t, COL_K // dk)),
            pl.BlockSpec((tb, dv), lambda b, t: (b * nt + t, COL_V // dv)),
            pl.BlockSpec((tb, dv), lambda b, t: (b * nt + t, COL_R // dv)),
            pl.BlockSpec((tb, SMALL_COLS), lambda b, t: (b * nt + t, 0)),
            pl.BlockSpec((SMALL_COLS, dk), const2),
            pl.BlockSpec((1, dk), const2),
            pl.BlockSpec((1, GLA_HV), const2),
            pl.BlockSpec((1, GLA_HEADS, GLA_HK, GLA_HV), lambda b, t: (b, 0, 0, 0)),
            pl.BlockSpec(gmat.shape, const2),
            pl.BlockSpec(masks.shape, lambda b, t: (0, 0, 0)),
        ],
        out_specs=[
            pl.BlockSpec((tb, dv), lambda b, t: (b * nt + t, 0)),
            pl.BlockSpec((1, GLA_HEADS, GLA_HK, GLA_HV), lambda b, t: (b, 0, 0, 0)),
        ],
        out_shape=[
            jax.ShapeDtypeStruct((n, dv), BF16),
            jax.ShapeDtypeStruct((batch, GLA_HEADS, GLA_HK, GLA_HV), F32),
        ],
        scratch_shapes=[pltpu.VMEM((GLA_HEADS, GLA_HV, GLA_HK), F32)],
        compiler_params=_params(("parallel", "arbitrary"), 40),
        name="gla",
    )(zm, zm, zm, zm, zs, w2ext, bg, go, s0, gmat, masks)


def _mla_proj_body(mq_ref, mkv_ref, zs_ref, wqn_ref, wqp_ref, gql_ref, gkl_ref, gq_ref, gqp_ref, gkp_ref,
                   tab_ref, p_ref, q_ref, ckv_ref, kpe_ref, kp2_ref):
    qlat = (_row_rms(mq_ref[...].astype(F32)) * gql_ref[...]).astype(BF16)
    qn = _dot(qlat, wqn_ref[...])
    qp = _dot(qlat, wqp_ref[...])
    tab = tab_ref[...]
    tab2 = jnp.concatenate([tab, tab], axis=1)
    for lo in range(0, MLA_HEADS * LANES, 2 * LANES):
        cols = slice(lo, lo + 2 * LANES)
        nope = (_slab_rms(qn[:, cols], p_ref) * (gq_ref[...] * Q_SCALE)).astype(BF16)
        pe = (_slab_rms(qp[:, cols], p_ref) * (gqp_ref[...] * Q_SCALE) * tab2).astype(BF16)
        for i in range(2):
            dst = 2 * lo + i * 2 * LANES
            q_ref[:, dst:dst + LANES] = nope[:, i * LANES:(i + 1) * LANES]
            q_ref[:, dst + LANES:dst + 2 * LANES] = pe[:, i * LANES:(i + 1) * LANES]
    ckv_ref[...] = _row_rms(mkv_ref[...].astype(F32)) * gkl_ref[...]
    slab = zs_ref[:, 0:LANES]
    rot = _row_rms(slab) * gkp_ref[...] * tab
    kp2 = rot + pltpu.roll(rot, MLA_ROPE, axis=1)
    kpe_ref[...] = kp2[:, 0:MLA_ROPE]
    kp2_ref[...] = kp2.astype(BF16)


def _mla_proj(zm, zs, wqn, wqp, gql, gkl, gq, gqp, gkp, tab, p128, tm, tab_blocks):
    n = zm.shape[0]
    const = lambda i: (0, 0)
    return pl.pallas_call(
        _mla_proj_body,
        grid=(n // tm,),
        in_specs=[
            pl.BlockSpec((tm, MLA_LORA), lambda i: (i, COL_MQ // MLA_LORA)),
            pl.BlockSpec((tm, MLA_LORA), lambda i: (i, COL_MKV // MLA_LORA)),
            pl.BlockSpec((tm, SMALL_COLS), lambda i: (i, 0)),
            pl.BlockSpec(wqn.shape, const),
            pl.BlockSpec(wqp.shape, const),
            pl.BlockSpec((1, MLA_LORA), const),
            pl.BlockSpec((1, MLA_LORA), const),
            pl.BlockSpec((1, 2 * LANES), const),
            pl.BlockSpec((1, 2 * LANES), const),
            pl.BlockSpec((1, LANES), const),
            pl.BlockSpec((tm, LANES), lambda i: (i % tab_blocks, 0)),
            pl.BlockSpec((2 * LANES, 2 * LANES), const),
        ],
        out_specs=[
            pl.BlockSpec((tm, 2 * LANES * MLA_HEADS), lambda i: (i, 0)),
            pl.BlockSpec((tm, MLA_LORA), lambda i: (i, 0)),
            pl.BlockSpec((tm, MLA_ROPE), lambda i: (i, 0)),
            pl.BlockSpec((tm, LANES), lambda i: (i, 0)),
        ],
        out_shape=[
            jax.ShapeDtypeStruct((n, 2 * LANES * MLA_HEADS), BF16),
            jax.ShapeDtypeStruct((n, MLA_LORA), F32),
            jax.ShapeDtypeStruct((n, MLA_ROPE), F32),
            jax.ShapeDtypeStruct((n, LANES), BF16),
        ],
        compiler_params=_params(("parallel",), 48),
        name="mla_proj",
    )(zm, zm, zs, wqn, wqp, gql, gkl, gq, gqp, gkp, tab, p128)


def _mla_expand_keys(c_ref, wk_ref, gk_ref, p_ref, kn_ref):
    cb = c_ref[...].astype(BF16)
    kn = _dot(cb, wk_ref[...])
    for lo in range(0, MLA_HEADS * LANES, 2 * LANES):
        cols = slice(lo, lo + 2 * LANES)
        kn_ref[:, cols] = (_slab_rms(kn[:, cols], p_ref) * gk_ref[...]).astype(BF16)
    return cb


def _mla_expand_k_body(c_ref, wk_ref, gk_ref, p_ref, kn_ref):
    _mla_expand_keys(c_ref, wk_ref, gk_ref, p_ref, kn_ref)


def _mla_expand_kv_body(c_ref, wk_ref, gk_ref, p_ref, wvt_ref, kn_ref, vt_ref):
    cb = _mla_expand_keys(c_ref, wk_ref, gk_ref, p_ref, kn_ref)
    vt = _dot_nt(wvt_ref[...], cb).astype(BF16)
    for h in range(MLA_HEADS):
        vt_ref[h, 0] = vt[h * MLA_VDIM:(h + 1) * MLA_VDIM, :]


def _mla_expand(ckv, wk, gk, p128, tm, wvt=None):
    n = ckv.shape[0]
    const = lambda i: (0, 0)
    width = MLA_HEADS * LANES
    in_specs = [
        pl.BlockSpec((tm, MLA_LORA), lambda i: (i, 0)),
        pl.BlockSpec(wk.shape, const),
        pl.BlockSpec((1, 2 * LANES), const),
        pl.BlockSpec((2 * LANES, 2 * LANES), const),
    ]
    out_specs = [pl.BlockSpec((tm, width), lambda i: (i, 0))]
    out_shape = [jax.ShapeDtypeStruct((n, width), BF16)]
    args = [ckv, wk, gk, p128]
    body = _mla_expand_k_body
    if wvt is not None:
        body = _mla_expand_kv_body
        in_specs.append(pl.BlockSpec(wvt.shape, const))
        out_specs.append(pl.BlockSpec((MLA_HEADS, 1, MLA_VDIM, tm), lambda i: (0, i, 0, 0)))
        out_shape.append(jax.ShapeDtypeStruct((MLA_HEADS, n // tm, MLA_VDIM, tm), BF16))
        args.append(wvt)
    return pl.pallas_call(
        body,
        grid=(n // tm,),
        in_specs=in_specs,
        out_specs=out_specs,
        out_shape=out_shape,
        compiler_params=_params(("parallel",), 40),
        name="mla_expand",
    )(*args)


def _attn_prompt_body(qi_tab, kb_tab, q_ref, kn_ref, kp_ref, vt_ref, o_ref,
                      kext_ref, vx_ref, bias_ref, m_ref, acc_ref, s0, s1, p0, p1, a0, a1, *, tq, nq):
    @pl.when((pl.program_id(0) == 0) & (pl.program_id(1) == 0))
    def _():
        kc = lax.broadcasted_iota(jnp.int32, (tq, tq), 0) // CHUNK
        qc = lax.broadcasted_iota(jnp.int32, (tq, tq), 1) // CHUNK
        bias_ref[...] = jnp.where(kc <= qc, 0.0, NEG_BIG)

    kext_ref[:, 0:LANES] = kn_ref[...]
    kext_ref[:, LANES:2 * LANES] = kp_ref[...]
    vx_ref[:, 0:MLA_VDIM, :] = vt_ref[0]
    vx_ref[:, MLA_VDIM:, :] = jnp.ones((nq, vx_ref.shape[1] - MLA_VDIM, tq), BF16)
    m_ref[...] = jnp.full(m_ref.shape, NEG_BIG, F32)
    acc_ref[...] = jnp.zeros(acc_ref.shape, F32)
    nblk = nq * (nq + 1) // 2
    s_bufs, p_bufs, a_bufs = (s0, s1), (p0, p1), (a0, a1)

    def rows(i):
        return pl.ds(pl.multiple_of(i * tq, tq), tq)

    def scores(t, par):
        s_bufs[par][...] = _dot_nt(kext_ref[rows(kb_tab[t]), :], q_ref[rows(qi_tab[t]), :])

    def softmax(t, par, diagonal):
        qi = qi_tab[t]
        s = s_bufs[par][...]
        if diagonal:
            s = s + bias_ref[...]
        m_old = m_ref[qi]
        m_new = jnp.maximum(m_old, jnp.max(s, axis=0, keepdims=True))
        a_bufs[par][...] = jnp.exp2(m_old - m_new)
        m_ref[qi] = m_new
        p_bufs[par][...] = jnp.exp2(s - m_new).astype(BF16)

    def values(t, par):
        qi = qi_tab[t]
        acc_ref[qi] = a_bufs[par][...] * acc_ref[qi] + _dot(vx_ref[kb_tab[t]], p_bufs[par][...])

    def step(u, par, diagonal):
        scores(u, par)
        softmax(u - 1, 1 - par, diagonal)
        values(u - 2, par)

    def sweep(first, stop, diagonal):
        trips = (stop - first) // ATTN_UNROLL

        def trip(j, carry):
            for i in range(ATTN_UNROLL):
                step(first + ATTN_UNROLL * j + i, i % 2, diagonal)
            return carry

        lax.fori_loop(0, trips, trip, 0)
        for u in range(first + trips * ATTN_UNROLL, stop):
            step(u, u % 2, diagonal)

    scores(0, 0)
    scores(1, 1)
    softmax(0, 0, True)
    sweep(2, nq + 1, True)
    step(nq + 1, 1, False)
    sweep(nq + 2, nblk, False)
    softmax(nblk - 1, 1, False)
    values(nblk - 2, 0)
    values(nblk - 1, 1)
    for qi in range(nq):
        acc = acc_ref[qi]
        o_ref[qi * tq:(qi + 1) * tq, :] = jnp.transpose(
            acc[0:MLA_VDIM] / acc[MLA_VDIM:MLA_VDIM + 1]).astype(BF16)


def _attn_prompt(qext, kn, kp2, vt, batch, seq, tq):
    n = qext.shape[0]
    nq = seq // tq
    assert nq % 2 == 0
    pairs = [(i, i) for i in range(nq)] + [(qi, kb) for qi in range(nq) for kb in range(qi)]
    qi_tab = jnp.asarray([p[0] for p in pairs], jnp.int32)
    kb_tab = jnp.asarray([p[1] for p in pairs], jnp.int32)
    ones_rows = 16
    grid_spec = pltpu.PrefetchScalarGridSpec(
        num_scalar_prefetch=2,
        grid=(batch, MLA_HEADS),
        in_specs=[
            pl.BlockSpec((seq, 2 * LANES), lambda b, h, *_: (b, h)),
            pl.BlockSpec((seq, LANES), lambda b, h, *_: (b, h)),
            pl.BlockSpec((seq, LANES), lambda b, h, *_: (b, 0)),
            pl.BlockSpec((1, nq, MLA_VDIM, tq), lambda b, h, *_: (h, b, 0, 0)),
        ],
        out_specs=pl.BlockSpec((seq, MLA_VDIM), lambda b, h, *_: (b, h)),
        scratch_shapes=[
            pltpu.VMEM((seq, 2 * LANES), BF16),
            pltpu.VMEM((nq, MLA_VDIM + ones_rows, tq), BF16),
            pltpu.VMEM((tq, tq), F32),
            pltpu.VMEM((nq, 1, tq), F32),
            pltpu.VMEM((nq, MLA_VDIM + ones_rows, tq), F32),
            pltpu.VMEM((tq, tq), F32), pltpu.VMEM((tq, tq), F32),
            pltpu.VMEM((tq, tq), BF16), pltpu.VMEM((tq, tq), BF16),
            pltpu.VMEM((1, tq), F32), pltpu.VMEM((1, tq), F32),
        ],
    )
    return pl.pallas_call(
        functools.partial(_attn_prompt_body, tq=tq, nq=nq),
        grid_spec=grid_spec,
        out_shape=jax.ShapeDtypeStruct((n, MLA_HEADS * MLA_VDIM), BF16),
        compiler_params=_params(("arbitrary", "arbitrary"), 40),
        name="attn_prompt",
    )(qi_tab, kb_tab, qext, kn, kp2, vt)


def _attn_sample_body(q_ref, hm_ref, knp_ref, kpp_ref, cp_ref, knn_ref, kpn_ref, cn_ref, wv_ref, o_ref, *, seq):
    q = q_ref[...]
    qn = jnp.concatenate([q[:, 2 * h * LANES:(2 * h + 1) * LANES] for h in range(MLA_HEADS)], axis=1)
    qbd = jnp.concatenate([qn] * MLA_HEADS, axis=0) * hm_ref[...]
    qpe = jnp.concatenate([q[:, (2 * h + 1) * LANES:(2 * h + 2) * LANES] for h in range(MLA_HEADS)], axis=0)
    s_past = _dot_nt(knp_ref[...], qbd) + _dot_nt(kpp_ref[...], qpe)
    s_new = _dot_nt(knn_ref[...], qbd) + _dot_nt(kpn_ref[...], qpe)
    m = jnp.maximum(jnp.max(s_past, axis=0, keepdims=True), jnp.max(s_new, axis=0, keepdims=True))
    p_past = jnp.exp2(s_past - m).astype(BF16)
    p_new = jnp.exp2(s_new - m).astype(BF16)

    def with_ones(c_ref):
        c = c_ref[...].astype(BF16)
        return jnp.concatenate([c, jnp.ones((c.shape[0], LANES), BF16)], axis=1)

    acc = _dot_tn(p_past, with_ones(cp_ref)) + _dot_tn(p_new, with_ones(cn_ref))
    lat = (acc[:, 0:MLA_LORA] / acc[:, MLA_LORA:MLA_LORA + 1]).astype(BF16)
    for h in range(MLA_HEADS):
        cols = slice(h * MLA_VDIM, (h + 1) * MLA_VDIM)
        o_ref[:, cols] = _dot(lat[h * seq:(h + 1) * seq], wv_ref[:, cols]).astype(BF16)


def _attn_sample(qext, kn_past, kp2_past, ckv_past, kn_new, kp2_new, ckv_new, wv, batch, past, seq):
    n = qext.shape[0]
    width = MLA_HEADS * LANES
    head_mask = jnp.asarray(np.kron(np.eye(MLA_HEADS, dtype=np.float32), np.ones((seq, LANES), np.float32)), BF16)
    stream = lambda b: (b, 0)
    const = lambda b: (0, 0)
    return pl.pallas_call(
        functools.partial(_attn_sample_body, seq=seq),
        grid=(batch,),
        in_specs=[
            pl.BlockSpec((seq, 2 * width), stream),
            pl.BlockSpec(head_mask.shape, const),
            pl.BlockSpec((past, width), stream),
            pl.BlockSpec((past, LANES), stream),
            pl.BlockSpec((past, MLA_LORA), stream),
            pl.BlockSpec((seq, width), stream),
            pl.BlockSpec((seq, LANES), stream),
            pl.BlockSpec((seq, MLA_LORA), stream),
            pl.BlockSpec(wv.shape, const),
        ],
        out_specs=pl.BlockSpec((seq, MLA_HEADS * MLA_VDIM), stream),
        out_shape=jax.ShapeDtypeStruct((n, MLA_HEADS * MLA_VDIM), BF16),
        compiler_params=_params(("parallel",), 48),
        name="attn_sample",
    )(qext, head_mask, kn_past, kp2_past, ckv_past, kn_new, kp2_new, ckv_new, wv)


MERGE_COL_CHUNK = 256


def _branch_merge_body(og_ref, at_ref, wg_ref, wm_ref, ga_ref, gb_ref, u_ref):
    og = og_ref[...]
    at = at_ref[...]
    for lo in range(0, u_ref.shape[1], MERGE_COL_CHUNK):
        cols = slice(lo, lo + MERGE_COL_CHUNK)
        u = (_sigmoid(ga_ref[:, cols].astype(F32)) * _dot(og, wg_ref[:, cols])
             + _sigmoid(gb_ref[:, cols].astype(F32)) * _dot(at, wm_ref[:, cols]))
        u_ref[:, cols] = u.astype(BF16)


def _out_proj_body(u_ref, wo_ref, x_ref, g2_ref, x1_ref, h2_ref):
    x1 = x_ref[...] + _dot(u_ref[...], wo_ref[...])
    x1_ref[...] = x1
    h2_ref[...] = (_row_rms(x1) * g2_ref[...]).astype(BF16)


def _merge(og, att, wg, wm, zm, wo, x, g2, tm_u, tm_x):
    n = x.shape[0]
    tn = 512
    u = pl.pallas_call(
        _branch_merge_body,
        grid=(n // tm_u, D_MODEL // tn),
        in_specs=[
            pl.BlockSpec((tm_u, D_MODEL), lambda i, j: (i, 0)),
            pl.BlockSpec((tm_u, D_MODEL), lambda i, j: (i, 0)),
            pl.BlockSpec((D_MODEL, tn), lambda i, j: (0, j)),
            pl.BlockSpec((D_MODEL, tn), lambda i, j: (0, j)),
            pl.BlockSpec((tm_u, tn), lambda i, j: (i, COL_GA // tn + j)),
            pl.BlockSpec((tm_u, tn), lambda i, j: (i, COL_GB // tn + j)),
        ],
        out_specs=pl.BlockSpec((tm_u, tn), lambda i, j: (i, j)),
        out_shape=jax.ShapeDtypeStruct((n, D_MODEL), BF16),
        compiler_params=_params(("parallel", "parallel"), 48),
        name="branch_merge",
    )(og, att, wg, wm, zm, zm)
    return pl.pallas_call(
        _out_proj_body,
        grid=(n // tm_x,),
        in_specs=[
            pl.BlockSpec((tm_x, D_MODEL), lambda i: (i, 0)),
            pl.BlockSpec((D_MODEL, D_MODEL), lambda i: (0, 0)),
            pl.BlockSpec((tm_x, D_MODEL), lambda i: (i, 0)),
            pl.BlockSpec((1, D_MODEL), lambda i: (0, 0)),
        ],
        out_specs=[
            pl.BlockSpec((tm_x, D_MODEL), lambda i: (i, 0)),
            pl.BlockSpec((tm_x, D_MODEL), lambda i: (i, 0)),
        ],
        out_shape=[
            jax.ShapeDtypeStruct((n, D_MODEL), F32),
            jax.ShapeDtypeStruct((n, D_MODEL), BF16),
        ],
        compiler_params=_params(("parallel",), 56),
        name="out_proj",
    )(u, wo, x, g2)


FFN_COL_CHUNK = 256


def _gelu_gate(a, a1, a2, cw, gt):
    c = cw[3:4, :] + cw[2:3, :] * a + cw[0:1, :] * a2 + cw[1:2, :] * a1
    return 0.5 * c * (1.0 + lax.erf(c * (2.0 ** -0.5))) * gt


def _ffn_up_seq_body(h_ref, wa_ref, wg_ref, cw_ref, hist_ref, act_ref, tail_ref, carry_ref, *, tm, tiles_per_seq):
    i = pl.program_id(0)
    j = pl.program_id(1)

    @pl.when(i % tiles_per_seq == 0)
    def _():
        carry_ref[j] = hist_ref[0]

    row = lax.broadcasted_iota(jnp.int32, (tm, 1), 0)
    h = h_ref[...]
    for lo in range(0, act_ref.shape[1], FFN_COL_CHUNK):
        cols = slice(lo, lo + FFN_COL_CHUNK)
        a = _dot(h, wa_ref[:, cols])
        gt = _dot(h, wg_ref[:, cols])
        prev = carry_ref[j, :, cols]
        a1 = jnp.where(row == 0, prev[7:8, :], pltpu.roll(a, 1, axis=0))
        a2 = jnp.where(row == 0, prev[6:7, :], jnp.where(row == 1, prev[7:8, :], pltpu.roll(a, 2, axis=0)))
        act_ref[:, cols] = _gelu_gate(a, a1, a2, cw_ref[:, cols], gt).astype(BF16)
        carry_ref[j, :, cols] = a[tm - 8:tm, :]
        tail_ref[0, :, cols] = a[tm - (CONV_W - 1):tm, :]


def _ffn_up_seq(h2, wa, wg, cw, hist8, seq, tm):
    n = h2.shape[0]
    tn = 512
    nj = D_FF // tn
    tps = seq // tm
    return pl.pallas_call(
        functools.partial(_ffn_up_seq_body, tm=tm, tiles_per_seq=tps),
        grid=(n // tm, nj),
        in_specs=[
            pl.BlockSpec((tm, D_MODEL), lambda i, j: (i, 0)),
            pl.BlockSpec((D_MODEL, tn), lambda i, j: (0, j)),
            pl.BlockSpec((D_MODEL, tn), lambda i, j: (0, j)),
            pl.BlockSpec((8, tn), lambda i, j: (0, j)),
            pl.BlockSpec((1, 8, tn), lambda i, j: (i // tps, 0, j)),
        ],
        out_specs=[
            pl.BlockSpec((tm, tn), lambda i, j: (i, j)),
            pl.BlockSpec((1, CONV_W - 1, tn), lambda i, j: (i, 0, j)),
        ],
        out_shape=[
            jax.ShapeDtypeStruct((n, D_FF), BF16),
            jax.ShapeDtypeStruct((n // tm, CONV_W - 1, D_FF), F32),
        ],
        scratch_shapes=[pltpu.VMEM((nj, 8, tn), F32)],
        compiler_params=_params(("arbitrary", "arbitrary"), 48),
        name="ffn_up_seq",
    )(h2, wa, wg, cw, hist8)


def _ffn_up_multi_body(h_ref, wa_ref, wg_ref, cw_ref, p1_ref, p2_ref, act_ref, a_ref, wab_ref, wgb_ref, *, tm, seq):
    wa = wa_ref[...].astype(BF16)
    wg = wg_ref[...].astype(BF16)
    wab_ref[...] = wa
    wgb_ref[...] = wg
    a = _dot(h_ref[...], wa)
    gt = _dot(h_ref[...], wg)
    pos = lax.broadcasted_iota(jnp.int32, (tm, 1), 0) % seq
    a1 = jnp.where(pos == 0, p1_ref[...], pltpu.roll(a, 1, axis=0))
    a2 = jnp.where(pos <= 1, p2_ref[...], pltpu.roll(a, 2, axis=0))
    act_ref[...] = _gelu_gate(a, a1, a2, cw_ref[...], gt).astype(BF16)
    a_ref[...] = a


def _ffn_up_multi(h2, wup, cw, p1, p2, seq):
    n = h2.shape[0]
    tn = 512
    nj = D_FF // tn
    col = lambda j: (0, j)
    return pl.pallas_call(
        functools.partial(_ffn_up_multi_body, tm=n, seq=seq),
        grid=(nj,),
        in_specs=[
            pl.BlockSpec((n, D_MODEL), lambda j: (0, 0)),
            pl.BlockSpec((D_MODEL, tn), col),
            pl.BlockSpec((D_MODEL, tn), lambda j: (0, nj + j)),
            pl.BlockSpec((8, tn), col),
            pl.BlockSpec((n, tn), col),
            pl.BlockSpec((n, tn), col),
        ],
        out_specs=[pl.BlockSpec((n, tn), col)] * 2 + [pl.BlockSpec((D_MODEL, tn), col)] * 2,
        out_shape=[jax.ShapeDtypeStruct((n, D_FF), BF16), jax.ShapeDtypeStruct((n, D_FF), F32)]
        + [jax.ShapeDtypeStruct((D_MODEL, D_FF), BF16)] * 2,
        compiler_params=_params(("parallel",), 48),
        name="ffn_up_multi",
    )(h2, wup, wup, cw, p1, p2)


def _ffn_down_body(act_ref, wd_ref, x1_ref, o_ref):
    o_ref[...] = x1_ref[...] + _dot(act_ref[...], wd_ref[...])


def _ffn_down_cast_body(act_ref, wd_ref, x1_ref, o_ref, wdb_ref):
    wd = wd_ref[...].astype(BF16)
    wdb_ref[...] = wd
    o_ref[...] = x1_ref[...] + _dot(act_ref[...], wd)


def _ffn_down(act, wd, x1, tm, emit_bf16_weights=False):
    n = act.shape[0]
    tn = 512
    out_specs = [pl.BlockSpec((tm, tn), lambda i, j: (i, j))]
    out_shape = [jax.ShapeDtypeStruct((n, D_MODEL), F32)]
    if emit_bf16_weights:
        assert n == tm
        out_specs.append(pl.BlockSpec((D_FF, tn), lambda i, j: (0, j)))
        out_shape.append(jax.ShapeDtypeStruct((D_FF, D_MODEL), BF16))
    return pl.pallas_call(
        _ffn_down_cast_body if emit_bf16_weights else _ffn_down_body,
        grid=(n // tm, D_MODEL // tn),
        in_specs=[
            pl.BlockSpec((tm, D_FF), lambda i, j: (i, 0)),
            pl.BlockSpec((D_FF, tn), lambda i, j: (0, j)),
            pl.BlockSpec((tm, tn), lambda i, j: (i, j)),
        ],
        out_specs=out_specs,
        out_shape=out_shape,
        compiler_params=_params(("parallel", "parallel"), 56),
        name="ffn_down",
    )(act, wd, x1)


def _swap_halves(w):
    half = w.shape[-1] // 2
    return jnp.concatenate([w[..., half:], w[..., :half]], axis=-1)


def _rope_table(pos):
    half = MLA_ROPE // 2
    inv = ROPE_THETA ** (-jnp.arange(half, dtype=F32) * 2.0 / MLA_ROPE)
    ang = pos.astype(F32)[:, None] * inv[None, :]
    cos, sin = jnp.cos(ang), jnp.sin(ang)
    return jnp.concatenate([cos, cos, -sin, sin], axis=-1)


def _layer_weights(w_in, g_norm1, gla_w_gate2, gla_b_gate, gla_g_out, w_br_gla, mla_g_qlat, mla_w_uq,
                   mla_g_kvlat, mla_w_ukv, mla_g_q, mla_g_k, mla_g_qpe, mla_g_kpe, w_br_mla, w_out,
                   g_norm2, ffn_w_up, ffn_conv_w, ffn_conv_b, ffn_w_down):
    w_main, w_narrow = _regroup_w_in(w_in.T)
    w_lr = w_narrow[0:GLA_GATE_RANK]
    w_kpe = w_narrow[GLA_GATE_RANK:GLA_GATE_RANK + MLA_ROPE]
    half = MLA_ROPE // 2
    w_small = jnp.concatenate(
        [w_kpe, w_kpe[half:], w_kpe[:half], w_lr,
         jnp.zeros((SMALL_COLS - SMALL_LR - GLA_GATE_RANK, D_MODEL), F32)], axis=0).astype(BF16)
    w2ext = jnp.zeros((SMALL_COLS, GLA_HEADS * GLA_HK), F32).at[SMALL_LR:SMALL_LR + GLA_GATE_RANK].set(
        gla_w_gate2).astype(BF16)
    wq = mla_w_uq.reshape(MLA_LORA, MLA_HEADS, MLA_NOPE + MLA_ROPE)
    wqn = wq[:, :, :MLA_NOPE].reshape(MLA_LORA, MLA_HEADS * MLA_NOPE).astype(BF16)
    wq_pe = wq[:, :, MLA_NOPE:]
    wqp = jnp.concatenate([wq_pe, _swap_halves(wq_pe)], axis=-1).reshape(MLA_LORA, MLA_HEADS * LANES).astype(BF16)
    wkv = mla_w_ukv.reshape(MLA_LORA, MLA_HEADS, MLA_NOPE + MLA_VDIM)
    wk = wkv[:, :, :MLA_NOPE].reshape(MLA_LORA, MLA_HEADS * MLA_NOPE).astype(BF16)
    wv = wkv[:, :, MLA_NOPE:].reshape(MLA_LORA, MLA_HEADS * MLA_VDIM).astype(BF16)
    cw = jnp.concatenate([ffn_conv_w, ffn_conv_b[None, :], jnp.zeros((8 - CONV_W - 1, D_FF), F32)], axis=0)
    return dict(
        w_main=w_main, w_small=w_small, g1=g_norm1[None, :], w2ext=w2ext, bg=gla_b_gate[None, :],
        go=gla_g_out[None, :], wg=w_br_gla.astype(BF16), wqn=wqn, wqp=wqp, gql=mla_g_qlat[None, :],
        gkl=mla_g_kvlat[None, :], gq=jnp.tile(mla_g_q, 2)[None, :],
        gqp=jnp.tile(jnp.concatenate([mla_g_qpe, _swap_halves(mla_g_qpe)]), 2)[None, :],
        gkp=jnp.concatenate([mla_g_kpe, _swap_halves(mla_g_kpe)])[None, :],
        wk=wk, wv=wv, wvt=wv.T, gk=jnp.tile(mla_g_k, 2)[None, :], wm=w_br_mla.astype(BF16),
        wo=w_out.astype(BF16),
        g2=g_norm2[None, :], wup=ffn_w_up, cw=cw, wd=ffn_w_down,
        p128=jnp.asarray(np.kron(np.eye(2, dtype=np.float32), np.full((LANES, LANES), 1.0 / LANES, np.float32)), BF16),
    )


def _trunk_front(x, w, s0, batch, seq, chunk, gla_tb, tm_in, tm, tab, tab_blocks, with_values):
    zm, zs = _in_proj(x, w["g1"], w["w_main"], w["w_small"], tm_in)
    og, s_new = _gla(zm, zs, w["w2ext"], w["bg"], w["go"], s0, batch, seq, chunk, gla_tb)
    qext, ckv, kpe, kp2 = _mla_proj(zm, zs, w["wqn"], w["wqp"], w["gql"], w["gkl"], w["gq"], w["gqp"],
                                    w["gkp"], tab, w["p128"], tm, tab_blocks)
    kv = _mla_expand(ckv, w["wk"], w["gk"], w["p128"], tm, w["wvt"] if with_values else None)
    return zm, og, s_new, qext, ckv, kpe, kp2, kv


def kernel(x_prompt, x_sample, state_gla, cache_mla_ckv, cache_mla_kpe, cache_ffn_conv, w_in, g_norm1, gla_w_gate2, gla_b_gate, gla_g_out, w_br_gla, mla_g_qlat, mla_w_uq, mla_g_kvlat, mla_w_ukv, mla_g_q, mla_g_k, mla_g_qpe, mla_g_kpe, w_br_mla, w_out, g_norm2, ffn_w_up, ffn_conv_w, ffn_conv_b, ffn_w_down):
    bp, tp, _ = x_prompt.shape
    bs, ts, _ = x_sample.shape
    depth = w_in.shape[0]
    past = cache_mla_ckv.shape[2]
    np_rows, ns_rows = bp * tp, bs * ts
    tm_p = 512
    tm_big = 1024
    tab_p = _rope_table(jnp.arange(tp))
    tab_s = jnp.tile(_rope_table(past + jnp.arange(ts)), (bs, 1))
    xp = x_prompt.reshape(np_rows, D_MODEL)
    xs = x_sample.reshape(ns_rows, D_MODEL)
    outs = [[] for _ in range(8)]
    layer_weights = (w_in, g_norm1, gla_w_gate2, gla_b_gate, gla_g_out, w_br_gla, mla_g_qlat, mla_w_uq,
                     mla_g_kvlat, mla_w_ukv, mla_g_q, mla_g_k, mla_g_qpe, mla_g_kpe, w_br_mla, w_out,
                     g_norm2, ffn_w_up, ffn_conv_w, ffn_conv_b, ffn_w_down)
    for l in range(depth):
        w = _layer_weights(*[a[l] for a in layer_weights])

        zm, og, ss, qext, ckv_s, kpe_s, kp2, (kn,) = _trunk_front(
            xs, w, state_gla[l], bs, ts, ts, ts, ns_rows, ns_rows, tab_s, 1, False)
        ckv_past = cache_mla_ckv[l].reshape(bs * past, MLA_LORA)
        (kn_past,) = _mla_expand(ckv_past, w["wk"], w["gk"], w["p128"], 512)
        kpe_past = cache_mla_kpe[l].reshape(bs * past, MLA_ROPE)
        kp2_past = jnp.concatenate([kpe_past, kpe_past], axis=-1).astype(BF16)
        att = _attn_sample(qext, kn_past, kp2_past, ckv_past, kn, kp2, ckv_s, w["wv"], bs, past, ts)
        x1, h2 = _merge(og, att, w["wg"], w["wm"], zm, w["wo"], xs, w["g2"], ns_rows, ns_rows)
        hist = cache_ffn_conv[l]
        zrow = jnp.zeros((bs, ts - 1, D_FF), F32)
        p1 = jnp.concatenate([hist[:, 1:2], zrow], axis=1).reshape(ns_rows, D_FF)
        p2 = jnp.concatenate([hist, zrow[:, 1:]], axis=1).reshape(ns_rows, D_FF)
        act, a_full, wa_bf, wg_bf = _ffn_up_multi(h2, w["wup"], w["cw"], p1, p2, ts)
        fs = a_full.reshape(bs, ts, D_FF)[:, ts - (CONV_W - 1):]
        xs, wd_bf = _ffn_down(act, w["wd"], x1, ns_rows, emit_bf16_weights=True)

        s0 = jnp.zeros((bp, GLA_HEADS, GLA_HK, GLA_HV), F32)
        zm, og, sp, qext, ckv_p, kpe_p, kp2, (kn, vt) = _trunk_front(
            xp, w, s0, bp, tp, CHUNK, 512, tm_big, tm_p, tab_p, tp // tm_p, True)
        att = _attn_prompt(qext, kn, kp2, vt, bp, tp, 512)
        x1, h2 = _merge(og, att, w["wg"], w["wm"], zm, w["wo"], xp, w["g2"], tm_big, tm_p)
        hist8 = jnp.zeros((bp, 8, D_FF), F32)
        act, tails = _ffn_up_seq(h2, wa_bf, wg_bf, w["cw"], hist8, tp, tm_big)
        fp = tails.reshape(bp, tp // tm_big, CONV_W - 1, D_FF)[:, -1]
        (xp,) = _ffn_down(act, wd_bf, x1, tm_big)

        for lst, val in zip(outs, (sp, ss, ckv_p.reshape(bp, tp, MLA_LORA), ckv_s.reshape(bs, ts, MLA_LORA),
                                   kpe_p.reshape(bp, tp, MLA_ROPE), kpe_s.reshape(bs, ts, MLA_ROPE), fp, fs)):
            lst.append(val)
    return (xp.reshape(bp, tp, D_MODEL), xs.reshape(bs, ts, D_MODEL)) + tuple(jnp.stack(o, 0) for o in outs)
```

```python
import functools

import numpy as np
import jax
import jax.numpy as jnp
from jax import lax
from jax.experimental import pallas as pl
from jax.experimental.pallas import tpu as pltpu

F32 = jnp.float32
BF16 = jnp.bfloat16

D_MODEL = 2048
CHUNK = 64
EPS = 1e-6
GLA_HEADS = 4
GLA_HK = 256
GLA_HV = 512
GLA_GATE_RANK = 16
GLA_GATE_TAU = 16.0
MLA_HEADS = 16
MLA_LORA = 512
MLA_NOPE = 128
MLA_ROPE = 64
MLA_VDIM = 128
MLA_SCALE = (MLA_NOPE + MLA_ROPE) ** -0.5
LOG2E = float(np.log2(np.e))
Q_SCALE = MLA_SCALE * LOG2E
ROPE_THETA = 10000.0
D_FF = 5632
CONV_W = 3
LANES = 128
NEG_BIG = -1e30
ATTN_UNROLL = 8
ATTN_QUERY_SPLIT = 2

COL_Q, COL_K, COL_V, COL_R, COL_MQ, COL_MKV, COL_GA, COL_GB, MAIN_COLS = (
    0, 1024, 2048, 4096, 6144, 6656, 7168, 9216, 11264)
SMALL_COLS = 256
SMALL_LR = 128

MIB = 1024 * 1024


def _params(semantics, vmem_mib):
    return pltpu.CompilerParams(dimension_semantics=semantics, vmem_limit_bytes=vmem_mib * MIB)


def _dot(a, b):
    return jnp.dot(a, b, preferred_element_type=F32)


def _dot_nt(a, b):
    return lax.dot_general(a, b, (((1,), (1,)), ((), ())), preferred_element_type=F32)


def _dot_tn(a, b):
    return lax.dot_general(a, b, (((0,), (0,)), ((), ())), preferred_element_type=F32)


def _sigmoid(x):
    return 1.0 / (1.0 + jnp.exp(-x))


def _row_rms(x):
    return x * lax.rsqrt(jnp.mean(x * x, axis=-1, keepdims=True) + EPS)


def _slab_rms(x, p_ref):
    ms = _dot((x * x).astype(BF16), p_ref[...])
    return x * lax.rsqrt(ms + EPS)


W_IN_SEGMENTS = ((0, COL_R, 0), (COL_R, COL_GA, GLA_GATE_RANK), (COL_GA, MAIN_COLS, GLA_GATE_RANK + MLA_ROPE))


def _regroup_body(main_ref, next_ref, o_ref, narrow_ref, *, tr):
    j = pl.program_id(0)

    @pl.when(j == 0)
    def _():
        narrow_ref[...] = jnp.zeros(narrow_ref.shape, F32)

    prev = 0
    for lo, hi, shift in W_IN_SEGMENTS:
        if shift > prev:
            @pl.when(j == lo // tr)
            def _(prev=prev, shift=shift):
                narrow_ref[prev:shift, :] = main_ref[prev:shift, :]

        @pl.when((j >= lo // tr) & (j < hi // tr))
        def _(shift=shift):
            if shift == 0:
                o_ref[...] = main_ref[...].astype(BF16)
            else:
                o_ref[0:tr - shift, :] = main_ref[shift:tr, :].astype(BF16)
                o_ref[tr - shift:tr, :] = next_ref[0:shift, :].astype(BF16)
        prev = shift


def _regroup_w_in(w_in_t):
    tr = 1024
    assert all(lo % tr == 0 and hi % tr == 0 and shift % 16 == 0 and shift < LANES
               for lo, hi, shift in W_IN_SEGMENTS)
    return pl.pallas_call(
        functools.partial(_regroup_body, tr=tr),
        grid=(MAIN_COLS // tr,),
        in_specs=[
            pl.BlockSpec((tr, D_MODEL), lambda j: (j, 0)),
            pl.BlockSpec((LANES, D_MODEL), lambda j: ((j + 1) * (tr // LANES), 0)),
        ],
        out_specs=[
            pl.BlockSpec((tr, D_MODEL), lambda j: (j, 0)),
            pl.BlockSpec((LANES, D_MODEL), lambda j: (0, 0)),
        ],
        out_shape=[
            jax.ShapeDtypeStruct((MAIN_COLS, D_MODEL), BF16),
            jax.ShapeDtypeStruct((LANES, D_MODEL), F32),
        ],
        compiler_params=_params(("arbitrary",), 48),
        name="regroup_w_in",
    )(w_in_t, w_in_t)


def _in_proj_body(x_ref, g_ref, wm_ref, ws_ref, zm_ref, zs_ref, h_ref):
    @pl.when(pl.program_id(1) == 0)
    def _():
        h_ref[...] = (_row_rms(x_ref[...]) * g_ref[...]).astype(BF16)
        zs_ref[...] = _dot_nt(h_ref[...], ws_ref[...])

    zm_ref[...] = _dot_nt(h_ref[...], wm_ref[...]).astype(BF16)


def _in_proj(x, g, wm, ws, tm):
    n = x.shape[0]
    tn = 1024
    return pl.pallas_call(
        _in_proj_body,
        grid=(n // tm, MAIN_COLS // tn),
        in_specs=[
            pl.BlockSpec((tm, D_MODEL), lambda i, j: (i, 0)),
            pl.BlockSpec((1, D_MODEL), lambda i, j: (0, 0)),
            pl.BlockSpec((tn, D_MODEL), lambda i, j: (j, 0)),
            pl.BlockSpec((SMALL_COLS, D_MODEL), lambda i, j: (0, 0)),
        ],
        out_specs=[
            pl.BlockSpec((tm, tn), lambda i, j: (i, j)),
            pl.BlockSpec((tm, SMALL_COLS), lambda i, j: (i, 0)),
        ],
        out_shape=[
            jax.ShapeDtypeStruct((n, MAIN_COLS), BF16),
            jax.ShapeDtypeStruct((n, SMALL_COLS), F32),
        ],
        scratch_shapes=[pltpu.VMEM((tm, D_MODEL), BF16)],
        compiler_params=_params(("parallel", "arbitrary"), 56),
        name="in_proj",
    )(x, g, wm, ws)


def _gla_tables(c):
    levels = int(np.log2(c))
    t = np.arange(c)[:, None]
    u = np.arange(c)[None, :]
    masks = [(u == t)]
    level2 = None
    for l in range(levels):
        m = c >> (l + 1)
        mid_t = (t // (2 * m)) * 2 * m + m
        upper = t >= mid_t
        if m == 2:
            level2 = np.where(upper, (u >= mid_t) & (u <= t), (u > t) & (u < mid_t))
        mid_u = (u // (2 * m)) * 2 * m + m
        masks.append((t // (2 * m) == u // (2 * m)) & upper & (u < mid_u))
    eye_h = np.eye(GLA_HEADS, dtype=np.float32)
    gmat = np.concatenate([np.kron(eye_h, (u <= t).astype(np.float32)),
                           np.kron(eye_h, level2.astype(np.float32))], axis=0)
    group = _gla_group_heads(c)
    masks = np.stack([np.kron(np.eye(group, dtype=np.float32), mk.astype(np.float32)) for mk in masks])
    return jnp.asarray(gmat, BF16), jnp.asarray(masks), levels


def _gla_group_heads(c):
    return min(GLA_HEADS, max(1, LANES // c))


def _gla_level_exponent(b, log_a, level2, m, c):
    if m == 1:
        row = lax.broadcasted_iota(jnp.int32, (c, 1), 0)
        return jnp.where(row % 2 == 1, log_a, 0.0)
    if m == 2:
        return level2
    parts = [jnp.broadcast_to(b[i + m - 1:i + m, :], (2 * m, GLA_HK)) for i in range(0, c, 2 * m)]
    ref = parts[0] if len(parts) == 1 else jnp.concatenate(parts, axis=0)
    return -jnp.abs(b - ref)


GLA_CHUNKS_PER_TRIP = 4


def _gla_body(q_ref, k_ref, v_ref, r_ref, zs_ref, w2_ref, bg_ref, go_ref, s0_ref, gmat_ref, mask_ref,
              og_ref, sout_ref, st_ref, *, c, nchunk, levels, single):
    t = pl.program_id(1)

    if not single:
        @pl.when(t == 0)
        def _():
            for h in range(GLA_HEADS):
                st_ref[h] = jnp.transpose(s0_ref[0, h])

    nrow = GLA_HEADS * c
    gw = _gla_group_heads(c) * c

    def stack(x, width):
        return jnp.concatenate([x[:, h * width:(h + 1) * width] for h in range(GLA_HEADS)], axis=0)

    def diag_blocks(prod):
        return [prod[g:g + gw, g:g + gw] for g in range(0, nrow, gw)]

    def decays(ci):
        rows = pl.ds(0, c) if single else pl.ds(pl.multiple_of(ci * c, c), c)
        q = stack(q_ref[rows, :], GLA_HK).astype(F32) * (GLA_HK ** -0.5)
        k = stack(k_ref[rows, :], GLA_HK).astype(F32)
        v = stack(v_ref[rows, :], GLA_HV)
        x = stack(_dot(zs_ref[rows, :].astype(BF16), w2_ref[...]) + bg_ref[...], GLA_HK) * LOG2E
        log_a = (jnp.minimum(x, 0.0) - jnp.log2(1.0 + jnp.exp2(-jnp.abs(x)))) * (1.0 / GLA_GATE_TAU)
        hi = log_a.astype(BF16)
        lo = (log_a - hi.astype(F32)).astype(BF16)
        gm = gmat_ref[...]
        pre = _dot(gm, hi) + _dot(gm, lo)
        b = pre[0:nrow]
        b_last = [b[h * c + c - 1:h * c + c, :] for h in range(GLA_HEADS)]
        q_in = (q * jnp.exp2(b)).astype(BF16)
        k_out = (k * jnp.exp2(jnp.concatenate([jnp.broadcast_to(r, (c, GLA_HK)) for r in b_last], axis=0)
                              - b)).astype(BF16)
        att = [mask_ref[0] * blk for blk in diag_blocks(_dot_nt(q.astype(BF16), k.astype(BF16)))]
        return dict(rows=rows, q=q, k=k, v=v, log_a=log_a, hi=hi, lo=lo, b=b, level2=pre[nrow:2 * nrow],
                    b_last=b_last, q_in=q_in, k_out=k_out, att=att)

    def level(ch, l):
        d = jnp.exp2(_gla_level_exponent(ch["b"], ch["log_a"], ch["level2"], c >> (l + 1), nrow))
        prod = _dot_nt((ch["q"] * d).astype(BF16), (ch["k"] * d).astype(BF16))
        ch["att"] = [a + mask_ref[1 + l] * blk for a, blk in zip(ch["att"], diag_blocks(prod))]

    def intra(ch):
        att = ch["att"]
        if len(att) == 1:
            att_full = att[0].astype(BF16)
        else:
            zero = jnp.zeros((gw, gw), BF16)
            att_full = jnp.concatenate(
                [jnp.concatenate([a.astype(BF16) if i == j else zero for j in range(len(att))], axis=1)
                 for i, a in enumerate(att)], axis=0)
        ch["o_intra"] = _dot(att_full, ch["v"])

    def carried(ch):
        rows, v, q_in, k_out, o_intra = ch["rows"], ch["v"], ch["q_in"], ch["k_out"], ch["o_intra"]
        for h in range(GLA_HEADS):
            hr = slice(h * c, (h + 1) * c)
            cv = slice(h * GLA_HV, (h + 1) * GLA_HV)
            if single:
                s0 = s0_ref[0, h]
                o = o_intra[hr] + _dot(q_in[hr], s0.astype(BF16))
                ones = jnp.ones((c, LANES), BF16)
                decay = jnp.exp2(_dot_tn(ch["hi"][hr], ones) + _dot_tn(ch["lo"][hr], ones))
                sout_ref[0, h] = (s0 * jnp.concatenate([decay] * (GLA_HV // LANES), axis=1)
                                  + _dot_tn(k_out[hr], v[hr]))
            else:
                st = st_ref[h]
                o = o_intra[hr] + _dot_nt(q_in[hr], st.astype(BF16))
                st_ref[h] = st * jnp.exp2(ch["b_last"][h]) + _dot_tn(v[hr], k_out[hr])
            gate = r_ref[rows, cv].astype(F32)
            og = _row_rms(o) * go_ref[...] * (gate * _sigmoid(gate))
            og_ref[rows, cv] = og.astype(BF16)

    def chunks(first, count):
        group = [decays(first + i) for i in range(count)]
        for l in range(levels):
            for ch in group:
                level(ch, l)
        for ch in group:
            intra(ch)
        for ch in group:
            carried(ch)

    if single:
        chunks(0, 1)
        return
    per_trip = GLA_CHUNKS_PER_TRIP if nchunk % GLA_CHUNKS_PER_TRIP == 0 else 1

    def trip(gi, carry):
        chunks(gi * per_trip, per_trip)
        return carry

    lax.fori_loop(0, nchunk // per_trip, trip, 0)

    @pl.when(t == pl.num_programs(1) - 1)
    def _():
        for h in range(GLA_HEADS):
            sout_ref[0, h] = jnp.transpose(st_ref[h])


def _gla(zm, zs, w2ext, bg, go, s0, batch, seq, c, tb):
    n = zm.shape[0]
    nt = seq // tb
    gmat, masks, levels = _gla_tables(c)
    dk, dv = GLA_HEADS * GLA_HK, GLA_HEADS * GLA_HV
    const2 = lambda b, t: (0, 0)
    return pl.pallas_call(
        functools.partial(_gla_body, c=c, nchunk=tb // c, levels=levels, single=(seq == c)),
        grid=(batch, nt),
        in_specs=[
            pl.BlockSpec((tb, dk), lambda b, t: (b * nt + t, COL_Q // dk)),
            pl.BlockSpec((tb, dk), lambda b, t: (b * nt + t, COL_K // dk)),
            pl.BlockSpec((tb, dv), lambda b, t: (b * nt + t, COL_V // dv)),
            pl.BlockSpec((tb, dv), lambda b, t: (b * nt + t, COL_R // dv)),
            pl.BlockSpec((tb, SMALL_COLS), lambda b, t: (b * nt + t, 0)),
            pl.BlockSpec((SMALL_COLS, dk), const2),
            pl.BlockSpec((1, dk), const2),
            pl.BlockSpec((1, GLA_HV), const2),
            pl.BlockSpec((1, GLA_HEADS, GLA_HK, GLA_HV), lambda b, t: (b, 0, 0, 0)),
            pl.BlockSpec(gmat.shape, const2),
            pl.BlockSpec(masks.shape, lambda b, t: (0, 0, 0)),
        ],
        out_specs=[
            pl.BlockSpec((tb, dv), lambda b, t: (b * nt + t, 0)),
            pl.BlockSpec((1, GLA_HEADS, GLA_HK, GLA_HV), lambda b, t: (b, 0, 0, 0)),
        ],
        out_shape=[
            jax.ShapeDtypeStruct((n, dv), BF16),
            jax.ShapeDtypeStruct((batch, GLA_HEADS, GLA_HK, GLA_HV), F32),
        ],
        scratch_shapes=[pltpu.VMEM((GLA_HEADS, GLA_HV, GLA_HK), F32)],
        compiler_params=_params(("parallel", "arbitrary"), 40),
        name="gla",
    )(zm, zm, zm, zm, zs, w2ext, bg, go, s0, gmat, masks)


def _mla_proj_body(mq_ref, mkv_ref, zs_ref, wqn_ref, wqp_ref, gql_ref, gkl_ref, gq_ref, gqp_ref, gkp_ref,
                   tab_ref, p_ref, q_ref, ckv_ref, kpe_ref, kp2_ref):
    qlat = (_row_rms(mq_ref[...].astype(F32)) * gql_ref[...]).astype(BF16)
    tab = tab_ref[...]
    tab2 = jnp.concatenate([tab, tab], axis=1)
    starts = list(range(0, MLA_HEADS * LANES, 2 * LANES))

    def project(lo):
        cols = slice(lo, lo + 2 * LANES)
        return _dot(qlat, wqn_ref[:, cols]), _dot(qlat, wqp_ref[:, cols])

    cur = project(starts[0])
    for g, lo in enumerate(starts):
        nxt = project(starts[g + 1]) if g + 1 < len(starts) else None
        nope = (_slab_rms(cur[0], p_ref) * (gq_ref[...] * Q_SCALE)).astype(BF16)
        pe = (_slab_rms(cur[1], p_ref) * (gqp_ref[...] * Q_SCALE) * tab2).astype(BF16)
        for i in range(2):
            dst = 2 * lo + i * 2 * LANES
            q_ref[:, dst:dst + LANES] = nope[:, i * LANES:(i + 1) * LANES]
            q_ref[:, dst + LANES:dst + 2 * LANES] = pe[:, i * LANES:(i + 1) * LANES]
        cur = nxt
    ckv_ref[...] = _row_rms(mkv_ref[...].astype(F32)) * gkl_ref[...]
    slab = zs_ref[:, 0:LANES]
    rot = _row_rms(slab) * gkp_ref[...] * tab
    kp2 = rot + pltpu.roll(rot, MLA_ROPE, axis=1)
    kpe_ref[...] = kp2[:, 0:MLA_ROPE]
    kp2_ref[...] = kp2.astype(BF16)


def _mla_proj(zm, zs, wqn, wqp, gql, gkl, gq, gqp, gkp, tab, p128, tm, tab_blocks):
    n = zm.shape[0]
    const = lambda i: (0, 0)
    return pl.pallas_call(
        _mla_proj_body,
        grid=(n // tm,),
        in_specs=[
            pl.BlockSpec((tm, MLA_LORA), lambda i: (i, COL_MQ // MLA_LORA)),
            pl.BlockSpec((tm, MLA_LORA), lambda i: (i, COL_MKV // MLA_LORA)),
            pl.BlockSpec((tm, SMALL_COLS), lambda i: (i, 0)),
            pl.BlockSpec(wqn.shape, const),
            pl.BlockSpec(wqp.shape, const),
            pl.BlockSpec((1, MLA_LORA), const),
            pl.BlockSpec((1, MLA_LORA), const),
            pl.BlockSpec((1, 2 * LANES), const),
            pl.BlockSpec((1, 2 * LANES), const),
            pl.BlockSpec((1, LANES), const),
            pl.BlockSpec((tm, LANES), lambda i: (i % tab_blocks, 0)),
            pl.BlockSpec((2 * LANES, 2 * LANES), const),
        ],
        out_specs=[
            pl.BlockSpec((tm, 2 * LANES * MLA_HEADS), lambda i: (i, 0)),
            pl.BlockSpec((tm, MLA_LORA), lambda i: (i, 0)),
            pl.BlockSpec((tm, MLA_ROPE), lambda i: (i, 0)),
            pl.BlockSpec((tm, LANES), lambda i: (i, 0)),
        ],
        out_shape=[
            jax.ShapeDtypeStruct((n, 2 * LANES * MLA_HEADS), BF16),
            jax.ShapeDtypeStruct((n, MLA_LORA), F32),
            jax.ShapeDtypeStruct((n, MLA_ROPE), F32),
            jax.ShapeDtypeStruct((n, LANES), BF16),
        ],
        compiler_params=_params(("parallel",), 48),
        name="mla_proj",
    )(zm, zm, zs, wqn, wqp, gql, gkl, gq, gqp, gkp, tab, p128)


def _mla_expand_keys(c_ref, wk_ref, gk_ref, p_ref, kn_ref):
    cb = c_ref[...].astype(BF16)
    starts = list(range(0, MLA_HEADS * LANES, 2 * LANES))
    cur = _dot(cb, wk_ref[:, starts[0]:starts[0] + 2 * LANES])
    for g, lo in enumerate(starts):
        nxt = _dot(cb, wk_ref[:, starts[g + 1]:starts[g + 1] + 2 * LANES]) if g + 1 < len(starts) else None
        kn_ref[:, lo:lo + 2 * LANES] = (_slab_rms(cur, p_ref) * gk_ref[...]).astype(BF16)
        cur = nxt
    return cb


def _mla_expand_k_body(c_ref, wk_ref, gk_ref, p_ref, kn_ref):
    _mla_expand_keys(c_ref, wk_ref, gk_ref, p_ref, kn_ref)


def _mla_expand_kv_body(c_ref, wk_ref, gk_ref, p_ref, wvt_ref, kn_ref, vt_ref):
    cb = _mla_expand_keys(c_ref, wk_ref, gk_ref, p_ref, kn_ref)
    vt = _dot_nt(wvt_ref[...], cb).astype(BF16)
    for h in range(MLA_HEADS):
        vt_ref[h, 0] = vt[h * MLA_VDIM:(h + 1) * MLA_VDIM, :]


def _mla_expand(ckv, wk, gk, p128, tm, wvt=None):
    n = ckv.shape[0]
    const = lambda i: (0, 0)
    width = MLA_HEADS * LANES
    in_specs = [
        pl.BlockSpec((tm, MLA_LORA), lambda i: (i, 0)),
        pl.BlockSpec(wk.shape, const),
        pl.BlockSpec((1, 2 * LANES), const),
        pl.BlockSpec((2 * LANES, 2 * LANES), const),
    ]
    out_specs = [pl.BlockSpec((tm, width), lambda i: (i, 0))]
    out_shape = [jax.ShapeDtypeStruct((n, width), BF16)]
    args = [ckv, wk, gk, p128]
    body = _mla_expand_k_body
    if wvt is not None:
        body = _mla_expand_kv_body
        in_specs.append(pl.BlockSpec(wvt.shape, const))
        out_specs.append(pl.BlockSpec((MLA_HEADS, 1, MLA_VDIM, tm), lambda i: (0, i, 0, 0)))
        out_shape.append(jax.ShapeDtypeStruct((MLA_HEADS, n // tm, MLA_VDIM, tm), BF16))
        args.append(wvt)
    return pl.pallas_call(
        body,
        grid=(n // tm,),
        in_specs=in_specs,
        out_specs=out_specs,
        out_shape=out_shape,
        compiler_params=_params(("parallel",), 40),
        name="mla_expand",
    )(*args)


def _attn_prompt_body(qi_tab, kb_tab, q_ref, kn_ref, kp_ref, vt_ref, o_ref,
                      kext_ref, vx_ref, bias_ref, m_ref, acc_ref, s0, s1, p0, p1, a0, a1, *, tq, nq):
    @pl.when((pl.program_id(0) == 0) & (pl.program_id(1) == 0))
    def _():
        kc = lax.broadcasted_iota(jnp.int32, (tq, tq), 0) // CHUNK
        qc = lax.broadcasted_iota(jnp.int32, (tq, tq), 1) // CHUNK
        bias_ref[...] = jnp.where(kc <= qc, 0.0, NEG_BIG)

    kext_ref[:, 0:LANES] = kn_ref[...]
    kext_ref[:, LANES:2 * LANES] = kp_ref[...]
    vx_ref[:, 0:MLA_VDIM, :] = vt_ref[0]
    vx_ref[:, MLA_VDIM:, :] = jnp.ones((nq, vx_ref.shape[1] - MLA_VDIM, tq), BF16)
    m_ref[...] = jnp.full(m_ref.shape, NEG_BIG, F32)
    acc_ref[...] = jnp.zeros(acc_ref.shape, F32)
    nblk = nq * (nq + 1) // 2
    s_bufs, p_bufs, a_bufs = (s0, s1), (p0, p1), (a0, a1)

    def rows(i):
        return pl.ds(pl.multiple_of(i * tq, tq), tq)

    qw = tq // ATTN_QUERY_SPLIT

    def scores(t, par, lo):
        q = q_ref[pl.ds(pl.multiple_of(qi_tab[t] * tq + lo, qw), qw), :]
        s_bufs[par][:, lo:lo + qw] = _dot_nt(kext_ref[rows(kb_tab[t]), :], q)

    def softmax(t, par, diagonal, lo):
        qi = qi_tab[t]
        cs = slice(lo, lo + qw)
        s = s_bufs[par][:, cs]
        if diagonal:
            s = s + bias_ref[:, cs]
        m_old = m_ref[qi, :, cs]
        m_new = jnp.maximum(m_old, jnp.max(s, axis=0, keepdims=True))
        a_bufs[par][:, cs] = jnp.exp2(m_old - m_new)
        m_ref[qi, :, cs] = m_new
        p_bufs[par][:, cs] = jnp.exp2(s - m_new).astype(BF16)

    def values(t, par, lo):
        qi = qi_tab[t]
        cs = slice(lo, lo + qw)
        acc_ref[qi, :, cs] = (a_bufs[par][:, cs] * acc_ref[qi, :, cs]
                              + _dot(vx_ref[kb_tab[t]], p_bufs[par][:, cs]))

    slices = range(0, tq, qw)

    def step(u, par, diagonal):
        for lo in slices:
            scores(u, par, lo)
            softmax(u - 1, 1 - par, diagonal, lo)
            values(u - 2, par, lo)

    def sweep(first, stop, diagonal):
        trips = (stop - first) // ATTN_UNROLL

        def trip(j, carry):
            for i in range(ATTN_UNROLL):
                step(first + ATTN_UNROLL * j + i, i % 2, diagonal)
            return carry

        lax.fori_loop(0, trips, trip, 0)
        for u in range(first + trips * ATTN_UNROLL, stop):
            step(u, u % 2, diagonal)

    for lo in slices:
        scores(0, 0, lo)
    for lo in slices:
        scores(1, 1, lo)
        softmax(0, 0, True, lo)
    sweep(2, nq + 1, True)
    step(nq + 1, 1, False)
    sweep(nq + 2, nblk, False)
    for lo in slices:
        softmax(nblk - 1, 1, False, lo)
        values(nblk - 2, 0, lo)
    for lo in slices:
        values(nblk - 1, 1, lo)
    for qi in range(nq):
        acc = acc_ref[qi]
        o_ref[qi * tq:(qi + 1) * tq, :] = jnp.transpose(
            acc[0:MLA_VDIM] / acc[MLA_VDIM:MLA_VDIM + 1]).astype(BF16)


def _attn_prompt(qext, kn, kp2, vt, batch, seq, tq):
    n = qext.shape[0]
    nq = seq // tq
    assert nq % 2 == 0
    pairs = [(i, i) for i in range(nq)] + [(qi, kb) for qi in range(nq) for kb in range(qi)]
    qi_tab = jnp.asarray([p[0] for p in pairs], jnp.int32)
    kb_tab = jnp.asarray([p[1] for p in pairs], jnp.int32)
    ones_rows = 16
    grid_spec = pltpu.PrefetchScalarGridSpec(
        num_scalar_prefetch=2,
        grid=(batch, MLA_HEADS),
        in_specs=[
            pl.BlockSpec((seq, 2 * LANES), lambda b, h, *_: (b, h)),
            pl.BlockSpec((seq, LANES), lambda b, h, *_: (b, h)),
            pl.BlockSpec((seq, LANES), lambda b, h, *_: (b, 0)),
            pl.BlockSpec((1, nq, MLA_VDIM, tq), lambda b, h, *_: (h, b, 0, 0)),
        ],
        out_specs=pl.BlockSpec((seq, MLA_VDIM), lambda b, h, *_: (b, h)),
        scratch_shapes=[
            pltpu.VMEM((seq, 2 * LANES), BF16),
            pltpu.VMEM((nq, MLA_VDIM + ones_rows, tq), BF16),
            pltpu.VMEM((tq, tq), F32),
            pltpu.VMEM((nq, 1, tq), F32),
            pltpu.VMEM((nq, MLA_VDIM + ones_rows, tq), F32),
            pltpu.VMEM((tq, tq), F32), pltpu.VMEM((tq, tq), F32),
            pltpu.VMEM((tq, tq), BF16), pltpu.VMEM((tq, tq), BF16),
            pltpu.VMEM((1, tq), F32), pltpu.VMEM((1, tq), F32),
        ],
    )
    return pl.pallas_call(
        functools.partial(_attn_prompt_body, tq=tq, nq=nq),
        grid_spec=grid_spec,
        out_shape=jax.ShapeDtypeStruct((n, MLA_HEADS * MLA_VDIM), BF16),
        compiler_params=_params(("arbitrary", "arbitrary"), 40),
        name="attn_prompt",
    )(qi_tab, kb_tab, qext, kn, kp2, vt)


SAMPLE_STREAMS_PER_STEP = 2


def _attn_sample_body(q_ref, hm_ref, knp_ref, kpp_ref, cp_ref, knn_ref, kpn_ref, cn_ref, wv_ref, o_ref, *, seq):
    past = knp_ref.shape[0] // SAMPLE_STREAMS_PER_STEP

    def with_ones(c):
        return jnp.concatenate([c.astype(BF16), jnp.ones((c.shape[0], LANES), BF16)], axis=1)

    streams = range(SAMPLE_STREAMS_PER_STEP)
    new = [slice(s * seq, (s + 1) * seq) for s in streams]
    old = [slice(s * past, (s + 1) * past) for s in streams]

    def scores(s):
        q = q_ref[new[s], :]
        qn = jnp.concatenate([q[:, 2 * h * LANES:(2 * h + 1) * LANES] for h in range(MLA_HEADS)], axis=1)
        qbd = jnp.concatenate([qn] * MLA_HEADS, axis=0) * hm_ref[...]
        qpe = jnp.concatenate([q[:, (2 * h + 1) * LANES:(2 * h + 2) * LANES] for h in range(MLA_HEADS)], axis=0)
        s_past = _dot_nt(knp_ref[old[s], :], qbd) + _dot_nt(kpp_ref[old[s], :], qpe)
        s_new = _dot_nt(knn_ref[new[s], :], qbd) + _dot_nt(kpn_ref[new[s], :], qpe)
        return s_past, s_new

    def latents(s, s_past, s_new):
        m = jnp.maximum(jnp.max(s_past, axis=0, keepdims=True), jnp.max(s_new, axis=0, keepdims=True))
        p_past = jnp.exp2(s_past - m).astype(BF16)
        p_new = jnp.exp2(s_new - m).astype(BF16)
        acc = (_dot_tn(p_past, with_ones(cp_ref[old[s], :]))
               + _dot_tn(p_new, with_ones(cn_ref[new[s], :])))
        return (acc[:, 0:MLA_LORA] / acc[:, MLA_LORA:MLA_LORA + 1]).astype(BF16)

    def outputs(s, lat):
        for h in range(MLA_HEADS):
            cols = slice(h * MLA_VDIM, (h + 1) * MLA_VDIM)
            o_ref[new[s], cols] = _dot(lat[h * seq:(h + 1) * seq], wv_ref[:, cols]).astype(BF16)

    sc = [scores(s) for s in streams]
    lat = [latents(s, *sc[s]) for s in streams]
    for s in streams:
        outputs(s, lat[s])


def _attn_sample(qext, kn_past, kp2_past, ckv_past, kn_new, kp2_new, ckv_new, wv, batch, past, seq):
    n = qext.shape[0]
    width = MLA_HEADS * LANES
    head_mask = jnp.asarray(np.kron(np.eye(MLA_HEADS, dtype=np.float32), np.ones((seq, LANES), np.float32)), BF16)
    group = SAMPLE_STREAMS_PER_STEP
    assert batch % group == 0
    seq, past = group * seq, group * past
    stream = lambda b: (b, 0)
    const = lambda b: (0, 0)
    return pl.pallas_call(
        functools.partial(_attn_sample_body, seq=seq // group),
        grid=(batch // group,),
        in_specs=[
            pl.BlockSpec((seq, 2 * width), stream),
            pl.BlockSpec(head_mask.shape, const),
            pl.BlockSpec((past, width), stream),
            pl.BlockSpec((past, LANES), stream),
            pl.BlockSpec((past, MLA_LORA), stream),
            pl.BlockSpec((seq, width), stream),
            pl.BlockSpec((seq, LANES), stream),
            pl.BlockSpec((seq, MLA_LORA), stream),
            pl.BlockSpec(wv.shape, const),
        ],
        out_specs=pl.BlockSpec((seq, MLA_HEADS * MLA_VDIM), stream),
        out_shape=jax.ShapeDtypeStruct((n, MLA_HEADS * MLA_VDIM), BF16),
        compiler_params=_params(("parallel",), 48),
        name="attn_sample",
    )(qext, head_mask, kn_past, kp2_past, ckv_past, kn_new, kp2_new, ckv_new, wv)


MERGE_COL_CHUNK = 256


def _branch_merge_body(og_ref, at_ref, wg_ref, wm_ref, ga_ref, gb_ref, u_ref):
    og = og_ref[...]
    at = at_ref[...]
    for lo in range(0, u_ref.shape[1], MERGE_COL_CHUNK):
        cols = slice(lo, lo + MERGE_COL_CHUNK)
        u = (_sigmoid(ga_ref[:, cols].astype(F32)) * _dot(og, wg_ref[:, cols])
             + _sigmoid(gb_ref[:, cols].astype(F32)) * _dot(at, wm_ref[:, cols]))
        u_ref[:, cols] = u.astype(BF16)


def _out_proj_body(u_ref, wo_ref, x_ref, g2_ref, x1_ref, h2_ref):
    x1 = x_ref[...] + _dot(u_ref[...], wo_ref[...])
    x1_ref[...] = x1
    h2_ref[...] = (_row_rms(x1) * g2_ref[...]).astype(BF16)


def _merge(og, att, wg, wm, zm, wo, x, g2, tm_u, tm_x):
    n = x.shape[0]
    tn = 512
    u = pl.pallas_call(
        _branch_merge_body,
        grid=(n // tm_u, D_MODEL // tn),
        in_specs=[
            pl.BlockSpec((tm_u, D_MODEL), lambda i, j: (i, 0)),
            pl.BlockSpec((tm_u, D_MODEL), lambda i, j: (i, 0)),
            pl.BlockSpec((D_MODEL, tn), lambda i, j: (0, j)),
            pl.BlockSpec((D_MODEL, tn), lambda i, j: (0, j)),
            pl.BlockSpec((tm_u, tn), lambda i, j: (i, COL_GA // tn + j)),
            pl.BlockSpec((tm_u, tn), lambda i, j: (i, COL_GB // tn + j)),
        ],
        out_specs=pl.BlockSpec((tm_u, tn), lambda i, j: (i, j)),
        out_shape=jax.ShapeDtypeStruct((n, D_MODEL), BF16),
        compiler_params=_params(("parallel", "parallel"), 48),
        name="branch_merge",
    )(og, att, wg, wm, zm, zm)
    return pl.pallas_call(
        _out_proj_body,
        grid=(n // tm_x,),
        in_specs=[
            pl.BlockSpec((tm_x, D_MODEL), lambda i: (i, 0)),
            pl.BlockSpec((D_MODEL, D_MODEL), lambda i: (0, 0)),
            pl.BlockSpec((tm_x, D_MODEL), lambda i: (i, 0)),
            pl.BlockSpec((1, D_MODEL), lambda i: (0, 0)),
        ],
        out_specs=[
            pl.BlockSpec((tm_x, D_MODEL), lambda i: (i, 0)),
            pl.BlockSpec((tm_x, D_MODEL), lambda i: (i, 0)),
        ],
        out_shape=[
            jax.ShapeDtypeStruct((n, D_MODEL), F32),
            jax.ShapeDtypeStruct((n, D_MODEL), BF16),
        ],
        compiler_params=_params(("parallel",), 56),
        name="out_proj",
    )(u, wo, x, g2)


FFN_COL_CHUNK = 256
FFN_ROW_PARTS = 1


def _gelu_gate(a, a1, a2, cw, gt):
    c = cw[3:4, :] + cw[2:3, :] * a + cw[0:1, :] * a2 + cw[1:2, :] * a1
    return 0.5 * c * (1.0 + lax.erf(c * (2.0 ** -0.5))) * gt


def _ffn_up_seq_body(h_ref, wa_ref, wg_ref, cw_ref, hist_ref, act_ref, tail_ref, carry_ref, *, tm, tiles_per_seq):
    i = pl.program_id(0)
    j = pl.program_id(1)

    @pl.when(i % tiles_per_seq == 0)
    def _():
        carry_ref[j] = hist_ref[0]

    tr = tm // FFN_ROW_PARTS
    row = lax.broadcasted_iota(jnp.int32, (tr, 1), 0)
    pieces = [(lo, r0) for lo in range(0, act_ref.shape[1], FFN_COL_CHUNK) for r0 in range(0, tm, tr)]

    def project(lo, r0):
        h = h_ref[r0:r0 + tr, :]
        return _dot(h, wa_ref[:, lo:lo + FFN_COL_CHUNK]), _dot(h, wg_ref[:, lo:lo + FFN_COL_CHUNK])

    nxt = project(*pieces[0])
    prev = None
    for g, (lo, r0) in enumerate(pieces):
        cols = slice(lo, lo + FFN_COL_CHUNK)
        a, gt = nxt
        nxt = project(*pieces[g + 1]) if g + 1 < len(pieces) else None
        if r0 == 0:
            prev = carry_ref[j, :, cols]
        a1 = jnp.where(row == 0, prev[7:8, :], pltpu.roll(a, 1, axis=0))
        a2 = jnp.where(row == 0, prev[6:7, :], jnp.where(row == 1, prev[7:8, :], pltpu.roll(a, 2, axis=0)))
        act_ref[r0:r0 + tr, cols] = _gelu_gate(a, a1, a2, cw_ref[:, cols], gt).astype(BF16)
        prev = a[tr - 8:tr, :]
        if r0 + tr == tm:
            carry_ref[j, :, cols] = prev
            tail_ref[0, :, cols] = a[tr - (CONV_W - 1):tr, :]


def _ffn_up_seq(h2, wa, wg, cw, hist8, seq, tm):
    n = h2.shape[0]
    tn = 512
    nj = D_FF // tn
    tps = seq // tm
    return pl.pallas_call(
        functools.partial(_ffn_up_seq_body, tm=tm, tiles_per_seq=tps),
        grid=(n // tm, nj),
        in_specs=[
            pl.BlockSpec((tm, D_MODEL), lambda i, j: (i, 0)),
            pl.BlockSpec((D_MODEL, tn), lambda i, j: (0, j)),
            pl.BlockSpec((D_MODEL, tn), lambda i, j: (0, j)),
            pl.BlockSpec((8, tn), lambda i, j: (0, j)),
            pl.BlockSpec((1, 8, tn), lambda i, j: (i // tps, 0, j)),
        ],
        out_specs=[
            pl.BlockSpec((tm, tn), lambda i, j: (i, j)),
            pl.BlockSpec((1, CONV_W - 1, tn), lambda i, j: (i, 0, j)),
        ],
        out_shape=[
            jax.ShapeDtypeStruct((n, D_FF), BF16),
            jax.ShapeDtypeStruct((n // tm, CONV_W - 1, D_FF), F32),
        ],
        scratch_shapes=[pltpu.VMEM((nj, 8, tn), F32)],
        compiler_params=_params(("arbitrary", "arbitrary"), 48),
        name="ffn_up_seq",
    )(h2, wa, wg, cw, hist8)


def _ffn_up_multi_body(h_ref, wa_ref, wg_ref, cw_ref, p1_ref, p2_ref, act_ref, a_ref, wab_ref, wgb_ref, *, tm, seq):
    wa = wa_ref[...].astype(BF16)
    wg = wg_ref[...].astype(BF16)
    wab_ref[...] = wa
    wgb_ref[...] = wg
    a = _dot(h_ref[...], wa)
    gt = _dot(h_ref[...], wg)
    pos = lax.broadcasted_iota(jnp.int32, (tm, 1), 0) % seq
    a1 = jnp.where(pos == 0, p1_ref[...], pltpu.roll(a, 1, axis=0))
    a2 = jnp.where(pos <= 1, p2_ref[...], pltpu.roll(a, 2, axis=0))
    act_ref[...] = _gelu_gate(a, a1, a2, cw_ref[...], gt).astype(BF16)
    a_ref[...] = a


def _ffn_up_multi(h2, wup, cw, p1, p2, seq):
    n = h2.shape[0]
    tn = 512
    nj = D_FF // tn
    col = lambda j: (0, j)
    return pl.pallas_call(
        functools.partial(_ffn_up_multi_body, tm=n, seq=seq),
        grid=(nj,),
        in_specs=[
            pl.BlockSpec((n, D_MODEL), lambda j: (0, 0)),
            pl.BlockSpec((D_MODEL, tn), col),
            pl.BlockSpec((D_MODEL, tn), lambda j: (0, nj + j)),
            pl.BlockSpec((8, tn), col),
            pl.BlockSpec((n, tn), col),
            pl.BlockSpec((n, tn), col),
        ],
        out_specs=[pl.BlockSpec((n, tn), col)] * 2 + [pl.BlockSpec((D_MODEL, tn), col)] * 2,
        out_shape=[jax.ShapeDtypeStruct((n, D_FF), BF16), jax.ShapeDtypeStruct((n, D_FF), F32)]
        + [jax.ShapeDtypeStruct((D_MODEL, D_FF), BF16)] * 2,
        compiler_params=_params(("parallel",), 48),
        name="ffn_up_multi",
    )(h2, wup, wup, cw, p1, p2)


def _ffn_down_body(act_ref, wd_ref, x1_ref, o_ref):
    o_ref[...] = x1_ref[...] + _dot(act_ref[...], wd_ref[...])


def _ffn_down_cast_body(act_ref, wd_ref, x1_ref, o_ref, wdb_ref):
    wd = wd_ref[...].astype(BF16)
    wdb_ref[...] = wd
    o_ref[...] = x1_ref[...] + _dot(act_ref[...], wd)


def _ffn_down(act, wd, x1, tm, emit_bf16_weights=False):
    n = act.shape[0]
    tn = 512
    out_specs = [pl.BlockSpec((tm, tn), lambda i, j: (i, j))]
    out_shape = [jax.ShapeDtypeStruct((n, D_MODEL), F32)]
    if emit_bf16_weights:
        assert n == tm
        out_specs.append(pl.BlockSpec((D_FF, tn), lambda i, j: (0, j)))
        out_shape.append(jax.ShapeDtypeStruct((D_FF, D_MODEL), BF16))
    return pl.pallas_call(
        _ffn_down_cast_body if emit_bf16_weights else _ffn_down_body,
        grid=(n // tm, D_MODEL // tn),
        in_specs=[
            pl.BlockSpec((tm, D_FF), lambda i, j: (i, 0)),
            pl.BlockSpec((D_FF, tn), lambda i, j: (0, j)),
            pl.BlockSpec((tm, tn), lambda i, j: (i, j)),
        ],
        out_specs=out_specs,
        out_shape=out_shape,
        compiler_params=_params(("parallel", "parallel"), 56),
        name="ffn_down",
    )(act, wd, x1)


def _swap_halves(w):
    half = w.shape[-1] // 2
    return jnp.concatenate([w[..., half:], w[..., :half]], axis=-1)


def _rope_table(pos):
    half = MLA_ROPE // 2
    inv = ROPE_THETA ** (-jnp.arange(half, dtype=F32) * 2.0 / MLA_ROPE)
    ang = pos.astype(F32)[:, None] * inv[None, :]
    cos, sin = jnp.cos(ang), jnp.sin(ang)
    return jnp.concatenate([cos, cos, -sin, sin], axis=-1)


def _layer_weights(w_in, g_norm1, gla_w_gate2, gla_b_gate, gla_g_out, w_br_gla, mla_g_qlat, mla_w_uq,
                   mla_g_kvlat, mla_w_ukv, mla_g_q, mla_g_k, mla_g_qpe, mla_g_kpe, w_br_mla, w_out,
                   g_norm2, ffn_w_up, ffn_conv_w, ffn_conv_b, ffn_w_down):
    w_main, w_narrow = _regroup_w_in(w_in.T)
    w_lr = w_narrow[0:GLA_GATE_RANK]
    w_kpe = w_narrow[GLA_GATE_RANK:GLA_GATE_RANK + MLA_ROPE]
    half = MLA_ROPE // 2
    w_small = jnp.concatenate(
        [w_kpe, w_kpe[half:], w_kpe[:half], w_lr,
         jnp.zeros((SMALL_COLS - SMALL_LR - GLA_GATE_RANK, D_MODEL), F32)], axis=0).astype(BF16)
    w2ext = jnp.zeros((SMALL_COLS, GLA_HEADS * GLA_HK), F32).at[SMALL_LR:SMALL_LR + GLA_GATE_RANK].set(
        gla_w_gate2).astype(BF16)
    wq = mla_w_uq.reshape(MLA_LORA, MLA_HEADS, MLA_NOPE + MLA_ROPE)
    wqn = wq[:, :, :MLA_NOPE].reshape(MLA_LORA, MLA_HEADS * MLA_NOPE).astype(BF16)
    wq_pe = wq[:, :, MLA_NOPE:]
    wqp = jnp.concatenate([wq_pe, _swap_halves(wq_pe)], axis=-1).reshape(MLA_LORA, MLA_HEADS * LANES).astype(BF16)
    wkv = mla_w_ukv.reshape(MLA_LORA, MLA_HEADS, MLA_NOPE + MLA_VDIM)
    wk = wkv[:, :, :MLA_NOPE].reshape(MLA_LORA, MLA_HEADS * MLA_NOPE).astype(BF16)
    wv = wkv[:, :, MLA_NOPE:].reshape(MLA_LORA, MLA_HEADS * MLA_VDIM).astype(BF16)
    cw = jnp.concatenate([ffn_conv_w, ffn_conv_b[None, :], jnp.zeros((8 - CONV_W - 1, D_FF), F32)], axis=0)
    return dict(
        w_main=w_main, w_small=w_small, g1=g_norm1[None, :], w2ext=w2ext, bg=gla_b_gate[None, :],
        go=gla_g_out[None, :], wg=w_br_gla.astype(BF16), wqn=wqn, wqp=wqp, gql=mla_g_qlat[None, :],
        gkl=mla_g_kvlat[None, :], gq=jnp.tile(mla_g_q, 2)[None, :],
        gqp=jnp.tile(jnp.concatenate([mla_g_qpe, _swap_halves(mla_g_qpe)]), 2)[None, :],
        gkp=jnp.concatenate([mla_g_kpe, _swap_halves(mla_g_kpe)])[None, :],
        wk=wk, wv=wv, wvt=wv.T, gk=jnp.tile(mla_g_k, 2)[None, :], wm=w_br_mla.astype(BF16),
        wo=w_out.astype(BF16),
        g2=g_norm2[None, :], wup=ffn_w_up, cw=cw, wd=ffn_w_down,
        p128=jnp.asarray(np.kron(np.eye(2, dtype=np.float32), np.full((LANES, LANES), 1.0 / LANES, np.float32)), BF16),
    )


def _trunk_front(x, w, s0, batch, seq, chunk, gla_tb, tm_in, tm, tab, tab_blocks, with_values):
    zm, zs = _in_proj(x, w["g1"], w["w_main"], w["w_small"], tm_in)
    og, s_new = _gla(zm, zs, w["w2ext"], w["bg"], w["go"], s0, batch, seq, chunk, gla_tb)
    qext, ckv, kpe, kp2 = _mla_proj(zm, zs, w["wqn"], w["wqp"], w["gql"], w["gkl"], w["gq"], w["gqp"],
                                    w["gkp"], tab, w["p128"], tm, tab_blocks)
    kv = _mla_expand(ckv, w["wk"], w["gk"], w["p128"], tm, w["wvt"] if with_values else None)
    return zm, og, s_new, qext, ckv, kpe, kp2, kv


def kernel(x_prompt, x_sample, state_gla, cache_mla_ckv, cache_mla_kpe, cache_ffn_conv, w_in, g_norm1, gla_w_gate2, gla_b_gate, gla_g_out, w_br_gla, mla_g_qlat, mla_w_uq, mla_g_kvlat, mla_w_ukv, mla_g_q, mla_g_k, mla_g_qpe, mla_g_kpe, w_br_mla, w_out, g_norm2, ffn_w_up, ffn_conv_w, ffn_conv_b, ffn_w_down):
    bp, tp, _ = x_prompt.shape
    bs, ts, _ = x_sample.shape
    depth = w_in.shape[0]
    past = cache_mla_ckv.shape[2]
    np_rows, ns_rows = bp * tp, bs * ts
    tm_p = 512
    tm_big = 1024
    tab_p = _rope_table(jnp.arange(tp))
    tab_s = jnp.tile(_rope_table(past + jnp.arange(ts)), (bs, 1))
    xp = x_prompt.reshape(np_rows, D_MODEL)
    xs = x_sample.reshape(ns_rows, D_MODEL)
    outs = [[] for _ in range(8)]
    layer_weights = (w_in, g_norm1, gla_w_gate2, gla_b_gate, gla_g_out, w_br_gla, mla_g_qlat, mla_w_uq,
                     mla_g_kvlat, mla_w_ukv, mla_g_q, mla_g_k, mla_g_qpe, mla_g_kpe, w_br_mla, w_out,
                     g_norm2, ffn_w_up, ffn_conv_w, ffn_conv_b, ffn_w_down)
    for l in range(depth):
        w = _layer_weights(*[a[l] for a in layer_weights])

        zm, og, ss, qext, ckv_s, kpe_s, kp2, (kn,) = _trunk_front(
            xs, w, state_gla[l], bs, ts, ts, ts, ns_rows, ns_rows, tab_s, 1, False)
        ckv_past = cache_mla_ckv[l].reshape(bs * past, MLA_LORA)
        (kn_past,) = _mla_expand(ckv_past, w["wk"], w["gk"], w["p128"], 512)
        kpe_past = cache_mla_kpe[l].reshape(bs * past, MLA_ROPE)
        kp2_past = jnp.concatenate([kpe_past, kpe_past], axis=-1).astype(BF16)
        att = _attn_sample(qext, kn_past, kp2_past, ckv_past, kn, kp2, ckv_s, w["wv"], bs, past, ts)
        x1, h2 = _merge(og, att, w["wg"], w["wm"], zm, w["wo"], xs, w["g2"], ns_rows, ns_rows)
        hist = cache_ffn_conv[l]
        zrow = jnp.zeros((bs, ts - 1, D_FF), F32)
        p1 = jnp.concatenate([hist[:, 1:2], zrow], axis=1).reshape(ns_rows, D_FF)
        p2 = jnp.concatenate([hist, zrow[:, 1:]], axis=1).reshape(ns_rows, D_FF)
        act, a_full, wa_bf, wg_bf = _ffn_up_multi(h2, w["wup"], w["cw"], p1, p2, ts)
        fs = a_full.reshape(bs, ts, D_FF)[:, ts - (CONV_W - 1):]
        xs, wd_bf = _ffn_down(act, w["wd"], x1, ns_rows, emit_bf16_weights=True)

        s0 = jnp.zeros((bp, GLA_HEADS, GLA_HK, GLA_HV), F32)
        zm, og, sp, qext, ckv_p, kpe_p, kp2, (kn, vt) = _trunk_front(
            xp, w, s0, bp, tp, CHUNK, 512, tm_big, tm_p, tab_p, tp // tm_p, True)
        att = _attn_prompt(qext, kn, kp2, vt, bp, tp, 512)
        x1, h2 = _merge(og, att, w["wg"], w["wm"], zm, w["wo"], xp, w["g2"], tm_big, tm_p)
        hist8 = jnp.zeros((bp, 8, D_FF), F32)
        act, tails = _ffn_up_seq(h2, wa_bf, wg_bf, w["cw"], hist8, tp, tm_big)
        fp = tails.reshape(bp, tp // tm_big, CONV_W - 1, D_FF)[:, -1]
        (xp,) = _ffn_down(act, wd_bf, x1, tm_big)

        for lst, val in zip(outs, (sp, ss, ckv_p.reshape(bp, tp, MLA_LORA), ckv_s.reshape(bs, ts, MLA_LORA),
                                   kpe_p.reshape(bp, tp, MLA_ROPE), kpe_s.reshape(bs, ts, MLA_ROPE), fp, fs)):
            lst.append(val)
    return (xp.reshape(bp, tp, D_MODEL), xs.reshape(bs, ts, D_MODEL)) + tuple(jnp.stack(o, 0) for o in outs)
```

```python
import functools

import numpy as np
import jax
import jax.numpy as jnp
from jax import lax
from jax.experimental import pallas as pl
from jax.experimental.pallas import tpu as pltpu

F32 = jnp.float32
BF16 = jnp.bfloat16

D_MODEL = 2048
CHUNK = 64
EPS = 1e-6
GLA_HEADS = 4
GLA_HK = 256
GLA_HV = 512
GLA_GATE_RANK = 16
GLA_GATE_TAU = 16.0
MLA_HEADS = 16
MLA_LORA = 512
MLA_NOPE = 128
MLA_ROPE = 64
MLA_VDIM = 128
MLA_SCALE = (MLA_NOPE + MLA_ROPE) ** -0.5
LOG2E = float(np.log2(np.e))
Q_SCALE = MLA_SCALE * LOG2E
ROPE_THETA = 10000.0
D_FF = 5632
CONV_W = 3
LANES = 128
NEG_BIG = -1e30
ATTN_UNROLL = 8
ATTN_QUERY_SPLIT = 2

COL_Q, COL_K, COL_V, COL_R, COL_MQ, COL_MKV, COL_GA, COL_GB, MAIN_COLS = (
    0, 1024, 2048, 4096, 6144, 6656, 7168, 9216, 11264)
SMALL_COLS = 256
SMALL_LR = 128

MIB = 1024 * 1024


def _params(semantics, vmem_mib):
    return pltpu.CompilerParams(dimension_semantics=semantics, vmem_limit_bytes=vmem_mib * MIB)


def _dot(a, b):
    return jnp.dot(a, b, preferred_element_type=F32)


def _dot_nt(a, b):
    return lax.dot_general(a, b, (((1,), (1,)), ((), ())), preferred_element_type=F32)


def _dot_tn(a, b):
    return lax.dot_general(a, b, (((0,), (0,)), ((), ())), preferred_element_type=F32)


def _sigmoid(x):
    return 1.0 / (1.0 + jnp.exp(-x))


def _row_rms(x):
    return x * lax.rsqrt(jnp.mean(x * x, axis=-1, keepdims=True) + EPS)


def _slab_rms(x, p_ref):
    ms = _dot((x * x).astype(BF16), p_ref[...])
    return x * lax.rsqrt(ms + EPS)


W_IN_SEGMENTS = ((0, COL_R, 0), (COL_R, COL_GA, GLA_GATE_RANK), (COL_GA, MAIN_COLS, GLA_GATE_RANK + MLA_ROPE))


def _regroup_body(main_ref, next_ref, o_ref, narrow_ref, *, tr):
    j = pl.program_id(0)

    @pl.when(j == 0)
    def _():
        narrow_ref[...] = jnp.zeros(narrow_ref.shape, F32)

    prev = 0
    for lo, hi, shift in W_IN_SEGMENTS:
        if shift > prev:
            @pl.when(j == lo // tr)
            def _(prev=prev, shift=shift):
                narrow_ref[prev:shift, :] = main_ref[prev:shift, :]

        @pl.when((j >= lo // tr) & (j < hi // tr))
        def _(shift=shift):
            if shift == 0:
                o_ref[...] = main_ref[...].astype(BF16)
            else:
                o_ref[0:tr - shift, :] = main_ref[shift:tr, :].astype(BF16)
                o_ref[tr - shift:tr, :] = next_ref[0:shift, :].astype(BF16)
        prev = shift


def _regroup_w_in(w_in_t):
    tr = 1024
    assert all(lo % tr == 0 and hi % tr == 0 and shift % 16 == 0 and shift < LANES
               for lo, hi, shift in W_IN_SEGMENTS)
    return pl.pallas_call(
        functools.partial(_regroup_body, tr=tr),
        grid=(MAIN_COLS // tr,),
        in_specs=[
            pl.BlockSpec((tr, D_MODEL), lambda j: (j, 0)),
            pl.BlockSpec((LANES, D_MODEL), lambda j: ((j + 1) * (tr // LANES), 0)),
        ],
        out_specs=[
            pl.BlockSpec((tr, D_MODEL), lambda j: (j, 0)),
            pl.BlockSpec((LANES, D_MODEL), lambda j: (0, 0)),
        ],
        out_shape=[
            jax.ShapeDtypeStruct((MAIN_COLS, D_MODEL), BF16),
            jax.ShapeDtypeStruct((LANES, D_MODEL), F32),
        ],
        compiler_params=_params(("arbitrary",), 48),
        name="regroup_w_in",
    )(w_in_t, w_in_t)


def _in_proj_body(x_ref, g_ref, wm_ref, ws_ref, zm_ref, zs_ref, h_ref):
    @pl.when(pl.program_id(1) == 0)
    def _():
        h_ref[...] = (_row_rms(x_ref[...]) * g_ref[...]).astype(BF16)
        zs_ref[...] = _dot_nt(h_ref[...], ws_ref[...])

    zm_ref[...] = _dot_nt(h_ref[...], wm_ref[...]).astype(BF16)


def _in_proj(x, g, wm, ws, tm):
    n = x.shape[0]
    tn = 1024
    return pl.pallas_call(
        _in_proj_body,
        grid=(n // tm, MAIN_COLS // tn),
        in_specs=[
            pl.BlockSpec((tm, D_MODEL), lambda i, j: (i, 0)),
            pl.BlockSpec((1, D_MODEL), lambda i, j: (0, 0)),
            pl.BlockSpec((tn, D_MODEL), lambda i, j: (j, 0)),
            pl.BlockSpec((SMALL_COLS, D_MODEL), lambda i, j: (0, 0)),
        ],
        out_specs=[
            pl.BlockSpec((tm, tn), lambda i, j: (i, j)),
            pl.BlockSpec((tm, SMALL_COLS), lambda i, j: (i, 0)),
        ],
        out_shape=[
            jax.ShapeDtypeStruct((n, MAIN_COLS), BF16),
            jax.ShapeDtypeStruct((n, SMALL_COLS), F32),
        ],
        scratch_shapes=[pltpu.VMEM((tm, D_MODEL), BF16)],
        compiler_params=_params(("parallel", "arbitrary"), 56),
        name="in_proj",
    )(x, g, wm, ws)


def _gla_tables(c):
    levels = int(np.log2(c))
    t = np.arange(c)[:, None]
    u = np.arange(c)[None, :]
    masks = [(u == t)]
    level2 = None
    for l in range(levels):
        m = c >> (l + 1)
        mid_t = (t // (2 * m)) * 2 * m + m
        upper = t >= mid_t
        if m == 2:
            level2 = np.where(upper, (u >= mid_t) & (u <= t), (u > t) & (u < mid_t))
        mid_u = (u // (2 * m)) * 2 * m + m
        masks.append((t // (2 * m) == u // (2 * m)) & upper & (u < mid_u))
    eye_h = np.eye(GLA_HEADS, dtype=np.float32)
    gmat = np.concatenate([np.kron(eye_h, (u <= t).astype(np.float32)),
                           np.kron(eye_h, level2.astype(np.float32))], axis=0)
    group = _gla_group_heads(c)
    masks = np.stack([np.kron(np.eye(group, dtype=np.float32), mk.astype(np.float32)) for mk in masks])
    return jnp.asarray(gmat, BF16), jnp.asarray(masks), levels


def _gla_group_heads(c):
    return min(GLA_HEADS, max(1, LANES // c))


def _gla_level_exponent(b, log_a, level2, m, c):
    if m == 1:
        row = lax.broadcasted_iota(jnp.int32, (c, 1), 0)
        return jnp.where(row % 2 == 1, log_a, 0.0)
    if m == 2:
        return level2
    parts = [jnp.broadcast_to(b[i + m - 1:i + m, :], (2 * m, GLA_HK)) for i in range(0, c, 2 * m)]
    ref = parts[0] if len(parts) == 1 else jnp.concatenate(parts, axis=0)
    return -jnp.abs(b - ref)


GLA_CHUNKS_PER_TRIP = 4


def _gla_body(q_ref, k_ref, v_ref, r_ref, zs_ref, w2_ref, bg_ref, go_ref, s0_ref, gmat_ref, mask_ref,
              og_ref, sout_ref, st_ref, *, c, nchunk, levels, single):
    t = pl.program_id(1)

    if not single:
        @pl.when(t == 0)
        def _():
            for h in range(GLA_HEADS):
                st_ref[h] = jnp.transpose(s0_ref[0, h])

    nrow = GLA_HEADS * c
    gw = _gla_group_heads(c) * c

    def stack(x, width):
        return jnp.concatenate([x[:, h * width:(h + 1) * width] for h in range(GLA_HEADS)], axis=0)

    def diag_blocks(prod):
        return [prod[g:g + gw, g:g + gw] for g in range(0, nrow, gw)]

    def decays(ci):
        rows = pl.ds(0, c) if single else pl.ds(pl.multiple_of(ci * c, c), c)
        q = stack(q_ref[rows, :], GLA_HK).astype(F32) * (GLA_HK ** -0.5)
        k = stack(k_ref[rows, :], GLA_HK).astype(F32)
        v = stack(v_ref[rows, :], GLA_HV)
        x = stack(_dot(zs_ref[rows, :].astype(BF16), w2_ref[...]) + bg_ref[...], GLA_HK) * LOG2E
        log_a = (jnp.minimum(x, 0.0) - jnp.log2(1.0 + jnp.exp2(-jnp.abs(x)))) * (1.0 / GLA_GATE_TAU)
        hi = log_a.astype(BF16)
        lo = (log_a - hi.astype(F32)).astype(BF16)
        gm = gmat_ref[...]
        pre = _dot(gm, hi) + _dot(gm, lo)
        b = pre[0:nrow]
        b_last = [b[h * c + c - 1:h * c + c, :] for h in range(GLA_HEADS)]
        q_in = (q * jnp.exp2(b)).astype(BF16)
        k_out = (k * jnp.exp2(jnp.concatenate([jnp.broadcast_to(r, (c, GLA_HK)) for r in b_last], axis=0)
                              - b)).astype(BF16)
        att = [mask_ref[0] * blk for blk in diag_blocks(_dot_nt(q.astype(BF16), k.astype(BF16)))]
        return dict(rows=rows, q=q, k=k, v=v, log_a=log_a, hi=hi, lo=lo, b=b, level2=pre[nrow:2 * nrow],
                    b_last=b_last, q_in=q_in, k_out=k_out, att=att)

    def level(ch, l):
        d = jnp.exp2(_gla_level_exponent(ch["b"], ch["log_a"], ch["level2"], c >> (l + 1), nrow))
        prod = _dot_nt((ch["q"] * d).astype(BF16), (ch["k"] * d).astype(BF16))
        ch["att"] = [a + mask_ref[1 + l] * blk for a, blk in zip(ch["att"], diag_blocks(prod))]

    def intra(ch):
        att = ch["att"]
        if len(att) == 1:
            att_full = att[0].astype(BF16)
        else:
            zero = jnp.zeros((gw, gw), BF16)
            att_full = jnp.concatenate(
                [jnp.concatenate([a.astype(BF16) if i == j else zero for j in range(len(att))], axis=1)
                 for i, a in enumerate(att)], axis=0)
        ch["o_intra"] = _dot(att_full, ch["v"])

    def carried(ch):
        v, q_in, k_out, o_intra = ch["v"], ch["q_in"], ch["k_out"], ch["o_intra"]
        outs = []
        for h in range(GLA_HEADS):
            hr = slice(h * c, (h + 1) * c)
            if single:
                s0 = s0_ref[0, h]
                outs.append(o_intra[hr] + _dot(q_in[hr], s0.astype(BF16)))
                ones = jnp.ones((c, LANES), BF16)
                decay = jnp.exp2(_dot_tn(ch["hi"][hr], ones) + _dot_tn(ch["lo"][hr], ones))
                sout_ref[0, h] = (s0 * jnp.concatenate([decay] * (GLA_HV // LANES), axis=1)
                                  + _dot_tn(k_out[hr], v[hr]))
            else:
                st = st_ref[h]
                outs.append(o_intra[hr] + _dot_nt(q_in[hr], st.astype(BF16)))
                st_ref[h] = st * jnp.exp2(ch["b_last"][h]) + _dot_tn(v[hr], k_out[hr])
        return outs

    def emit(ch, outs):
        for h, o in enumerate(outs):
            cv = slice(h * GLA_HV, (h + 1) * GLA_HV)
            gate = r_ref[ch["rows"], cv].astype(F32)
            og = _row_rms(o) * go_ref[...] * (gate * _sigmoid(gate))
            og_ref[ch["rows"], cv] = og.astype(BF16)

    def chunks(first, count):
        group = [decays(first + i) for i in range(count)]
        for l in range(levels):
            for ch in group:
                level(ch, l)
        for ch in group:
            intra(ch)
        pending = None
        for ch in group:
            outs = carried(ch)
            if pending is not None:
                emit(*pending)
            pending = (ch, outs)
        emit(*pending)

    if single:
        chunks(0, 1)
        return
    per_trip = GLA_CHUNKS_PER_TRIP if nchunk % GLA_CHUNKS_PER_TRIP == 0 else 1

    def trip(gi, carry):
        chunks(gi * per_trip, per_trip)
        return carry

    lax.fori_loop(0, nchunk // per_trip, trip, 0)

    @pl.when(t == pl.num_programs(1) - 1)
    def _():
        for h in range(GLA_HEADS):
            sout_ref[0, h] = jnp.transpose(st_ref[h])


def _gla(zm, zs, w2ext, bg, go, s0, batch, seq, c, tb):
    n = zm.shape[0]
    nt = seq // tb
    gmat, masks, levels = _gla_tables(c)
    dk, dv = GLA_HEADS * GLA_HK, GLA_HEADS * GLA_HV
    const2 = lambda b, t: (0, 0)
    return pl.pallas_call(
        functools.partial(_gla_body, c=c, nchunk=tb // c, levels=levels, single=(seq == c)),
        grid=(batch, nt),
        in_specs=[
            pl.BlockSpec((tb, dk), lambda b, t: (b * nt + t, COL_Q // dk)),
            pl.BlockSpec((tb, dk), lambda b, t: (b * nt + t, COL_K // dk)),
            pl.BlockSpec((tb, dv), lambda b, t: (b * nt + t, COL_V // dv)),
            pl.BlockSpec((tb, dv), lambda b, t: (b * nt + t, COL_R // dv)),
            pl.BlockSpec((tb, SMALL_COLS), lambda b, t: (b * nt + t, 0)),
            pl.BlockSpec((SMALL_COLS, dk), const2),
            pl.BlockSpec((1, dk), const2),
            pl.BlockSpec((1, GLA_HV), const2),
            pl.BlockSpec((1, GLA_HEADS, GLA_HK, GLA_HV), lambda b, t: (b, 0, 0, 0)),
            pl.BlockSpec(gmat.shape, const2),
            pl.BlockSpec(masks.shape, lambda b, t: (0, 0, 0)),
        ],
        out_specs=[
            pl.BlockSpec((tb, dv), lambda b, t: (b * nt + t, 0)),
            pl.BlockSpec((1, GLA_HEADS, GLA_HK, GLA_HV), lambda b, t: (b, 0, 0, 0)),
        ],
        out_shape=[
            jax.ShapeDtypeStruct((n, dv), BF16),
            jax.ShapeDtypeStruct((batch, GLA_HEADS, GLA_HK, GLA_HV), F32),
        ],
        scratch_shapes=[pltpu.VMEM((GLA_HEADS, GLA_HV, GLA_HK), F32)],
        compiler_params=_params(("parallel", "arbitrary"), 40),
        name="gla",
    )(zm, zm, zm, zm, zs, w2ext, bg, go, s0, gmat, masks)


def _mla_proj_body(mq_ref, mkv_ref, zs_ref, wqn_ref, wqp_ref, gql_ref, gkl_ref, gq_ref, gqp_ref, gkp_ref,
                   tab_ref, p_ref, q_ref, ckv_ref, kpe_ref, kp2_ref):
    qlat = (_row_rms(mq_ref[...].astype(F32)) * gql_ref[...]).astype(BF16)
    tab = tab_ref[...]
    tab2 = jnp.concatenate([tab, tab], axis=1)
    starts = list(range(0, MLA_HEADS * LANES, 2 * LANES))

    def project(lo):
        cols = slice(lo, lo + 2 * LANES)
        return _dot(qlat, wqn_ref[:, cols]), _dot(qlat, wqp_ref[:, cols])

    cur = project(starts[0])
    for g, lo in enumerate(starts):
        nxt = project(starts[g + 1]) if g + 1 < len(starts) else None
        nope = (_slab_rms(cur[0], p_ref) * (gq_ref[...] * Q_SCALE)).astype(BF16)
        pe = (_slab_rms(cur[1], p_ref) * (gqp_ref[...] * Q_SCALE) * tab2).astype(BF16)
        for i in range(2):
            dst = 2 * lo + i * 2 * LANES
            q_ref[:, dst:dst + LANES] = nope[:, i * LANES:(i + 1) * LANES]
            q_ref[:, dst + LANES:dst + 2 * LANES] = pe[:, i * LANES:(i + 1) * LANES]
        cur = nxt
    ckv_ref[...] = _row_rms(mkv_ref[...].astype(F32)) * gkl_ref[...]
    slab = zs_ref[:, 0:LANES]
    rot = _row_rms(slab) * gkp_ref[...] * tab
    kp2 = rot + pltpu.roll(rot, MLA_ROPE, axis=1)
    kpe_ref[...] = kp2[:, 0:MLA_ROPE]
    kp2_ref[...] = kp2.astype(BF16)


def _mla_proj(zm, zs, wqn, wqp, gql, gkl, gq, gqp, gkp, tab, p128, tm, tab_blocks):
    n = zm.shape[0]
    const = lambda i: (0, 0)
    return pl.pallas_call(
        _mla_proj_body,
        grid=(n // tm,),
        in_specs=[
            pl.BlockSpec((tm, MLA_LORA), lambda i: (i, COL_MQ // MLA_LORA)),
            pl.BlockSpec((tm, MLA_LORA), lambda i: (i, COL_MKV // MLA_LORA)),
            pl.BlockSpec((tm, SMALL_COLS), lambda i: (i, 0)),
            pl.BlockSpec(wqn.shape, const),
            pl.BlockSpec(wqp.shape, const),
            pl.BlockSpec((1, MLA_LORA), const),
            pl.BlockSpec((1, MLA_LORA), const),
            pl.BlockSpec((1, 2 * LANES), const),
            pl.BlockSpec((1, 2 * LANES), const),
            pl.BlockSpec((1, LANES), const),
            pl.BlockSpec((tm, LANES), lambda i: (i % tab_blocks, 0)),
            pl.BlockSpec((2 * LANES, 2 * LANES), const),
        ],
        out_specs=[
            pl.BlockSpec((tm, 2 * LANES * MLA_HEADS), lambda i: (i, 0)),
            pl.BlockSpec((tm, MLA_LORA), lambda i: (i, 0)),
            pl.BlockSpec((tm, MLA_ROPE), lambda i: (i, 0)),
            pl.BlockSpec((tm, LANES), lambda i: (i, 0)),
        ],
        out_shape=[
            jax.ShapeDtypeStruct((n, 2 * LANES * MLA_HEADS), BF16),
            jax.ShapeDtypeStruct((n, MLA_LORA), F32),
            jax.ShapeDtypeStruct((n, MLA_ROPE), F32),
            jax.ShapeDtypeStruct((n, LANES), BF16),
        ],
        compiler_params=_params(("parallel",), 48),
        name="mla_proj",
    )(zm, zm, zs, wqn, wqp, gql, gkl, gq, gqp, gkp, tab, p128)


def _mla_expand_keys(c_ref, wk_ref, gk_ref, p_ref, kn_ref):
    cb = c_ref[...].astype(BF16)
    starts = list(range(0, MLA_HEADS * LANES, 2 * LANES))
    cur = _dot(cb, wk_ref[:, starts[0]:starts[0] + 2 * LANES])
    for g, lo in enumerate(starts):
        nxt = _dot(cb, wk_ref[:, starts[g + 1]:starts[g + 1] + 2 * LANES]) if g + 1 < len(starts) else None
        kn_ref[:, lo:lo + 2 * LANES] = (_slab_rms(cur, p_ref) * gk_ref[...]).astype(BF16)
        cur = nxt
    return cb


def _mla_expand_k_body(c_ref, wk_ref, gk_ref, p_ref, kn_ref):
    _mla_expand_keys(c_ref, wk_ref, gk_ref, p_ref, kn_ref)


def _mla_expand_kv_body(c_ref, wk_ref, gk_ref, p_ref, wvt_ref, kn_ref, vt_ref):
    cb = _mla_expand_keys(c_ref, wk_ref, gk_ref, p_ref, kn_ref)
    vt = _dot_nt(wvt_ref[...], cb).astype(BF16)
    for h in range(MLA_HEADS):
        vt_ref[h, 0] = vt[h * MLA_VDIM:(h + 1) * MLA_VDIM, :]


def _mla_expand(ckv, wk, gk, p128, tm, wvt=None):
    n = ckv.shape[0]
    const = lambda i: (0, 0)
    width = MLA_HEADS * LANES
    in_specs = [
        pl.BlockSpec((tm, MLA_LORA), lambda i: (i, 0)),
        pl.BlockSpec(wk.shape, const),
        pl.BlockSpec((1, 2 * LANES), const),
        pl.BlockSpec((2 * LANES, 2 * LANES), const),
    ]
    out_specs = [pl.BlockSpec((tm, width), lambda i: (i, 0))]
    out_shape = [jax.ShapeDtypeStruct((n, width), BF16)]
    args = [ckv, wk, gk, p128]
    body = _mla_expand_k_body
    if wvt is not None:
        body = _mla_expand_kv_body
        in_specs.append(pl.BlockSpec(wvt.shape, const))
        out_specs.append(pl.BlockSpec((MLA_HEADS, 1, MLA_VDIM, tm), lambda i: (0, i, 0, 0)))
        out_shape.append(jax.ShapeDtypeStruct((MLA_HEADS, n // tm, MLA_VDIM, tm), BF16))
        args.append(wvt)
    return pl.pallas_call(
        body,
        grid=(n // tm,),
        in_specs=in_specs,
        out_specs=out_specs,
        out_shape=out_shape,
        compiler_params=_params(("parallel",), 40),
        name="mla_expand",
    )(*args)


def _attn_prompt_body(qi_tab, kb_tab, q_ref, kn_ref, kp_ref, vt_ref, o_ref,
                      kext_ref, vx_ref, bias_ref, m_ref, acc_ref, s0, s1, p0, p1, a0, a1, *, tq, nq):
    @pl.when((pl.program_id(0) == 0) & (pl.program_id(1) == 0))
    def _():
        kc = lax.broadcasted_iota(jnp.int32, (tq, tq), 0) // CHUNK
        qc = lax.broadcasted_iota(jnp.int32, (tq, tq), 1) // CHUNK
        bias_ref[...] = jnp.where(kc <= qc, 0.0, NEG_BIG)

    kext_ref[:, 0:LANES] = kn_ref[...]
    kext_ref[:, LANES:2 * LANES] = kp_ref[...]
    vx_ref[:, 0:MLA_VDIM, :] = vt_ref[0]
    vx_ref[:, MLA_VDIM:, :] = jnp.ones((nq, vx_ref.shape[1] - MLA_VDIM, tq), BF16)
    m_ref[...] = jnp.full(m_ref.shape, NEG_BIG, F32)
    acc_ref[...] = jnp.zeros(acc_ref.shape, F32)
    nblk = nq * (nq + 1) // 2
    s_bufs, p_bufs, a_bufs = (s0, s1), (p0, p1), (a0, a1)

    def rows(i):
        return pl.ds(pl.multiple_of(i * tq, tq), tq)

    qw = tq // ATTN_QUERY_SPLIT

    def diagonal(t):
        return isinstance(t, int) and t < nq

    def keys_used(t, lo):
        return lo + qw if diagonal(t) else tq

    def scores(t, par, lo):
        nk = keys_used(t, lo)
        q = q_ref[pl.ds(pl.multiple_of(qi_tab[t] * tq + lo, qw), qw), :]
        k = kext_ref[pl.ds(pl.multiple_of(kb_tab[t] * tq, tq), nk), :]
        s_bufs[par][0:nk, lo:lo + qw] = _dot_nt(k, q)

    def softmax(t, par, lo):
        qi = qi_tab[t]
        nk = keys_used(t, lo)
        cs = slice(lo, lo + qw)
        s = s_bufs[par][0:nk, cs]
        if diagonal(t):
            s = s + bias_ref[0:nk, cs]
        m_old = m_ref[qi, :, cs]
        m_new = jnp.maximum(m_old, jnp.max(s, axis=0, keepdims=True))
        a_bufs[par][:, cs] = jnp.exp2(m_old - m_new)
        m_ref[qi, :, cs] = m_new
        p_bufs[par][0:nk, cs] = jnp.exp2(s - m_new).astype(BF16)

    def values(t, par, lo):
        qi = qi_tab[t]
        nk = keys_used(t, lo)
        cs = slice(lo, lo + qw)
        acc_ref[qi, :, cs] = (a_bufs[par][:, cs] * acc_ref[qi, :, cs]
                              + _dot(vx_ref[kb_tab[t]][:, 0:nk], p_bufs[par][0:nk, cs]))

    slices = range(0, tq, qw)

    def step(u, par):
        for lo in slices:
            softmax(u - 1, 1 - par, lo)
            scores(u, par, lo)
            values(u - 2, par, lo)

    for lo in slices:
        scores(0, 0, lo)
    for lo in slices:
        scores(1, 1, lo)
        softmax(0, 0, lo)
    first_loop_step = nq + 2
    for u in range(2, first_loop_step):
        step(u, u % 2)
    trips = (nblk - first_loop_step) // ATTN_UNROLL

    def trip(j, carry):
        for i in range(ATTN_UNROLL):
            step(first_loop_step + ATTN_UNROLL * j + i, i % 2)
        return carry

    lax.fori_loop(0, trips, trip, 0)
    for u in range(first_loop_step + trips * ATTN_UNROLL, nblk):
        step(u, u % 2)
    for lo in slices:
        softmax(nblk - 1, 1, lo)
        values(nblk - 2, 0, lo)
    for lo in slices:
        values(nblk - 1, 1, lo)
    for qi in range(nq):
        acc = acc_ref[qi]
        o_ref[qi * tq:(qi + 1) * tq, :] = jnp.transpose(
            acc[0:MLA_VDIM] / acc[MLA_VDIM:MLA_VDIM + 1]).astype(BF16)


def _attn_prompt(qext, kn, kp2, vt, batch, seq, tq):
    n = qext.shape[0]
    nq = seq // tq
    assert nq % 2 == 0
    pairs = [(i, i) for i in range(nq)] + [(qi, kb) for qi in range(nq) for kb in range(qi)]
    qi_tab = jnp.asarray([p[0] for p in pairs], jnp.int32)
    kb_tab = jnp.asarray([p[1] for p in pairs], jnp.int32)
    ones_rows = 16
    grid_spec = pltpu.PrefetchScalarGridSpec(
        num_scalar_prefetch=2,
        grid=(batch, MLA_HEADS),
        in_specs=[
            pl.BlockSpec((seq, 2 * LANES), lambda b, h, *_: (b, h)),
            pl.BlockSpec((seq, LANES), lambda b, h, *_: (b, h)),
            pl.BlockSpec((seq, LANES), lambda b, h, *_: (b, 0)),
            pl.BlockSpec((1, nq, MLA_VDIM, tq), lambda b, h, *_: (h, b, 0, 0)),
        ],
        out_specs=pl.BlockSpec((seq, MLA_VDIM), lambda b, h, *_: (b, h)),
        scratch_shapes=[
            pltpu.VMEM((seq, 2 * LANES), BF16),
            pltpu.VMEM((nq, MLA_VDIM + ones_rows, tq), BF16),
            pltpu.VMEM((tq, tq), F32),
            pltpu.VMEM((nq, 1, tq), F32),
            pltpu.VMEM((nq, MLA_VDIM + ones_rows, tq), F32),
            pltpu.VMEM((tq, tq), F32), pltpu.VMEM((tq, tq), F32),
            pltpu.VMEM((tq, tq), BF16), pltpu.VMEM((tq, tq), BF16),
            pltpu.VMEM((1, tq), F32), pltpu.VMEM((1, tq), F32),
        ],
    )
    return pl.pallas_call(
        functools.partial(_attn_prompt_body, tq=tq, nq=nq),
        grid_spec=grid_spec,
        out_shape=jax.ShapeDtypeStruct((n, MLA_HEADS * MLA_VDIM), BF16),
        compiler_params=_params(("arbitrary", "arbitrary"), 40),
        name="attn_prompt",
    )(qi_tab, kb_tab, qext, kn, kp2, vt)


SAMPLE_STREAMS_PER_STEP = 2


def _attn_sample_body(q_ref, hm_ref, knp_ref, kpp_ref, cp_ref, knn_ref, kpn_ref, cn_ref, wv_ref, o_ref, *, seq):
    past = knp_ref.shape[0] // SAMPLE_STREAMS_PER_STEP

    def with_ones(c):
        return jnp.concatenate([c.astype(BF16), jnp.ones((c.shape[0], LANES), BF16)], axis=1)

    streams = range(SAMPLE_STREAMS_PER_STEP)
    new = [slice(s * seq, (s + 1) * seq) for s in streams]
    old = [slice(s * past, (s + 1) * past) for s in streams]

    def scores(s):
        q = q_ref[new[s], :]
        qn = jnp.concatenate([q[:, 2 * h * LANES:(2 * h + 1) * LANES] for h in range(MLA_HEADS)], axis=1)
        qbd = jnp.concatenate([qn] * MLA_HEADS, axis=0) * hm_ref[...]
        qpe = jnp.concatenate([q[:, (2 * h + 1) * LANES:(2 * h + 2) * LANES] for h in range(MLA_HEADS)], axis=0)
        s_past = _dot_nt(knp_ref[old[s], :], qbd) + _dot_nt(kpp_ref[old[s], :], qpe)
        s_new = _dot_nt(knn_ref[new[s], :], qbd) + _dot_nt(kpn_ref[new[s], :], qpe)
        return s_past, s_new

    def latents(s, s_past, s_new):
        m = jnp.maximum(jnp.max(s_past, axis=0, keepdims=True), jnp.max(s_new, axis=0, keepdims=True))
        p_past = jnp.exp2(s_past - m).astype(BF16)
        p_new = jnp.exp2(s_new - m).astype(BF16)
        acc = (_dot_tn(p_past, with_ones(cp_ref[old[s], :]))
               + _dot_tn(p_new, with_ones(cn_ref[new[s], :])))
        return (acc[:, 0:MLA_LORA] / acc[:, MLA_LORA:MLA_LORA + 1]).astype(BF16)

    def outputs(s, lat):
        for h in range(MLA_HEADS):
            cols = slice(h * MLA_VDIM, (h + 1) * MLA_VDIM)
            o_ref[new[s], cols] = _dot(lat[h * seq:(h + 1) * seq], wv_ref[:, cols]).astype(BF16)

    sc = [scores(s) for s in streams]
    lat = [latents(s, *sc[s]) for s in streams]
    for s in streams:
        outputs(s, lat[s])


def _attn_sample(qext, kn_past, kp2_past, ckv_past, kn_new, kp2_new, ckv_new, wv, batch, past, seq):
    n = qext.shape[0]
    width = MLA_HEADS * LANES
    head_mask = jnp.asarray(np.kron(np.eye(MLA_HEADS, dtype=np.float32), np.ones((seq, LANES), np.float32)), BF16)
    group = SAMPLE_STREAMS_PER_STEP
    assert batch % group == 0
    seq, past = group * seq, group * past
    stream = lambda b: (b, 0)
    const = lambda b: (0, 0)
    return pl.pallas_call(
        functools.partial(_attn_sample_body, seq=seq // group),
        grid=(batch // group,),
        in_specs=[
            pl.BlockSpec((seq, 2 * width), stream),
            pl.BlockSpec(head_mask.shape, const),
            pl.BlockSpec((past, width), stream),
            pl.BlockSpec((past, LANES), stream),
            pl.BlockSpec((past, MLA_LORA), stream),
            pl.BlockSpec((seq, width), stream),
            pl.BlockSpec((seq, LANES), stream),
            pl.BlockSpec((seq, MLA_LORA), stream),
            pl.BlockSpec(wv.shape, const),
        ],
        out_specs=pl.BlockSpec((seq, MLA_HEADS * MLA_VDIM), stream),
        out_shape=jax.ShapeDtypeStruct((n, MLA_HEADS * MLA_VDIM), BF16),
        compiler_params=_params(("parallel",), 48),
        name="attn_sample",
    )(qext, head_mask, kn_past, kp2_past, ckv_past, kn_new, kp2_new, ckv_new, wv)


MERGE_COL_CHUNK = 256


def _branch_merge_body(og_ref, at_ref, wg_ref, wm_ref, ga_ref, gb_ref, u_ref):
    og = og_ref[...]
    at = at_ref[...]
    starts = list(range(0, u_ref.shape[1], MERGE_COL_CHUNK))

    def gates(lo):
        cols = slice(lo, lo + MERGE_COL_CHUNK)
        return _sigmoid(ga_ref[:, cols].astype(F32)), _sigmoid(gb_ref[:, cols].astype(F32))

    nxt = gates(starts[0])
    for g, lo in enumerate(starts):
        cols = slice(lo, lo + MERGE_COL_CHUNK)
        sa, sb = nxt
        ya = _dot(og, wg_ref[:, cols])
        yb = _dot(at, wm_ref[:, cols])
        nxt = gates(starts[g + 1]) if g + 1 < len(starts) else None
        u_ref[:, cols] = (sa * ya + sb * yb).astype(BF16)


def _out_proj_body(u_ref, wo_ref, x_ref, g2_ref, x1_ref, h2_ref):
    x1 = x_ref[...] + _dot(u_ref[...], wo_ref[...])
    x1_ref[...] = x1
    h2_ref[...] = (_row_rms(x1) * g2_ref[...]).astype(BF16)


def _merge(og, att, wg, wm, zm, wo, x, g2, tm_u, tm_x):
    n = x.shape[0]
    tn = 512
    u = pl.pallas_call(
        _branch_merge_body,
        grid=(n // tm_u, D_MODEL // tn),
        in_specs=[
            pl.BlockSpec((tm_u, D_MODEL), lambda i, j: (i, 0)),
            pl.BlockSpec((tm_u, D_MODEL), lambda i, j: (i, 0)),
            pl.BlockSpec((D_MODEL, tn), lambda i, j: (0, j)),
            pl.BlockSpec((D_MODEL, tn), lambda i, j: (0, j)),
            pl.BlockSpec((tm_u, tn), lambda i, j: (i, COL_GA // tn + j)),
            pl.BlockSpec((tm_u, tn), lambda i, j: (i, COL_GB // tn + j)),
        ],
        out_specs=pl.BlockSpec((tm_u, tn), lambda i, j: (i, j)),
        out_shape=jax.ShapeDtypeStruct((n, D_MODEL), BF16),
        compiler_params=_params(("parallel", "parallel"), 48),
        name="branch_merge",
    )(og, att, wg, wm, zm, zm)
    return pl.pallas_call(
        _out_proj_body,
        grid=(n // tm_x,),
        in_specs=[
            pl.BlockSpec((tm_x, D_MODEL), lambda i: (i, 0)),
            pl.BlockSpec((D_MODEL, D_MODEL), lambda i: (0, 0)),
            pl.BlockSpec((tm_x, D_MODEL), lambda i: (i, 0)),
            pl.BlockSpec((1, D_MODEL), lambda i: (0, 0)),
        ],
        out_specs=[
            pl.BlockSpec((tm_x, D_MODEL), lambda i: (i, 0)),
            pl.BlockSpec((tm_x, D_MODEL), lambda i: (i, 0)),
        ],
        out_shape=[
            jax.ShapeDtypeStruct((n, D_MODEL), F32),
            jax.ShapeDtypeStruct((n, D_MODEL), BF16),
        ],
        compiler_params=_params(("parallel",), 56),
        name="out_proj",
    )(u, wo, x, g2)


FFN_COL_CHUNK = 256
FFN_ROW_PARTS = 1


def _gelu_gate(a, a1, a2, cw, gt):
    c = cw[3:4, :] + cw[2:3, :] * a + cw[0:1, :] * a2 + cw[1:2, :] * a1
    return 0.5 * c * (1.0 + lax.erf(c * (2.0 ** -0.5))) * gt


def _ffn_up_seq_body(h_ref, wa_ref, wg_ref, cw_ref, hist_ref, act_ref, tail_ref, carry_ref, *, tm, tiles_per_seq):
    i = pl.program_id(0)
    j = pl.program_id(1)

    @pl.when(i % tiles_per_seq == 0)
    def _():
        carry_ref[j] = hist_ref[0]

    tr = tm // FFN_ROW_PARTS
    row = lax.broadcasted_iota(jnp.int32, (tr, 1), 0)
    pieces = [(lo, r0) for lo in range(0, act_ref.shape[1], FFN_COL_CHUNK) for r0 in range(0, tm, tr)]

    def project(lo, r0):
        h = h_ref[r0:r0 + tr, :]
        return _dot(h, wa_ref[:, lo:lo + FFN_COL_CHUNK]), _dot(h, wg_ref[:, lo:lo + FFN_COL_CHUNK])

    nxt = project(*pieces[0])
    prev = None
    for g, (lo, r0) in enumerate(pieces):
        cols = slice(lo, lo + FFN_COL_CHUNK)
        a, gt = nxt
        nxt = project(*pieces[g + 1]) if g + 1 < len(pieces) else None
        if r0 == 0:
            prev = carry_ref[j, :, cols]
        a1 = jnp.where(row == 0, prev[7:8, :], pltpu.roll(a, 1, axis=0))
        a2 = jnp.where(row == 0, prev[6:7, :], jnp.where(row == 1, prev[7:8, :], pltpu.roll(a, 2, axis=0)))
        act_ref[r0:r0 + tr, cols] = _gelu_gate(a, a1, a2, cw_ref[:, cols], gt).astype(BF16)
        prev = a[tr - 8:tr, :]
        if r0 + tr == tm:
            carry_ref[j, :, cols] = prev
            tail_ref[0, :, cols] = a[tr - (CONV_W - 1):tr, :]


def _ffn_up_seq(h2, wa, wg, cw, hist8, seq, tm):
    n = h2.shape[0]
    tn = 512
    nj = D_FF // tn
    tps = seq // tm
    return pl.pallas_call(
        functools.partial(_ffn_up_seq_body, tm=tm, tiles_per_seq=tps),
        grid=(n // tm, nj),
        in_specs=[
            pl.BlockSpec((tm, D_MODEL), lambda i, j: (i, 0)),
            pl.BlockSpec((D_MODEL, tn), lambda i, j: (0, j)),
            pl.BlockSpec((D_MODEL, tn), lambda i, j: (0, j)),
            pl.BlockSpec((8, tn), lambda i, j: (0, j)),
            pl.BlockSpec((1, 8, tn), lambda i, j: (i // tps, 0, j)),
        ],
        out_specs=[
            pl.BlockSpec((tm, tn), lambda i, j: (i, j)),
            pl.BlockSpec((1, CONV_W - 1, tn), lambda i, j: (i, 0, j)),
        ],
        out_shape=[
            jax.ShapeDtypeStruct((n, D_FF), BF16),
            jax.ShapeDtypeStruct((n // tm, CONV_W - 1, D_FF), F32),
        ],
        scratch_shapes=[pltpu.VMEM((nj, 8, tn), F32)],
        compiler_params=_params(("arbitrary", "arbitrary"), 48),
        name="ffn_up_seq",
    )(h2, wa, wg, cw, hist8)


def _ffn_up_multi_body(h_ref, wa_ref, wg_ref, cw_ref, p1_ref, p2_ref, act_ref, a_ref, wab_ref, wgb_ref, *, tm, seq):
    wa = wa_ref[...].astype(BF16)
    wg = wg_ref[...].astype(BF16)
    wab_ref[...] = wa
    wgb_ref[...] = wg
    a = _dot(h_ref[...], wa)
    gt = _dot(h_ref[...], wg)
    pos = lax.broadcasted_iota(jnp.int32, (tm, 1), 0) % seq
    a1 = jnp.where(pos == 0, p1_ref[...], pltpu.roll(a, 1, axis=0))
    a2 = jnp.where(pos <= 1, p2_ref[...], pltpu.roll(a, 2, axis=0))
    act_ref[...] = _gelu_gate(a, a1, a2, cw_ref[...], gt).astype(BF16)
    a_ref[...] = a


def _ffn_up_multi(h2, wup, cw, p1, p2, seq):
    n = h2.shape[0]
    tn = 512
    nj = D_FF // tn
    col = lambda j: (0, j)
    return pl.pallas_call(
        functools.partial(_ffn_up_multi_body, tm=n, seq=seq),
        grid=(nj,),
        in_specs=[
            pl.BlockSpec((n, D_MODEL), lambda j: (0, 0)),
            pl.BlockSpec((D_MODEL, tn), col),
            pl.BlockSpec((D_MODEL, tn), lambda j: (0, nj + j)),
            pl.BlockSpec((8, tn), col),
            pl.BlockSpec((n, tn), col),
            pl.BlockSpec((n, tn), col),
        ],
        out_specs=[pl.BlockSpec((n, tn), col)] * 2 + [pl.BlockSpec((D_MODEL, tn), col)] * 2,
        out_shape=[jax.ShapeDtypeStruct((n, D_FF), BF16), jax.ShapeDtypeStruct((n, D_FF), F32)]
        + [jax.ShapeDtypeStruct((D_MODEL, D_FF), BF16)] * 2,
        compiler_params=_params(("parallel",), 48),
        name="ffn_up_multi",
    )(h2, wup, wup, cw, p1, p2)


def _ffn_down_body(act_ref, wd_ref, x1_ref, o_ref):
    o_ref[...] = x1_ref[...] + _dot(act_ref[...], wd_ref[...])


def _ffn_down_cast_body(act_ref, wd_ref, x1_ref, o_ref, wdb_ref):
    wd = wd_ref[...].astype(BF16)
    wdb_ref[...] = wd
    o_ref[...] = x1_ref[...] + _dot(act_ref[...], wd)


def _ffn_down(act, wd, x1, tm, emit_bf16_weights=False):
    n = act.shape[0]
    tn = 512
    out_specs = [pl.BlockSpec((tm, tn), lambda i, j: (i, j))]
    out_shape = [jax.ShapeDtypeStruct((n, D_MODEL), F32)]
    if emit_bf16_weights:
        assert n == tm
        out_specs.append(pl.BlockSpec((D_FF, tn), lambda i, j: (0, j)))
        out_shape.append(jax.ShapeDtypeStruct((D_FF, D_MODEL), BF16))
    return pl.pallas_call(
        _ffn_down_cast_body if emit_bf16_weights else _ffn_down_body,
        grid=(n // tm, D_MODEL // tn),
        in_specs=[
            pl.BlockSpec((tm, D_FF), lambda i, j: (i, 0)),
            pl.BlockSpec((D_FF, tn), lambda i, j: (0, j)),
            pl.BlockSpec((tm, tn), lambda i, j: (i, j)),
        ],
        out_specs=out_specs,
        out_shape=out_shape,
        compiler_params=_params(("parallel", "parallel"), 56),
        name="ffn_down",
    )(act, wd, x1)


def _swap_halves(w):
    half = w.shape[-1] // 2
    return jnp.concatenate([w[..., half:], w[..., :half]], axis=-1)


def _rope_table(pos):
    half = MLA_ROPE // 2
    inv = ROPE_THETA ** (-jnp.arange(half, dtype=F32) * 2.0 / MLA_ROPE)
    ang = pos.astype(F32)[:, None] * inv[None, :]
    cos, sin = jnp.cos(ang), jnp.sin(ang)
    return jnp.concatenate([cos, cos, -sin, sin], axis=-1)


def _layer_weights(w_in, g_norm1, gla_w_gate2, gla_b_gate, gla_g_out, w_br_gla, mla_g_qlat, mla_w_uq,
                   mla_g_kvlat, mla_w_ukv, mla_g_q, mla_g_k, mla_g_qpe, mla_g_kpe, w_br_mla, w_out,
                   g_norm2, ffn_w_up, ffn_conv_w, ffn_conv_b, ffn_w_down):
    w_main, w_narrow = _regroup_w_in(w_in.T)
    w_lr = w_narrow[0:GLA_GATE_RANK]
    w_kpe = w_narrow[GLA_GATE_RANK:GLA_GATE_RANK + MLA_ROPE]
    half = MLA_ROPE // 2
    w_small = jnp.concatenate(
        [w_kpe, w_kpe[half:], w_kpe[:half], w_lr,
         jnp.zeros((SMALL_COLS - SMALL_LR - GLA_GATE_RANK, D_MODEL), F32)], axis=0).astype(BF16)
    w2ext = jnp.zeros((SMALL_COLS, GLA_HEADS * GLA_HK), F32).at[SMALL_LR:SMALL_LR + GLA_GATE_RANK].set(
        gla_w_gate2).astype(BF16)
    wq = mla_w_uq.reshape(MLA_LORA, MLA_HEADS, MLA_NOPE + MLA_ROPE)
    wqn = wq[:, :, :MLA_NOPE].reshape(MLA_LORA, MLA_HEADS * MLA_NOPE).astype(BF16)
    wq_pe = wq[:, :, MLA_NOPE:]
    wqp = jnp.concatenate([wq_pe, _swap_halves(wq_pe)], axis=-1).reshape(MLA_LORA, MLA_HEADS * LANES).astype(BF16)
    wkv = mla_w_ukv.reshape(MLA_LORA, MLA_HEADS, MLA_NOPE + MLA_VDIM)
    wk = wkv[:, :, :MLA_NOPE].reshape(MLA_LORA, MLA_HEADS * MLA_NOPE).astype(BF16)
    wv = wkv[:, :, MLA_NOPE:].reshape(MLA_LORA, MLA_HEADS * MLA_VDIM).astype(BF16)
    cw = jnp.concatenate([ffn_conv_w, ffn_conv_b[None, :], jnp.zeros((8 - CONV_W - 1, D_FF), F32)], axis=0)
    return dict(
        w_main=w_main, w_small=w_small, g1=g_norm1[None, :], w2ext=w2ext, bg=gla_b_gate[None, :],
        go=gla_g_out[None, :], wg=w_br_gla.astype(BF16), wqn=wqn, wqp=wqp, gql=mla_g_qlat[None, :],
        gkl=mla_g_kvlat[None, :], gq=jnp.tile(mla_g_q, 2)[None, :],
        gqp=jnp.tile(jnp.concatenate([mla_g_qpe, _swap_halves(mla_g_qpe)]), 2)[None, :],
        gkp=jnp.concatenate([mla_g_kpe, _swap_halves(mla_g_kpe)])[None, :],
        wk=wk, wv=wv, wvt=wv.T, gk=jnp.tile(mla_g_k, 2)[None, :], wm=w_br_mla.astype(BF16),
        wo=w_out.astype(BF16),
        g2=g_norm2[None, :], wup=ffn_w_up, cw=cw, wd=ffn_w_down,
        p128=jnp.asarray(np.kron(np.eye(2, dtype=np.float32), np.full((LANES, LANES), 1.0 / LANES, np.float32)), BF16),
    )


def _trunk_front(x, w, s0, batch, seq, chunk, gla_tb, tm_in, tm, tab, tab_blocks, with_values):
    zm, zs = _in_proj(x, w["g1"], w["w_main"], w["w_small"], tm_in)
    og, s_new = _gla(zm, zs, w["w2ext"], w["bg"], w["go"], s0, batch, seq, chunk, gla_tb)
    qext, ckv, kpe, kp2 = _mla_proj(zm, zs, w["wqn"], w["wqp"], w["gql"], w["gkl"], w["gq"], w["gqp"],
                                    w["gkp"], tab, w["p128"], tm, tab_blocks)
    kv = _mla_expand(ckv, w["wk"], w["gk"], w["p128"], tm, w["wvt"] if with_values else None)
    return zm, og, s_new, qext, ckv, kpe, kp2, kv


def kernel(x_prompt, x_sample, state_gla, cache_mla_ckv, cache_mla_kpe, cache_ffn_conv, w_in, g_norm1, gla_w_gate2, gla_b_gate, gla_g_out, w_br_gla, mla_g_qlat, mla_w_uq, mla_g_kvlat, mla_w_ukv, mla_g_q, mla_g_k, mla_g_qpe, mla_g_kpe, w_br_mla, w_out, g_norm2, ffn_w_up, ffn_conv_w, ffn_conv_b, ffn_w_down):
    bp, tp, _ = x_prompt.shape
    bs, ts, _ = x_sample.shape
    depth = w_in.shape[0]
    past = cache_mla_ckv.shape[2]
    np_rows, ns_rows = bp * tp, bs * ts
    tm_p = 512
    tm_big = 1024
    tab_p = _rope_table(jnp.arange(tp))
    tab_s = jnp.tile(_rope_table(past + jnp.arange(ts)), (bs, 1))
    xp = x_prompt.reshape(np_rows, D_MODEL)
    xs = x_sample.reshape(ns_rows, D_MODEL)
    outs = [[] for _ in range(8)]
    layer_weights = (w_in, g_norm1, gla_w_gate2, gla_b_gate, gla_g_out, w_br_gla, mla_g_qlat, mla_w_uq,
                     mla_g_kvlat, mla_w_ukv, mla_g_q, mla_g_k, mla_g_qpe, mla_g_kpe, w_br_mla, w_out,
                     g_norm2, ffn_w_up, ffn_conv_w, ffn_conv_b, ffn_w_down)
    for l in range(depth):
        w = _layer_weights(*[a[l] for a in layer_weights])

        zm, og, ss, qext, ckv_s, kpe_s, kp2, (kn,) = _trunk_front(
            xs, w, state_gla[l], bs, ts, ts, ts, ns_rows, ns_rows, tab_s, 1, False)
        ckv_past = cache_mla_ckv[l].reshape(bs * past, MLA_LORA)
        (kn_past,) = _mla_expand(ckv_past, w["wk"], w["gk"], w["p128"], 512)
        kpe_past = cache_mla_kpe[l].reshape(bs * past, MLA_ROPE)
        kp2_past = jnp.concatenate([kpe_past, kpe_past], axis=-1).astype(BF16)
        att = _attn_sample(qext, kn_past, kp2_past, ckv_past, kn, kp2, ckv_s, w["wv"], bs, past, ts)
        x1, h2 = _merge(og, att, w["wg"], w["wm"], zm, w["wo"], xs, w["g2"], ns_rows, ns_rows)
        hist = cache_ffn_conv[l]
        zrow = jnp.zeros((bs, ts - 1, D_FF), F32)
        p1 = jnp.concatenate([hist[:, 1:2], zrow], axis=1).reshape(ns_rows, D_FF)
        p2 = jnp.concatenate([hist, zrow[:, 1:]], axis=1).reshape(ns_rows, D_FF)
        act, a_full, wa_bf, wg_bf = _ffn_up_multi(h2, w["wup"], w["cw"], p1, p2, ts)
        fs = a_full.reshape(bs, ts, D_FF)[:, ts - (CONV_W - 1):]
        xs, wd_bf = _ffn_down(act, w["wd"], x1, ns_rows, emit_bf16_weights=True)

        s0 = jnp.zeros((bp, GLA_HEADS, GLA_HK, GLA_HV), F32)
        zm, og, sp, qext, ckv_p, kpe_p, kp2, (kn, vt) = _trunk_front(
            xp, w, s0, bp, tp, CHUNK, 512, tm_big, tm_p, tab_p, tp // tm_p, True)
        att = _attn_prompt(qext, kn, kp2, vt, bp, tp, 512)
        x1, h2 = _merge(og, att, w["wg"], w["wm"], zm, w["wo"], xp, w["g2"], tm_big, tm_p)
        hist8 = jnp.zeros((bp, 8, D_FF), F32)
        act, tails = _ffn_up_seq(h2, wa_bf, wg_bf, w["cw"], hist8, tp, tm_big)
        fp = tails.reshape(bp, tp // tm_big, CONV_W - 1, D_FF)[:, -1]
        (xp,) = _ffn_down(act, wd_bf, x1, tm_big)

        for lst, val in zip(outs, (sp, ss, ckv_p.reshape(bp, tp, MLA_LORA), ckv_s.reshape(bs, ts, MLA_LORA),
                                   kpe_p.reshape(bp, tp, MLA_ROPE), kpe_s.reshape(bs, ts, MLA_ROPE), fp, fs)):
            lst.append(val)
    return (xp.reshape(bp, tp, D_MODEL), xs.reshape(bs, ts, D_MODEL)) + tuple(jnp.stack(o, 0) for o in outs)
```

```python
import functools

import numpy as np
import jax
import jax.numpy as jnp
from jax import lax
from jax.experimental import pallas as pl
from jax.experimental.pallas import tpu as pltpu

F32 = jnp.float32
BF16 = jnp.bfloat16

D_MODEL = 2048
CHUNK = 64
EPS = 1e-6
GLA_HEADS = 4
GLA_HK = 256
GLA_HV = 512
GLA_GATE_RANK = 16
GLA_GATE_TAU = 16.0
MLA_HEADS = 16
MLA_LORA = 512
MLA_NOPE = 128
MLA_ROPE = 64
MLA_VDIM = 128
MLA_SCALE = (MLA_NOPE + MLA_ROPE) ** -0.5
LOG2E = float(np.log2(np.e))
Q_SCALE = MLA_SCALE * LOG2E
ROPE_THETA = 10000.0
D_FF = 5632
CONV_W = 3
LANES = 128
NEG_BIG = -1e30
ATTN_UNROLL = 8
ATTN_QUERY_SPLIT = 2

COL_Q, COL_K, COL_V, COL_R, COL_MQ, COL_MKV, COL_GA, COL_GB, MAIN_COLS = (
    0, 1024, 2048, 4096, 6144, 6656, 7168, 9216, 11264)
SMALL_COLS = 256
SMALL_LR = 128

MIB = 1024 * 1024


def _params(semantics, vmem_mib):
    return pltpu.CompilerParams(dimension_semantics=semantics, vmem_limit_bytes=vmem_mib * MIB)


def _dot(a, b):
    return jnp.dot(a, b, preferred_element_type=F32)


def _dot_nt(a, b):
    return lax.dot_general(a, b, (((1,), (1,)), ((), ())), preferred_element_type=F32)


def _dot_tn(a, b):
    return lax.dot_general(a, b, (((0,), (0,)), ((), ())), preferred_element_type=F32)


def _sigmoid(x):
    return 1.0 / (1.0 + jnp.exp(-x))


def _row_rms(x):
    return x * lax.rsqrt(jnp.mean(x * x, axis=-1, keepdims=True) + EPS)


def _slab_rms(x, p_ref):
    ms = _dot((x * x).astype(BF16), p_ref[...])
    return x * lax.rsqrt(ms + EPS)


W_IN_SEGMENTS = ((0, COL_R, 0), (COL_R, COL_GA, GLA_GATE_RANK), (COL_GA, MAIN_COLS, GLA_GATE_RANK + MLA_ROPE))


def _small_weight_rows(narrow):
    lr = narrow[0:GLA_GATE_RANK]
    kpe = narrow[GLA_GATE_RANK:GLA_GATE_RANK + MLA_ROPE]
    half = MLA_ROPE // 2
    pad = jnp.zeros((SMALL_COLS - SMALL_LR - GLA_GATE_RANK, D_MODEL), narrow.dtype)
    return jnp.concatenate([kpe, kpe[half:], kpe[:half], lr, pad], axis=0)


def _regroup_body(x_ref, g_ref, main_ref, next_ref, o_ref, narrow_ref, zm_ref, zs_ref, h_ref, *, tr):
    j = pl.program_id(0)

    @pl.when(j == 0)
    def _():
        narrow_ref[...] = jnp.zeros(narrow_ref.shape, F32)
        h_ref[...] = (_row_rms(x_ref[...]) * g_ref[...]).astype(BF16)

    _regroup_tile(j, main_ref, next_ref, o_ref, narrow_ref, tr)
    zm_ref[...] = _dot_nt(h_ref[...], o_ref[...]).astype(BF16)

    @pl.when(j == pl.num_programs(0) - 1)
    def _():
        zs_ref[...] = _dot_nt(h_ref[...], _small_weight_rows(narrow_ref[...]).astype(BF16))


def _regroup_tile(j, main_ref, next_ref, o_ref, narrow_ref, tr):
    prev = 0
    for lo, hi, shift in W_IN_SEGMENTS:
        if shift > prev:
            @pl.when(j == lo // tr)
            def _(prev=prev, shift=shift):
                narrow_ref[prev:shift, :] = main_ref[prev:shift, :]

        @pl.when((j >= lo // tr) & (j < hi // tr))
        def _(shift=shift):
            if shift == 0:
                o_ref[...] = main_ref[...].astype(BF16)
            else:
                o_ref[0:tr - shift, :] = main_ref[shift:tr, :].astype(BF16)
                o_ref[tr - shift:tr, :] = next_ref[0:shift, :].astype(BF16)
        prev = shift


def _regroup_in_proj(x, g, w_in_t):
    n = x.shape[0]
    tr = 1024
    assert all(lo % tr == 0 and hi % tr == 0 and shift % 16 == 0 and shift < LANES
               for lo, hi, shift in W_IN_SEGMENTS)
    const = lambda j: (0, 0)
    return pl.pallas_call(
        functools.partial(_regroup_body, tr=tr),
        grid=(MAIN_COLS // tr,),
        in_specs=[
            pl.BlockSpec((n, D_MODEL), const),
            pl.BlockSpec((1, D_MODEL), const),
            pl.BlockSpec((tr, D_MODEL), lambda j: (j, 0)),
            pl.BlockSpec((LANES, D_MODEL), lambda j: ((j + 1) * (tr // LANES), 0)),
        ],
        out_specs=[
            pl.BlockSpec((tr, D_MODEL), lambda j: (j, 0)),
            pl.BlockSpec((LANES, D_MODEL), const),
            pl.BlockSpec((n, tr), lambda j: (0, j)),
            pl.BlockSpec((n, SMALL_COLS), const),
        ],
        out_shape=[
            jax.ShapeDtypeStruct((MAIN_COLS, D_MODEL), BF16),
            jax.ShapeDtypeStruct((LANES, D_MODEL), F32),
            jax.ShapeDtypeStruct((n, MAIN_COLS), BF16),
            jax.ShapeDtypeStruct((n, SMALL_COLS), F32),
        ],
        scratch_shapes=[pltpu.VMEM((n, D_MODEL), BF16)],
        compiler_params=_params(("arbitrary",), 48),
        name="regroup_in_proj",
    )(x, g, w_in_t, w_in_t)


def _in_proj_body(x_ref, g_ref, wm_ref, ws_ref, zm_ref, zs_ref, h_ref):
    @pl.when(pl.program_id(1) == 0)
    def _():
        h_ref[...] = (_row_rms(x_ref[...]) * g_ref[...]).astype(BF16)
        zs_ref[...] = _dot_nt(h_ref[...], ws_ref[...])

    zm_ref[...] = _dot_nt(h_ref[...], wm_ref[...]).astype(BF16)


def _in_proj(x, g, wm, ws, tm):
    n = x.shape[0]
    tn = 1024
    return pl.pallas_call(
        _in_proj_body,
        grid=(n // tm, MAIN_COLS // tn),
        in_specs=[
            pl.BlockSpec((tm, D_MODEL), lambda i, j: (i, 0)),
            pl.BlockSpec((1, D_MODEL), lambda i, j: (0, 0)),
            pl.BlockSpec((tn, D_MODEL), lambda i, j: (j, 0)),
            pl.BlockSpec((SMALL_COLS, D_MODEL), lambda i, j: (0, 0)),
        ],
        out_specs=[
            pl.BlockSpec((tm, tn), lambda i, j: (i, j)),
            pl.BlockSpec((tm, SMALL_COLS), lambda i, j: (i, 0)),
        ],
        out_shape=[
            jax.ShapeDtypeStruct((n, MAIN_COLS), BF16),
            jax.ShapeDtypeStruct((n, SMALL_COLS), F32),
        ],
        scratch_shapes=[pltpu.VMEM((tm, D_MODEL), BF16)],
        compiler_params=_params(("parallel", "arbitrary"), 56),
        name="in_proj",
    )(x, g, wm, ws)


def _gla_tables(c):
    levels = int(np.log2(c))
    t = np.arange(c)[:, None]
    u = np.arange(c)[None, :]
    masks = [(u == t)]
    level2 = None
    for l in range(levels):
        m = c >> (l + 1)
        mid_t = (t // (2 * m)) * 2 * m + m
        upper = t >= mid_t
        if m == 2:
            level2 = np.where(upper, (u >= mid_t) & (u <= t), (u > t) & (u < mid_t))
        mid_u = (u // (2 * m)) * 2 * m + m
        masks.append((t // (2 * m) == u // (2 * m)) & upper & (u < mid_u))
    eye_h = np.eye(GLA_HEADS, dtype=np.float32)
    gmat = np.concatenate([np.kron(eye_h, (u <= t).astype(np.float32)),
                           np.kron(eye_h, level2.astype(np.float32))], axis=0)
    group = _gla_group_heads(c)
    masks = np.stack([np.kron(np.eye(group, dtype=np.float32), mk.astype(np.float32)) for mk in masks])
    return jnp.asarray(gmat, BF16), jnp.asarray(masks), levels


def _gla_group_heads(c):
    return min(GLA_HEADS, max(1, LANES // c))


def _gla_level_exponent(b, log_a, level2, m, c):
    if m == 1:
        row = lax.broadcasted_iota(jnp.int32, (c, 1), 0)
        return jnp.where(row % 2 == 1, log_a, 0.0)
    if m == 2:
        return level2
    parts = [jnp.broadcast_to(b[i + m - 1:i + m, :], (2 * m, GLA_HK)) for i in range(0, c, 2 * m)]
    ref = parts[0] if len(parts) == 1 else jnp.concatenate(parts, axis=0)
    return -jnp.abs(b - ref)


GLA_CHUNKS_PER_TRIP = 4


def _gla_body(q_ref, k_ref, v_ref, r_ref, zs_ref, w2_ref, bg_ref, go_ref, s0_ref, gmat_ref, mask_ref,
              og_ref, sout_ref, st_ref, *, c, nchunk, levels, single):
    t = pl.program_id(1)

    if not single:
        @pl.when(t == 0)
        def _():
            for h in range(GLA_HEADS):
                st_ref[h] = jnp.transpose(s0_ref[0, h])

    nrow = GLA_HEADS * c
    gw = _gla_group_heads(c) * c

    def stack(x, width):
        return jnp.concatenate([x[:, h * width:(h + 1) * width] for h in range(GLA_HEADS)], axis=0)

    def diag_blocks(prod):
        return [prod[g:g + gw, g:g + gw] for g in range(0, nrow, gw)]

    def decays(ci):
        rows = pl.ds(0, c) if single else pl.ds(pl.multiple_of(ci * c, c), c)
        q = stack(q_ref[rows, :], GLA_HK).astype(F32) * (GLA_HK ** -0.5)
        k = stack(k_ref[rows, :], GLA_HK).astype(F32)
        v = stack(v_ref[rows, :], GLA_HV)
        x = stack(_dot(zs_ref[rows, :].astype(BF16), w2_ref[...]) + bg_ref[...], GLA_HK) * LOG2E
        log_a = (jnp.minimum(x, 0.0) - jnp.log2(1.0 + jnp.exp2(-jnp.abs(x)))) * (1.0 / GLA_GATE_TAU)
        hi = log_a.astype(BF16)
        lo = (log_a - hi.astype(F32)).astype(BF16)
        gm = gmat_ref[...]
        pre = _dot(gm, hi) + _dot(gm, lo)
        b = pre[0:nrow]
        b_last = [b[h * c + c - 1:h * c + c, :] for h in range(GLA_HEADS)]
        q_in = (q * jnp.exp2(b)).astype(BF16)
        k_out = (k * jnp.exp2(jnp.concatenate([jnp.broadcast_to(r, (c, GLA_HK)) for r in b_last], axis=0)
                              - b)).astype(BF16)
        att = [mask_ref[0] * blk for blk in diag_blocks(_dot_nt(q.astype(BF16), k.astype(BF16)))]
        return dict(rows=rows, q=q, k=k, v=v, log_a=log_a, hi=hi, lo=lo, b=b, level2=pre[nrow:2 * nrow],
                    b_last=b_last, q_in=q_in, k_out=k_out, att=att)

    def level(ch, l):
        d = jnp.exp2(_gla_level_exponent(ch["b"], ch["log_a"], ch["level2"], c >> (l + 1), nrow))
        prod = _dot_nt((ch["q"] * d).astype(BF16), (ch["k"] * d).astype(BF16))
        ch["att"] = [a + mask_ref[1 + l] * blk for a, blk in zip(ch["att"], diag_blocks(prod))]

    def intra(ch):
        att = ch["att"]
        if len(att) == 1:
            att_full = att[0].astype(BF16)
        else:
            zero = jnp.zeros((gw, gw), BF16)
            att_full = jnp.concatenate(
                [jnp.concatenate([a.astype(BF16) if i == j else zero for j in range(len(att))], axis=1)
                 for i, a in enumerate(att)], axis=0)
        ch["o_intra"] = _dot(att_full, ch["v"])

    def carried(ch):
        v, q_in, k_out, o_intra = ch["v"], ch["q_in"], ch["k_out"], ch["o_intra"]
        outs = []
        for h in range(GLA_HEADS):
            hr = slice(h * c, (h + 1) * c)
            if single:
                s0 = s0_ref[0, h]
                outs.append(o_intra[hr] + _dot(q_in[hr], s0.astype(BF16)))
                ones = jnp.ones((c, LANES), BF16)
                decay = jnp.exp2(_dot_tn(ch["hi"][hr], ones) + _dot_tn(ch["lo"][hr], ones))
                sout_ref[0, h] = (s0 * jnp.concatenate([decay] * (GLA_HV // LANES), axis=1)
                                  + _dot_tn(k_out[hr], v[hr]))
            else:
                st = st_ref[h]
                outs.append(o_intra[hr] + _dot_nt(q_in[hr], st.astype(BF16)))
                st_ref[h] = st * jnp.exp2(ch["b_last"][h]) + _dot_tn(v[hr], k_out[hr])
        return outs

    def emit(ch, outs):
        for h, o in enumerate(outs):
            cv = slice(h * GLA_HV, (h + 1) * GLA_HV)
            gate = r_ref[ch["rows"], cv].astype(F32)
            og = _row_rms(o) * go_ref[...] * (gate * _sigmoid(gate))
            og_ref[ch["rows"], cv] = og.astype(BF16)

    def chunks(first, count):
        group = [decays(first + i) for i in range(count)]
        for l in range(levels):
            for ch in group:
                level(ch, l)
        for ch in group:
            intra(ch)
        pending = None
        for ch in group:
            outs = carried(ch)
            if pending is not None:
                emit(*pending)
            pending = (ch, outs)
        emit(*pending)

    if single:
        chunks(0, 1)
        return
    per_trip = GLA_CHUNKS_PER_TRIP if nchunk % GLA_CHUNKS_PER_TRIP == 0 else 1

    def trip(gi, carry):
        chunks(gi * per_trip, per_trip)
        return carry

    lax.fori_loop(0, nchunk // per_trip, trip, 0)

    @pl.when(t == pl.num_programs(1) - 1)
    def _():
        for h in range(GLA_HEADS):
            sout_ref[0, h] = jnp.transpose(st_ref[h])


def _gla(zm, zs, w2ext, bg, go, s0, batch, seq, c, tb):
    n = zm.shape[0]
    nt = seq // tb
    gmat, masks, levels = _gla_tables(c)
    dk, dv = GLA_HEADS * GLA_HK, GLA_HEADS * GLA_HV
    const2 = lambda b, t: (0, 0)
    return pl.pallas_call(
        functools.partial(_gla_body, c=c, nchunk=tb // c, levels=levels, single=(seq == c)),
        grid=(batch, nt),
        in_specs=[
            pl.BlockSpec((tb, dk), lambda b, t: (b * nt + t, COL_Q // dk)),
            pl.BlockSpec((tb, dk), lambda b, t: (b * nt + t, COL_K // dk)),
            pl.BlockSpec((tb, dv), lambda b, t: (b * nt + t, COL_V // dv)),
            pl.BlockSpec((tb, dv), lambda b, t: (b * nt + t, COL_R // dv)),
            pl.BlockSpec((tb, SMALL_COLS), lambda b, t: (b * nt + t, 0)),
            pl.BlockSpec((SMALL_COLS, dk), const2),
            pl.BlockSpec((1, dk), const2),
            pl.BlockSpec((1, GLA_HV), const2),
            pl.BlockSpec((1, GLA_HEADS, GLA_HK, GLA_HV), lambda b, t: (b, 0, 0, 0)),
            pl.BlockSpec(gmat.shape, const2),
            pl.BlockSpec(masks.shape, lambda b, t: (0, 0, 0)),
        ],
        out_specs=[
            pl.BlockSpec((tb, dv), lambda b, t: (b * nt + t, 0)),
            pl.BlockSpec((1, GLA_HEADS, GLA_HK, GLA_HV), lambda b, t: (b, 0, 0, 0)),
        ],
        out_shape=[
            jax.ShapeDtypeStruct((n, dv), BF16),
            jax.ShapeDtypeStruct((batch, GLA_HEADS, GLA_HK, GLA_HV), F32),
        ],
        scratch_shapes=[pltpu.VMEM((GLA_HEADS, GLA_HV, GLA_HK), F32)],
        compiler_params=_params(("parallel", "arbitrary"), 40),
        name="gla",
    )(zm, zm, zm, zm, zs, w2ext, bg, go, s0, gmat, masks)


def _mla_proj_body(mq_ref, mkv_ref, zs_ref, wqn_ref, wqp_ref, gql_ref, gkl_ref, gq_ref, gqp_ref, gkp_ref,
                   tab_ref, p_ref, q_ref, ckv_ref, kpe_ref, kp2_ref):
    qlat = (_row_rms(mq_ref[...].astype(F32)) * gql_ref[...]).astype(BF16)
    tab = tab_ref[...]
    tab2 = jnp.concatenate([tab, tab], axis=1)
    starts = list(range(0, MLA_HEADS * LANES, 2 * LANES))

    def project(lo):
        cols = slice(lo, lo + 2 * LANES)
        return _dot(qlat, wqn_ref[:, cols]), _dot(qlat, wqp_ref[:, cols])

    cur = project(starts[0])
    for g, lo in enumerate(starts):
        nxt = project(starts[g + 1]) if g + 1 < len(starts) else None
        nope = (_slab_rms(cur[0], p_ref) * (gq_ref[...] * Q_SCALE)).astype(BF16)
        pe = (_slab_rms(cur[1], p_ref) * (gqp_ref[...] * Q_SCALE) * tab2).astype(BF16)
        for i in range(2):
            dst = 2 * lo + i * 2 * LANES
            q_ref[:, dst:dst + LANES] = nope[:, i * LANES:(i + 1) * LANES]
            q_ref[:, dst + LANES:dst + 2 * LANES] = pe[:, i * LANES:(i + 1) * LANES]
        cur = nxt
    ckv_ref[...] = _row_rms(mkv_ref[...].astype(F32)) * gkl_ref[...]
    slab = zs_ref[:, 0:LANES]
    rot = _row_rms(slab) * gkp_ref[...] * tab
    kp2 = rot + pltpu.roll(rot, MLA_ROPE, axis=1)
    kpe_ref[...] = kp2[:, 0:MLA_ROPE]
    kp2_ref[...] = kp2.astype(BF16)


def _mla_proj(zm, zs, wqn, wqp, gql, gkl, gq, gqp, gkp, tab, p128, tm, tab_blocks):
    n = zm.shape[0]
    const = lambda i: (0, 0)
    return pl.pallas_call(
        _mla_proj_body,
        grid=(n // tm,),
        in_specs=[
            pl.BlockSpec((tm, MLA_LORA), lambda i: (i, COL_MQ // MLA_LORA)),
            pl.BlockSpec((tm, MLA_LORA), lambda i: (i, COL_MKV // MLA_LORA)),
            pl.BlockSpec((tm, SMALL_COLS), lambda i: (i, 0)),
            pl.BlockSpec(wqn.shape, const),
            pl.BlockSpec(wqp.shape, const),
            pl.BlockSpec((1, MLA_LORA), const),
            pl.BlockSpec((1, MLA_LORA), const),
            pl.BlockSpec((1, 2 * LANES), const),
            pl.BlockSpec((1, 2 * LANES), const),
            pl.BlockSpec((1, LANES), const),
            pl.BlockSpec((tm, LANES), lambda i: (i % tab_blocks, 0)),
            pl.BlockSpec((2 * LANES, 2 * LANES), const),
        ],
        out_specs=[
            pl.BlockSpec((tm, 2 * LANES * MLA_HEADS), lambda i: (i, 0)),
            pl.BlockSpec((tm, MLA_LORA), lambda i: (i, 0)),
            pl.BlockSpec((tm, MLA_ROPE), lambda i: (i, 0)),
            pl.BlockSpec((tm, LANES), lambda i: (i, 0)),
        ],
        out_shape=[
            jax.ShapeDtypeStruct((n, 2 * LANES * MLA_HEADS), BF16),
            jax.ShapeDtypeStruct((n, MLA_LORA), F32),
            jax.ShapeDtypeStruct((n, MLA_ROPE), F32),
            jax.ShapeDtypeStruct((n, LANES), BF16),
        ],
        compiler_params=_params(("parallel",), 48),
        name="mla_proj",
    )(zm, zm, zs, wqn, wqp, gql, gkl, gq, gqp, gkp, tab, p128)


def _mla_expand_keys(c_ref, wk_ref, gk_ref, p_ref, kn_ref):
    cb = c_ref[...].astype(BF16)
    starts = list(range(0, MLA_HEADS * LANES, 2 * LANES))
    cur = _dot(cb, wk_ref[:, starts[0]:starts[0] + 2 * LANES])
    for g, lo in enumerate(starts):
        nxt = _dot(cb, wk_ref[:, starts[g + 1]:starts[g + 1] + 2 * LANES]) if g + 1 < len(starts) else None
        kn_ref[:, lo:lo + 2 * LANES] = (_slab_rms(cur, p_ref) * gk_ref[...]).astype(BF16)
        cur = nxt
    return cb


def _mla_expand_k_body(c_ref, wk_ref, gk_ref, p_ref, kn_ref):
    _mla_expand_keys(c_ref, wk_ref, gk_ref, p_ref, kn_ref)


def _mla_expand_kv_body(c_ref, wk_ref, gk_ref, p_ref, wvt_ref, kn_ref, vt_ref):
    cb = _mla_expand_keys(c_ref, wk_ref, gk_ref, p_ref, kn_ref)
    vt = _dot_nt(wvt_ref[...], cb).astype(BF16)
    for h in range(MLA_HEADS):
        vt_ref[h, 0] = vt[h * MLA_VDIM:(h + 1) * MLA_VDIM, :]


def _mla_expand(ckv, wk, gk, p128, tm, wvt=None):
    n = ckv.shape[0]
    const = lambda i: (0, 0)
    width = MLA_HEADS * LANES
    in_specs = [
        pl.BlockSpec((tm, MLA_LORA), lambda i: (i, 0)),
        pl.BlockSpec(wk.shape, const),
        pl.BlockSpec((1, 2 * LANES), const),
        pl.BlockSpec((2 * LANES, 2 * LANES), const),
    ]
    out_specs = [pl.BlockSpec((tm, width), lambda i: (i, 0))]
    out_shape = [jax.ShapeDtypeStruct((n, width), BF16)]
    args = [ckv, wk, gk, p128]
    body = _mla_expand_k_body
    if wvt is not None:
        body = _mla_expand_kv_body
        in_specs.append(pl.BlockSpec(wvt.shape, const))
        out_specs.append(pl.BlockSpec((MLA_HEADS, 1, MLA_VDIM, tm), lambda i: (0, i, 0, 0)))
        out_shape.append(jax.ShapeDtypeStruct((MLA_HEADS, n // tm, MLA_VDIM, tm), BF16))
        args.append(wvt)
    return pl.pallas_call(
        body,
        grid=(n // tm,),
        in_specs=in_specs,
        out_specs=out_specs,
        out_shape=out_shape,
        compiler_params=_params(("parallel",), 40),
        name="mla_expand",
    )(*args)


def _attn_prompt_body(qi_tab, kb_tab, q_ref, kn_ref, kp_ref, vt_ref, o_ref,
                      kext_ref, vx_ref, bias_ref, m_ref, acc_ref, s0, s1, p0, p1, a0, a1, *, tq, nq):
    @pl.when((pl.program_id(0) == 0) & (pl.program_id(1) == 0))
    def _():
        kc = lax.broadcasted_iota(jnp.int32, (tq, tq), 0) // CHUNK
        qc = lax.broadcasted_iota(jnp.int32, (tq, tq), 1) // CHUNK
        bias_ref[...] = jnp.where(kc <= qc, 0.0, NEG_BIG)

    kext_ref[:, 0:LANES] = kn_ref[...]
    kext_ref[:, LANES:2 * LANES] = kp_ref[...]
    vx_ref[:, 0:MLA_VDIM, :] = vt_ref[0]
    vx_ref[:, MLA_VDIM:, :] = jnp.ones((nq, vx_ref.shape[1] - MLA_VDIM, tq), BF16)
    m_ref[...] = jnp.full(m_ref.shape, NEG_BIG, F32)
    acc_ref[...] = jnp.zeros(acc_ref.shape, F32)
    nblk = nq * (nq + 1) // 2
    s_bufs, p_bufs, a_bufs = (s0, s1), (p0, p1), (a0, a1)

    def rows(i):
        return pl.ds(pl.multiple_of(i * tq, tq), tq)

    qw = tq // ATTN_QUERY_SPLIT

    def diagonal(t):
        return isinstance(t, int) and t < nq

    def keys_used(t, lo):
        return lo + qw if diagonal(t) else tq

    def scores(t, par, lo):
        nk = keys_used(t, lo)
        q = q_ref[pl.ds(pl.multiple_of(qi_tab[t] * tq + lo, qw), qw), :]
        k = kext_ref[pl.ds(pl.multiple_of(kb_tab[t] * tq, tq), nk), :]
        s_bufs[par][0:nk, lo:lo + qw] = _dot_nt(k, q)

    def softmax(t, par, lo):
        qi = qi_tab[t]
        nk = keys_used(t, lo)
        cs = slice(lo, lo + qw)
        s = s_bufs[par][0:nk, cs]
        if diagonal(t):
            s = s + bias_ref[0:nk, cs]
        m_old = m_ref[qi, :, cs]
        m_new = jnp.maximum(m_old, jnp.max(s, axis=0, keepdims=True))
        a_bufs[par][:, cs] = jnp.exp2(m_old - m_new)
        m_ref[qi, :, cs] = m_new
        p_bufs[par][0:nk, cs] = jnp.exp2(s - m_new).astype(BF16)

    def values(t, par, lo):
        qi = qi_tab[t]
        nk = keys_used(t, lo)
        cs = slice(lo, lo + qw)
        acc_ref[qi, :, cs] = (a_bufs[par][:, cs] * acc_ref[qi, :, cs]
                              + _dot(vx_ref[kb_tab[t]][:, 0:nk], p_bufs[par][0:nk, cs]))

    slices = range(0, tq, qw)

    def step(u, par):
        for lo in slices:
            softmax(u - 1, 1 - par, lo)
            scores(u, par, lo)
            values(u - 2, par, lo)

    for lo in slices:
        scores(0, 0, lo)
    for lo in slices:
        scores(1, 1, lo)
        softmax(0, 0, lo)
    first_loop_step = nq + 2
    for u in range(2, first_loop_step):
        step(u, u % 2)
    trips = (nblk - first_loop_step) // ATTN_UNROLL

    def trip(j, carry):
        for i in range(ATTN_UNROLL):
            step(first_loop_step + ATTN_UNROLL * j + i, i % 2)
        return carry

    lax.fori_loop(0, trips, trip, 0)
    for u in range(first_loop_step + trips * ATTN_UNROLL, nblk):
        step(u, u % 2)
    for lo in slices:
        softmax(nblk - 1, 1, lo)
        values(nblk - 2, 0, lo)
    for lo in slices:
        values(nblk - 1, 1, lo)
    for qi in range(nq):
        acc = acc_ref[qi]
        o_ref[qi * tq:(qi + 1) * tq, :] = jnp.transpose(
            acc[0:MLA_VDIM] / acc[MLA_VDIM:MLA_VDIM + 1]).astype(BF16)


def _attn_prompt(qext, kn, kp2, vt, batch, seq, tq):
    n = qext.shape[0]
    nq = seq // tq
    assert nq % 2 == 0
    pairs = [(i, i) for i in range(nq)] + [(qi, kb) for qi in range(nq) for kb in range(qi)]
    qi_tab = jnp.asarray([p[0] for p in pairs], jnp.int32)
    kb_tab = jnp.asarray([p[1] for p in pairs], jnp.int32)
    ones_rows = 16
    grid_spec = pltpu.PrefetchScalarGridSpec(
        num_scalar_prefetch=2,
        grid=(batch, MLA_HEADS),
        in_specs=[
            pl.BlockSpec((seq, 2 * LANES), lambda b, h, *_: (b, h)),
            pl.BlockSpec((seq, LANES), lambda b, h, *_: (b, h)),
            pl.BlockSpec((seq, LANES), lambda b, h, *_: (b, 0)),
            pl.BlockSpec((1, nq, MLA_VDIM, tq), lambda b, h, *_: (h, b, 0, 0)),
        ],
        out_specs=pl.BlockSpec((seq, MLA_VDIM), lambda b, h, *_: (b, h)),
        scratch_shapes=[
            pltpu.VMEM((seq, 2 * LANES), BF16),
            pltpu.VMEM((nq, MLA_VDIM + ones_rows, tq), BF16),
            pltpu.VMEM((tq, tq), F32),
            pltpu.VMEM((nq, 1, tq), F32),
            pltpu.VMEM((nq, MLA_VDIM + ones_rows, tq), F32),
            pltpu.VMEM((tq, tq), F32), pltpu.VMEM((tq, tq), F32),
            pltpu.VMEM((tq, tq), BF16), pltpu.VMEM((tq, tq), BF16),
            pltpu.VMEM((1, tq), F32), pltpu.VMEM((1, tq), F32),
        ],
    )
    return pl.pallas_call(
        functools.partial(_attn_prompt_body, tq=tq, nq=nq),
        grid_spec=grid_spec,
        out_shape=jax.ShapeDtypeStruct((n, MLA_HEADS * MLA_VDIM), BF16),
        compiler_params=_params(("arbitrary", "arbitrary"), 40),
        name="attn_prompt",
    )(qi_tab, kb_tab, qext, kn, kp2, vt)


SAMPLE_STREAMS_PER_STEP = 2


def _attn_sample_body(q_ref, hm_ref, knp_ref, kpp_ref, cp_ref, knn_ref, kpn_ref, cn_ref, wv_ref, o_ref, *, seq):
    past = knp_ref.shape[0] // SAMPLE_STREAMS_PER_STEP

    def with_ones(c):
        return jnp.concatenate([c.astype(BF16), jnp.ones((c.shape[0], LANES), BF16)], axis=1)

    streams = range(SAMPLE_STREAMS_PER_STEP)
    new = [slice(s * seq, (s + 1) * seq) for s in streams]
    old = [slice(s * past, (s + 1) * past) for s in streams]

    def scores(s):
        q = q_ref[new[s], :]
        qn = jnp.concatenate([q[:, 2 * h * LANES:(2 * h + 1) * LANES] for h in range(MLA_HEADS)], axis=1)
        qbd = jnp.concatenate([qn] * MLA_HEADS, axis=0) * hm_ref[...]
        qpe = jnp.concatenate([q[:, (2 * h + 1) * LANES:(2 * h + 2) * LANES] for h in range(MLA_HEADS)], axis=0)
        s_past = _dot_nt(knp_ref[old[s], :], qbd) + _dot_nt(kpp_ref[old[s], :], qpe)
        s_new = _dot_nt(knn_ref[new[s], :], qbd) + _dot_nt(kpn_ref[new[s], :], qpe)
        return s_past, s_new

    def latents(s, s_past, s_new):
        m = jnp.maximum(jnp.max(s_past, axis=0, keepdims=True), jnp.max(s_new, axis=0, keepdims=True))
        p_past = jnp.exp2(s_past - m).astype(BF16)
        p_new = jnp.exp2(s_new - m).astype(BF16)
        acc = (_dot_tn(p_past, with_ones(cp_ref[old[s], :]))
               + _dot_tn(p_new, with_ones(cn_ref[new[s], :])))
        return (acc[:, 0:MLA_LORA] / acc[:, MLA_LORA:MLA_LORA + 1]).astype(BF16)

    def outputs(s, lat):
        for h in range(MLA_HEADS):
            cols = slice(h * MLA_VDIM, (h + 1) * MLA_VDIM)
            o_ref[new[s], cols] = _dot(lat[h * seq:(h + 1) * seq], wv_ref[:, cols]).astype(BF16)

    sc = [scores(s) for s in streams]
    lat = [latents(s, *sc[s]) for s in streams]
    for s in streams:
        outputs(s, lat[s])


def _attn_sample(qext, kn_past, kp2_past, ckv_past, kn_new, kp2_new, ckv_new, wv, batch, past, seq):
    n = qext.shape[0]
    width = MLA_HEADS * LANES
    head_mask = jnp.asarray(np.kron(np.eye(MLA_HEADS, dtype=np.float32), np.ones((seq, LANES), np.float32)), BF16)
    group = SAMPLE_STREAMS_PER_STEP
    assert batch % group == 0
    seq, past = group * seq, group * past
    stream = lambda b: (b, 0)
    const = lambda b: (0, 0)
    return pl.pallas_call(
        functools.partial(_attn_sample_body, seq=seq // group),
        grid=(batch // group,),
        in_specs=[
            pl.BlockSpec((seq, 2 * width), stream),
            pl.BlockSpec(head_mask.shape, const),
            pl.BlockSpec((past, width), stream),
            pl.BlockSpec((past, LANES), stream),
            pl.BlockSpec((past, MLA_LORA), stream),
            pl.BlockSpec((seq, width), stream),
            pl.BlockSpec((seq, LANES), stream),
            pl.BlockSpec((seq, MLA_LORA), stream),
            pl.BlockSpec(wv.shape, const),
        ],
        out_specs=pl.BlockSpec((seq, MLA_HEADS * MLA_VDIM), stream),
        out_shape=jax.ShapeDtypeStruct((n, MLA_HEADS * MLA_VDIM), BF16),
        compiler_params=_params(("parallel",), 48),
        name="attn_sample",
    )(qext, head_mask, kn_past, kp2_past, ckv_past, kn_new, kp2_new, ckv_new, wv)


MERGE_COL_CHUNK = 256


def _branch_merge_body(og_ref, at_ref, wg_ref, wm_ref, ga_ref, gb_ref, u_ref):
    og = og_ref[...]
    at = at_ref[...]
    starts = list(range(0, u_ref.shape[1], MERGE_COL_CHUNK))

    def gates(lo):
        cols = slice(lo, lo + MERGE_COL_CHUNK)
        return _sigmoid(ga_ref[:, cols].astype(F32)), _sigmoid(gb_ref[:, cols].astype(F32))

    nxt = gates(starts[0])
    for g, lo in enumerate(starts):
        cols = slice(lo, lo + MERGE_COL_CHUNK)
        sa, sb = nxt
        ya = _dot(og, wg_ref[:, cols])
        yb = _dot(at, wm_ref[:, cols])
        nxt = gates(starts[g + 1]) if g + 1 < len(starts) else None
        u_ref[:, cols] = (sa * ya + sb * yb).astype(BF16)


def _out_proj_body(u_ref, wo_ref, x_ref, g2_ref, x1_ref, h2_ref):
    x1 = x_ref[...] + _dot(u_ref[...], wo_ref[...])
    x1_ref[...] = x1
    h2_ref[...] = (_row_rms(x1) * g2_ref[...]).astype(BF16)


def _merge(og, att, wg, wm, zm, wo, x, g2, tm_u, tm_x):
    n = x.shape[0]
    tn = 512
    u = pl.pallas_call(
        _branch_merge_body,
        grid=(n // tm_u, D_MODEL // tn),
        in_specs=[
            pl.BlockSpec((tm_u, D_MODEL), lambda i, j: (i, 0)),
            pl.BlockSpec((tm_u, D_MODEL), lambda i, j: (i, 0)),
            pl.BlockSpec((D_MODEL, tn), lambda i, j: (0, j)),
            pl.BlockSpec((D_MODEL, tn), lambda i, j: (0, j)),
            pl.BlockSpec((tm_u, tn), lambda i, j: (i, COL_GA // tn + j)),
            pl.BlockSpec((tm_u, tn), lambda i, j: (i, COL_GB // tn + j)),
        ],
        out_specs=pl.BlockSpec((tm_u, tn), lambda i, j: (i, j)),
        out_shape=jax.ShapeDtypeStruct((n, D_MODEL), BF16),
        compiler_params=_params(("parallel", "parallel"), 48),
        name="branch_merge",
    )(og, att, wg, wm, zm, zm)
    return pl.pallas_call(
        _out_proj_body,
        grid=(n // tm_x,),
        in_specs=[
            pl.BlockSpec((tm_x, D_MODEL), lambda i: (i, 0)),
            pl.BlockSpec((D_MODEL, D_MODEL), lambda i: (0, 0)),
            pl.BlockSpec((tm_x, D_MODEL), lambda i: (i, 0)),
            pl.BlockSpec((1, D_MODEL), lambda i: (0, 0)),
        ],
        out_specs=[
            pl.BlockSpec((tm_x, D_MODEL), lambda i: (i, 0)),
            pl.BlockSpec((tm_x, D_MODEL), lambda i: (i, 0)),
        ],
        out_shape=[
            jax.ShapeDtypeStruct((n, D_MODEL), F32),
            jax.ShapeDtypeStruct((n, D_MODEL), BF16),
        ],
        compiler_params=_params(("parallel",), 56),
        name="out_proj",
    )(u, wo, x, g2)


FFN_COL_CHUNK = 256
FFN_ROW_PARTS = 1


def _gelu_gate(a, a1, a2, cw, gt):
    c = cw[3:4, :] + cw[2:3, :] * a + cw[0:1, :] * a2 + cw[1:2, :] * a1
    return 0.5 * c * (1.0 + lax.erf(c * (2.0 ** -0.5))) * gt


def _ffn_up_seq_body(h_ref, wa_ref, wg_ref, cw_ref, hist_ref, act_ref, tail_ref, carry_ref, *, tm, tiles_per_seq):
    i = pl.program_id(0)
    j = pl.program_id(1)

    @pl.when(i % tiles_per_seq == 0)
    def _():
        carry_ref[j] = hist_ref[0]

    tr = tm // FFN_ROW_PARTS
    row = lax.broadcasted_iota(jnp.int32, (tr, 1), 0)
    pieces = [(lo, r0) for lo in range(0, act_ref.shape[1], FFN_COL_CHUNK) for r0 in range(0, tm, tr)]

    def project(lo, r0):
        h = h_ref[r0:r0 + tr, :]
        return _dot(h, wa_ref[:, lo:lo + FFN_COL_CHUNK]), _dot(h, wg_ref[:, lo:lo + FFN_COL_CHUNK])

    nxt = project(*pieces[0])
    prev = None
    for g, (lo, r0) in enumerate(pieces):
        cols = slice(lo, lo + FFN_COL_CHUNK)
        a, gt = nxt
        nxt = project(*pieces[g + 1]) if g + 1 < len(pieces) else None
        if r0 == 0:
            prev = carry_ref[j, :, cols]
        a1 = jnp.where(row == 0, prev[7:8, :], pltpu.roll(a, 1, axis=0))
        a2 = jnp.where(row == 0, prev[6:7, :], jnp.where(row == 1, prev[7:8, :], pltpu.roll(a, 2, axis=0)))
        act_ref[r0:r0 + tr, cols] = _gelu_gate(a, a1, a2, cw_ref[:, cols], gt).astype(BF16)
        prev = a[tr - 8:tr, :]
        if r0 + tr == tm:
            carry_ref[j, :, cols] = prev
            tail_ref[0, :, cols] = a[tr - (CONV_W - 1):tr, :]


def _ffn_up_seq(h2, wa, wg, cw, hist8, seq, tm):
    n = h2.shape[0]
    tn = 512
    nj = D_FF // tn
    tps = seq // tm
    return pl.pallas_call(
        functools.partial(_ffn_up_seq_body, tm=tm, tiles_per_seq=tps),
        grid=(n // tm, nj),
        in_specs=[
            pl.BlockSpec((tm, D_MODEL), lambda i, j: (i, 0)),
            pl.BlockSpec((D_MODEL, tn), lambda i, j: (0, j)),
            pl.BlockSpec((D_MODEL, tn), lambda i, j: (0, j)),
            pl.BlockSpec((8, tn), lambda i, j: (0, j)),
            pl.BlockSpec((1, 8, tn), lambda i, j: (i // tps, 0, j)),
        ],
        out_specs=[
            pl.BlockSpec((tm, tn), lambda i, j: (i, j)),
            pl.BlockSpec((1, CONV_W - 1, tn), lambda i, j: (i, 0, j)),
        ],
        out_shape=[
            jax.ShapeDtypeStruct((n, D_FF), BF16),
            jax.ShapeDtypeStruct((n // tm, CONV_W - 1, D_FF), F32),
        ],
        scratch_shapes=[pltpu.VMEM((nj, 8, tn), F32)],
        compiler_params=_params(("arbitrary", "arbitrary"), 48),
        name="ffn_up_seq",
    )(h2, wa, wg, cw, hist8)


def _ffn_up_multi_body(h_ref, wa_ref, wg_ref, cw_ref, p1_ref, p2_ref, act_ref, a_ref, wab_ref, wgb_ref, *, tm, seq):
    wa = wa_ref[...].astype(BF16)
    wg = wg_ref[...].astype(BF16)
    wab_ref[...] = wa
    wgb_ref[...] = wg
    a = _dot(h_ref[...], wa)
    gt = _dot(h_ref[...], wg)
    pos = lax.broadcasted_iota(jnp.int32, (tm, 1), 0) % seq
    a1 = jnp.where(pos == 0, p1_ref[...], pltpu.roll(a, 1, axis=0))
    a2 = jnp.where(pos <= 1, p2_ref[...], pltpu.roll(a, 2, axis=0))
    act_ref[...] = _gelu_gate(a, a1, a2, cw_ref[...], gt).astype(BF16)
    a_ref[...] = a


def _ffn_up_multi(h2, wup, cw, p1, p2, seq):
    n = h2.shape[0]
    tn = 512
    nj = D_FF // tn
    col = lambda j: (0, j)
    return pl.pallas_call(
        functools.partial(_ffn_up_multi_body, tm=n, seq=seq),
        grid=(nj,),
        in_specs=[
            pl.BlockSpec((n, D_MODEL), lambda j: (0, 0)),
            pl.BlockSpec((D_MODEL, tn), col),
            pl.BlockSpec((D_MODEL, tn), lambda j: (0, nj + j)),
            pl.BlockSpec((8, tn), col),
            pl.BlockSpec((n, tn), col),
            pl.BlockSpec((n, tn), col),
        ],
        out_specs=[pl.BlockSpec((n, tn), col)] * 2 + [pl.BlockSpec((D_MODEL, tn), col)] * 2,
        out_shape=[jax.ShapeDtypeStruct((n, D_FF), BF16), jax.ShapeDtypeStruct((n, D_FF), F32)]
        + [jax.ShapeDtypeStruct((D_MODEL, D_FF), BF16)] * 2,
        compiler_params=_params(("parallel",), 48),
        name="ffn_up_multi",
    )(h2, wup, wup, cw, p1, p2)


def _ffn_down_body(act_ref, wd_ref, x1_ref, o_ref):
    o_ref[...] = x1_ref[...] + _dot(act_ref[...], wd_ref[...])


def _ffn_down_cast_body(act_ref, wd_ref, x1_ref, o_ref, wdb_ref):
    wd = wd_ref[...].astype(BF16)
    wdb_ref[...] = wd
    o_ref[...] = x1_ref[...] + _dot(act_ref[...], wd)


def _ffn_down(act, wd, x1, tm, emit_bf16_weights=False):
    n = act.shape[0]
    tn = 512
    out_specs = [pl.BlockSpec((tm, tn), lambda i, j: (i, j))]
    out_shape = [jax.ShapeDtypeStruct((n, D_MODEL), F32)]
    if emit_bf16_weights:
        assert n == tm
        out_specs.append(pl.BlockSpec((D_FF, tn), lambda i, j: (0, j)))
        out_shape.append(jax.ShapeDtypeStruct((D_FF, D_MODEL), BF16))
    return pl.pallas_call(
        _ffn_down_cast_body if emit_bf16_weights else _ffn_down_body,
        grid=(n // tm, D_MODEL // tn),
        in_specs=[
            pl.BlockSpec((tm, D_FF), lambda i, j: (i, 0)),
            pl.BlockSpec((D_FF, tn), lambda i, j: (0, j)),
            pl.BlockSpec((tm, tn), lambda i, j: (i, j)),
        ],
        out_specs=out_specs,
        out_shape=out_shape,
        compiler_params=_params(("parallel", "parallel"), 56),
        name="ffn_down",
    )(act, wd, x1)


def _swap_halves(w):
    half = w.shape[-1] // 2
    return jnp.concatenate([w[..., half:], w[..., :half]], axis=-1)


def _rope_table(pos):
    half = MLA_ROPE // 2
    inv = ROPE_THETA ** (-np.arange(half, dtype=np.float64) * 2.0 / MLA_ROPE)
    ang = np.asarray(pos, np.float64)[:, None] * inv[None, :]
    cos, sin = np.cos(ang), np.sin(ang)
    return jnp.asarray(np.concatenate([cos, cos, -sin, sin], axis=-1), F32)


def _layer_weights(w_in, g_norm1, gla_w_gate2, gla_b_gate, gla_g_out, w_br_gla, mla_g_qlat, mla_w_uq,
                   mla_g_kvlat, mla_w_ukv, mla_g_q, mla_g_k, mla_g_qpe, mla_g_kpe, w_br_mla, w_out,
                   g_norm2, ffn_w_up, ffn_conv_w, ffn_conv_b, ffn_w_down):
    w2ext = jnp.zeros((SMALL_COLS, GLA_HEADS * GLA_HK), F32).at[SMALL_LR:SMALL_LR + GLA_GATE_RANK].set(
        gla_w_gate2).astype(BF16)
    wq = mla_w_uq.reshape(MLA_LORA, MLA_HEADS, MLA_NOPE + MLA_ROPE)
    wqn = wq[:, :, :MLA_NOPE].reshape(MLA_LORA, MLA_HEADS * MLA_NOPE).astype(BF16)
    wq_pe = wq[:, :, MLA_NOPE:]
    wqp = jnp.concatenate([wq_pe, _swap_halves(wq_pe)], axis=-1).reshape(MLA_LORA, MLA_HEADS * LANES).astype(BF16)
    wkv = mla_w_ukv.reshape(MLA_LORA, MLA_HEADS, MLA_NOPE + MLA_VDIM)
    wk = wkv[:, :, :MLA_NOPE].reshape(MLA_LORA, MLA_HEADS * MLA_NOPE).astype(BF16)
    wv = wkv[:, :, MLA_NOPE:].reshape(MLA_LORA, MLA_HEADS * MLA_VDIM).astype(BF16)
    cw = jnp.concatenate([ffn_conv_w, ffn_conv_b[None, :], jnp.zeros((8 - CONV_W - 1, D_FF), F32)], axis=0)
    return dict(
        w_in_t=w_in.T, g1=g_norm1[None, :], w2ext=w2ext, bg=gla_b_gate[None, :],
        go=gla_g_out[None, :], wg=w_br_gla.astype(BF16), wqn=wqn, wqp=wqp, gql=mla_g_qlat[None, :],
        gkl=mla_g_kvlat[None, :], gq=jnp.tile(mla_g_q, 2)[None, :],
        gqp=jnp.tile(jnp.concatenate([mla_g_qpe, _swap_halves(mla_g_qpe)]), 2)[None, :],
        gkp=jnp.concatenate([mla_g_kpe, _swap_halves(mla_g_kpe)])[None, :],
        wk=wk, wv=wv, wvt=wv.T, gk=jnp.tile(mla_g_k, 2)[None, :], wm=w_br_mla.astype(BF16),
        wo=w_out.astype(BF16),
        g2=g_norm2[None, :], wup=ffn_w_up, cw=cw, wd=ffn_w_down,
        p128=jnp.asarray(np.kron(np.eye(2, dtype=np.float32), np.full((LANES, LANES), 1.0 / LANES, np.float32)), BF16),
    )


def _trunk_front(x, w, s0, batch, seq, chunk, gla_tb, tm_in, tm, tab, tab_blocks, with_values):
    if "w_main" not in w:
        w["w_main"], narrow, zm, zs = _regroup_in_proj(x, w["g1"], w["w_in_t"])
        w["w_small"] = _small_weight_rows(narrow).astype(BF16)
    else:
        zm, zs = _in_proj(x, w["g1"], w["w_main"], w["w_small"], tm_in)
    og, s_new = _gla(zm, zs, w["w2ext"], w["bg"], w["go"], s0, batch, seq, chunk, gla_tb)
    qext, ckv, kpe, kp2 = _mla_proj(zm, zs, w["wqn"], w["wqp"], w["gql"], w["gkl"], w["gq"], w["gqp"],
                                    w["gkp"], tab, w["p128"], tm, tab_blocks)
    kv = _mla_expand(ckv, w["wk"], w["gk"], w["p128"], tm, w["wvt"] if with_values else None)
    return zm, og, s_new, qext, ckv, kpe, kp2, kv


def kernel(x_prompt, x_sample, state_gla, cache_mla_ckv, cache_mla_kpe, cache_ffn_conv, w_in, g_norm1, gla_w_gate2, gla_b_gate, gla_g_out, w_br_gla, mla_g_qlat, mla_w_uq, mla_g_kvlat, mla_w_ukv, mla_g_q, mla_g_k, mla_g_qpe, mla_g_kpe, w_br_mla, w_out, g_norm2, ffn_w_up, ffn_conv_w, ffn_conv_b, ffn_w_down):
    bp, tp, _ = x_prompt.shape
    bs, ts, _ = x_sample.shape
    depth = w_in.shape[0]
    past = cache_mla_ckv.shape[2]
    np_rows, ns_rows = bp * tp, bs * ts
    tm_p = 512
    tm_big = 1024
    tab_p = _rope_table(np.arange(tp))
    tab_s = jnp.tile(_rope_table(past + np.arange(ts)), (bs, 1))
    xp = x_prompt.reshape(np_rows, D_MODEL)
    xs = x_sample.reshape(ns_rows, D_MODEL)
    outs = [[] for _ in range(8)]
    layer_weights = (w_in, g_norm1, gla_w_gate2, gla_b_gate, gla_g_out, w_br_gla, mla_g_qlat, mla_w_uq,
                     mla_g_kvlat, mla_w_ukv, mla_g_q, mla_g_k, mla_g_qpe, mla_g_kpe, w_br_mla, w_out,
                     g_norm2, ffn_w_up, ffn_conv_w, ffn_conv_b, ffn_w_down)
    for l in range(depth):
        w = _layer_weights(*[a[l] for a in layer_weights])

        zm, og, ss, qext, ckv_s, kpe_s, kp2, (kn,) = _trunk_front(
            xs, w, state_gla[l], bs, ts, ts, ts, ns_rows, ns_rows, tab_s, 1, False)
        ckv_past = cache_mla_ckv[l].reshape(bs * past, MLA_LORA)
        (kn_past,) = _mla_expand(ckv_past, w["wk"], w["gk"], w["p128"], 512)
        kpe_past = cache_mla_kpe[l].reshape(bs * past, MLA_ROPE)
        kp2_past = jnp.concatenate([kpe_past, kpe_past], axis=-1).astype(BF16)
        att = _attn_sample(qext, kn_past, kp2_past, ckv_past, kn, kp2, ckv_s, w["wv"], bs, past, ts)
        x1, h2 = _merge(og, att, w["wg"], w["wm"], zm, w["wo"], xs, w["g2"], ns_rows, ns_rows)
        hist = cache_ffn_conv[l]
        zrow = jnp.zeros((bs, ts - 1, D_FF), F32)
        p1 = jnp.concatenate([hist[:, 1:2], zrow], axis=1).reshape(ns_rows, D_FF)
        p2 = jnp.concatenate([hist, zrow[:, 1:]], axis=1).reshape(ns_rows, D_FF)
        act, a_full, wa_bf, wg_bf = _ffn_up_multi(h2, w["wup"], w["cw"], p1, p2, ts)
        fs = a_full.reshape(bs, ts, D_FF)[:, ts - (CONV_W - 1):]
        xs, wd_bf = _ffn_down(act, w["wd"], x1, ns_rows, emit_bf16_weights=True)

        s0 = jnp.zeros((bp, GLA_HEADS, GLA_HK, GLA_HV), F32)
        zm, og, sp, qext, ckv_p, kpe_p, kp2, (kn, vt) = _trunk_front(
            xp, w, s0, bp, tp, CHUNK, 512, tm_big, tm_p, tab_p, tp // tm_p, True)
        att = _attn_prompt(qext, kn, kp2, vt, bp, tp, 512)
        x1, h2 = _merge(og, att, w["wg"], w["wm"], zm, w["wo"], xp, w["g2"], tm_big, tm_p)
        hist8 = jnp.zeros((bp, 8, D_FF), F32)
        act, tails = _ffn_up_seq(h2, wa_bf, wg_bf, w["cw"], hist8, tp, tm_big)
        fp = tails.reshape(bp, tp // tm_big, CONV_W - 1, D_FF)[:, -1]
        (xp,) = _ffn_down(act, wd_bf, x1, tm_big)

        for lst, val in zip(outs, (sp, ss, ckv_p.reshape(bp, tp, MLA_LORA), ckv_s.reshape(bs, ts, MLA_LORA),
                                   kpe_p.reshape(bp, tp, MLA_ROPE), kpe_s.reshape(bs, ts, MLA_ROPE), fp, fs)):
            lst.append(val)
    return (xp.reshape(bp, tp, D_MODEL), xs.reshape(bs, ts, D_MODEL)) + tuple(jnp.stack(o, 0) for o in outs)
```

```python
import functools

import numpy as np
import jax
import jax.numpy as jnp
from jax import lax
from jax.experimental import pallas as pl
from jax.experimental.pallas import tpu as pltpu

F32 = jnp.float32
BF16 = jnp.bfloat16

D_MODEL = 2048
CHUNK = 64
EPS = 1e-6
GLA_HEADS = 4
GLA_HK = 256
GLA_HV = 512
GLA_GATE_RANK = 16
GLA_GATE_TAU = 16.0
MLA_HEADS = 16
MLA_LORA = 512
MLA_NOPE = 128
MLA_ROPE = 64
MLA_VDIM = 128
MLA_SCALE = (MLA_NOPE + MLA_ROPE) ** -0.5
LOG2E = float(np.log2(np.e))
Q_SCALE = MLA_SCALE * LOG2E
ROPE_THETA = 10000.0
D_FF = 5632
CONV_W = 3
LANES = 128
NEG_BIG = -1e30
ATTN_UNROLL = 8
ATTN_QUERY_SPLIT = 2

COL_Q, COL_K, COL_V, COL_R, COL_MQ, COL_MKV, COL_GA, COL_GB, MAIN_COLS = (
    0, 1024, 2048, 4096, 6144, 6656, 7168, 9216, 11264)
SMALL_COLS = 256
SMALL_LR = 128

MIB = 1024 * 1024


def _params(semantics, vmem_mib):
    return pltpu.CompilerParams(dimension_semantics=semantics, vmem_limit_bytes=vmem_mib * MIB)


def _dot(a, b):
    return jnp.dot(a, b, preferred_element_type=F32)


def _dot_nt(a, b):
    return lax.dot_general(a, b, (((1,), (1,)), ((), ())), preferred_element_type=F32)


def _dot_tn(a, b):
    return lax.dot_general(a, b, (((0,), (0,)), ((), ())), preferred_element_type=F32)


def _sigmoid(x):
    return 1.0 / (1.0 + jnp.exp(-x))


def _row_rms(x):
    return x * lax.rsqrt(jnp.mean(x * x, axis=-1, keepdims=True) + EPS)


def _slab_rms(x, p_ref):
    ms = _dot((x * x).astype(BF16), p_ref[...])
    return x * lax.rsqrt(ms + EPS)


W_IN_SEGMENTS = ((0, COL_R, 0), (COL_R, COL_GA, GLA_GATE_RANK), (COL_GA, MAIN_COLS, GLA_GATE_RANK + MLA_ROPE))


def _small_weight_rows(narrow):
    lr = narrow[0:GLA_GATE_RANK]
    kpe = narrow[GLA_GATE_RANK:GLA_GATE_RANK + MLA_ROPE]
    half = MLA_ROPE // 2
    pad = jnp.zeros((SMALL_COLS - SMALL_LR - GLA_GATE_RANK, D_MODEL), narrow.dtype)
    return jnp.concatenate([kpe, kpe[half:], kpe[:half], lr, pad], axis=0)


def _regroup_body(x_ref, g_ref, main_ref, next_ref, o_ref, narrow_ref, zm_ref, zs_ref, h_ref, *, tr):
    j = pl.program_id(0)

    @pl.when(j == 0)
    def _():
        narrow_ref[...] = jnp.zeros(narrow_ref.shape, F32)
        h_ref[...] = (_row_rms(x_ref[...]) * g_ref[...]).astype(BF16)

    _regroup_tile(j, main_ref, next_ref, o_ref, narrow_ref, tr)
    zm_ref[...] = _dot_nt(h_ref[...], o_ref[...]).astype(BF16)

    @pl.when(j == pl.num_programs(0) - 1)
    def _():
        zs_ref[...] = _dot_nt(h_ref[...], _small_weight_rows(narrow_ref[...]).astype(BF16))


def _regroup_tile(j, main_ref, next_ref, o_ref, narrow_ref, tr):
    prev = 0
    for lo, hi, shift in W_IN_SEGMENTS:
        if shift > prev:
            @pl.when(j == lo // tr)
            def _(prev=prev, shift=shift):
                narrow_ref[prev:shift, :] = main_ref[prev:shift, :]

        @pl.when((j >= lo // tr) & (j < hi // tr))
        def _(shift=shift):
            if shift == 0:
                o_ref[...] = main_ref[...].astype(BF16)
            else:
                o_ref[0:tr - shift, :] = main_ref[shift:tr, :].astype(BF16)
                o_ref[tr - shift:tr, :] = next_ref[0:shift, :].astype(BF16)
        prev = shift


def _regroup_in_proj(x, g, w_in_t):
    n = x.shape[0]
    tr = 1024
    assert all(lo % tr == 0 and hi % tr == 0 and shift % 16 == 0 and shift < LANES
               for lo, hi, shift in W_IN_SEGMENTS)
    const = lambda j: (0, 0)
    return pl.pallas_call(
        functools.partial(_regroup_body, tr=tr),
        grid=(MAIN_COLS // tr,),
        in_specs=[
            pl.BlockSpec((n, D_MODEL), const),
            pl.BlockSpec((1, D_MODEL), const),
            pl.BlockSpec((tr, D_MODEL), lambda j: (j, 0)),
            pl.BlockSpec((LANES, D_MODEL), lambda j: ((j + 1) * (tr // LANES), 0)),
        ],
        out_specs=[
            pl.BlockSpec((tr, D_MODEL), lambda j: (j, 0)),
            pl.BlockSpec((LANES, D_MODEL), const),
            pl.BlockSpec((n, tr), lambda j: (0, j)),
            pl.BlockSpec((n, SMALL_COLS), const),
        ],
        out_shape=[
            jax.ShapeDtypeStruct((MAIN_COLS, D_MODEL), BF16),
            jax.ShapeDtypeStruct((LANES, D_MODEL), F32),
            jax.ShapeDtypeStruct((n, MAIN_COLS), BF16),
            jax.ShapeDtypeStruct((n, SMALL_COLS), F32),
        ],
        scratch_shapes=[pltpu.VMEM((n, D_MODEL), BF16)],
        compiler_params=_params(("arbitrary",), 48),
        name="regroup_in_proj",
    )(x, g, w_in_t, w_in_t)


def _in_proj_body(x_ref, g_ref, wm_ref, ws_ref, zm_ref, zs_ref, h_ref):
    @pl.when(pl.program_id(1) == 0)
    def _():
        h_ref[...] = (_row_rms(x_ref[...]) * g_ref[...]).astype(BF16)
        zs_ref[...] = _dot_nt(h_ref[...], ws_ref[...])

    zm_ref[...] = _dot_nt(h_ref[...], wm_ref[...]).astype(BF16)


def _in_proj(x, g, wm, ws, tm):
    n = x.shape[0]
    tn = 1024
    return pl.pallas_call(
        _in_proj_body,
        grid=(n // tm, MAIN_COLS // tn),
        in_specs=[
            pl.BlockSpec((tm, D_MODEL), lambda i, j: (i, 0)),
            pl.BlockSpec((1, D_MODEL), lambda i, j: (0, 0)),
            pl.BlockSpec((tn, D_MODEL), lambda i, j: (j, 0)),
            pl.BlockSpec((SMALL_COLS, D_MODEL), lambda i, j: (0, 0)),
        ],
        out_specs=[
            pl.BlockSpec((tm, tn), lambda i, j: (i, j)),
            pl.BlockSpec((tm, SMALL_COLS), lambda i, j: (i, 0)),
        ],
        out_shape=[
            jax.ShapeDtypeStruct((n, MAIN_COLS), BF16),
            jax.ShapeDtypeStruct((n, SMALL_COLS), F32),
        ],
        scratch_shapes=[pltpu.VMEM((tm, D_MODEL), BF16)],
        compiler_params=_params(("parallel", "arbitrary"), 56),
        name="in_proj",
    )(x, g, wm, ws)


def _gla_tables(c):
    levels = int(np.log2(c))
    t = np.arange(c)[:, None]
    u = np.arange(c)[None, :]
    masks = [(u == t)]
    level2 = None
    for l in range(levels):
        m = c >> (l + 1)
        mid_t = (t // (2 * m)) * 2 * m + m
        upper = t >= mid_t
        if m == 2:
            level2 = np.where(upper, (u >= mid_t) & (u <= t), (u > t) & (u < mid_t))
        mid_u = (u // (2 * m)) * 2 * m + m
        masks.append((t // (2 * m) == u // (2 * m)) & upper & (u < mid_u))
    eye_h = np.eye(GLA_HEADS, dtype=np.float32)
    gmat = np.concatenate([np.kron(eye_h, (u <= t).astype(np.float32)),
                           np.kron(eye_h, level2.astype(np.float32))], axis=0)
    group = _gla_group_heads(c)
    masks = np.stack([np.kron(np.eye(group, dtype=np.float32), mk.astype(np.float32)) for mk in masks])
    return jnp.asarray(gmat, BF16), jnp.asarray(masks), levels


def _gla_group_heads(c):
    return min(GLA_HEADS, max(1, LANES // c))


def _gla_level_exponent(b, log_a, level2, m, c):
    if m == 1:
        row = lax.broadcasted_iota(jnp.int32, (c, 1), 0)
        return jnp.where(row % 2 == 1, log_a, 0.0)
    if m == 2:
        return level2
    parts = [jnp.broadcast_to(b[i + m - 1:i + m, :], (2 * m, GLA_HK)) for i in range(0, c, 2 * m)]
    ref = parts[0] if len(parts) == 1 else jnp.concatenate(parts, axis=0)
    return -jnp.abs(b - ref)


GLA_CHUNKS_PER_TRIP = 4


def _gla_body(q_ref, k_ref, v_ref, r_ref, zs_ref, w2_ref, bg_ref, go_ref, s0_ref, gmat_ref, mask_ref,
              og_ref, sout_ref, st_ref, *, c, nchunk, levels, single):
    t = pl.program_id(1)

    if not single:
        @pl.when(t == 0)
        def _():
            for h in range(GLA_HEADS):
                st_ref[h] = jnp.transpose(s0_ref[0, h])

    nrow = GLA_HEADS * c
    gw = _gla_group_heads(c) * c

    def stack(x, width):
        return jnp.concatenate([x[:, h * width:(h + 1) * width] for h in range(GLA_HEADS)], axis=0)

    def diag_blocks(prod):
        return [prod[g:g + gw, g:g + gw] for g in range(0, nrow, gw)]

    def decays(ci):
        rows = pl.ds(0, c) if single else pl.ds(pl.multiple_of(ci * c, c), c)
        qb = stack(q_ref[rows, :], GLA_HK) * (GLA_HK ** -0.5)
        kb = stack(k_ref[rows, :], GLA_HK)
        q = qb.astype(F32)
        k = kb.astype(F32)
        v = stack(v_ref[rows, :], GLA_HV)
        x = stack(_dot(zs_ref[rows, :].astype(BF16), w2_ref[...]) + bg_ref[...], GLA_HK) * LOG2E
        log_a = (jnp.minimum(x, 0.0) - jnp.log2(1.0 + jnp.exp2(-jnp.abs(x)))) * (1.0 / GLA_GATE_TAU)
        hi = log_a.astype(BF16)
        lo = (log_a - hi.astype(F32)).astype(BF16)
        gm = gmat_ref[...]
        pre = _dot(gm, hi) + _dot(gm, lo)
        b = pre[0:nrow]
        b_last = [b[h * c + c - 1:h * c + c, :] for h in range(GLA_HEADS)]
        q_in = (q * jnp.exp2(b)).astype(BF16)
        k_out = (k * jnp.exp2(jnp.concatenate([jnp.broadcast_to(r, (c, GLA_HK)) for r in b_last], axis=0)
                              - b)).astype(BF16)
        att = [mask_ref[0] * blk for blk in diag_blocks(_dot_nt(qb, kb))]
        return dict(rows=rows, qb=qb, kb=kb, v=v, log_a=log_a, hi=hi, lo=lo, b=b, level2=pre[nrow:2 * nrow],
                    b_last=b_last, q_in=q_in, k_out=k_out, att=att)

    def level(ch, l):
        d = jnp.exp2(_gla_level_exponent(ch["b"], ch["log_a"], ch["level2"], c >> (l + 1), nrow)).astype(BF16)
        prod = _dot_nt(ch["qb"] * d, ch["kb"] * d)
        ch["att"] = [a + mask_ref[1 + l] * blk for a, blk in zip(ch["att"], diag_blocks(prod))]

    def intra(ch):
        att = ch["att"]
        if len(att) == 1:
            att_full = att[0].astype(BF16)
        else:
            zero = jnp.zeros((gw, gw), BF16)
            att_full = jnp.concatenate(
                [jnp.concatenate([a.astype(BF16) if i == j else zero for j in range(len(att))], axis=1)
                 for i, a in enumerate(att)], axis=0)
        ch["o_intra"] = _dot(att_full, ch["v"])

    def carried(ch):
        v, q_in, k_out, o_intra = ch["v"], ch["q_in"], ch["k_out"], ch["o_intra"]
        outs = []
        for h in range(GLA_HEADS):
            hr = slice(h * c, (h + 1) * c)
            if single:
                s0 = s0_ref[0, h]
                outs.append(o_intra[hr] + _dot(q_in[hr], s0.astype(BF16)))
                ones = jnp.ones((c, LANES), BF16)
                decay = jnp.exp2(_dot_tn(ch["hi"][hr], ones) + _dot_tn(ch["lo"][hr], ones))
                sout_ref[0, h] = (s0 * jnp.concatenate([decay] * (GLA_HV // LANES), axis=1)
                                  + _dot_tn(k_out[hr], v[hr]))
            else:
                st = st_ref[h]
                outs.append(o_intra[hr] + _dot_nt(q_in[hr], st.astype(BF16)))
                st_ref[h] = st * jnp.exp2(ch["b_last"][h]) + _dot_tn(v[hr], k_out[hr])
        return outs

    def emit(ch, outs):
        for h, o in enumerate(outs):
            cv = slice(h * GLA_HV, (h + 1) * GLA_HV)
            gate = r_ref[ch["rows"], cv].astype(F32)
            og = _row_rms(o) * go_ref[...] * (gate * _sigmoid(gate))
            og_ref[ch["rows"], cv] = og.astype(BF16)

    def chunks(first, count):
        group = [decays(first + i) for i in range(count)]
        for l in range(levels):
            for ch in group:
                level(ch, l)
        for ch in group:
            intra(ch)
        pending = None
        for ch in group:
            outs = carried(ch)
            if pending is not None:
                emit(*pending)
            pending = (ch, outs)
        emit(*pending)

    if single:
        chunks(0, 1)
        return
    per_trip = GLA_CHUNKS_PER_TRIP if nchunk % GLA_CHUNKS_PER_TRIP == 0 else 1

    def trip(gi, carry):
        chunks(gi * per_trip, per_trip)
        return carry

    lax.fori_loop(0, nchunk // per_trip, trip, 0)

    @pl.when(t == pl.num_programs(1) - 1)
    def _():
        for h in range(GLA_HEADS):
            sout_ref[0, h] = jnp.transpose(st_ref[h])


def _gla(zm, zs, w2ext, bg, go, s0, batch, seq, c, tb):
    n = zm.shape[0]
    nt = seq // tb
    gmat, masks, levels = _gla_tables(c)
    dk, dv = GLA_HEADS * GLA_HK, GLA_HEADS * GLA_HV
    const2 = lambda b, t: (0, 0)
    return pl.pallas_call(
        functools.partial(_gla_body, c=c, nchunk=tb // c, levels=levels, single=(seq == c)),
        grid=(batch, nt),
        in_specs=[
            pl.BlockSpec((tb, dk), lambda b, t: (b * nt + t, COL_Q // dk)),
            pl.BlockSpec((tb, dk), lambda b, t: (b * nt + t, COL_K // dk)),
            pl.BlockSpec((tb, dv), lambda b, t: (b * nt + t, COL_V // dv)),
            pl.BlockSpec((tb, dv), lambda b, t: (b * nt + t, COL_R // dv)),
            pl.BlockSpec((tb, SMALL_COLS), lambda b, t: (b * nt + t, 0)),
            pl.BlockSpec((SMALL_COLS, dk), const2),
            pl.BlockSpec((1, dk), const2),
            pl.BlockSpec((1, GLA_HV), const2),
            pl.BlockSpec((1, GLA_HEADS, GLA_HK, GLA_HV), lambda b, t: (b, 0, 0, 0)),
            pl.BlockSpec(gmat.shape, const2),
            pl.BlockSpec(masks.shape, lambda b, t: (0, 0, 0)),
        ],
        out_specs=[
            pl.BlockSpec((tb, dv), lambda b, t: (b * nt + t, 0)),
            pl.BlockSpec((1, GLA_HEADS, GLA_HK, GLA_HV), lambda b, t: (b, 0, 0, 0)),
        ],
        out_shape=[
            jax.ShapeDtypeStruct((n, dv), BF16),
            jax.ShapeDtypeStruct((batch, GLA_HEADS, GLA_HK, GLA_HV), F32),
        ],
        scratch_shapes=[pltpu.VMEM((GLA_HEADS, GLA_HV, GLA_HK), F32)],
        compiler_params=_params(("parallel", "arbitrary"), 40),
        name="gla",
    )(zm, zm, zm, zm, zs, w2ext, bg, go, s0, gmat, masks)


def _mla_proj_body(mq_ref, mkv_ref, zs_ref, wqn_ref, wqp_ref, gql_ref, gkl_ref, gq_ref, gqp_ref, gkp_ref,
                   tab_ref, p_ref, q_ref, ckv_ref, kpe_ref, kp2_ref):
    qlat = (_row_rms(mq_ref[...].astype(F32)) * gql_ref[...]).astype(BF16)
    tab = tab_ref[...]
    tab2 = jnp.concatenate([tab, tab], axis=1)
    starts = list(range(0, MLA_HEADS * LANES, 2 * LANES))

    def project(lo):
        cols = slice(lo, lo + 2 * LANES)
        return _dot(qlat, wqn_ref[:, cols]), _dot(qlat, wqp_ref[:, cols])

    cur = project(starts[0])
    for g, lo in enumerate(starts):
        nxt = project(starts[g + 1]) if g + 1 < len(starts) else None
        nope = (_slab_rms(cur[0], p_ref) * (gq_ref[...] * Q_SCALE)).astype(BF16)
        pe = (_slab_rms(cur[1], p_ref) * (gqp_ref[...] * Q_SCALE) * tab2).astype(BF16)
        for i in range(2):
            dst = 2 * lo + i * 2 * LANES
            q_ref[:, dst:dst + LANES] = nope[:, i * LANES:(i + 1) * LANES]
            q_ref[:, dst + LANES:dst + 2 * LANES] = pe[:, i * LANES:(i + 1) * LANES]
        cur = nxt
    ckv_ref[...] = _row_rms(mkv_ref[...].astype(F32)) * gkl_ref[...]
    slab = zs_ref[:, 0:LANES]
    rot = _row_rms(slab) * gkp_ref[...] * tab
    kp2 = rot + pltpu.roll(rot, MLA_ROPE, axis=1)
    kpe_ref[...] = kp2[:, 0:MLA_ROPE]
    kp2_ref[...] = kp2.astype(BF16)


def _mla_proj(zm, zs, wqn, wqp, gql, gkl, gq, gqp, gkp, tab, p128, tm, tab_blocks):
    n = zm.shape[0]
    const = lambda i: (0, 0)
    return pl.pallas_call(
        _mla_proj_body,
        grid=(n // tm,),
        in_specs=[
            pl.BlockSpec((tm, MLA_LORA), lambda i: (i, COL_MQ // MLA_LORA)),
            pl.BlockSpec((tm, MLA_LORA), lambda i: (i, COL_MKV // MLA_LORA)),
            pl.BlockSpec((tm, SMALL_COLS), lambda i: (i, 0)),
            pl.BlockSpec(wqn.shape, const),
            pl.BlockSpec(wqp.shape, const),
            pl.BlockSpec((1, MLA_LORA), const),
            pl.BlockSpec((1, MLA_LORA), const),
            pl.BlockSpec((1, 2 * LANES), const),
            pl.BlockSpec((1, 2 * LANES), const),
            pl.BlockSpec((1, LANES), const),
            pl.BlockSpec((tm, LANES), lambda i: (i % tab_blocks, 0)),
            pl.BlockSpec((2 * LANES, 2 * LANES), const),
        ],
        out_specs=[
            pl.BlockSpec((tm, 2 * LANES * MLA_HEADS), lambda i: (i, 0)),
            pl.BlockSpec((tm, MLA_LORA), lambda i: (i, 0)),
            pl.BlockSpec((tm, MLA_ROPE), lambda i: (i, 0)),
            pl.BlockSpec((tm, LANES), lambda i: (i, 0)),
        ],
        out_shape=[
            jax.ShapeDtypeStruct((n, 2 * LANES * MLA_HEADS), BF16),
            jax.ShapeDtypeStruct((n, MLA_LORA), F32),
            jax.ShapeDtypeStruct((n, MLA_ROPE), F32),
            jax.ShapeDtypeStruct((n, LANES), BF16),
        ],
        compiler_params=_params(("parallel",), 48),
        name="mla_proj",
    )(zm, zm, zs, wqn, wqp, gql, gkl, gq, gqp, gkp, tab, p128)


def _mla_expand_keys(c_ref, wk_ref, gk_ref, p_ref, kn_ref):
    cb = c_ref[...].astype(BF16)
    starts = list(range(0, MLA_HEADS * LANES, 2 * LANES))
    cur = _dot(cb, wk_ref[:, starts[0]:starts[0] + 2 * LANES])
    for g, lo in enumerate(starts):
        nxt = _dot(cb, wk_ref[:, starts[g + 1]:starts[g + 1] + 2 * LANES]) if g + 1 < len(starts) else None
        kn_ref[:, lo:lo + 2 * LANES] = (_slab_rms(cur, p_ref) * gk_ref[...]).astype(BF16)
        cur = nxt
    return cb


def _mla_expand_k_body(c_ref, wk_ref, gk_ref, p_ref, kn_ref):
    _mla_expand_keys(c_ref, wk_ref, gk_ref, p_ref, kn_ref)


def _mla_expand_kv_body(c_ref, wk_ref, gk_ref, p_ref, wvt_ref, kn_ref, vt_ref):
    cb = _mla_expand_keys(c_ref, wk_ref, gk_ref, p_ref, kn_ref)
    vt = _dot_nt(wvt_ref[...], cb).astype(BF16)
    for h in range(MLA_HEADS):
        vt_ref[h, 0] = vt[h * MLA_VDIM:(h + 1) * MLA_VDIM, :]


def _mla_expand(ckv, wk, gk, p128, tm, wvt=None):
    n = ckv.shape[0]
    const = lambda i: (0, 0)
    width = MLA_HEADS * LANES
    in_specs = [
        pl.BlockSpec((tm, MLA_LORA), lambda i: (i, 0)),
        pl.BlockSpec(wk.shape, const),
        pl.BlockSpec((1, 2 * LANES), const),
        pl.BlockSpec((2 * LANES, 2 * LANES), const),
    ]
    out_specs = [pl.BlockSpec((tm, width), lambda i: (i, 0))]
    out_shape = [jax.ShapeDtypeStruct((n, width), BF16)]
    args = [ckv, wk, gk, p128]
    body = _mla_expand_k_body
    if wvt is not None:
        body = _mla_expand_kv_body
        in_specs.append(pl.BlockSpec(wvt.shape, const))
        out_specs.append(pl.BlockSpec((MLA_HEADS, 1, MLA_VDIM, tm), lambda i: (0, i, 0, 0)))
        out_shape.append(jax.ShapeDtypeStruct((MLA_HEADS, n // tm, MLA_VDIM, tm), BF16))
        args.append(wvt)
    return pl.pallas_call(
        body,
        grid=(n // tm,),
        in_specs=in_specs,
        out_specs=out_specs,
        out_shape=out_shape,
        compiler_params=_params(("parallel",), 40),
        name="mla_expand",
    )(*args)


def _attn_prompt_body(qi_tab, kb_tab, q_ref, kn_ref, kp_ref, vt_ref, o_ref,
                      kext_ref, vx_ref, bias_ref, m_ref, acc_ref, s0, s1, p0, p1, a0, a1, *, tq, nq):
    @pl.when((pl.program_id(0) == 0) & (pl.program_id(1) == 0))
    def _():
        kc = lax.broadcasted_iota(jnp.int32, (tq, tq), 0) // CHUNK
        qc = lax.broadcasted_iota(jnp.int32, (tq, tq), 1) // CHUNK
        bias_ref[...] = jnp.where(kc <= qc, 0.0, NEG_BIG)

    kext_ref[:, 0:LANES] = kn_ref[...]
    kext_ref[:, LANES:2 * LANES] = kp_ref[...]
    vx_ref[:, 0:MLA_VDIM, :] = vt_ref[0]
    vx_ref[:, MLA_VDIM:, :] = jnp.ones((nq, vx_ref.shape[1] - MLA_VDIM, tq), BF16)
    m_ref[...] = jnp.full(m_ref.shape, NEG_BIG, F32)
    acc_ref[...] = jnp.zeros(acc_ref.shape, F32)
    nblk = nq * (nq + 1) // 2
    s_bufs, p_bufs, a_bufs = (s0, s1), (p0, p1), (a0, a1)

    def rows(i):
        return pl.ds(pl.multiple_of(i * tq, tq), tq)

    qw = tq // ATTN_QUERY_SPLIT

    def diagonal(t):
        return isinstance(t, int) and t < nq

    def keys_used(t, lo):
        return lo + qw if diagonal(t) else tq

    def scores(t, par, lo):
        nk = keys_used(t, lo)
        q = q_ref[pl.ds(pl.multiple_of(qi_tab[t] * tq + lo, qw), qw), :]
        k = kext_ref[pl.ds(pl.multiple_of(kb_tab[t] * tq, tq), nk), :]
        s_bufs[par][0:nk, lo:lo + qw] = _dot_nt(k, q)

    def softmax(t, par, lo):
        qi = qi_tab[t]
        nk = keys_used(t, lo)
        cs = slice(lo, lo + qw)
        s = s_bufs[par][0:nk, cs]
        if diagonal(t):
            s = s + bias_ref[0:nk, cs]
        m_old = m_ref[qi, :, cs]
        m_new = jnp.maximum(m_old, jnp.max(s, axis=0, keepdims=True))
        a_bufs[par][:, cs] = jnp.exp2(m_old - m_new)
        m_ref[qi, :, cs] = m_new
        p_bufs[par][0:nk, cs] = jnp.exp2(s - m_new).astype(BF16)

    def values(t, par, lo):
        qi = qi_tab[t]
        nk = keys_used(t, lo)
        cs = slice(lo, lo + qw)
        acc_ref[qi, :, cs] = (a_bufs[par][:, cs] * acc_ref[qi, :, cs]
                              + _dot(vx_ref[kb_tab[t]][:, 0:nk], p_bufs[par][0:nk, cs]))

    slices = range(0, tq, qw)

    def step(u, par):
        for lo in slices:
            softmax(u - 1, 1 - par, lo)
            scores(u, par, lo)
            values(u - 2, par, lo)

    for lo in slices:
        scores(0, 0, lo)
    for lo in slices:
        scores(1, 1, lo)
        softmax(0, 0, lo)
    first_loop_step = nq + 2
    for u in range(2, first_loop_step):
        step(u, u % 2)
    trips = (nblk - first_loop_step) // ATTN_UNROLL

    def trip(j, carry):
        for i in range(ATTN_UNROLL):
            step(first_loop_step + ATTN_UNROLL * j + i, i % 2)
        return carry

    lax.fori_loop(0, trips, trip, 0)
    for u in range(first_loop_step + trips * ATTN_UNROLL, nblk):
        step(u, u % 2)
    for lo in slices:
        softmax(nblk - 1, 1, lo)
        values(nblk - 2, 0, lo)
    for lo in slices:
        values(nblk - 1, 1, lo)
    for qi in range(nq):
        acc = acc_ref[qi]
        o_ref[qi * tq:(qi + 1) * tq, :] = jnp.transpose(
            acc[0:MLA_VDIM] / acc[MLA_VDIM:MLA_VDIM + 1]).astype(BF16)


def _attn_prompt(qext, kn, kp2, vt, batch, seq, tq):
    n = qext.shape[0]
    nq = seq // tq
    assert nq % 2 == 0
    pairs = [(i, i) for i in range(nq)] + [(qi, kb) for qi in range(nq) for kb in range(qi)]
    qi_tab = jnp.asarray([p[0] for p in pairs], jnp.int32)
    kb_tab = jnp.asarray([p[1] for p in pairs], jnp.int32)
    ones_rows = 16
    grid_spec = pltpu.PrefetchScalarGridSpec(
        num_scalar_prefetch=2,
        grid=(batch, MLA_HEADS),
        in_specs=[
            pl.BlockSpec((seq, 2 * LANES), lambda b, h, *_: (b, h)),
            pl.BlockSpec((seq, LANES), lambda b, h, *_: (b, h)),
            pl.BlockSpec((seq, LANES), lambda b, h, *_: (b, 0)),
            pl.BlockSpec((1, nq, MLA_VDIM, tq), lambda b, h, *_: (h, b, 0, 0)),
        ],
        out_specs=pl.BlockSpec((seq, MLA_VDIM), lambda b, h, *_: (b, h)),
        scratch_shapes=[
            pltpu.VMEM((seq, 2 * LANES), BF16),
            pltpu.VMEM((nq, MLA_VDIM + ones_rows, tq), BF16),
            pltpu.VMEM((tq, tq), F32),
            pltpu.VMEM((nq, 1, tq), F32),
            pltpu.VMEM((nq, MLA_VDIM + ones_rows, tq), F32),
            pltpu.VMEM((tq, tq), F32), pltpu.VMEM((tq, tq), F32),
            pltpu.VMEM((tq, tq), BF16), pltpu.VMEM((tq, tq), BF16),
            pltpu.VMEM((1, tq), F32), pltpu.VMEM((1, tq), F32),
        ],
    )
    return pl.pallas_call(
        functools.partial(_attn_prompt_body, tq=tq, nq=nq),
        grid_spec=grid_spec,
        out_shape=jax.ShapeDtypeStruct((n, MLA_HEADS * MLA_VDIM), BF16),
        compiler_params=_params(("arbitrary", "arbitrary"), 40),
        name="attn_prompt",
    )(qi_tab, kb_tab, qext, kn, kp2, vt)


SAMPLE_STREAMS_PER_STEP = 2


def _attn_sample_body(q_ref, hm_ref, knp_ref, kpp_ref, cp_ref, knn_ref, kpn_ref, cn_ref, wv_ref, o_ref, *, seq):
    past = knp_ref.shape[0] // SAMPLE_STREAMS_PER_STEP

    def with_ones(c):
        return jnp.concatenate([c.astype(BF16), jnp.ones((c.shape[0], LANES), BF16)], axis=1)

    streams = range(SAMPLE_STREAMS_PER_STEP)
    new = [slice(s * seq, (s + 1) * seq) for s in streams]
    old = [slice(s * past, (s + 1) * past) for s in streams]

    def scores(s):
        q = q_ref[new[s], :]
        qn = jnp.concatenate([q[:, 2 * h * LANES:(2 * h + 1) * LANES] for h in range(MLA_HEADS)], axis=1)
        qbd = jnp.concatenate([qn] * MLA_HEADS, axis=0) * hm_ref[...]
        qpe = jnp.concatenate([q[:, (2 * h + 1) * LANES:(2 * h + 2) * LANES] for h in range(MLA_HEADS)], axis=0)
        s_past = _dot_nt(knp_ref[old[s], :], qbd) + _dot_nt(kpp_ref[old[s], :], qpe)
        s_new = _dot_nt(knn_ref[new[s], :], qbd) + _dot_nt(kpn_ref[new[s], :], qpe)
        return s_past, s_new

    def latents(s, s_past, s_new):
        m = jnp.maximum(jnp.max(s_past, axis=0, keepdims=True), jnp.max(s_new, axis=0, keepdims=True))
        p_past = jnp.exp2(s_past - m).astype(BF16)
        p_new = jnp.exp2(s_new - m).astype(BF16)
        acc = (_dot_tn(p_past, with_ones(cp_ref[old[s], :]))
               + _dot_tn(p_new, with_ones(cn_ref[new[s], :])))
        return (acc[:, 0:MLA_LORA] / acc[:, MLA_LORA:MLA_LORA + 1]).astype(BF16)

    def outputs(s, lat):
        for h in range(MLA_HEADS):
            cols = slice(h * MLA_VDIM, (h + 1) * MLA_VDIM)
            o_ref[new[s], cols] = _dot(lat[h * seq:(h + 1) * seq], wv_ref[:, cols]).astype(BF16)

    sc = [scores(s) for s in streams]
    lat = [latents(s, *sc[s]) for s in streams]
    for s in streams:
        outputs(s, lat[s])


def _attn_sample(qext, kn_past, kp2_past, ckv_past, kn_new, kp2_new, ckv_new, wv, batch, past, seq):
    n = qext.shape[0]
    width = MLA_HEADS * LANES
    head_mask = jnp.asarray(np.kron(np.eye(MLA_HEADS, dtype=np.float32), np.ones((seq, LANES), np.float32)), BF16)
    group = SAMPLE_STREAMS_PER_STEP
    assert batch % group == 0
    seq, past = group * seq, group * past
    stream = lambda b: (b, 0)
    const = lambda b: (0, 0)
    return pl.pallas_call(
        functools.partial(_attn_sample_body, seq=seq // group),
        grid=(batch // group,),
        in_specs=[
            pl.BlockSpec((seq, 2 * width), stream),
            pl.BlockSpec(head_mask.shape, const),
            pl.BlockSpec((past, width), stream),
            pl.BlockSpec((past, LANES), stream),
            pl.BlockSpec((past, MLA_LORA), stream),
            pl.BlockSpec((seq, width), stream),
            pl.BlockSpec((seq, LANES), stream),
            pl.BlockSpec((seq, MLA_LORA), stream),
            pl.BlockSpec(wv.shape, const),
        ],
        out_specs=pl.BlockSpec((seq, MLA_HEADS * MLA_VDIM), stream),
        out_shape=jax.ShapeDtypeStruct((n, MLA_HEADS * MLA_VDIM), BF16),
        compiler_params=_params(("parallel",), 48),
        name="attn_sample",
    )(qext, head_mask, kn_past, kp2_past, ckv_past, kn_new, kp2_new, ckv_new, wv)


MERGE_COL_CHUNK = 256


def _branch_merge_body(og_ref, at_ref, wg_ref, wm_ref, ga_ref, gb_ref, u_ref):
    og = og_ref[...]
    at = at_ref[...]
    starts = list(range(0, u_ref.shape[1], MERGE_COL_CHUNK))

    def gates(lo):
        cols = slice(lo, lo + MERGE_COL_CHUNK)
        return _sigmoid(ga_ref[:, cols].astype(F32)), _sigmoid(gb_ref[:, cols].astype(F32))

    nxt = gates(starts[0])
    for g, lo in enumerate(starts):
        cols = slice(lo, lo + MERGE_COL_CHUNK)
        sa, sb = nxt
        ya = _dot(og, wg_ref[:, cols])
        yb = _dot(at, wm_ref[:, cols])
        nxt = gates(starts[g + 1]) if g + 1 < len(starts) else None
        u_ref[:, cols] = (sa * ya + sb * yb).astype(BF16)


def _out_proj_body(u_ref, wo_ref, x_ref, g2_ref, x1_ref, h2_ref):
    x1 = x_ref[...] + _dot(u_ref[...], wo_ref[...])
    x1_ref[...] = x1
    h2_ref[...] = (_row_rms(x1) * g2_ref[...]).astype(BF16)


def _merge(og, att, wg, wm, zm, wo, x, g2, tm_u, tm_x):
    n = x.shape[0]
    tn = 1024
    u = pl.pallas_call(
        _branch_merge_body,
        grid=(n // tm_u, D_MODEL // tn),
        in_specs=[
            pl.BlockSpec((tm_u, D_MODEL), lambda i, j: (i, 0)),
            pl.BlockSpec((tm_u, D_MODEL), lambda i, j: (i, 0)),
            pl.BlockSpec((D_MODEL, tn), lambda i, j: (0, j)),
            pl.BlockSpec((D_MODEL, tn), lambda i, j: (0, j)),
            pl.BlockSpec((tm_u, tn), lambda i, j: (i, COL_GA // tn + j)),
            pl.BlockSpec((tm_u, tn), lambda i, j: (i, COL_GB // tn + j)),
        ],
        out_specs=pl.BlockSpec((tm_u, tn), lambda i, j: (i, j)),
        out_shape=jax.ShapeDtypeStruct((n, D_MODEL), BF16),
        compiler_params=_params(("parallel", "parallel"), 56),
        name="branch_merge",
    )(og, att, wg, wm, zm, zm)
    return pl.pallas_call(
        _out_proj_body,
        grid=(n // tm_x,),
        in_specs=[
            pl.BlockSpec((tm_x, D_MODEL), lambda i: (i, 0)),
            pl.BlockSpec((D_MODEL, D_MODEL), lambda i: (0, 0)),
            pl.BlockSpec((tm_x, D_MODEL), lambda i: (i, 0)),
            pl.BlockSpec((1, D_MODEL), lambda i: (0, 0)),
        ],
        out_specs=[
            pl.BlockSpec((tm_x, D_MODEL), lambda i: (i, 0)),
            pl.BlockSpec((tm_x, D_MODEL), lambda i: (i, 0)),
        ],
        out_shape=[
            jax.ShapeDtypeStruct((n, D_MODEL), F32),
            jax.ShapeDtypeStruct((n, D_MODEL), BF16),
        ],
        compiler_params=_params(("parallel",), 56),
        name="out_proj",
    )(u, wo, x, g2)


FFN_COL_CHUNK = 256


def _gelu_gate(a, a1, a2, cw, gt):
    c = cw[3:4, :] + cw[2:3, :] * a + cw[0:1, :] * a2 + cw[1:2, :] * a1
    return 0.5 * c * (1.0 + lax.erf(c * (2.0 ** -0.5))) * gt


def _ffn_up_seq_body(h_ref, wa_ref, wg_ref, cw_ref, hist_ref, act_ref, tail_ref, carry_ref, *, tm, tiles_per_seq):
    i = pl.program_id(0)
    j = pl.program_id(1)

    @pl.when(i % tiles_per_seq == 0)
    def _():
        carry_ref[j] = hist_ref[0]

    row = lax.broadcasted_iota(jnp.int32, (tm, 1), 0)
    starts = list(range(0, act_ref.shape[1], FFN_COL_CHUNK))

    def project(lo):
        h = h_ref[...]
        return _dot(h, wa_ref[:, lo:lo + FFN_COL_CHUNK]), _dot(h, wg_ref[:, lo:lo + FFN_COL_CHUNK])

    nxt = project(starts[0])
    for g, lo in enumerate(starts):
        cols = slice(lo, lo + FFN_COL_CHUNK)
        a, gt = nxt
        nxt = project(starts[g + 1]) if g + 1 < len(starts) else None
        prev = carry_ref[j, :, cols]
        a1 = jnp.where(row == 0, prev[7:8, :], pltpu.roll(a, 1, axis=0))
        a2 = jnp.where(row == 0, prev[6:7, :], jnp.where(row == 1, prev[7:8, :], pltpu.roll(a, 2, axis=0)))
        act_ref[:, cols] = _gelu_gate(a, a1, a2, cw_ref[:, cols], gt).astype(BF16)
        carry_ref[j, :, cols] = a[tm - 8:tm, :]
        tail_ref[0, :, cols] = a[tm - (CONV_W - 1):tm, :]


def _ffn_up_seq(h2, wa, wg, cw, hist8, seq, tm):
    n = h2.shape[0]
    tn = 512
    nj = D_FF // tn
    tps = seq // tm
    return pl.pallas_call(
        functools.partial(_ffn_up_seq_body, tm=tm, tiles_per_seq=tps),
        grid=(n // tm, nj),
        in_specs=[
            pl.BlockSpec((tm, D_MODEL), lambda i, j: (i, 0)),
            pl.BlockSpec((D_MODEL, tn), lambda i, j: (0, j)),
            pl.BlockSpec((D_MODEL, tn), lambda i, j: (0, j)),
            pl.BlockSpec((8, tn), lambda i, j: (0, j)),
            pl.BlockSpec((1, 8, tn), lambda i, j: (i // tps, 0, j)),
        ],
        out_specs=[
            pl.BlockSpec((tm, tn), lambda i, j: (i, j)),
            pl.BlockSpec((1, CONV_W - 1, tn), lambda i, j: (i, 0, j)),
        ],
        out_shape=[
            jax.ShapeDtypeStruct((n, D_FF), BF16),
            jax.ShapeDtypeStruct((n // tm, CONV_W - 1, D_FF), F32),
        ],
        scratch_shapes=[pltpu.VMEM((nj, 8, tn), F32)],
        compiler_params=_params(("arbitrary", "arbitrary"), 48),
        name="ffn_up_seq",
    )(h2, wa, wg, cw, hist8)


def _ffn_up_multi_body(h_ref, wa_ref, wg_ref, cw_ref, p1_ref, p2_ref, act_ref, a_ref, wab_ref, wgb_ref, *, tm, seq):
    wa = wa_ref[...].astype(BF16)
    wg = wg_ref[...].astype(BF16)
    wab_ref[...] = wa
    wgb_ref[...] = wg
    a = _dot(h_ref[...], wa)
    gt = _dot(h_ref[...], wg)
    pos = lax.broadcasted_iota(jnp.int32, (tm, 1), 0) % seq
    a1 = jnp.where(pos == 0, p1_ref[...], pltpu.roll(a, 1, axis=0))
    a2 = jnp.where(pos <= 1, p2_ref[...], pltpu.roll(a, 2, axis=0))
    act_ref[...] = _gelu_gate(a, a1, a2, cw_ref[...], gt).astype(BF16)
    a_ref[...] = a


def _ffn_up_multi(h2, wup, cw, p1, p2, seq):
    n = h2.shape[0]
    tn = 512
    nj = D_FF // tn
    col = lambda j: (0, j)
    return pl.pallas_call(
        functools.partial(_ffn_up_multi_body, tm=n, seq=seq),
        grid=(nj,),
        in_specs=[
            pl.BlockSpec((n, D_MODEL), lambda j: (0, 0)),
            pl.BlockSpec((D_MODEL, tn), col),
            pl.BlockSpec((D_MODEL, tn), lambda j: (0, nj + j)),
            pl.BlockSpec((8, tn), col),
            pl.BlockSpec((n, tn), col),
            pl.BlockSpec((n, tn), col),
        ],
        out_specs=[pl.BlockSpec((n, tn), col)] * 2 + [pl.BlockSpec((D_MODEL, tn), col)] * 2,
        out_shape=[jax.ShapeDtypeStruct((n, D_FF), BF16), jax.ShapeDtypeStruct((n, D_FF), F32)]
        + [jax.ShapeDtypeStruct((D_MODEL, D_FF), BF16)] * 2,
        compiler_params=_params(("parallel",), 48),
        name="ffn_up_multi",
    )(h2, wup, wup, cw, p1, p2)


def _ffn_down_body(act_ref, wd_ref, x1_ref, o_ref):
    o_ref[...] = x1_ref[...] + _dot(act_ref[...], wd_ref[...])


def _ffn_down_cast_body(act_ref, wd_ref, x1_ref, o_ref, wdb_ref):
    wd = wd_ref[...].astype(BF16)
    wdb_ref[...] = wd
    o_ref[...] = x1_ref[...] + _dot(act_ref[...], wd)


def _ffn_down(act, wd, x1, tm, emit_bf16_weights=False):
    n = act.shape[0]
    tn = 512
    out_specs = [pl.BlockSpec((tm, tn), lambda i, j: (i, j))]
    out_shape = [jax.ShapeDtypeStruct((n, D_MODEL), F32)]
    if emit_bf16_weights:
        assert n == tm
        out_specs.append(pl.BlockSpec((D_FF, tn), lambda i, j: (0, j)))
        out_shape.append(jax.ShapeDtypeStruct((D_FF, D_MODEL), BF16))
    return pl.pallas_call(
        _ffn_down_cast_body if emit_bf16_weights else _ffn_down_body,
        grid=(n // tm, D_MODEL // tn),
        in_specs=[
            pl.BlockSpec((tm, D_FF), lambda i, j: (i, 0)),
            pl.BlockSpec((D_FF, tn), lambda i, j: (0, j)),
            pl.BlockSpec((tm, tn), lambda i, j: (i, j)),
        ],
        out_specs=out_specs,
        out_shape=out_shape,
        compiler_params=_params(("parallel", "parallel"), 56),
        name="ffn_down",
    )(act, wd, x1)


def _swap_halves(w):
    half = w.shape[-1] // 2
    return jnp.concatenate([w[..., half:], w[..., :half]], axis=-1)


def _rope_table(pos):
    half = MLA_ROPE // 2
    inv = ROPE_THETA ** (-np.arange(half, dtype=np.float64) * 2.0 / MLA_ROPE)
    ang = np.asarray(pos, np.float64)[:, None] * inv[None, :]
    cos, sin = np.cos(ang), np.sin(ang)
    return jnp.asarray(np.concatenate([cos, cos, -sin, sin], axis=-1), F32)


def _layer_weights(w_in, g_norm1, gla_w_gate2, gla_b_gate, gla_g_out, w_br_gla, mla_g_qlat, mla_w_uq,
                   mla_g_kvlat, mla_w_ukv, mla_g_q, mla_g_k, mla_g_qpe, mla_g_kpe, w_br_mla, w_out,
                   g_norm2, ffn_w_up, ffn_conv_w, ffn_conv_b, ffn_w_down):
    w2ext = jnp.zeros((SMALL_COLS, GLA_HEADS * GLA_HK), F32).at[SMALL_LR:SMALL_LR + GLA_GATE_RANK].set(
        gla_w_gate2).astype(BF16)
    wq = mla_w_uq.reshape(MLA_LORA, MLA_HEADS, MLA_NOPE + MLA_ROPE)
    wqn = wq[:, :, :MLA_NOPE].reshape(MLA_LORA, MLA_HEADS * MLA_NOPE).astype(BF16)
    wq_pe = wq[:, :, MLA_NOPE:]
    wqp = jnp.concatenate([wq_pe, _swap_halves(wq_pe)], axis=-1).reshape(MLA_LORA, MLA_HEADS * LANES).astype(BF16)
    wkv = mla_w_ukv.reshape(MLA_LORA, MLA_HEADS, MLA_NOPE + MLA_VDIM)
    wk = wkv[:, :, :MLA_NOPE].reshape(MLA_LORA, MLA_HEADS * MLA_NOPE).astype(BF16)
    wv = wkv[:, :, MLA_NOPE:].reshape(MLA_LORA, MLA_HEADS * MLA_VDIM).astype(BF16)
    cw = jnp.concatenate([ffn_conv_w, ffn_conv_b[None, :], jnp.zeros((8 - CONV_W - 1, D_FF), F32)], axis=0)
    return dict(
        w_in_t=w_in.T, g1=g_norm1[None, :], w2ext=w2ext, bg=gla_b_gate[None, :],
        go=gla_g_out[None, :], wg=w_br_gla.astype(BF16), wqn=wqn, wqp=wqp, gql=mla_g_qlat[None, :],
        gkl=mla_g_kvlat[None, :], gq=jnp.tile(mla_g_q, 2)[None, :],
        gqp=jnp.tile(jnp.concatenate([mla_g_qpe, _swap_halves(mla_g_qpe)]), 2)[None, :],
        gkp=jnp.concatenate([mla_g_kpe, _swap_halves(mla_g_kpe)])[None, :],
        wk=wk, wv=wv, wvt=wv.T, gk=jnp.tile(mla_g_k, 2)[None, :], wm=w_br_mla.astype(BF16),
        wo=w_out.astype(BF16),
        g2=g_norm2[None, :], wup=ffn_w_up, cw=cw, wd=ffn_w_down,
        p128=jnp.asarray(np.kron(np.eye(2, dtype=np.float32), np.full((LANES, LANES), 1.0 / LANES, np.float32)), BF16),
    )


def _trunk_front(x, w, s0, batch, seq, chunk, gla_tb, tm_in, tm, tab, tab_blocks, with_values):
    if "w_main" not in w:
        w["w_main"], narrow, zm, zs = _regroup_in_proj(x, w["g1"], w["w_in_t"])
        w["w_small"] = _small_weight_rows(narrow).astype(BF16)
    else:
        zm, zs = _in_proj(x, w["g1"], w["w_main"], w["w_small"], tm_in)
    og, s_new = _gla(zm, zs, w["w2ext"], w["bg"], w["go"], s0, batch, seq, chunk, gla_tb)
    qext, ckv, kpe, kp2 = _mla_proj(zm, zs, w["wqn"], w["wqp"], w["gql"], w["gkl"], w["gq"], w["gqp"],
                                    w["gkp"], tab, w["p128"], tm, tab_blocks)
    kv = _mla_expand(ckv, w["wk"], w["gk"], w["p128"], tm, w["wvt"] if with_values else None)
    return zm, og, s_new, qext, ckv, kpe, kp2, kv


def kernel(x_prompt, x_sample, state_gla, cache_mla_ckv, cache_mla_kpe, cache_ffn_conv, w_in, g_norm1, gla_w_gate2, gla_b_gate, gla_g_out, w_br_gla, mla_g_qlat, mla_w_uq, mla_g_kvlat, mla_w_ukv, mla_g_q, mla_g_k, mla_g_qpe, mla_g_kpe, w_br_mla, w_out, g_norm2, ffn_w_up, ffn_conv_w, ffn_conv_b, ffn_w_down):
    bp, tp, _ = x_prompt.shape
    bs, ts, _ = x_sample.shape
    depth = w_in.shape[0]
    past = cache_mla_ckv.shape[2]
    np_rows, ns_rows = bp * tp, bs * ts
    tm_p = 512
    tm_big = 1024
    tm_ffn = 1024
    tab_p = _rope_table(np.arange(tp))
    tab_s = jnp.tile(_rope_table(past + np.arange(ts)), (bs, 1))
    xp = x_prompt.reshape(np_rows, D_MODEL)
    xs = x_sample.reshape(ns_rows, D_MODEL)
    outs = [[] for _ in range(8)]
    layer_weights = (w_in, g_norm1, gla_w_gate2, gla_b_gate, gla_g_out, w_br_gla, mla_g_qlat, mla_w_uq,
                     mla_g_kvlat, mla_w_ukv, mla_g_q, mla_g_k, mla_g_qpe, mla_g_kpe, w_br_mla, w_out,
                     g_norm2, ffn_w_up, ffn_conv_w, ffn_conv_b, ffn_w_down)
    for l in range(depth):
        w = _layer_weights(*[a[l] for a in layer_weights])

        zm, og, ss, qext, ckv_s, kpe_s, kp2, (kn,) = _trunk_front(
            xs, w, state_gla[l], bs, ts, ts, ts, ns_rows, ns_rows, tab_s, 1, False)
        ckv_past = cache_mla_ckv[l].reshape(bs * past, MLA_LORA)
        (kn_past,) = _mla_expand(ckv_past, w["wk"], w["gk"], w["p128"], 1024)
        kpe_past = cache_mla_kpe[l].reshape(bs * past, MLA_ROPE)
        kp2_past = jnp.concatenate([kpe_past, kpe_past], axis=-1).astype(BF16)
        att = _attn_sample(qext, kn_past, kp2_past, ckv_past, kn, kp2, ckv_s, w["wv"], bs, past, ts)
        x1, h2 = _merge(og, att, w["wg"], w["wm"], zm, w["wo"], xs, w["g2"], ns_rows, ns_rows)
        hist = cache_ffn_conv[l]
        zrow = jnp.zeros((bs, ts - 1, D_FF), F32)
        p1 = jnp.concatenate([hist[:, 1:2], zrow], axis=1).reshape(ns_rows, D_FF)
        p2 = jnp.concatenate([hist, zrow[:, 1:]], axis=1).reshape(ns_rows, D_FF)
        act, a_full, wa_bf, wg_bf = _ffn_up_multi(h2, w["wup"], w["cw"], p1, p2, ts)
        fs = a_full.reshape(bs, ts, D_FF)[:, ts - (CONV_W - 1):]
        xs, wd_bf = _ffn_down(act, w["wd"], x1, ns_rows, emit_bf16_weights=True)

        s0 = jnp.zeros((bp, GLA_HEADS, GLA_HK, GLA_HV), F32)
        zm, og, sp, qext, ckv_p, kpe_p, kp2, (kn, vt) = _trunk_front(
            xp, w, s0, bp, tp, CHUNK, 512, tm_big, tm_p, tab_p, tp // tm_p, True)
        att = _attn_prompt(qext, kn, kp2, vt, bp, tp, 512)
        x1, h2 = _merge(og, att, w["wg"], w["wm"], zm, w["wo"], xp, w["g2"], tm_big, tm_p)
        hist8 = jnp.zeros((bp, 8, D_FF), F32)
        act, tails = _ffn_up_seq(h2, wa_bf, wg_bf, w["cw"], hist8, tp, tm_ffn)
        fp = tails.reshape(bp, tp // tm_ffn, CONV_W - 1, D_FF)[:, -1]
        (xp,) = _ffn_down(act, wd_bf, x1, tm_big)

        for lst, val in zip(outs, (sp, ss, ckv_p.reshape(bp, tp, MLA_LORA), ckv_s.reshape(bs, ts, MLA_LORA),
                                   kpe_p.reshape(bp, tp, MLA_ROPE), kpe_s.reshape(bs, ts, MLA_ROPE), fp, fs)):
            lst.append(val)
    return (xp.reshape(bp, tp, D_MODEL), xs.reshape(bs, ts, D_MODEL)) + tuple(jnp.stack(o, 0) for o in outs)
```

```python
import functools

import numpy as np
import jax
import jax.numpy as jnp
from jax import lax
from jax.experimental import pallas as pl
from jax.experimental.pallas import tpu as pltpu

F32 = jnp.float32
BF16 = jnp.bfloat16

D_MODEL = 2048
CHUNK = 64
EPS = 1e-6
GLA_HEADS = 4
GLA_HK = 256
GLA_HV = 512
GLA_GATE_RANK = 16
GLA_GATE_TAU = 16.0
MLA_HEADS = 16
MLA_LORA = 512
MLA_NOPE = 128
MLA_ROPE = 64
MLA_VDIM = 128
MLA_SCALE = (MLA_NOPE + MLA_ROPE) ** -0.5
LOG2E = float(np.log2(np.e))
Q_SCALE = MLA_SCALE * LOG2E
ROPE_THETA = 10000.0
D_FF = 5632
CONV_W = 3
LANES = 128
NEG_BIG = -1e30
ATTN_UNROLL = 8
ATTN_QUERY_SPLIT = 2

COL_Q, COL_K, COL_V, COL_R, COL_MQ, COL_MKV, COL_GA, COL_GB, MAIN_COLS = (
    0, 1024, 2048, 4096, 6144, 6656, 7168, 9216, 11264)
SMALL_COLS = 256
SMALL_LR = 128

MIB = 1024 * 1024


def _params(semantics, vmem_mib):
    return pltpu.CompilerParams(dimension_semantics=semantics, vmem_limit_bytes=vmem_mib * MIB)


def _dot(a, b):
    return jnp.dot(a, b, preferred_element_type=F32)


def _dot_nt(a, b):
    return lax.dot_general(a, b, (((1,), (1,)), ((), ())), preferred_element_type=F32)


def _dot_tn(a, b):
    return lax.dot_general(a, b, (((0,), (0,)), ((), ())), preferred_element_type=F32)


def _sigmoid(x):
    return 1.0 / (1.0 + jnp.exp(-x))


def _row_rms(x):
    return x * lax.rsqrt(jnp.mean(x * x, axis=-1, keepdims=True) + EPS)


def _slab_rms(x, p_ref):
    ms = _dot((x * x).astype(BF16), p_ref[...])
    return x * lax.rsqrt(ms + EPS)


W_IN_SEGMENTS = ((0, COL_R, 0), (COL_R, COL_GA, GLA_GATE_RANK), (COL_GA, MAIN_COLS, GLA_GATE_RANK + MLA_ROPE))


def _small_weight_rows(narrow):
    lr = narrow[0:GLA_GATE_RANK]
    kpe = narrow[GLA_GATE_RANK:GLA_GATE_RANK + MLA_ROPE]
    half = MLA_ROPE // 2
    pad = jnp.zeros((SMALL_COLS - SMALL_LR - GLA_GATE_RANK, D_MODEL), narrow.dtype)
    return jnp.concatenate([kpe, kpe[half:], kpe[:half], lr, pad], axis=0)


def _regroup_body(x_ref, g_ref, main_ref, next_ref, o_ref, narrow_ref, zm_ref, zs_ref, h_ref, *, tr):
    j = pl.program_id(0)

    @pl.when(j == 0)
    def _():
        narrow_ref[...] = jnp.zeros(narrow_ref.shape, F32)
        h_ref[...] = (_row_rms(x_ref[...]) * g_ref[...]).astype(BF16)

    _regroup_tile(j, main_ref, next_ref, o_ref, narrow_ref, tr)
    zm_ref[...] = _dot_nt(h_ref[...], o_ref[...]).astype(BF16)

    @pl.when(j == pl.num_programs(0) - 1)
    def _():
        zs_ref[...] = _dot_nt(h_ref[...], _small_weight_rows(narrow_ref[...]).astype(BF16))


def _regroup_tile(j, main_ref, next_ref, o_ref, narrow_ref, tr):
    prev = 0
    for lo, hi, shift in W_IN_SEGMENTS:
        if shift > prev:
            @pl.when(j == lo // tr)
            def _(prev=prev, shift=shift):
                narrow_ref[prev:shift, :] = main_ref[prev:shift, :]

        @pl.when((j >= lo // tr) & (j < hi // tr))
        def _(shift=shift):
            if shift == 0:
                o_ref[...] = main_ref[...].astype(BF16)
            else:
                o_ref[0:tr - shift, :] = main_ref[shift:tr, :].astype(BF16)
                o_ref[tr - shift:tr, :] = next_ref[0:shift, :].astype(BF16)
        prev = shift


def _regroup_in_proj(x, g, w_in_t):
    n = x.shape[0]
    tr = 1024
    assert all(lo % tr == 0 and hi % tr == 0 and shift % 16 == 0 and shift < LANES
               for lo, hi, shift in W_IN_SEGMENTS)
    const = lambda j: (0, 0)
    return pl.pallas_call(
        functools.partial(_regroup_body, tr=tr),
        grid=(MAIN_COLS // tr,),
        in_specs=[
            pl.BlockSpec((n, D_MODEL), const),
            pl.BlockSpec((1, D_MODEL), const),
            pl.BlockSpec((tr, D_MODEL), lambda j: (j, 0)),
            pl.BlockSpec((LANES, D_MODEL), lambda j: ((j + 1) * (tr // LANES), 0)),
        ],
        out_specs=[
            pl.BlockSpec((tr, D_MODEL), lambda j: (j, 0)),
            pl.BlockSpec((LANES, D_MODEL), const),
            pl.BlockSpec((n, tr), lambda j: (0, j)),
            pl.BlockSpec((n, SMALL_COLS), const),
        ],
        out_shape=[
            jax.ShapeDtypeStruct((MAIN_COLS, D_MODEL), BF16),
            jax.ShapeDtypeStruct((LANES, D_MODEL), F32),
            jax.ShapeDtypeStruct((n, MAIN_COLS), BF16),
            jax.ShapeDtypeStruct((n, SMALL_COLS), F32),
        ],
        scratch_shapes=[pltpu.VMEM((n, D_MODEL), BF16)],
        compiler_params=_params(("arbitrary",), 48),
        name="regroup_in_proj",
    )(x, g, w_in_t, w_in_t)


def _in_proj_body(x_ref, g_ref, wm_ref, ws_ref, zm_ref, zs_ref, h_ref):
    @pl.when(pl.program_id(1) == 0)
    def _():
        h_ref[...] = (_row_rms(x_ref[...]) * g_ref[...]).astype(BF16)
        zs_ref[...] = _dot_nt(h_ref[...], ws_ref[...])

    zm_ref[...] = _dot_nt(h_ref[...], wm_ref[...]).astype(BF16)


def _in_proj(x, g, wm, ws, tm):
    n = x.shape[0]
    tn = 1024
    return pl.pallas_call(
        _in_proj_body,
        grid=(n // tm, MAIN_COLS // tn),
        in_specs=[
            pl.BlockSpec((tm, D_MODEL), lambda i, j: (i, 0)),
            pl.BlockSpec((1, D_MODEL), lambda i, j: (0, 0)),
            pl.BlockSpec((tn, D_MODEL), lambda i, j: (j, 0)),
            pl.BlockSpec((SMALL_COLS, D_MODEL), lambda i, j: (0, 0)),
        ],
        out_specs=[
            pl.BlockSpec((tm, tn), lambda i, j: (i, j)),
            pl.BlockSpec((tm, SMALL_COLS), lambda i, j: (i, 0)),
        ],
        out_shape=[
            jax.ShapeDtypeStruct((n, MAIN_COLS), BF16),
            jax.ShapeDtypeStruct((n, SMALL_COLS), F32),
        ],
        scratch_shapes=[pltpu.VMEM((tm, D_MODEL), BF16)],
        compiler_params=_params(("parallel", "arbitrary"), 56),
        name="in_proj",
    )(x, g, wm, ws)


def _gla_tables(c):
    levels = int(np.log2(c))
    t = np.arange(c)[:, None]
    u = np.arange(c)[None, :]
    masks = [(u == t)]
    level2 = None
    for l in range(levels):
        m = c >> (l + 1)
        mid_t = (t // (2 * m)) * 2 * m + m
        upper = t >= mid_t
        if m == 2:
            level2 = np.where(upper, (u >= mid_t) & (u <= t), (u > t) & (u < mid_t))
        mid_u = (u // (2 * m)) * 2 * m + m
        masks.append((t // (2 * m) == u // (2 * m)) & upper & (u < mid_u))
    eye_h = np.eye(GLA_HEADS, dtype=np.float32)
    gmat = np.concatenate([np.kron(eye_h, (u <= t).astype(np.float32)),
                           np.kron(eye_h, level2.astype(np.float32))], axis=0)
    group = _gla_group_heads(c)
    masks = np.stack([np.kron(np.eye(group, dtype=np.float32), mk.astype(np.float32)) for mk in masks])
    return jnp.asarray(gmat, BF16), jnp.asarray(masks), levels


def _gla_group_heads(c):
    return min(GLA_HEADS, max(1, LANES // c))


def _gla_level_exponent(b, log_a, level2, m, c):
    if m == 1:
        row = lax.broadcasted_iota(jnp.int32, (c, 1), 0)
        return jnp.where(row % 2 == 1, log_a, 0.0)
    if m == 2:
        return level2
    parts = [jnp.broadcast_to(b[i + m - 1:i + m, :], (2 * m, GLA_HK)) for i in range(0, c, 2 * m)]
    ref = parts[0] if len(parts) == 1 else jnp.concatenate(parts, axis=0)
    return -jnp.abs(b - ref)


GLA_CHUNKS_PER_TRIP = 4


def _gla_body(q_ref, k_ref, v_ref, r_ref, zs_ref, w2_ref, bg_ref, go_ref, s0_ref, gmat_ref, mask_ref,
              og_ref, sout_ref, st_ref, *, c, nchunk, levels, single):
    t = pl.program_id(1)

    if not single:
        @pl.when(t == 0)
        def _():
            for h in range(GLA_HEADS):
                st_ref[h] = jnp.transpose(s0_ref[0, h])

    nrow = GLA_HEADS * c
    gw = _gla_group_heads(c) * c

    def stack(x, width):
        return jnp.concatenate([x[:, h * width:(h + 1) * width] for h in range(GLA_HEADS)], axis=0)

    def diag_blocks(prod):
        return [prod[g:g + gw, g:g + gw] for g in range(0, nrow, gw)]

    def decays(ci):
        rows = pl.ds(0, c) if single else pl.ds(pl.multiple_of(ci * c, c), c)
        qb = stack(q_ref[rows, :], GLA_HK) * (GLA_HK ** -0.5)
        kb = stack(k_ref[rows, :], GLA_HK)
        q = qb.astype(F32)
        k = kb.astype(F32)
        v = stack(v_ref[rows, :], GLA_HV)
        x = stack(_dot(zs_ref[rows, :].astype(BF16), w2_ref[...]) + bg_ref[...], GLA_HK) * LOG2E
        log_a = (jnp.minimum(x, 0.0) - jnp.log2(1.0 + jnp.exp2(-jnp.abs(x)))) * (1.0 / GLA_GATE_TAU)
        hi = log_a.astype(BF16)
        lo = (log_a - hi.astype(F32)).astype(BF16)
        gm = gmat_ref[...]
        pre = _dot(gm, hi) + _dot(gm, lo)
        b = pre[0:nrow]
        b_last = [b[h * c + c - 1:h * c + c, :] for h in range(GLA_HEADS)]
        q_in = (q * jnp.exp2(b)).astype(BF16)
        k_out = (k * jnp.exp2(jnp.concatenate([jnp.broadcast_to(r, (c, GLA_HK)) for r in b_last], axis=0)
                              - b)).astype(BF16)
        att = [mask_ref[0] * blk for blk in diag_blocks(_dot_nt(qb, kb))]
        return dict(rows=rows, qb=qb, kb=kb, v=v, log_a=log_a, hi=hi, lo=lo, b=b, level2=pre[nrow:2 * nrow],
                    b_last=b_last, q_in=q_in, k_out=k_out, att=att)

    def level(ch, l):
        d = jnp.exp2(_gla_level_exponent(ch["b"], ch["log_a"], ch["level2"], c >> (l + 1), nrow)).astype(BF16)
        prod = _dot_nt(ch["qb"] * d, ch["kb"] * d)
        ch["att"] = [a + mask_ref[1 + l] * blk for a, blk in zip(ch["att"], diag_blocks(prod))]

    def intra(ch):
        att = ch["att"]
        if len(att) == 1:
            att_full = att[0].astype(BF16)
        else:
            zero = jnp.zeros((gw, gw), BF16)
            att_full = jnp.concatenate(
                [jnp.concatenate([a.astype(BF16) if i == j else zero for j in range(len(att))], axis=1)
                 for i, a in enumerate(att)], axis=0)
        ch["o_intra"] = _dot(att_full, ch["v"])

    def carried(ch):
        v, q_in, k_out, o_intra = ch["v"], ch["q_in"], ch["k_out"], ch["o_intra"]
        outs = []
        for h in range(GLA_HEADS):
            hr = slice(h * c, (h + 1) * c)
            if single:
                s0 = s0_ref[0, h]
                outs.append(o_intra[hr] + _dot(q_in[hr], s0.astype(BF16)))
                ones = jnp.ones((c, LANES), BF16)
                decay = jnp.exp2(_dot_tn(ch["hi"][hr], ones) + _dot_tn(ch["lo"][hr], ones))
                sout_ref[0, h] = (s0 * jnp.concatenate([decay] * (GLA_HV // LANES), axis=1)
                                  + _dot_tn(k_out[hr], v[hr]))
            else:
                st = st_ref[h]
                outs.append(o_intra[hr] + _dot_nt(q_in[hr], st.astype(BF16)))
                st_ref[h] = st * jnp.exp2(ch["b_last"][h]) + _dot_tn(v[hr], k_out[hr])
        return outs

    def emit(ch, outs):
        for h, o in enumerate(outs):
            cv = slice(h * GLA_HV, (h + 1) * GLA_HV)
            gate = r_ref[ch["rows"], cv].astype(F32)
            og = _row_rms(o) * go_ref[...] * (gate * _sigmoid(gate))
            og_ref[ch["rows"], cv] = og.astype(BF16)

    def chunks(first, count):
        group = [decays(first + i) for i in range(count)]
        for l in range(levels):
            for ch in group:
                level(ch, l)
        for ch in group:
            intra(ch)
        pending = None
        for ch in group:
            outs = carried(ch)
            if pending is not None:
                emit(*pending)
            pending = (ch, outs)
        emit(*pending)

    if single:
        chunks(0, 1)
        return
    per_trip = GLA_CHUNKS_PER_TRIP if nchunk % GLA_CHUNKS_PER_TRIP == 0 else 1

    def trip(gi, carry):
        chunks(gi * per_trip, per_trip)
        return carry

    lax.fori_loop(0, nchunk // per_trip, trip, 0)

    @pl.when(t == pl.num_programs(1) - 1)
    def _():
        for h in range(GLA_HEADS):
            sout_ref[0, h] = jnp.transpose(st_ref[h])


def _gla(zm, zs, w2ext, bg, go, s0, batch, seq, c, tb):
    n = zm.shape[0]
    nt = seq // tb
    gmat, masks, levels = _gla_tables(c)
    dk, dv = GLA_HEADS * GLA_HK, GLA_HEADS * GLA_HV
    const2 = lambda b, t: (0, 0)
    return pl.pallas_call(
        functools.partial(_gla_body, c=c, nchunk=tb // c, levels=levels, single=(seq == c)),
        grid=(batch, nt),
        in_specs=[
            pl.BlockSpec((tb, dk), lambda b, t: (b * nt + t, COL_Q // dk)),
            pl.BlockSpec((tb, dk), lambda b, t: (b * nt + t, COL_K // dk)),
            pl.BlockSpec((tb, dv), lambda b, t: (b * nt + t, COL_V // dv)),
            pl.BlockSpec((tb, dv), lambda b, t: (b * nt + t, COL_R // dv)),
            pl.BlockSpec((tb, SMALL_COLS), lambda b, t: (b * nt + t, 0)),
            pl.BlockSpec((SMALL_COLS, dk), const2),
            pl.BlockSpec((1, dk), const2),
            pl.BlockSpec((1, GLA_HV), const2),
            pl.BlockSpec((1, GLA_HEADS, GLA_HK, GLA_HV), lambda b, t: (b, 0, 0, 0)),
            pl.BlockSpec(gmat.shape, const2),
            pl.BlockSpec(masks.shape, lambda b, t: (0, 0, 0)),
        ],
        out_specs=[
            pl.BlockSpec((tb, dv), lambda b, t: (b * nt + t, 0)),
            pl.BlockSpec((1, GLA_HEADS, GLA_HK, GLA_HV), lambda b, t: (b, 0, 0, 0)),
        ],
        out_shape=[
            jax.ShapeDtypeStruct((n, dv), BF16),
            jax.ShapeDtypeStruct((batch, GLA_HEADS, GLA_HK, GLA_HV), F32),
        ],
        scratch_shapes=[pltpu.VMEM((GLA_HEADS, GLA_HV, GLA_HK), F32)],
        compiler_params=_params(("parallel", "arbitrary"), 40),
        name="gla",
    )(zm, zm, zm, zm, zs, w2ext, bg, go, s0, gmat, masks)


def _mla_proj_body(mq_ref, mkv_ref, zs_ref, wqn_ref, wqp_ref, gql_ref, gkl_ref, gq_ref, gqp_ref, gkp_ref,
                   tab_ref, p_ref, q_ref, ckv_ref, kpe_ref, kp2_ref):
    qlat = (_row_rms(mq_ref[...].astype(F32)) * gql_ref[...]).astype(BF16)
    tab = tab_ref[...]
    tab2 = jnp.concatenate([tab, tab], axis=1)
    starts = list(range(0, MLA_HEADS * LANES, 2 * LANES))

    def project(lo):
        cols = slice(lo, lo + 2 * LANES)
        return _dot(qlat, wqn_ref[:, cols]), _dot(qlat, wqp_ref[:, cols])

    cur = project(starts[0])
    for g, lo in enumerate(starts):
        nxt = project(starts[g + 1]) if g + 1 < len(starts) else None
        nope = (_slab_rms(cur[0], p_ref) * (gq_ref[...] * Q_SCALE)).astype(BF16)
        pe = (_slab_rms(cur[1], p_ref) * (gqp_ref[...] * Q_SCALE) * tab2).astype(BF16)
        for i in range(2):
            dst = 2 * lo + i * 2 * LANES
            q_ref[:, dst:dst + LANES] = nope[:, i * LANES:(i + 1) * LANES]
            q_ref[:, dst + LANES:dst + 2 * LANES] = pe[:, i * LANES:(i + 1) * LANES]
        cur = nxt
    ckv_ref[...] = _row_rms(mkv_ref[...].astype(F32)) * gkl_ref[...]
    slab = zs_ref[:, 0:LANES]
    rot = _row_rms(slab) * gkp_ref[...] * tab
    kp2 = rot + pltpu.roll(rot, MLA_ROPE, axis=1)
    kpe_ref[...] = kp2[:, 0:MLA_ROPE]
    kp2_ref[...] = kp2.astype(BF16)


def _mla_proj(zm, zs, wqn, wqp, gql, gkl, gq, gqp, gkp, tab, p128, tm, tab_blocks):
    n = zm.shape[0]
    const = lambda i: (0, 0)
    return pl.pallas_call(
        _mla_proj_body,
        grid=(n // tm,),
        in_specs=[
            pl.BlockSpec((tm, MLA_LORA), lambda i: (i, COL_MQ // MLA_LORA)),
            pl.BlockSpec((tm, MLA_LORA), lambda i: (i, COL_MKV // MLA_LORA)),
            pl.BlockSpec((tm, SMALL_COLS), lambda i: (i, 0)),
            pl.BlockSpec(wqn.shape, const),
            pl.BlockSpec(wqp.shape, const),
            pl.BlockSpec((1, MLA_LORA), const),
            pl.BlockSpec((1, MLA_LORA), const),
            pl.BlockSpec((1, 2 * LANES), const),
            pl.BlockSpec((1, 2 * LANES), const),
            pl.BlockSpec((1, LANES), const),
            pl.BlockSpec((tm, LANES), lambda i: (i % tab_blocks, 0)),
            pl.BlockSpec((2 * LANES, 2 * LANES), const),
        ],
        out_specs=[
            pl.BlockSpec((tm, 2 * LANES * MLA_HEADS), lambda i: (i, 0)),
            pl.BlockSpec((tm, MLA_LORA), lambda i: (i, 0)),
            pl.BlockSpec((tm, MLA_ROPE), lambda i: (i, 0)),
            pl.BlockSpec((tm, LANES), lambda i: (i, 0)),
        ],
        out_shape=[
            jax.ShapeDtypeStruct((n, 2 * LANES * MLA_HEADS), BF16),
            jax.ShapeDtypeStruct((n, MLA_LORA), F32),
            jax.ShapeDtypeStruct((n, MLA_ROPE), F32),
            jax.ShapeDtypeStruct((n, LANES), BF16),
        ],
        compiler_params=_params(("parallel",), 48),
        name="mla_proj",
    )(zm, zm, zs, wqn, wqp, gql, gkl, gq, gqp, gkp, tab, p128)


def _mla_expand_keys(c_ref, wk_ref, gk_ref, p_ref, kn_ref):
    cb = c_ref[...].astype(BF16)
    starts = list(range(0, MLA_HEADS * LANES, 2 * LANES))
    cur = _dot(cb, wk_ref[:, starts[0]:starts[0] + 2 * LANES])
    for g, lo in enumerate(starts):
        nxt = _dot(cb, wk_ref[:, starts[g + 1]:starts[g + 1] + 2 * LANES]) if g + 1 < len(starts) else None
        kn_ref[:, lo:lo + 2 * LANES] = (_slab_rms(cur, p_ref) * gk_ref[...]).astype(BF16)
        cur = nxt
    return cb


def _mla_expand_k_body(c_ref, wk_ref, gk_ref, p_ref, kn_ref):
    _mla_expand_keys(c_ref, wk_ref, gk_ref, p_ref, kn_ref)


def _mla_expand_kv_body(c_ref, wk_ref, gk_ref, p_ref, wvt_ref, kn_ref, vt_ref):
    cb = _mla_expand_keys(c_ref, wk_ref, gk_ref, p_ref, kn_ref)
    vt = _dot_nt(wvt_ref[...], cb).astype(BF16)
    for h in range(MLA_HEADS):
        vt_ref[h, 0] = vt[h * MLA_VDIM:(h + 1) * MLA_VDIM, :]


def _mla_expand(ckv, wk, gk, p128, tm, wvt=None):
    n = ckv.shape[0]
    const = lambda i: (0, 0)
    width = MLA_HEADS * LANES
    in_specs = [
        pl.BlockSpec((tm, MLA_LORA), lambda i: (i, 0)),
        pl.BlockSpec(wk.shape, const),
        pl.BlockSpec((1, 2 * LANES), const),
        pl.BlockSpec((2 * LANES, 2 * LANES), const),
    ]
    out_specs = [pl.BlockSpec((tm, width), lambda i: (i, 0))]
    out_shape = [jax.ShapeDtypeStruct((n, width), BF16)]
    args = [ckv, wk, gk, p128]
    body = _mla_expand_k_body
    if wvt is not None:
        body = _mla_expand_kv_body
        in_specs.append(pl.BlockSpec(wvt.shape, const))
        out_specs.append(pl.BlockSpec((MLA_HEADS, 1, MLA_VDIM, tm), lambda i: (0, i, 0, 0)))
        out_shape.append(jax.ShapeDtypeStruct((MLA_HEADS, n // tm, MLA_VDIM, tm), BF16))
        args.append(wvt)
    return pl.pallas_call(
        body,
        grid=(n // tm,),
        in_specs=in_specs,
        out_specs=out_specs,
        out_shape=out_shape,
        compiler_params=_params(("parallel",), 40),
        name="mla_expand",
    )(*args)


def _attn_prompt_body(qi_tab, kb_tab, q_ref, kn_ref, kp_ref, vt_ref, o_ref,
                      kext_ref, vx_ref, bias_ref, m_ref, acc_ref, s0, s1, p0, p1, a0, a1, *, tq, nq):
    @pl.when((pl.program_id(0) == 0) & (pl.program_id(1) == 0))
    def _():
        kc = lax.broadcasted_iota(jnp.int32, (tq, tq), 0) // CHUNK
        qc = lax.broadcasted_iota(jnp.int32, (tq, tq), 1) // CHUNK
        bias_ref[...] = jnp.where(kc <= qc, 0.0, NEG_BIG)

    kext_ref[:, 0:LANES] = kn_ref[...]
    kext_ref[:, LANES:2 * LANES] = kp_ref[...]
    vx_ref[:, 0:MLA_VDIM, :] = vt_ref[0]
    vx_ref[:, MLA_VDIM:, :] = jnp.ones((nq, vx_ref.shape[1] - MLA_VDIM, tq), BF16)
    m_ref[...] = jnp.full(m_ref.shape, NEG_BIG, F32)
    acc_ref[...] = jnp.zeros(acc_ref.shape, F32)
    nblk = nq * (nq + 1) // 2
    s_bufs, p_bufs, a_bufs = (s0, s1), (p0, p1), (a0, a1)

    def rows(i):
        return pl.ds(pl.multiple_of(i * tq, tq), tq)

    qw = tq // ATTN_QUERY_SPLIT

    def diagonal(t):
        return isinstance(t, int) and t < nq

    def keys_used(t, lo):
        return lo + qw if diagonal(t) else tq

    def scores(t, par, lo):
        nk = keys_used(t, lo)
        q = q_ref[pl.ds(pl.multiple_of(qi_tab[t] * tq + lo, qw), qw), :]
        k = kext_ref[pl.ds(pl.multiple_of(kb_tab[t] * tq, tq), nk), :]
        s_bufs[par][0:nk, lo:lo + qw] = _dot_nt(k, q)

    def softmax(t, par, lo):
        qi = qi_tab[t]
        nk = keys_used(t, lo)
        cs = slice(lo, lo + qw)
        s = s_bufs[par][0:nk, cs]
        if diagonal(t):
            s = s + bias_ref[0:nk, cs]
        m_old = m_ref[qi, :, cs]
        m_new = jnp.maximum(m_old, jnp.max(s, axis=0, keepdims=True))
        a_bufs[par][:, cs] = jnp.exp2(m_old - m_new)
        m_ref[qi, :, cs] = m_new
        p_bufs[par][0:nk, cs] = jnp.exp2(s - m_new).astype(BF16)

    def values(t, par, lo):
        qi = qi_tab[t]
        nk = keys_used(t, lo)
        cs = slice(lo, lo + qw)
        acc_ref[qi, :, cs] = (a_bufs[par][:, cs] * acc_ref[qi, :, cs]
                              + _dot(vx_ref[kb_tab[t]][:, 0:nk], p_bufs[par][0:nk, cs]))

    slices = range(0, tq, qw)

    def step(u, par):
        for lo in slices:
            softmax(u - 1, 1 - par, lo)
            scores(u, par, lo)
            values(u - 2, par, lo)

    for lo in slices:
        scores(0, 0, lo)
    for lo in slices:
        scores(1, 1, lo)
        softmax(0, 0, lo)
    first_loop_step = nq + 2
    for u in range(2, first_loop_step):
        step(u, u % 2)
    trips = (nblk - first_loop_step) // ATTN_UNROLL

    def trip(j, carry):
        for i in range(ATTN_UNROLL):
            step(first_loop_step + ATTN_UNROLL * j + i, i % 2)
        return carry

    lax.fori_loop(0, trips, trip, 0)
    for u in range(first_loop_step + trips * ATTN_UNROLL, nblk):
        step(u, u % 2)
    for lo in slices:
        softmax(nblk - 1, 1, lo)
        values(nblk - 2, 0, lo)
    for lo in slices:
        values(nblk - 1, 1, lo)
    for qi in range(nq):
        acc = acc_ref[qi]
        o_ref[qi * tq:(qi + 1) * tq, :] = jnp.transpose(
            acc[0:MLA_VDIM] / acc[MLA_VDIM:MLA_VDIM + 1]).astype(BF16)


def _attn_prompt(qext, kn, kp2, vt, batch, seq, tq):
    n = qext.shape[0]
    nq = seq // tq
    assert nq % 2 == 0
    pairs = [(i, i) for i in range(nq)] + [(qi, kb) for qi in range(nq) for kb in range(qi)]
    qi_tab = jnp.asarray([p[0] for p in pairs], jnp.int32)
    kb_tab = jnp.asarray([p[1] for p in pairs], jnp.int32)
    ones_rows = 16
    grid_spec = pltpu.PrefetchScalarGridSpec(
        num_scalar_prefetch=2,
        grid=(batch, MLA_HEADS),
        in_specs=[
            pl.BlockSpec((seq, 2 * LANES), lambda b, h, *_: (b, h)),
            pl.BlockSpec((seq, LANES), lambda b, h, *_: (b, h)),
            pl.BlockSpec((seq, LANES), lambda b, h, *_: (b, 0)),
            pl.BlockSpec((1, nq, MLA_VDIM, tq), lambda b, h, *_: (h, b, 0, 0)),
        ],
        out_specs=pl.BlockSpec((seq, MLA_VDIM), lambda b, h, *_: (b, h)),
        scratch_shapes=[
            pltpu.VMEM((seq, 2 * LANES), BF16),
            pltpu.VMEM((nq, MLA_VDIM + ones_rows, tq), BF16),
            pltpu.VMEM((tq, tq), F32),
            pltpu.VMEM((nq, 1, tq), F32),
            pltpu.VMEM((nq, MLA_VDIM + ones_rows, tq), F32),
            pltpu.VMEM((tq, tq), F32), pltpu.VMEM((tq, tq), F32),
            pltpu.VMEM((tq, tq), BF16), pltpu.VMEM((tq, tq), BF16),
            pltpu.VMEM((1, tq), F32), pltpu.VMEM((1, tq), F32),
        ],
    )
    return pl.pallas_call(
        functools.partial(_attn_prompt_body, tq=tq, nq=nq),
        grid_spec=grid_spec,
        out_shape=jax.ShapeDtypeStruct((n, MLA_HEADS * MLA_VDIM), BF16),
        compiler_params=_params(("arbitrary", "arbitrary"), 40),
        name="attn_prompt",
    )(qi_tab, kb_tab, qext, kn, kp2, vt)


SAMPLE_STREAMS_PER_STEP = 2


def _attn_sample_body(q_ref, hm_ref, kpp_ref, cp_ref, knn_ref, kpn_ref, cn_ref, wk_ref, gk_ref, p_ref, wv_ref,
                      o_ref, kn_sc, *, seq):
    past = cp_ref.shape[0] // SAMPLE_STREAMS_PER_STEP

    def with_ones(cb):
        return jnp.concatenate([cb, jnp.ones((cb.shape[0], LANES), BF16)], axis=1)

    streams = range(SAMPLE_STREAMS_PER_STEP)
    new = [slice(s * seq, (s + 1) * seq) for s in streams]
    old = [slice(s * past, (s + 1) * past) for s in streams]
    starts = list(range(0, MLA_HEADS * LANES, 2 * LANES))

    def project(cb, lo):
        return _dot(cb, wk_ref[:, lo:lo + 2 * LANES])

    def normalise(s, lo, kraw):
        kn_sc[s, :, lo:lo + 2 * LANES] = (_slab_rms(kraw, p_ref) * gk_ref[...]).astype(BF16)

    def scores(s):
        q = q_ref[new[s], :]
        qn = jnp.concatenate([q[:, 2 * h * LANES:(2 * h + 1) * LANES] for h in range(MLA_HEADS)], axis=1)
        qbd = jnp.concatenate([qn] * MLA_HEADS, axis=0) * hm_ref[...]
        qpe = jnp.concatenate([q[:, (2 * h + 1) * LANES:(2 * h + 2) * LANES] for h in range(MLA_HEADS)], axis=0)
        s_past = _dot_nt(kn_sc[s], qbd) + _dot_nt(kpp_ref[old[s], :], qpe)
        s_new = _dot_nt(knn_ref[new[s], :], qbd) + _dot_nt(kpn_ref[new[s], :], qpe)
        return s_past, s_new

    def latents(s, cb, s_past, s_new):
        m = jnp.maximum(jnp.max(s_past, axis=0, keepdims=True), jnp.max(s_new, axis=0, keepdims=True))
        p_past = jnp.exp2(s_past - m).astype(BF16)
        p_new = jnp.exp2(s_new - m).astype(BF16)
        acc = (_dot_tn(p_past, with_ones(cb))
               + _dot_tn(p_new, with_ones(cn_ref[new[s], :].astype(BF16))))
        return (acc[:, 0:MLA_LORA] / acc[:, MLA_LORA:MLA_LORA + 1]).astype(BF16)

    def outputs(s, lat):
        for h in range(MLA_HEADS):
            cols = slice(h * MLA_VDIM, (h + 1) * MLA_VDIM)
            o_ref[new[s], cols] = _dot(lat[h * seq:(h + 1) * seq], wv_ref[:, cols]).astype(BF16)

    cbs = [cp_ref[old[s], :].astype(BF16) for s in streams]
    cur = [project(cbs[s], starts[0]) for s in streams]
    for g, lo in enumerate(starts):
        nxt = [project(cbs[s], starts[g + 1]) for s in streams] if g + 1 < len(starts) else None
        for s in streams:
            normalise(s, lo, cur[s])
        cur = nxt
    sc = [scores(s) for s in streams]
    lat = [latents(s, cbs[s], *sc[s]) for s in streams]
    for s in streams:
        outputs(s, lat[s])


def _attn_sample(qext, kp2_past, ckv_past, kn_new, kp2_new, ckv_new, wk, gk, p128, wv, batch, past, seq):
    n = qext.shape[0]
    width = MLA_HEADS * LANES
    head_mask = jnp.asarray(np.kron(np.eye(MLA_HEADS, dtype=np.float32), np.ones((seq, LANES), np.float32)), BF16)
    group = SAMPLE_STREAMS_PER_STEP
    assert batch % group == 0
    seq, past = group * seq, group * past
    stream = lambda b: (b, 0)
    const = lambda b: (0, 0)
    return pl.pallas_call(
        functools.partial(_attn_sample_body, seq=seq // group),
        grid=(batch // group,),
        in_specs=[
            pl.BlockSpec((seq, 2 * width), stream),
            pl.BlockSpec(head_mask.shape, const),
            pl.BlockSpec((past, LANES), stream),
            pl.BlockSpec((past, MLA_LORA), stream),
            pl.BlockSpec((seq, width), stream),
            pl.BlockSpec((seq, LANES), stream),
            pl.BlockSpec((seq, MLA_LORA), stream),
            pl.BlockSpec(wk.shape, const),
            pl.BlockSpec((1, 2 * LANES), const),
            pl.BlockSpec((2 * LANES, 2 * LANES), const),
            pl.BlockSpec(wv.shape, const),
        ],
        out_specs=pl.BlockSpec((seq, MLA_HEADS * MLA_VDIM), stream),
        out_shape=jax.ShapeDtypeStruct((n, MLA_HEADS * MLA_VDIM), BF16),
        scratch_shapes=[pltpu.VMEM((group, past // group, width), BF16)],
        compiler_params=_params(("parallel",), 48),
        name="attn_sample",
    )(qext, head_mask, kp2_past, ckv_past, kn_new, kp2_new, ckv_new, wk, gk, p128, wv)


MERGE_COL_CHUNK = 256


def _branch_merge_body(og_ref, at_ref, wg_ref, wm_ref, ga_ref, gb_ref, u_ref):
    og = og_ref[...]
    at = at_ref[...]
    starts = list(range(0, u_ref.shape[1], MERGE_COL_CHUNK))

    def gates(lo):
        cols = slice(lo, lo + MERGE_COL_CHUNK)
        return _sigmoid(ga_ref[:, cols].astype(F32)), _sigmoid(gb_ref[:, cols].astype(F32))

    nxt = gates(starts[0])
    for g, lo in enumerate(starts):
        cols = slice(lo, lo + MERGE_COL_CHUNK)
        sa, sb = nxt
        ya = _dot(og, wg_ref[:, cols])
        yb = _dot(at, wm_ref[:, cols])
        nxt = gates(starts[g + 1]) if g + 1 < len(starts) else None
        u_ref[:, cols] = (sa * ya + sb * yb).astype(BF16)


def _out_proj_body(u_ref, wo_ref, x_ref, g2_ref, x1_ref, h2_ref):
    x1 = x_ref[...] + _dot(u_ref[...], wo_ref[...])
    x1_ref[...] = x1
    h2_ref[...] = (_row_rms(x1) * g2_ref[...]).astype(BF16)


def _merge(og, att, wg, wm, zm, wo, x, g2, tm_u, tm_x):
    n = x.shape[0]
    tn = 1024
    u = pl.pallas_call(
        _branch_merge_body,
        grid=(n // tm_u, D_MODEL // tn),
        in_specs=[
            pl.BlockSpec((tm_u, D_MODEL), lambda i, j: (i, 0)),
            pl.BlockSpec((tm_u, D_MODEL), lambda i, j: (i, 0)),
            pl.BlockSpec((D_MODEL, tn), lambda i, j: (0, j)),
            pl.BlockSpec((D_MODEL, tn), lambda i, j: (0, j)),
            pl.BlockSpec((tm_u, tn), lambda i, j: (i, COL_GA // tn + j)),
            pl.BlockSpec((tm_u, tn), lambda i, j: (i, COL_GB // tn + j)),
        ],
        out_specs=pl.BlockSpec((tm_u, tn), lambda i, j: (i, j)),
        out_shape=jax.ShapeDtypeStruct((n, D_MODEL), BF16),
        compiler_params=_params(("parallel", "parallel"), 56),
        name="branch_merge",
    )(og, att, wg, wm, zm, zm)
    return pl.pallas_call(
        _out_proj_body,
        grid=(n // tm_x,),
        in_specs=[
            pl.BlockSpec((tm_x, D_MODEL), lambda i: (i, 0)),
            pl.BlockSpec((D_MODEL, D_MODEL), lambda i: (0, 0)),
            pl.BlockSpec((tm_x, D_MODEL), lambda i: (i, 0)),
            pl.BlockSpec((1, D_MODEL), lambda i: (0, 0)),
        ],
        out_specs=[
            pl.BlockSpec((tm_x, D_MODEL), lambda i: (i, 0)),
            pl.BlockSpec((tm_x, D_MODEL), lambda i: (i, 0)),
        ],
        out_shape=[
            jax.ShapeDtypeStruct((n, D_MODEL), F32),
            jax.ShapeDtypeStruct((n, D_MODEL), BF16),
        ],
        compiler_params=_params(("parallel",), 56),
        name="out_proj",
    )(u, wo, x, g2)


FFN_COL_CHUNK = 256


def _gelu_gate(a, a1, a2, cw, gt):
    c = cw[3:4, :] + cw[2:3, :] * a + cw[0:1, :] * a2 + cw[1:2, :] * a1
    return 0.5 * c * (1.0 + lax.erf(c * (2.0 ** -0.5))) * gt


def _ffn_up_seq_body(h_ref, wa_ref, wg_ref, cw_ref, hist_ref, act_ref, tail_ref, carry_ref, *, tm, tiles_per_seq):
    i = pl.program_id(0)
    j = pl.program_id(1)

    @pl.when(i % tiles_per_seq == 0)
    def _():
        carry_ref[j] = hist_ref[0]

    row = lax.broadcasted_iota(jnp.int32, (tm, 1), 0)
    starts = list(range(0, act_ref.shape[1], FFN_COL_CHUNK))

    def project(lo):
        h = h_ref[...]
        return _dot(h, wa_ref[:, lo:lo + FFN_COL_CHUNK]), _dot(h, wg_ref[:, lo:lo + FFN_COL_CHUNK])

    nxt = project(starts[0])
    for g, lo in enumerate(starts):
        cols = slice(lo, lo + FFN_COL_CHUNK)
        a, gt = nxt
        nxt = project(starts[g + 1]) if g + 1 < len(starts) else None
        prev = carry_ref[j, :, cols]
        a1 = jnp.where(row == 0, prev[7:8, :], pltpu.roll(a, 1, axis=0))
        a2 = jnp.where(row == 0, prev[6:7, :], jnp.where(row == 1, prev[7:8, :], pltpu.roll(a, 2, axis=0)))
        act_ref[:, cols] = _gelu_gate(a, a1, a2, cw_ref[:, cols], gt).astype(BF16)
        carry_ref[j, :, cols] = a[tm - 8:tm, :]
        tail_ref[0, :, cols] = a[tm - (CONV_W - 1):tm, :]


def _ffn_up_seq(h2, wa, wg, cw, hist8, seq, tm):
    n = h2.shape[0]
    tn = 512
    nj = D_FF // tn
    tps = seq // tm
    return pl.pallas_call(
        functools.partial(_ffn_up_seq_body, tm=tm, tiles_per_seq=tps),
        grid=(n // tm, nj),
        in_specs=[
            pl.BlockSpec((tm, D_MODEL), lambda i, j: (i, 0)),
            pl.BlockSpec((D_MODEL, tn), lambda i, j: (0, j)),
            pl.BlockSpec((D_MODEL, tn), lambda i, j: (0, j)),
            pl.BlockSpec((8, tn), lambda i, j: (0, j)),
            pl.BlockSpec((1, 8, tn), lambda i, j: (i // tps, 0, j)),
        ],
        out_specs=[
            pl.BlockSpec((tm, tn), lambda i, j: (i, j)),
            pl.BlockSpec((1, CONV_W - 1, tn), lambda i, j: (i, 0, j)),
        ],
        out_shape=[
            jax.ShapeDtypeStruct((n, D_FF), BF16),
            jax.ShapeDtypeStruct((n // tm, CONV_W - 1, D_FF), F32),
        ],
        scratch_shapes=[pltpu.VMEM((nj, 8, tn), F32)],
        compiler_params=_params(("arbitrary", "arbitrary"), 48),
        name="ffn_up_seq",
    )(h2, wa, wg, cw, hist8)


def _ffn_up_multi_body(h_ref, wa_ref, wg_ref, cw_ref, p1_ref, p2_ref, act_ref, a_ref, wab_ref, wgb_ref, *, tm, seq):
    wa = wa_ref[...].astype(BF16)
    wg = wg_ref[...].astype(BF16)
    wab_ref[...] = wa
    wgb_ref[...] = wg
    a = _dot(h_ref[...], wa)
    gt = _dot(h_ref[...], wg)
    pos = lax.broadcasted_iota(jnp.int32, (tm, 1), 0) % seq
    a1 = jnp.where(pos == 0, p1_ref[...], pltpu.roll(a, 1, axis=0))
    a2 = jnp.where(pos <= 1, p2_ref[...], pltpu.roll(a, 2, axis=0))
    act_ref[...] = _gelu_gate(a, a1, a2, cw_ref[...], gt).astype(BF16)
    a_ref[...] = a


def _ffn_up_multi(h2, wup, cw, p1, p2, seq):
    n = h2.shape[0]
    tn = 512
    nj = D_FF // tn
    col = lambda j: (0, j)
    return pl.pallas_call(
        functools.partial(_ffn_up_multi_body, tm=n, seq=seq),
        grid=(nj,),
        in_specs=[
            pl.BlockSpec((n, D_MODEL), lambda j: (0, 0)),
            pl.BlockSpec((D_MODEL, tn), col),
            pl.BlockSpec((D_MODEL, tn), lambda j: (0, nj + j)),
            pl.BlockSpec((8, tn), col),
            pl.BlockSpec((n, tn), col),
            pl.BlockSpec((n, tn), col),
        ],
        out_specs=[pl.BlockSpec((n, tn), col)] * 2 + [pl.BlockSpec((D_MODEL, tn), col)] * 2,
        out_shape=[jax.ShapeDtypeStruct((n, D_FF), BF16), jax.ShapeDtypeStruct((n, D_FF), F32)]
        + [jax.ShapeDtypeStruct((D_MODEL, D_FF), BF16)] * 2,
        compiler_params=_params(("parallel",), 48),
        name="ffn_up_multi",
    )(h2, wup, wup, cw, p1, p2)


def _ffn_down_body(act_ref, wd_ref, x1_ref, o_ref):
    o_ref[...] = x1_ref[...] + _dot(act_ref[...], wd_ref[...])


def _ffn_down_cast_body(act_ref, wd_ref, x1_ref, o_ref, wdb_ref):
    wd = wd_ref[...].astype(BF16)
    wdb_ref[...] = wd
    o_ref[...] = x1_ref[...] + _dot(act_ref[...], wd)


def _ffn_down(act, wd, x1, tm, emit_bf16_weights=False):
    n = act.shape[0]
    tn = 512
    out_specs = [pl.BlockSpec((tm, tn), lambda i, j: (i, j))]
    out_shape = [jax.ShapeDtypeStruct((n, D_MODEL), F32)]
    if emit_bf16_weights:
        assert n == tm
        out_specs.append(pl.BlockSpec((D_FF, tn), lambda i, j: (0, j)))
        out_shape.append(jax.ShapeDtypeStruct((D_FF, D_MODEL), BF16))
    return pl.pallas_call(
        _ffn_down_cast_body if emit_bf16_weights else _ffn_down_body,
        grid=(n // tm, D_MODEL // tn),
        in_specs=[
            pl.BlockSpec((tm, D_FF), lambda i, j: (i, 0)),
            pl.BlockSpec((D_FF, tn), lambda i, j: (0, j)),
            pl.BlockSpec((tm, tn), lambda i, j: (i, j)),
        ],
        out_specs=out_specs,
        out_shape=out_shape,
        compiler_params=_params(("parallel", "parallel"), 56),
        name="ffn_down",
    )(act, wd, x1)


def _swap_halves(w):
    half = w.shape[-1] // 2
    return jnp.concatenate([w[..., half:], w[..., :half]], axis=-1)


def _rope_table(pos):
    half = MLA_ROPE // 2
    inv = ROPE_THETA ** (-np.arange(half, dtype=np.float64) * 2.0 / MLA_ROPE)
    ang = np.asarray(pos, np.float64)[:, None] * inv[None, :]
    cos, sin = np.cos(ang), np.sin(ang)
    return jnp.asarray(np.concatenate([cos, cos, -sin, sin], axis=-1), F32)


def _layer_weights(w_in, g_norm1, gla_w_gate2, gla_b_gate, gla_g_out, w_br_gla, mla_g_qlat, mla_w_uq,
                   mla_g_kvlat, mla_w_ukv, mla_g_q, mla_g_k, mla_g_qpe, mla_g_kpe, w_br_mla, w_out,
                   g_norm2, ffn_w_up, ffn_conv_w, ffn_conv_b, ffn_w_down):
    w2ext = jnp.zeros((SMALL_COLS, GLA_HEADS * GLA_HK), F32).at[SMALL_LR:SMALL_LR + GLA_GATE_RANK].set(
        gla_w_gate2).astype(BF16)
    wq = mla_w_uq.reshape(MLA_LORA, MLA_HEADS, MLA_NOPE + MLA_ROPE)
    wqn = wq[:, :, :MLA_NOPE].reshape(MLA_LORA, MLA_HEADS * MLA_NOPE).astype(BF16)
    wq_pe = wq[:, :, MLA_NOPE:]
    wqp = jnp.concatenate([wq_pe, _swap_halves(wq_pe)], axis=-1).reshape(MLA_LORA, MLA_HEADS * LANES).astype(BF16)
    wkv = mla_w_ukv.reshape(MLA_LORA, MLA_HEADS, MLA_NOPE + MLA_VDIM)
    wk = wkv[:, :, :MLA_NOPE].reshape(MLA_LORA, MLA_HEADS * MLA_NOPE).astype(BF16)
    wv = wkv[:, :, MLA_NOPE:].reshape(MLA_LORA, MLA_HEADS * MLA_VDIM).astype(BF16)
    cw = jnp.concatenate([ffn_conv_w, ffn_conv_b[None, :], jnp.zeros((8 - CONV_W - 1, D_FF), F32)], axis=0)
    return dict(
        w_in_t=w_in.T, g1=g_norm1[None, :], w2ext=w2ext, bg=gla_b_gate[None, :],
        go=gla_g_out[None, :], wg=w_br_gla.astype(BF16), wqn=wqn, wqp=wqp, gql=mla_g_qlat[None, :],
        gkl=mla_g_kvlat[None, :], gq=jnp.tile(mla_g_q, 2)[None, :],
        gqp=jnp.tile(jnp.concatenate([mla_g_qpe, _swap_halves(mla_g_qpe)]), 2)[None, :],
        gkp=jnp.concatenate([mla_g_kpe, _swap_halves(mla_g_kpe)])[None, :],
        wk=wk, wv=wv, wvt=wv.T, gk=jnp.tile(mla_g_k, 2)[None, :], wm=w_br_mla.astype(BF16),
        wo=w_out.astype(BF16),
        g2=g_norm2[None, :], wup=ffn_w_up, cw=cw, wd=ffn_w_down,
        p128=jnp.asarray(np.kron(np.eye(2, dtype=np.float32), np.full((LANES, LANES), 1.0 / LANES, np.float32)), BF16),
    )


def _trunk_front(x, w, s0, batch, seq, chunk, gla_tb, tm_in, tm, tab, tab_blocks, with_values):
    if "w_main" not in w:
        w["w_main"], narrow, zm, zs = _regroup_in_proj(x, w["g1"], w["w_in_t"])
        w["w_small"] = _small_weight_rows(narrow).astype(BF16)
    else:
        zm, zs = _in_proj(x, w["g1"], w["w_main"], w["w_small"], tm_in)
    og, s_new = _gla(zm, zs, w["w2ext"], w["bg"], w["go"], s0, batch, seq, chunk, gla_tb)
    qext, ckv, kpe, kp2 = _mla_proj(zm, zs, w["wqn"], w["wqp"], w["gql"], w["gkl"], w["gq"], w["gqp"],
                                    w["gkp"], tab, w["p128"], tm, tab_blocks)
    kv = _mla_expand(ckv, w["wk"], w["gk"], w["p128"], tm, w["wvt"] if with_values else None)
    return zm, og, s_new, qext, ckv, kpe, kp2, kv


def kernel(x_prompt, x_sample, state_gla, cache_mla_ckv, cache_mla_kpe, cache_ffn_conv, w_in, g_norm1, gla_w_gate2, gla_b_gate, gla_g_out, w_br_gla, mla_g_qlat, mla_w_uq, mla_g_kvlat, mla_w_ukv, mla_g_q, mla_g_k, mla_g_qpe, mla_g_kpe, w_br_mla, w_out, g_norm2, ffn_w_up, ffn_conv_w, ffn_conv_b, ffn_w_down):
    bp, tp, _ = x_prompt.shape
    bs, ts, _ = x_sample.shape
    depth = w_in.shape[0]
    past = cache_mla_ckv.shape[2]
    np_rows, ns_rows = bp * tp, bs * ts
    tm_p = 512
    tm_big = 1024
    tm_ffn = 1024
    tab_p = _rope_table(np.arange(tp))
    tab_s = jnp.tile(_rope_table(past + np.arange(ts)), (bs, 1))
    xp = x_prompt.reshape(np_rows, D_MODEL)
    xs = x_sample.reshape(ns_rows, D_MODEL)
    outs = [[] for _ in range(8)]
    layer_weights = (w_in, g_norm1, gla_w_gate2, gla_b_gate, gla_g_out, w_br_gla, mla_g_qlat, mla_w_uq,
                     mla_g_kvlat, mla_w_ukv, mla_g_q, mla_g_k, mla_g_qpe, mla_g_kpe, w_br_mla, w_out,
                     g_norm2, ffn_w_up, ffn_conv_w, ffn_conv_b, ffn_w_down)
    for l in range(depth):
        w = _layer_weights(*[a[l] for a in layer_weights])

        zm, og, ss, qext, ckv_s, kpe_s, kp2, (kn,) = _trunk_front(
            xs, w, state_gla[l], bs, ts, ts, ts, ns_rows, ns_rows, tab_s, 1, False)
        ckv_past = cache_mla_ckv[l].reshape(bs * past, MLA_LORA)
        kpe_past = cache_mla_kpe[l].reshape(bs * past, MLA_ROPE)
        kp2_past = jnp.concatenate([kpe_past, kpe_past], axis=-1).astype(BF16)
        att = _attn_sample(qext, kp2_past, ckv_past, kn, kp2, ckv_s, w["wk"], w["gk"], w["p128"], w["wv"],
                           bs, past, ts)
        x1, h2 = _merge(og, att, w["wg"], w["wm"], zm, w["wo"], xs, w["g2"], ns_rows, ns_rows)
        hist = cache_ffn_conv[l]
        zrow = jnp.zeros((bs, ts - 1, D_FF), F32)
        p1 = jnp.concatenate([hist[:, 1:2], zrow], axis=1).reshape(ns_rows, D_FF)
        p2 = jnp.concatenate([hist, zrow[:, 1:]], axis=1).reshape(ns_rows, D_FF)
        act, a_full, wa_bf, wg_bf = _ffn_up_multi(h2, w["wup"], w["cw"], p1, p2, ts)
        fs = a_full.reshape(bs, ts, D_FF)[:, ts - (CONV_W - 1):]
        xs, wd_bf = _ffn_down(act, w["wd"], x1, ns_rows, emit_bf16_weights=True)

        s0 = jnp.zeros((bp, GLA_HEADS, GLA_HK, GLA_HV), F32)
        zm, og, sp, qext, ckv_p, kpe_p, kp2, (kn, vt) = _trunk_front(
            xp, w, s0, bp, tp, CHUNK, 512, tm_big, tm_p, tab_p, tp // tm_p, True)
        att = _attn_prompt(qext, kn, kp2, vt, bp, tp, 512)
        x1, h2 = _merge(og, att, w["wg"], w["wm"], zm, w["wo"], xp, w["g2"], tm_big, tm_p)
        hist8 = jnp.zeros((bp, 8, D_FF), F32)
        act, tails = _ffn_up_seq(h2, wa_bf, wg_bf, w["cw"], hist8, tp, tm_ffn)
        fp = tails.reshape(bp, tp // tm_ffn, CONV_W - 1, D_FF)[:, -1]
        (xp,) = _ffn_down(act, wd_bf, x1, tm_big)

        for lst, val in zip(outs, (sp, ss, ckv_p.reshape(bp, tp, MLA_LORA), ckv_s.reshape(bs, ts, MLA_LORA),
                                   kpe_p.reshape(bp, tp, MLA_ROPE), kpe_s.reshape(bs, ts, MLA_ROPE), fp, fs)):
            lst.append(val)
    return (xp.reshape(bp, tp, D_MODEL), xs.reshape(bs, ts, D_MODEL)) + tuple(jnp.stack(o, 0) for o in outs)
```

```python
import functools

import numpy as np
import jax
import jax.numpy as jnp
from jax import lax
from jax.experimental import pallas as pl
from jax.experimental.pallas import tpu as pltpu

F32 = jnp.float32
BF16 = jnp.bfloat16

D_MODEL = 2048
CHUNK = 64
EPS = 1e-6
GLA_HEADS = 4
GLA_HK = 256
GLA_HV = 512
GLA_GATE_RANK = 16
GLA_GATE_TAU = 16.0
MLA_HEADS = 16
MLA_LORA = 512
MLA_NOPE = 128
MLA_ROPE = 64
MLA_VDIM = 128
MLA_SCALE = (MLA_NOPE + MLA_ROPE) ** -0.5
LOG2E = float(np.log2(np.e))
Q_SCALE = MLA_SCALE * LOG2E
ROPE_THETA = 10000.0
D_FF = 5632
CONV_W = 3
LANES = 128
NEG_BIG = -1e30
ATTN_UNROLL = 12
ATTN_QUERY_SPLIT = 2

COL_Q, COL_K, COL_V, COL_R, COL_MQ, COL_MKV, COL_GA, COL_GB, MAIN_COLS = (
    0, 1024, 2048, 4096, 6144, 6656, 7168, 9216, 11264)
SMALL_COLS = 256
SMALL_LR = 128

MIB = 1024 * 1024


def _params(semantics, vmem_mib):
    return pltpu.CompilerParams(dimension_semantics=semantics, vmem_limit_bytes=vmem_mib * MIB)


def _dot(a, b):
    return jnp.dot(a, b, preferred_element_type=F32)


def _dot_nt(a, b):
    return lax.dot_general(a, b, (((1,), (1,)), ((), ())), preferred_element_type=F32)


def _dot_tn(a, b):
    return lax.dot_general(a, b, (((0,), (0,)), ((), ())), preferred_element_type=F32)


def _sigmoid(x):
    return 1.0 / (1.0 + jnp.exp(-x))


def _row_rms(x):
    return x * lax.rsqrt(jnp.mean(x * x, axis=-1, keepdims=True) + EPS)


def _slab_rms(x, p_ref):
    ms = _dot((x * x).astype(BF16), p_ref[...])
    return x * lax.rsqrt(ms + EPS)


W_IN_SEGMENTS = ((0, COL_R, 0), (COL_R, COL_GA, GLA_GATE_RANK), (COL_GA, MAIN_COLS, GLA_GATE_RANK + MLA_ROPE))


def _small_weight_rows(narrow):
    lr = narrow[0:GLA_GATE_RANK]
    kpe = narrow[GLA_GATE_RANK:GLA_GATE_RANK + MLA_ROPE]
    half = MLA_ROPE // 2
    pad = jnp.zeros((SMALL_COLS - SMALL_LR - GLA_GATE_RANK, D_MODEL), narrow.dtype)
    return jnp.concatenate([kpe, kpe[half:], kpe[:half], lr, pad], axis=0)


def _regroup_body(x_ref, g_ref, main_ref, next_ref, o_ref, narrow_ref, zm_ref, zs_ref, h_ref, *, tr):
    j = pl.program_id(0)

    @pl.when(j == 0)
    def _():
        narrow_ref[...] = jnp.zeros(narrow_ref.shape, F32)
        h_ref[...] = (_row_rms(x_ref[...]) * g_ref[...]).astype(BF16)

    _regroup_tile(j, main_ref, next_ref, o_ref, narrow_ref, tr)
    zm_ref[...] = _dot_nt(h_ref[...], o_ref[...]).astype(BF16)

    @pl.when(j == pl.num_programs(0) - 1)
    def _():
        zs_ref[...] = _dot_nt(h_ref[...], _small_weight_rows(narrow_ref[...]).astype(BF16))


def _regroup_tile(j, main_ref, next_ref, o_ref, narrow_ref, tr):
    prev = 0
    for lo, hi, shift in W_IN_SEGMENTS:
        if shift > prev:
            @pl.when(j == lo // tr)
            def _(prev=prev, shift=shift):
                narrow_ref[prev:shift, :] = main_ref[prev:shift, :]

        @pl.when((j >= lo // tr) & (j < hi // tr))
        def _(shift=shift):
            if shift == 0:
                o_ref[...] = main_ref[...].astype(BF16)
            else:
                o_ref[0:tr - shift, :] = main_ref[shift:tr, :].astype(BF16)
                o_ref[tr - shift:tr, :] = next_ref[0:shift, :].astype(BF16)
        prev = shift


def _regroup_in_proj(x, g, w_in_t):
    n = x.shape[0]
    tr = 1024
    assert all(lo % tr == 0 and hi % tr == 0 and shift % 16 == 0 and shift < LANES
               for lo, hi, shift in W_IN_SEGMENTS)
    const = lambda j: (0, 0)
    return pl.pallas_call(
        functools.partial(_regroup_body, tr=tr),
        grid=(MAIN_COLS // tr,),
        in_specs=[
            pl.BlockSpec((n, D_MODEL), const),
            pl.BlockSpec((1, D_MODEL), const),
            pl.BlockSpec((tr, D_MODEL), lambda j: (j, 0)),
            pl.BlockSpec((LANES, D_MODEL), lambda j: ((j + 1) * (tr // LANES), 0)),
        ],
        out_specs=[
            pl.BlockSpec((tr, D_MODEL), lambda j: (j, 0)),
            pl.BlockSpec((LANES, D_MODEL), const),
            pl.BlockSpec((n, tr), lambda j: (0, j)),
            pl.BlockSpec((n, SMALL_COLS), const),
        ],
        out_shape=[
            jax.ShapeDtypeStruct((MAIN_COLS, D_MODEL), BF16),
            jax.ShapeDtypeStruct((LANES, D_MODEL), F32),
            jax.ShapeDtypeStruct((n, MAIN_COLS), BF16),
            jax.ShapeDtypeStruct((n, SMALL_COLS), F32),
        ],
        scratch_shapes=[pltpu.VMEM((n, D_MODEL), BF16)],
        compiler_params=_params(("arbitrary",), 48),
        name="regroup_in_proj",
    )(x, g, w_in_t, w_in_t)


def _in_proj_body(x_ref, g_ref, wm_ref, ws_ref, zm_ref, zs_ref, h_ref):
    @pl.when(pl.program_id(1) == 0)
    def _():
        h_ref[...] = (_row_rms(x_ref[...]) * g_ref[...]).astype(BF16)
        zs_ref[...] = _dot_nt(h_ref[...], ws_ref[...])

    zm_ref[...] = _dot_nt(h_ref[...], wm_ref[...]).astype(BF16)


def _in_proj(x, g, wm, ws, tm):
    n = x.shape[0]
    tn = 1024
    return pl.pallas_call(
        _in_proj_body,
        grid=(n // tm, MAIN_COLS // tn),
        in_specs=[
            pl.BlockSpec((tm, D_MODEL), lambda i, j: (i, 0)),
            pl.BlockSpec((1, D_MODEL), lambda i, j: (0, 0)),
            pl.BlockSpec((tn, D_MODEL), lambda i, j: (j, 0)),
            pl.BlockSpec((SMALL_COLS, D_MODEL), lambda i, j: (0, 0)),
        ],
        out_specs=[
            pl.BlockSpec((tm, tn), lambda i, j: (i, j)),
            pl.BlockSpec((tm, SMALL_COLS), lambda i, j: (i, 0)),
        ],
        out_shape=[
            jax.ShapeDtypeStruct((n, MAIN_COLS), BF16),
            jax.ShapeDtypeStruct((n, SMALL_COLS), F32),
        ],
        scratch_shapes=[pltpu.VMEM((tm, D_MODEL), BF16)],
        compiler_params=_params(("parallel", "arbitrary"), 56),
        name="in_proj",
    )(x, g, wm, ws)


def _gla_tables(c):
    levels = int(np.log2(c))
    t = np.arange(c)[:, None]
    u = np.arange(c)[None, :]
    masks = [(u == t)]
    level2 = None
    for l in range(levels):
        m = c >> (l + 1)
        mid_t = (t // (2 * m)) * 2 * m + m
        upper = t >= mid_t
        if m == 2:
            level2 = np.where(upper, (u >= mid_t) & (u <= t), (u > t) & (u < mid_t))
        mid_u = (u // (2 * m)) * 2 * m + m
        masks.append((t // (2 * m) == u // (2 * m)) & upper & (u < mid_u))
    eye_h = np.eye(GLA_HEADS, dtype=np.float32)
    gmat = np.concatenate([np.kron(eye_h, (u <= t).astype(np.float32)),
                           np.kron(eye_h, level2.astype(np.float32))], axis=0)
    group = _gla_group_heads(c)
    masks = np.stack([np.kron(np.eye(group, dtype=np.float32), mk.astype(np.float32)) for mk in masks])
    return jnp.asarray(gmat, BF16), jnp.asarray(masks), levels


def _gla_group_heads(c):
    return min(GLA_HEADS, max(1, LANES // c))


def _gla_level_exponent(b, log_a, level2, m, c):
    if m == 1:
        row = lax.broadcasted_iota(jnp.int32, (c, 1), 0)
        return jnp.where(row % 2 == 1, log_a, 0.0)
    if m == 2:
        return level2
    parts = [jnp.broadcast_to(b[i + m - 1:i + m, :], (2 * m, GLA_HK)) for i in range(0, c, 2 * m)]
    ref = parts[0] if len(parts) == 1 else jnp.concatenate(parts, axis=0)
    return -jnp.abs(b - ref)


GLA_CHUNKS_PER_TRIP = 4


def _gla_body(q_ref, k_ref, v_ref, r_ref, zs_ref, w2_ref, bg_ref, go_ref, s0_ref, gmat_ref, mask_ref,
              og_ref, sout_ref, st_ref, *, c, nchunk, levels, single):
    t = pl.program_id(1)

    if not single:
        @pl.when(t == 0)
        def _():
            for h in range(GLA_HEADS):
                st_ref[h] = jnp.transpose(s0_ref[0, h])

    nrow = GLA_HEADS * c
    gw = _gla_group_heads(c) * c

    def stack(x, width):
        return jnp.concatenate([x[:, h * width:(h + 1) * width] for h in range(GLA_HEADS)], axis=0)

    def diag_blocks(prod):
        return [prod[g:g + gw, g:g + gw] for g in range(0, nrow, gw)]

    def decays(ci):
        rows = pl.ds(0, c) if single else pl.ds(pl.multiple_of(ci * c, c), c)
        qb = stack(q_ref[rows, :], GLA_HK) * (GLA_HK ** -0.5)
        kb = stack(k_ref[rows, :], GLA_HK)
        q = qb.astype(F32)
        k = kb.astype(F32)
        v = stack(v_ref[rows, :], GLA_HV)
        x = stack(_dot(zs_ref[rows, :].astype(BF16), w2_ref[...]) + bg_ref[...], GLA_HK) * LOG2E
        log_a = (jnp.minimum(x, 0.0) - jnp.log2(1.0 + jnp.exp2(-jnp.abs(x)))) * (1.0 / GLA_GATE_TAU)
        hi = log_a.astype(BF16)
        lo = (log_a - hi.astype(F32)).astype(BF16)
        gm = gmat_ref[...]
        pre = _dot(gm, hi) + _dot(gm, lo)
        b = pre[0:nrow]
        b_last = [b[h * c + c - 1:h * c + c, :] for h in range(GLA_HEADS)]
        q_in = (q * jnp.exp2(b)).astype(BF16)
        k_out = (k * jnp.exp2(jnp.concatenate([jnp.broadcast_to(r, (c, GLA_HK)) for r in b_last], axis=0)
                              - b)).astype(BF16)
        att = [mask_ref[0] * blk for blk in diag_blocks(_dot_nt(qb, kb))]
        return dict(rows=rows, qb=qb, kb=kb, v=v, log_a=log_a, hi=hi, lo=lo, b=b, level2=pre[nrow:2 * nrow],
                    b_last=b_last, q_in=q_in, k_out=k_out, att=att)

    def level(ch, l):
        d = jnp.exp2(_gla_level_exponent(ch["b"], ch["log_a"], ch["level2"], c >> (l + 1), nrow)).astype(BF16)
        prod = _dot_nt(ch["qb"] * d, ch["kb"] * d)
        ch["att"] = [a + mask_ref[1 + l] * blk for a, blk in zip(ch["att"], diag_blocks(prod))]

    def intra(ch):
        att = ch["att"]
        if len(att) == 1:
            att_full = att[0].astype(BF16)
        else:
            zero = jnp.zeros((gw, gw), BF16)
            att_full = jnp.concatenate(
                [jnp.concatenate([a.astype(BF16) if i == j else zero for j in range(len(att))], axis=1)
                 for i, a in enumerate(att)], axis=0)
        ch["o_intra"] = _dot(att_full, ch["v"])

    def carried(ch):
        v, q_in, k_out, o_intra = ch["v"], ch["q_in"], ch["k_out"], ch["o_intra"]
        outs = []
        for h in range(GLA_HEADS):
            hr = slice(h * c, (h + 1) * c)
            if single:
                s0 = s0_ref[0, h]
                outs.append(o_intra[hr] + _dot(q_in[hr], s0.astype(BF16)))
                ones = jnp.ones((c, LANES), BF16)
                decay = jnp.exp2(_dot_tn(ch["hi"][hr], ones) + _dot_tn(ch["lo"][hr], ones))
                sout_ref[0, h] = (s0 * jnp.concatenate([decay] * (GLA_HV // LANES), axis=1)
                                  + _dot_tn(k_out[hr], v[hr]))
            else:
                st = st_ref[h]
                outs.append(o_intra[hr] + _dot_nt(q_in[hr], st.astype(BF16)))
                st_ref[h] = st * jnp.exp2(ch["b_last"][h]) + _dot_tn(v[hr], k_out[hr])
        return outs

    def emit(ch, outs):
        for h, o in enumerate(outs):
            cv = slice(h * GLA_HV, (h + 1) * GLA_HV)
            gate = r_ref[ch["rows"], cv].astype(F32)
            og = _row_rms(o) * go_ref[...] * (gate * _sigmoid(gate))
            og_ref[ch["rows"], cv] = og.astype(BF16)

    def chunks(first, count):
        group = [decays(first + i) for i in range(count)]
        for l in range(levels):
            for ch in group:
                level(ch, l)
        for ch in group:
            intra(ch)
        pending = None
        for ch in group:
            outs = carried(ch)
            if pending is not None:
                emit(*pending)
            pending = (ch, outs)
        emit(*pending)

    if single:
        chunks(0, 1)
        return
    per_trip = GLA_CHUNKS_PER_TRIP if nchunk % GLA_CHUNKS_PER_TRIP == 0 else 1

    def trip(gi, carry):
        chunks(gi * per_trip, per_trip)
        return carry

    lax.fori_loop(0, nchunk // per_trip, trip, 0)

    @pl.when(t == pl.num_programs(1) - 1)
    def _():
        for h in range(GLA_HEADS):
            sout_ref[0, h] = jnp.transpose(st_ref[h])


def _gla(zm, zs, w2ext, bg, go, s0, batch, seq, c, tb):
    n = zm.shape[0]
    nt = seq // tb
    gmat, masks, levels = _gla_tables(c)
    dk, dv = GLA_HEADS * GLA_HK, GLA_HEADS * GLA_HV
    const2 = lambda b, t: (0, 0)
    return pl.pallas_call(
        functools.partial(_gla_body, c=c, nchunk=tb // c, levels=levels, single=(seq == c)),
        grid=(batch, nt),
        in_specs=[
            pl.BlockSpec((tb, dk), lambda b, t: (b * nt + t, COL_Q // dk)),
            pl.BlockSpec((tb, dk), lambda b, t: (b * nt + t, COL_K // dk)),
            pl.BlockSpec((tb, dv), lambda b, t: (b * nt + t, COL_V // dv)),
            pl.BlockSpec((tb, dv), lambda b, t: (b * nt + t, COL_R // dv)),
            pl.BlockSpec((tb, SMALL_COLS), lambda b, t: (b * nt + t, 0)),
            pl.BlockSpec((SMALL_COLS, dk), const2),
            pl.BlockSpec((1, dk), const2),
            pl.BlockSpec((1, GLA_HV), const2),
            pl.BlockSpec((1, GLA_HEADS, GLA_HK, GLA_HV), lambda b, t: (b, 0, 0, 0)),
            pl.BlockSpec(gmat.shape, const2),
            pl.BlockSpec(masks.shape, lambda b, t: (0, 0, 0)),
        ],
        out_specs=[
            pl.BlockSpec((tb, dv), lambda b, t: (b * nt + t, 0)),
            pl.BlockSpec((1, GLA_HEADS, GLA_HK, GLA_HV), lambda b, t: (b, 0, 0, 0)),
        ],
        out_shape=[
            jax.ShapeDtypeStruct((n, dv), BF16),
            jax.ShapeDtypeStruct((batch, GLA_HEADS, GLA_HK, GLA_HV), F32),
        ],
        scratch_shapes=[pltpu.VMEM((GLA_HEADS, GLA_HV, GLA_HK), F32)],
        compiler_params=_params(("parallel", "arbitrary"), 40),
        name="gla",
    )(zm, zm, zm, zm, zs, w2ext, bg, go, s0, gmat, masks)


def _mla_proj_body(mq_ref, mkv_ref, zs_ref, wqn_ref, wqp_ref, gql_ref, gkl_ref, gq_ref, gqp_ref, gkp_ref,
                   tab_ref, p_ref, q_ref, ckv_ref, kpe_ref, kp2_ref):
    qlat = (_row_rms(mq_ref[...].astype(F32)) * gql_ref[...]).astype(BF16)
    tab = tab_ref[...]
    tab2 = jnp.concatenate([tab, tab], axis=1)
    starts = list(range(0, MLA_HEADS * LANES, 2 * LANES))

    def project(lo):
        cols = slice(lo, lo + 2 * LANES)
        return _dot(qlat, wqn_ref[:, cols]), _dot(qlat, wqp_ref[:, cols])

    cur = project(starts[0])
    for g, lo in enumerate(starts):
        nxt = project(starts[g + 1]) if g + 1 < len(starts) else None
        nope = (_slab_rms(cur[0], p_ref) * (gq_ref[...] * Q_SCALE)).astype(BF16)
        pe = (_slab_rms(cur[1], p_ref) * (gqp_ref[...] * Q_SCALE) * tab2).astype(BF16)
        for i in range(2):
            dst = 2 * lo + i * 2 * LANES
            q_ref[:, dst:dst + LANES] = nope[:, i * LANES:(i + 1) * LANES]
            q_ref[:, dst + LANES:dst + 2 * LANES] = pe[:, i * LANES:(i + 1) * LANES]
        cur = nxt
    ckv_ref[...] = _row_rms(mkv_ref[...].astype(F32)) * gkl_ref[...]
    slab = zs_ref[:, 0:LANES]
    rot = _row_rms(slab) * gkp_ref[...] * tab
    kp2 = rot + pltpu.roll(rot, MLA_ROPE, axis=1)
    kpe_ref[...] = kp2[:, 0:MLA_ROPE]
    kp2_ref[...] = kp2.astype(BF16)


def _mla_proj(zm, zs, wqn, wqp, gql, gkl, gq, gqp, gkp, tab, p128, tm, tab_blocks):
    n = zm.shape[0]
    const = lambda i: (0, 0)
    return pl.pallas_call(
        _mla_proj_body,
        grid=(n // tm,),
        in_specs=[
            pl.BlockSpec((tm, MLA_LORA), lambda i: (i, COL_MQ // MLA_LORA)),
            pl.BlockSpec((tm, MLA_LORA), lambda i: (i, COL_MKV // MLA_LORA)),
            pl.BlockSpec((tm, SMALL_COLS), lambda i: (i, 0)),
            pl.BlockSpec(wqn.shape, const),
            pl.BlockSpec(wqp.shape, const),
            pl.BlockSpec((1, MLA_LORA), const),
            pl.BlockSpec((1, MLA_LORA), const),
            pl.BlockSpec((1, 2 * LANES), const),
            pl.BlockSpec((1, 2 * LANES), const),
            pl.BlockSpec((1, LANES), const),
            pl.BlockSpec((tm, LANES), lambda i: (i % tab_blocks, 0)),
            pl.BlockSpec((2 * LANES, 2 * LANES), const),
        ],
        out_specs=[
            pl.BlockSpec((tm, 2 * LANES * MLA_HEADS), lambda i: (i, 0)),
            pl.BlockSpec((tm, MLA_LORA), lambda i: (i, 0)),
            pl.BlockSpec((tm, MLA_ROPE), lambda i: (i, 0)),
            pl.BlockSpec((tm, LANES), lambda i: (i, 0)),
        ],
        out_shape=[
            jax.ShapeDtypeStruct((n, 2 * LANES * MLA_HEADS), BF16),
            jax.ShapeDtypeStruct((n, MLA_LORA), F32),
            jax.ShapeDtypeStruct((n, MLA_ROPE), F32),
            jax.ShapeDtypeStruct((n, LANES), BF16),
        ],
        compiler_params=_params(("parallel",), 48),
        name="mla_proj",
    )(zm, zm, zs, wqn, wqp, gql, gkl, gq, gqp, gkp, tab, p128)


def _mla_expand_keys(c_ref, wk_ref, gk_ref, p_ref, kn_ref):
    cb = c_ref[...].astype(BF16)
    starts = list(range(0, MLA_HEADS * LANES, 2 * LANES))
    cur = _dot(cb, wk_ref[:, starts[0]:starts[0] + 2 * LANES])
    for g, lo in enumerate(starts):
        nxt = _dot(cb, wk_ref[:, starts[g + 1]:starts[g + 1] + 2 * LANES]) if g + 1 < len(starts) else None
        kn_ref[:, lo:lo + 2 * LANES] = (_slab_rms(cur, p_ref) * gk_ref[...]).astype(BF16)
        cur = nxt
    return cb


def _mla_expand_k_body(c_ref, wk_ref, gk_ref, p_ref, kn_ref):
    _mla_expand_keys(c_ref, wk_ref, gk_ref, p_ref, kn_ref)


def _mla_expand_kv_body(c_ref, wk_ref, gk_ref, p_ref, wvt_ref, kn_ref, vt_ref):
    cb = _mla_expand_keys(c_ref, wk_ref, gk_ref, p_ref, kn_ref)
    vt = _dot_nt(wvt_ref[...], cb).astype(BF16)
    for h in range(MLA_HEADS):
        vt_ref[h, 0] = vt[h * MLA_VDIM:(h + 1) * MLA_VDIM, :]


def _mla_expand(ckv, wk, gk, p128, tm, wvt=None):
    n = ckv.shape[0]
    const = lambda i: (0, 0)
    width = MLA_HEADS * LANES
    in_specs = [
        pl.BlockSpec((tm, MLA_LORA), lambda i: (i, 0)),
        pl.BlockSpec(wk.shape, const),
        pl.BlockSpec((1, 2 * LANES), const),
        pl.BlockSpec((2 * LANES, 2 * LANES), const),
    ]
    out_specs = [pl.BlockSpec((tm, width), lambda i: (i, 0))]
    out_shape = [jax.ShapeDtypeStruct((n, width), BF16)]
    args = [ckv, wk, gk, p128]
    body = _mla_expand_k_body
    if wvt is not None:
        body = _mla_expand_kv_body
        in_specs.append(pl.BlockSpec(wvt.shape, const))
        out_specs.append(pl.BlockSpec((MLA_HEADS, 1, MLA_VDIM, tm), lambda i: (0, i, 0, 0)))
        out_shape.append(jax.ShapeDtypeStruct((MLA_HEADS, n // tm, MLA_VDIM, tm), BF16))
        args.append(wvt)
    return pl.pallas_call(
        body,
        grid=(n // tm,),
        in_specs=in_specs,
        out_specs=out_specs,
        out_shape=out_shape,
        compiler_params=_params(("parallel",), 40),
        name="mla_expand",
    )(*args)


def _attn_prompt_body(qi_tab, kb_tab, q_ref, kn_ref, kp_ref, vt_ref, o_ref,
                      kext_ref, vx_ref, bias_ref, m_ref, acc_ref, s0, s1, p0, p1, a0, a1, *, tq, nq):
    @pl.when((pl.program_id(0) == 0) & (pl.program_id(1) == 0))
    def _():
        kc = lax.broadcasted_iota(jnp.int32, (tq, tq), 0) // CHUNK
        qc = lax.broadcasted_iota(jnp.int32, (tq, tq), 1) // CHUNK
        bias_ref[...] = jnp.where(kc <= qc, 0.0, NEG_BIG)

    kext_ref[:, 0:LANES] = kn_ref[...]
    kext_ref[:, LANES:2 * LANES] = kp_ref[...]
    vx_ref[:, 0:MLA_VDIM, :] = vt_ref[0]
    vx_ref[:, MLA_VDIM:, :] = jnp.ones((nq, vx_ref.shape[1] - MLA_VDIM, tq), BF16)
    m_ref[...] = jnp.full(m_ref.shape, NEG_BIG, F32)
    acc_ref[...] = jnp.zeros(acc_ref.shape, F32)
    nblk = nq * (nq + 1) // 2
    s_bufs, p_bufs, a_bufs = (s0, s1), (p0, p1), (a0, a1)

    def rows(i):
        return pl.ds(pl.multiple_of(i * tq, tq), tq)

    qw = tq // ATTN_QUERY_SPLIT

    def diagonal(t):
        return isinstance(t, int) and t < nq

    def keys_used(t, lo):
        return lo + qw if diagonal(t) else tq

    def scores(t, par, lo):
        nk = keys_used(t, lo)
        q = q_ref[pl.ds(pl.multiple_of(qi_tab[t] * tq + lo, qw), qw), :]
        k = kext_ref[pl.ds(pl.multiple_of(kb_tab[t] * tq, tq), nk), :]
        s_bufs[par][0:nk, lo:lo + qw] = _dot_nt(k, q)

    def softmax(t, par, lo):
        qi = qi_tab[t]
        nk = keys_used(t, lo)
        cs = slice(lo, lo + qw)
        s = s_bufs[par][0:nk, cs]
        if diagonal(t):
            s = s + bias_ref[0:nk, cs]
        m_old = m_ref[qi, :, cs]
        m_new = jnp.maximum(m_old, jnp.max(s, axis=0, keepdims=True))
        a_bufs[par][:, cs] = jnp.exp2(m_old - m_new)
        m_ref[qi, :, cs] = m_new
        p_bufs[par][0:nk, cs] = jnp.exp2(s - m_new).astype(BF16)

    def values(t, par, lo):
        qi = qi_tab[t]
        nk = keys_used(t, lo)
        cs = slice(lo, lo + qw)
        acc_ref[qi, :, cs] = (a_bufs[par][:, cs] * acc_ref[qi, :, cs]
                              + _dot(vx_ref[kb_tab[t]][:, 0:nk], p_bufs[par][0:nk, cs]))

    slices = range(0, tq, qw)

    def step(u, par):
        for lo in slices:
            softmax(u - 1, 1 - par, lo)
            scores(u, par, lo)
            values(u - 2, par, lo)

    for lo in slices:
        scores(0, 0, lo)
    for lo in slices:
        scores(1, 1, lo)
        softmax(0, 0, lo)
    first_loop_step = nq + 2
    for u in range(2, first_loop_step):
        step(u, u % 2)
    trips = (nblk - first_loop_step) // ATTN_UNROLL

    def trip(j, carry):
        for i in range(ATTN_UNROLL):
            step(first_loop_step + ATTN_UNROLL * j + i, i % 2)
        return carry

    lax.fori_loop(0, trips, trip, 0)
    for u in range(first_loop_step + trips * ATTN_UNROLL, nblk):
        step(u, u % 2)
    for lo in slices:
        softmax(nblk - 1, 1, lo)
        values(nblk - 2, 0, lo)
    for lo in slices:
        values(nblk - 1, 1, lo)
    for qi in range(nq):
        acc = acc_ref[qi]
        o_ref[qi * tq:(qi + 1) * tq, :] = jnp.transpose(
            acc[0:MLA_VDIM] / acc[MLA_VDIM:MLA_VDIM + 1]).astype(BF16)


def _attn_prompt(qext, kn, kp2, vt, batch, seq, tq):
    n = qext.shape[0]
    nq = seq // tq
    assert nq % 2 == 0
    pairs = [(i, i) for i in range(nq)] + [(qi, kb) for qi in range(nq) for kb in range(qi)]
    qi_tab = jnp.asarray([p[0] for p in pairs], jnp.int32)
    kb_tab = jnp.asarray([p[1] for p in pairs], jnp.int32)
    ones_rows = 16
    grid_spec = pltpu.PrefetchScalarGridSpec(
        num_scalar_prefetch=2,
        grid=(batch, MLA_HEADS),
        in_specs=[
            pl.BlockSpec((seq, 2 * LANES), lambda b, h, *_: (b, h)),
            pl.BlockSpec((seq, LANES), lambda b, h, *_: (b, h)),
            pl.BlockSpec((seq, LANES), lambda b, h, *_: (b, 0)),
            pl.BlockSpec((1, nq, MLA_VDIM, tq), lambda b, h, *_: (h, b, 0, 0)),
        ],
        out_specs=pl.BlockSpec((seq, MLA_VDIM), lambda b, h, *_: (b, h)),
        scratch_shapes=[
            pltpu.VMEM((seq, 2 * LANES), BF16),
            pltpu.VMEM((nq, MLA_VDIM + ones_rows, tq), BF16),
            pltpu.VMEM((tq, tq), F32),
            pltpu.VMEM((nq, 1, tq), F32),
            pltpu.VMEM((nq, MLA_VDIM + ones_rows, tq), F32),
            pltpu.VMEM((tq, tq), F32), pltpu.VMEM((tq, tq), F32),
            pltpu.VMEM((tq, tq), BF16), pltpu.VMEM((tq, tq), BF16),
            pltpu.VMEM((1, tq), F32), pltpu.VMEM((1, tq), F32),
        ],
    )
    return pl.pallas_call(
        functools.partial(_attn_prompt_body, tq=tq, nq=nq),
        grid_spec=grid_spec,
        out_shape=jax.ShapeDtypeStruct((n, MLA_HEADS * MLA_VDIM), BF16),
        compiler_params=_params(("arbitrary", "arbitrary"), 40),
        name="attn_prompt",
    )(qi_tab, kb_tab, qext, kn, kp2, vt)


SAMPLE_STREAMS_PER_STEP = 2


def _attn_sample_body(q_ref, hm_ref, kpp_ref, cp_ref, knn_ref, kpn_ref, cn_ref, wk_ref, gk_ref, p_ref, wv_ref,
                      o_ref, kn_sc, *, seq):
    past = cp_ref.shape[0] // SAMPLE_STREAMS_PER_STEP

    def with_ones(cb):
        return jnp.concatenate([cb, jnp.ones((cb.shape[0], LANES), BF16)], axis=1)

    streams = range(SAMPLE_STREAMS_PER_STEP)
    new = [slice(s * seq, (s + 1) * seq) for s in streams]
    old = [slice(s * past, (s + 1) * past) for s in streams]
    starts = list(range(0, MLA_HEADS * LANES, 2 * LANES))

    def project(cb, lo):
        return _dot(cb, wk_ref[:, lo:lo + 2 * LANES])

    def normalise(s, lo, kraw):
        kn_sc[s, :, lo:lo + 2 * LANES] = (_slab_rms(kraw, p_ref) * gk_ref[...]).astype(BF16)

    def scores(s):
        q = q_ref[new[s], :]
        qn = jnp.concatenate([q[:, 2 * h * LANES:(2 * h + 1) * LANES] for h in range(MLA_HEADS)], axis=1)
        qbd = jnp.concatenate([qn] * MLA_HEADS, axis=0) * hm_ref[...]
        qpe = jnp.concatenate([q[:, (2 * h + 1) * LANES:(2 * h + 2) * LANES] for h in range(MLA_HEADS)], axis=0)
        s_past = _dot_nt(kn_sc[s], qbd) + _dot_nt(kpp_ref[old[s], :], qpe)
        s_new = _dot_nt(knn_ref[new[s], :], qbd) + _dot_nt(kpn_ref[new[s], :], qpe)
        return s_past, s_new

    def latents(s, cb, s_past, s_new):
        m = jnp.maximum(jnp.max(s_past, axis=0, keepdims=True), jnp.max(s_new, axis=0, keepdims=True))
        p_past = jnp.exp2(s_past - m).astype(BF16)
        p_new = jnp.exp2(s_new - m).astype(BF16)
        acc = (_dot_tn(p_past, with_ones(cb))
               + _dot_tn(p_new, with_ones(cn_ref[new[s], :].astype(BF16))))
        return (acc[:, 0:MLA_LORA] / acc[:, MLA_LORA:MLA_LORA + 1]).astype(BF16)

    def outputs(s, lat):
        for h in range(MLA_HEADS):
            cols = slice(h * MLA_VDIM, (h + 1) * MLA_VDIM)
            o_ref[new[s], cols] = _dot(lat[h * seq:(h + 1) * seq], wv_ref[:, cols]).astype(BF16)

    cbs = [cp_ref[old[s], :].astype(BF16) for s in streams]
    cur = [project(cbs[s], starts[0]) for s in streams]
    for g, lo in enumerate(starts):
        nxt = [project(cbs[s], starts[g + 1]) for s in streams] if g + 1 < len(starts) else None
        for s in streams:
            normalise(s, lo, cur[s])
        cur = nxt
    sc = [scores(s) for s in streams]
    lat = [latents(s, cbs[s], *sc[s]) for s in streams]
    for s in streams:
        outputs(s, lat[s])


def _attn_sample(qext, kp2_past, ckv_past, kn_new, kp2_new, ckv_new, wk, gk, p128, wv, batch, past, seq):
    n = qext.shape[0]
    width = MLA_HEADS * LANES
    head_mask = jnp.asarray(np.kron(np.eye(MLA_HEADS, dtype=np.float32), np.ones((seq, LANES), np.float32)), BF16)
    group = SAMPLE_STREAMS_PER_STEP
    assert batch % group == 0
    seq, past = group * seq, group * past
    stream = lambda b: (b, 0)
    const = lambda b: (0, 0)
    return pl.pallas_call(
        functools.partial(_attn_sample_body, seq=seq // group),
        grid=(batch // group,),
        in_specs=[
            pl.BlockSpec((seq, 2 * width), stream),
            pl.BlockSpec(head_mask.shape, const),
            pl.BlockSpec((past, LANES), stream),
            pl.BlockSpec((past, MLA_LORA), stream),
            pl.BlockSpec((seq, width), stream),
            pl.BlockSpec((seq, LANES), stream),
            pl.BlockSpec((seq, MLA_LORA), stream),
            pl.BlockSpec(wk.shape, const),
            pl.BlockSpec((1, 2 * LANES), const),
            pl.BlockSpec((2 * LANES, 2 * LANES), const),
            pl.BlockSpec(wv.shape, const),
        ],
        out_specs=pl.BlockSpec((seq, MLA_HEADS * MLA_VDIM), stream),
        out_shape=jax.ShapeDtypeStruct((n, MLA_HEADS * MLA_VDIM), BF16),
        scratch_shapes=[pltpu.VMEM((group, past // group, width), BF16)],
        compiler_params=_params(("parallel",), 48),
        name="attn_sample",
    )(qext, head_mask, kp2_past, ckv_past, kn_new, kp2_new, ckv_new, wk, gk, p128, wv)


MERGE_COL_CHUNK = 256


def _branch_merge_body(og_ref, at_ref, wg_ref, wm_ref, ga_ref, gb_ref, u_ref):
    og = og_ref[...]
    at = at_ref[...]
    starts = list(range(0, u_ref.shape[1], MERGE_COL_CHUNK))

    def gates(lo):
        cols = slice(lo, lo + MERGE_COL_CHUNK)
        return _sigmoid(ga_ref[:, cols].astype(F32)), _sigmoid(gb_ref[:, cols].astype(F32))

    nxt = gates(starts[0])
    for g, lo in enumerate(starts):
        cols = slice(lo, lo + MERGE_COL_CHUNK)
        sa, sb = nxt
        ya = _dot(og, wg_ref[:, cols])
        yb = _dot(at, wm_ref[:, cols])
        nxt = gates(starts[g + 1]) if g + 1 < len(starts) else None
        u_ref[:, cols] = (sa * ya + sb * yb).astype(BF16)


def _out_proj_body(u_ref, wo_ref, x_ref, g2_ref, x1_ref, h2_ref):
    x1 = x_ref[...] + _dot(u_ref[...], wo_ref[...])
    x1_ref[...] = x1
    h2_ref[...] = (_row_rms(x1) * g2_ref[...]).astype(BF16)


def _merge(og, att, wg, wm, zm, wo, x, g2, tm_u, tm_x):
    n = x.shape[0]
    tn = 1024 if tm_u >= 1024 else 512
    u = pl.pallas_call(
        _branch_merge_body,
        grid=(n // tm_u, D_MODEL // tn),
        in_specs=[
            pl.BlockSpec((tm_u, D_MODEL), lambda i, j: (i, 0)),
            pl.BlockSpec((tm_u, D_MODEL), lambda i, j: (i, 0)),
            pl.BlockSpec((D_MODEL, tn), lambda i, j: (0, j)),
            pl.BlockSpec((D_MODEL, tn), lambda i, j: (0, j)),
            pl.BlockSpec((tm_u, tn), lambda i, j: (i, COL_GA // tn + j)),
            pl.BlockSpec((tm_u, tn), lambda i, j: (i, COL_GB // tn + j)),
        ],
        out_specs=pl.BlockSpec((tm_u, tn), lambda i, j: (i, j)),
        out_shape=jax.ShapeDtypeStruct((n, D_MODEL), BF16),
        compiler_params=_params(("parallel", "parallel"), 56),
        name="branch_merge",
    )(og, att, wg, wm, zm, zm)
    return pl.pallas_call(
        _out_proj_body,
        grid=(n // tm_x,),
        in_specs=[
            pl.BlockSpec((tm_x, D_MODEL), lambda i: (i, 0)),
            pl.BlockSpec((D_MODEL, D_MODEL), lambda i: (0, 0)),
            pl.BlockSpec((tm_x, D_MODEL), lambda i: (i, 0)),
            pl.BlockSpec((1, D_MODEL), lambda i: (0, 0)),
        ],
        out_specs=[
            pl.BlockSpec((tm_x, D_MODEL), lambda i: (i, 0)),
            pl.BlockSpec((tm_x, D_MODEL), lambda i: (i, 0)),
        ],
        out_shape=[
            jax.ShapeDtypeStruct((n, D_MODEL), F32),
            jax.ShapeDtypeStruct((n, D_MODEL), BF16),
        ],
        compiler_params=_params(("parallel",), 56),
        name="out_proj",
    )(u, wo, x, g2)


FFN_COL_CHUNK = 256


def _gelu_gate(a, a1, a2, cw, gt):
    c = cw[3:4, :] + cw[2:3, :] * a + cw[0:1, :] * a2 + cw[1:2, :] * a1
    return 0.5 * c * (1.0 + lax.erf(c * (2.0 ** -0.5))) * gt


def _ffn_up_seq_body(h_ref, wa_ref, wg_ref, cw_ref, hist_ref, act_ref, tail_ref, carry_ref, *, tm, tiles_per_seq):
    i = pl.program_id(0)
    j = pl.program_id(1)

    @pl.when(i % tiles_per_seq == 0)
    def _():
        carry_ref[j] = hist_ref[0]

    row = lax.broadcasted_iota(jnp.int32, (tm, 1), 0)
    starts = list(range(0, act_ref.shape[1], FFN_COL_CHUNK))

    def project(lo):
        h = h_ref[...]
        return _dot(h, wa_ref[:, lo:lo + FFN_COL_CHUNK]), _dot(h, wg_ref[:, lo:lo + FFN_COL_CHUNK])

    nxt = project(starts[0])
    for g, lo in enumerate(starts):
        cols = slice(lo, lo + FFN_COL_CHUNK)
        a, gt = nxt
        nxt = project(starts[g + 1]) if g + 1 < len(starts) else None
        prev = carry_ref[j, :, cols]
        a1 = jnp.where(row == 0, prev[7:8, :], pltpu.roll(a, 1, axis=0))
        a2 = jnp.where(row == 0, prev[6:7, :], jnp.where(row == 1, prev[7:8, :], pltpu.roll(a, 2, axis=0)))
        act_ref[:, cols] = _gelu_gate(a, a1, a2, cw_ref[:, cols], gt).astype(BF16)
        carry_ref[j, :, cols] = a[tm - 8:tm, :]
        tail_ref[0, :, cols] = a[tm - (CONV_W - 1):tm, :]


def _ffn_up_seq(h2, wa, wg, cw, hist8, seq, tm):
    n = h2.shape[0]
    tn = 512
    nj = D_FF // tn
    tps = seq // tm
    return pl.pallas_call(
        functools.partial(_ffn_up_seq_body, tm=tm, tiles_per_seq=tps),
        grid=(n // tm, nj),
        in_specs=[
            pl.BlockSpec((tm, D_MODEL), lambda i, j: (i, 0)),
            pl.BlockSpec((D_MODEL, tn), lambda i, j: (0, j)),
            pl.BlockSpec((D_MODEL, tn), lambda i, j: (0, j)),
            pl.BlockSpec((8, tn), lambda i, j: (0, j)),
            pl.BlockSpec((1, 8, tn), lambda i, j: (i // tps, 0, j)),
        ],
        out_specs=[
            pl.BlockSpec((tm, tn), lambda i, j: (i, j)),
            pl.BlockSpec((1, CONV_W - 1, tn), lambda i, j: (i, 0, j)),
        ],
        out_shape=[
            jax.ShapeDtypeStruct((n, D_FF), BF16),
            jax.ShapeDtypeStruct((n // tm, CONV_W - 1, D_FF), F32),
        ],
        scratch_shapes=[pltpu.VMEM((nj, 8, tn), F32)],
        compiler_params=_params(("arbitrary", "arbitrary"), 48),
        name="ffn_up_seq",
    )(h2, wa, wg, cw, hist8)


def _ffn_up_multi_body(h_ref, wa_ref, wg_ref, cw_ref, p1_ref, p2_ref, act_ref, a_ref, wab_ref, wgb_ref, *, tm, seq):
    wa = wa_ref[...].astype(BF16)
    wg = wg_ref[...].astype(BF16)
    wab_ref[...] = wa
    wgb_ref[...] = wg
    a = _dot(h_ref[...], wa)
    gt = _dot(h_ref[...], wg)
    pos = lax.broadcasted_iota(jnp.int32, (tm, 1), 0) % seq
    a1 = jnp.where(pos == 0, p1_ref[...], pltpu.roll(a, 1, axis=0))
    a2 = jnp.where(pos <= 1, p2_ref[...], pltpu.roll(a, 2, axis=0))
    act_ref[...] = _gelu_gate(a, a1, a2, cw_ref[...], gt).astype(BF16)
    a_ref[...] = a


def _ffn_up_multi(h2, wup, cw, p1, p2, seq):
    n = h2.shape[0]
    tn = 512
    nj = D_FF // tn
    col = lambda j: (0, j)
    return pl.pallas_call(
        functools.partial(_ffn_up_multi_body, tm=n, seq=seq),
        grid=(nj,),
        in_specs=[
            pl.BlockSpec((n, D_MODEL), lambda j: (0, 0)),
            pl.BlockSpec((D_MODEL, tn), col),
            pl.BlockSpec((D_MODEL, tn), lambda j: (0, nj + j)),
            pl.BlockSpec((8, tn), col),
            pl.BlockSpec((n, tn), col),
            pl.BlockSpec((n, tn), col),
        ],
        out_specs=[pl.BlockSpec((n, tn), col)] * 2 + [pl.BlockSpec((D_MODEL, tn), col)] * 2,
        out_shape=[jax.ShapeDtypeStruct((n, D_FF), BF16), jax.ShapeDtypeStruct((n, D_FF), F32)]
        + [jax.ShapeDtypeStruct((D_MODEL, D_FF), BF16)] * 2,
        compiler_params=_params(("parallel",), 48),
        name="ffn_up_multi",
    )(h2, wup, wup, cw, p1, p2)


def _ffn_down_body(act_ref, wd_ref, x1_ref, o_ref):
    o_ref[...] = x1_ref[...] + _dot(act_ref[...], wd_ref[...])


def _ffn_down_cast_body(act_ref, wd_ref, x1_ref, o_ref, wdb_ref):
    wd = wd_ref[...].astype(BF16)
    wdb_ref[...] = wd
    o_ref[...] = x1_ref[...] + _dot(act_ref[...], wd)


def _ffn_down(act, wd, x1, tm, emit_bf16_weights=False):
    n = act.shape[0]
    tn = 512
    out_specs = [pl.BlockSpec((tm, tn), lambda i, j: (i, j))]
    out_shape = [jax.ShapeDtypeStruct((n, D_MODEL), F32)]
    if emit_bf16_weights:
        assert n == tm
        out_specs.append(pl.BlockSpec((D_FF, tn), lambda i, j: (0, j)))
        out_shape.append(jax.ShapeDtypeStruct((D_FF, D_MODEL), BF16))
    return pl.pallas_call(
        _ffn_down_cast_body if emit_bf16_weights else _ffn_down_body,
        grid=(n // tm, D_MODEL // tn),
        in_specs=[
            pl.BlockSpec((tm, D_FF), lambda i, j: (i, 0)),
            pl.BlockSpec((D_FF, tn), lambda i, j: (0, j)),
            pl.BlockSpec((tm, tn), lambda i, j: (i, j)),
        ],
        out_specs=out_specs,
        out_shape=out_shape,
        compiler_params=_params(("parallel", "parallel"), 56),
        name="ffn_down",
    )(act, wd, x1)


def _swap_halves(w):
    half = w.shape[-1] // 2
    return jnp.concatenate([w[..., half:], w[..., :half]], axis=-1)


def _rope_table(pos):
    half = MLA_ROPE // 2
    inv = ROPE_THETA ** (-np.arange(half, dtype=np.float64) * 2.0 / MLA_ROPE)
    ang = np.asarray(pos, np.float64)[:, None] * inv[None, :]
    cos, sin = np.cos(ang), np.sin(ang)
    return jnp.asarray(np.concatenate([cos, cos, -sin, sin], axis=-1), F32)


def _layer_weights(w_in, g_norm1, gla_w_gate2, gla_b_gate, gla_g_out, w_br_gla, mla_g_qlat, mla_w_uq,
                   mla_g_kvlat, mla_w_ukv, mla_g_q, mla_g_k, mla_g_qpe, mla_g_kpe, w_br_mla, w_out,
                   g_norm2, ffn_w_up, ffn_conv_w, ffn_conv_b, ffn_w_down):
    w2ext = jnp.zeros((SMALL_COLS, GLA_HEADS * GLA_HK), F32).at[SMALL_LR:SMALL_LR + GLA_GATE_RANK].set(
        gla_w_gate2).astype(BF16)
    wq = mla_w_uq.reshape(MLA_LORA, MLA_HEADS, MLA_NOPE + MLA_ROPE)
    wqn = wq[:, :, :MLA_NOPE].reshape(MLA_LORA, MLA_HEADS * MLA_NOPE).astype(BF16)
    wq_pe = wq[:, :, MLA_NOPE:]
    wqp = jnp.concatenate([wq_pe, _swap_halves(wq_pe)], axis=-1).reshape(MLA_LORA, MLA_HEADS * LANES).astype(BF16)
    wkv = mla_w_ukv.reshape(MLA_LORA, MLA_HEADS, MLA_NOPE + MLA_VDIM)
    wk = wkv[:, :, :MLA_NOPE].reshape(MLA_LORA, MLA_HEADS * MLA_NOPE).astype(BF16)
    wv = wkv[:, :, MLA_NOPE:].reshape(MLA_LORA, MLA_HEADS * MLA_VDIM).astype(BF16)
    cw = jnp.concatenate([ffn_conv_w, ffn_conv_b[None, :], jnp.zeros((8 - CONV_W - 1, D_FF), F32)], axis=0)
    return dict(
        w_in_t=w_in.T, g1=g_norm1[None, :], w2ext=w2ext, bg=gla_b_gate[None, :],
        go=gla_g_out[None, :], wg=w_br_gla.astype(BF16), wqn=wqn, wqp=wqp, gql=mla_g_qlat[None, :],
        gkl=mla_g_kvlat[None, :], gq=jnp.tile(mla_g_q, 2)[None, :],
        gqp=jnp.tile(jnp.concatenate([mla_g_qpe, _swap_halves(mla_g_qpe)]), 2)[None, :],
        gkp=jnp.concatenate([mla_g_kpe, _swap_halves(mla_g_kpe)])[None, :],
        wk=wk, wv=wv, wvt=wv.T, gk=jnp.tile(mla_g_k, 2)[None, :], wm=w_br_mla.astype(BF16),
        wo=w_out.astype(BF16),
        g2=g_norm2[None, :], wup=ffn_w_up, cw=cw, wd=ffn_w_down,
        p128=jnp.asarray(np.kron(np.eye(2, dtype=np.float32), np.full((LANES, LANES), 1.0 / LANES, np.float32)), BF16),
    )


def _trunk_front(x, w, s0, batch, seq, chunk, gla_tb, tm_in, tm, tab, tab_blocks, with_values):
    if "w_main" not in w:
        w["w_main"], narrow, zm, zs = _regroup_in_proj(x, w["g1"], w["w_in_t"])
        w["w_small"] = _small_weight_rows(narrow).astype(BF16)
    else:
        zm, zs = _in_proj(x, w["g1"], w["w_main"], w["w_small"], tm_in)
    og, s_new = _gla(zm, zs, w["w2ext"], w["bg"], w["go"], s0, batch, seq, chunk, gla_tb)
    qext, ckv, kpe, kp2 = _mla_proj(zm, zs, w["wqn"], w["wqp"], w["gql"], w["gkl"], w["gq"], w["gqp"],
                                    w["gkp"], tab, w["p128"], tm, tab_blocks)
    kv = _mla_expand(ckv, w["wk"], w["gk"], w["p128"], tm, w["wvt"] if with_values else None)
    return zm, og, s_new, qext, ckv, kpe, kp2, kv


def kernel(x_prompt, x_sample, state_gla, cache_mla_ckv, cache_mla_kpe, cache_ffn_conv, w_in, g_norm1, gla_w_gate2, gla_b_gate, gla_g_out, w_br_gla, mla_g_qlat, mla_w_uq, mla_g_kvlat, mla_w_ukv, mla_g_q, mla_g_k, mla_g_qpe, mla_g_kpe, w_br_mla, w_out, g_norm2, ffn_w_up, ffn_conv_w, ffn_conv_b, ffn_w_down):
    bp, tp, _ = x_prompt.shape
    bs, ts, _ = x_sample.shape
    depth = w_in.shape[0]
    past = cache_mla_ckv.shape[2]
    np_rows, ns_rows = bp * tp, bs * ts
    tm_p = 512
    tm_big = 1024
    tm_ffn = 1024
    tab_p = _rope_table(np.arange(tp))
    tab_s = jnp.tile(_rope_table(past + np.arange(ts)), (bs, 1))
    xp = x_prompt.reshape(np_rows, D_MODEL)
    xs = x_sample.reshape(ns_rows, D_MODEL)
    outs = [[] for _ in range(8)]
    layer_weights = (w_in, g_norm1, gla_w_gate2, gla_b_gate, gla_g_out, w_br_gla, mla_g_qlat, mla_w_uq,
                     mla_g_kvlat, mla_w_ukv, mla_g_q, mla_g_k, mla_g_qpe, mla_g_kpe, w_br_mla, w_out,
                     g_norm2, ffn_w_up, ffn_conv_w, ffn_conv_b, ffn_w_down)
    for l in range(depth):
        w = _layer_weights(*[a[l] for a in layer_weights])

        zm, og, ss, qext, ckv_s, kpe_s, kp2, (kn,) = _trunk_front(
            xs, w, state_gla[l], bs, ts, ts, ts, ns_rows, ns_rows, tab_s, 1, False)
        ckv_past = cache_mla_ckv[l].reshape(bs * past, MLA_LORA)
        kpe_past = cache_mla_kpe[l].reshape(bs * past, MLA_ROPE)
        kp2_past = jnp.concatenate([kpe_past, kpe_past], axis=-1).astype(BF16)
        att = _attn_sample(qext, kp2_past, ckv_past, kn, kp2, ckv_s, w["wk"], w["gk"], w["p128"], w["wv"],
                           bs, past, ts)
        x1, h2 = _merge(og, att, w["wg"], w["wm"], zm, w["wo"], xs, w["g2"], ns_rows, ns_rows)
        hist = cache_ffn_conv[l]
        zrow = jnp.zeros((bs, ts - 1, D_FF), F32)
        p1 = jnp.concatenate([hist[:, 1:2], zrow], axis=1).reshape(ns_rows, D_FF)
        p2 = jnp.concatenate([hist, zrow[:, 1:]], axis=1).reshape(ns_rows, D_FF)
        act, a_full, wa_bf, wg_bf = _ffn_up_multi(h2, w["wup"], w["cw"], p1, p2, ts)
        fs = a_full.reshape(bs, ts, D_FF)[:, ts - (CONV_W - 1):]
        xs, wd_bf = _ffn_down(act, w["wd"], x1, ns_rows, emit_bf16_weights=True)

        s0 = jnp.zeros((bp, GLA_HEADS, GLA_HK, GLA_HV), F32)
        zm, og, sp, qext, ckv_p, kpe_p, kp2, (kn, vt) = _trunk_front(
            xp, w, s0, bp, tp, CHUNK, 512, tm_big, tm_p, tab_p, tp // tm_p, True)
        att = _attn_prompt(qext, kn, kp2, vt, bp, tp, 512)
        x1, h2 = _merge(og, att, w["wg"], w["wm"], zm, w["wo"], xp, w["g2"], tm_big, tm_p)
        hist8 = jnp.zeros((bp, 8, D_FF), F32)
        act, tails = _ffn_up_seq(h2, wa_bf, wg_bf, w["cw"], hist8, tp, tm_ffn)
        fp = tails.reshape(bp, tp // tm_ffn, CONV_W - 1, D_FF)[:, -1]
        (xp,) = _ffn_down(act, wd_bf, x1, tm_big)

        for lst, val in zip(outs, (sp, ss, ckv_p.reshape(bp, tp, MLA_LORA), ckv_s.reshape(bs, ts, MLA_LORA),
                                   kpe_p.reshape(bp, tp, MLA_ROPE), kpe_s.reshape(bs, ts, MLA_ROPE), fp, fs)):
            lst.append(val)
    return (xp.reshape(bp, tp, D_MODEL), xs.reshape(bs, ts, D_MODEL)) + tuple(jnp.stack(o, 0) for o in outs)
```

```python
import functools

import numpy as np
import jax
import jax.numpy as jnp
from jax import lax
from jax.experimental import pallas as pl
from jax.experimental.pallas import tpu as pltpu

F32 = jnp.float32
BF16 = jnp.bfloat16

D_MODEL = 2048
CHUNK = 64
EPS = 1e-6
GLA_HEADS = 4
GLA_HK = 256
GLA_HV = 512
GLA_GATE_RANK = 16
GLA_GATE_TAU = 16.0
MLA_HEADS = 16
MLA_LORA = 512
MLA_NOPE = 128
MLA_ROPE = 64
MLA_VDIM = 128
MLA_SCALE = (MLA_NOPE + MLA_ROPE) ** -0.5
LOG2E = float(np.log2(np.e))
Q_SCALE = MLA_SCALE * LOG2E
ROPE_THETA = 10000.0
D_FF = 5632
CONV_W = 3
LANES = 128
NEG_BIG = -1e30
ATTN_UNROLL = 12
ATTN_QUERY_SPLIT = 2

COL_Q, COL_K, COL_V, COL_R, COL_MQ, COL_MKV, COL_GA, COL_GB, MAIN_COLS = (
    0, 1024, 2048, 4096, 6144, 6656, 7168, 9216, 11264)
SMALL_COLS = 256
SMALL_LR = 128

MIB = 1024 * 1024


def _params(semantics, vmem_mib):
    return pltpu.CompilerParams(dimension_semantics=semantics, vmem_limit_bytes=vmem_mib * MIB)


def _dot(a, b):
    return jnp.dot(a, b, preferred_element_type=F32)


def _dot_nt(a, b):
    return lax.dot_general(a, b, (((1,), (1,)), ((), ())), preferred_element_type=F32)


def _dot_tn(a, b):
    return lax.dot_general(a, b, (((0,), (0,)), ((), ())), preferred_element_type=F32)


def _sigmoid(x):
    return 1.0 / (1.0 + jnp.exp(-x))


def _row_rms(x):
    return x * lax.rsqrt(jnp.mean(x * x, axis=-1, keepdims=True) + EPS)


def _slab_rms(x, p_ref):
    ms = _dot((x * x).astype(BF16), p_ref[...])
    return x * lax.rsqrt(ms + EPS)


W_IN_SEGMENTS = ((0, COL_R, 0), (COL_R, COL_GA, GLA_GATE_RANK), (COL_GA, MAIN_COLS, GLA_GATE_RANK + MLA_ROPE))


def _small_weight_rows(narrow):
    lr = narrow[0:GLA_GATE_RANK]
    kpe = narrow[GLA_GATE_RANK:GLA_GATE_RANK + MLA_ROPE]
    half = MLA_ROPE // 2
    pad = jnp.zeros((SMALL_COLS - SMALL_LR - GLA_GATE_RANK, D_MODEL), narrow.dtype)
    return jnp.concatenate([kpe, kpe[half:], kpe[:half], lr, pad], axis=0)


def _regroup_body(x_ref, g_ref, main_ref, next_ref, o_ref, narrow_ref, zm_ref, zs_ref, h_ref, *, tr):
    j = pl.program_id(0)

    @pl.when(j == 0)
    def _():
        narrow_ref[...] = jnp.zeros(narrow_ref.shape, F32)
        h_ref[...] = (_row_rms(x_ref[...]) * g_ref[...]).astype(BF16)

    _regroup_tile(j, main_ref, next_ref, o_ref, narrow_ref, tr)
    zm_ref[...] = _dot_nt(h_ref[...], o_ref[...]).astype(BF16)

    @pl.when(j == pl.num_programs(0) - 1)
    def _():
        zs_ref[...] = _dot_nt(h_ref[...], _small_weight_rows(narrow_ref[...]).astype(BF16))


def _regroup_tile(j, main_ref, next_ref, o_ref, narrow_ref, tr):
    prev = 0
    for lo, hi, shift in W_IN_SEGMENTS:
        if shift > prev:
            @pl.when(j == lo // tr)
            def _(prev=prev, shift=shift):
                narrow_ref[prev:shift, :] = main_ref[prev:shift, :]

        @pl.when((j >= lo // tr) & (j < hi // tr))
        def _(shift=shift):
            if shift == 0:
                o_ref[...] = main_ref[...].astype(BF16)
            else:
                o_ref[0:tr - shift, :] = main_ref[shift:tr, :].astype(BF16)
                o_ref[tr - shift:tr, :] = next_ref[0:shift, :].astype(BF16)
        prev = shift


def _regroup_in_proj(x, g, w_in_t):
    n = x.shape[0]
    tr = 1024
    assert all(lo % tr == 0 and hi % tr == 0 and shift % 16 == 0 and shift < LANES
               for lo, hi, shift in W_IN_SEGMENTS)
    const = lambda j: (0, 0)
    return pl.pallas_call(
        functools.partial(_regroup_body, tr=tr),
        grid=(MAIN_COLS // tr,),
        in_specs=[
            pl.BlockSpec((n, D_MODEL), const),
            pl.BlockSpec((1, D_MODEL), const),
            pl.BlockSpec((tr, D_MODEL), lambda j: (j, 0)),
            pl.BlockSpec((LANES, D_MODEL), lambda j: ((j + 1) * (tr // LANES), 0)),
        ],
        out_specs=[
            pl.BlockSpec((tr, D_MODEL), lambda j: (j, 0)),
            pl.BlockSpec((LANES, D_MODEL), const),
            pl.BlockSpec((n, tr), lambda j: (0, j)),
            pl.BlockSpec((n, SMALL_COLS), const),
        ],
        out_shape=[
            jax.ShapeDtypeStruct((MAIN_COLS, D_MODEL), BF16),
            jax.ShapeDtypeStruct((LANES, D_MODEL), F32),
            jax.ShapeDtypeStruct((n, MAIN_COLS), BF16),
            jax.ShapeDtypeStruct((n, SMALL_COLS), F32),
        ],
        scratch_shapes=[pltpu.VMEM((n, D_MODEL), BF16)],
        compiler_params=_params(("arbitrary",), 48),
        name="regroup_in_proj",
    )(x, g, w_in_t, w_in_t)


def _in_proj_body(x_ref, g_ref, wm_ref, ws_ref, zm_ref, zs_ref, h_ref):
    @pl.when(pl.program_id(1) == 0)
    def _():
        h_ref[...] = (_row_rms(x_ref[...]) * g_ref[...]).astype(BF16)
        zs_ref[...] = _dot_nt(h_ref[...], ws_ref[...])

    zm_ref[...] = _dot_nt(h_ref[...], wm_ref[...]).astype(BF16)


def _in_proj(x, g, wm, ws, tm):
    n = x.shape[0]
    tn = 1024
    return pl.pallas_call(
        _in_proj_body,
        grid=(n // tm, MAIN_COLS // tn),
        in_specs=[
            pl.BlockSpec((tm, D_MODEL), lambda i, j: (i, 0)),
            pl.BlockSpec((1, D_MODEL), lambda i, j: (0, 0)),
            pl.BlockSpec((tn, D_MODEL), lambda i, j: (j, 0)),
            pl.BlockSpec((SMALL_COLS, D_MODEL), lambda i, j: (0, 0)),
        ],
        out_specs=[
            pl.BlockSpec((tm, tn), lambda i, j: (i, j)),
            pl.BlockSpec((tm, SMALL_COLS), lambda i, j: (i, 0)),
        ],
        out_shape=[
            jax.ShapeDtypeStruct((n, MAIN_COLS), BF16),
            jax.ShapeDtypeStruct((n, SMALL_COLS), F32),
        ],
        scratch_shapes=[pltpu.VMEM((tm, D_MODEL), BF16)],
        compiler_params=_params(("parallel", "arbitrary"), 56),
        name="in_proj",
    )(x, g, wm, ws)


def _gla_tables(c):
    levels = int(np.log2(c))
    t = np.arange(c)[:, None]
    u = np.arange(c)[None, :]
    masks = [(u == t)]
    level2 = None
    for l in range(levels):
        m = c >> (l + 1)
        mid_t = (t // (2 * m)) * 2 * m + m
        upper = t >= mid_t
        if m == 2:
            level2 = np.where(upper, (u >= mid_t) & (u <= t), (u > t) & (u < mid_t))
        mid_u = (u // (2 * m)) * 2 * m + m
        masks.append((t // (2 * m) == u // (2 * m)) & upper & (u < mid_u))
    eye_h = np.eye(GLA_HEADS, dtype=np.float32)
    gmat = np.concatenate([np.kron(eye_h, (u <= t).astype(np.float32)),
                           np.kron(eye_h, level2.astype(np.float32))], axis=0)
    group = _gla_group_heads(c)
    masks = np.stack([np.kron(np.eye(group, dtype=np.float32), mk.astype(np.float32)) for mk in masks])
    return jnp.asarray(gmat, BF16), jnp.asarray(masks), levels


def _gla_group_heads(c):
    return min(GLA_HEADS, max(1, LANES // c))


def _gla_level_exponent(b, log_a, level2, m, c):
    if m == 1:
        row = lax.broadcasted_iota(jnp.int32, (c, 1), 0)
        return jnp.where(row % 2 == 1, log_a, 0.0)
    if m == 2:
        return level2
    parts = [jnp.broadcast_to(b[i + m - 1:i + m, :], (2 * m, GLA_HK)) for i in range(0, c, 2 * m)]
    ref = parts[0] if len(parts) == 1 else jnp.concatenate(parts, axis=0)
    return -jnp.abs(b - ref)


GLA_CHUNKS_PER_TRIP = 4


def _gla_body(q_ref, k_ref, v_ref, r_ref, zs_ref, w2_ref, bg_ref, go_ref, s0_ref, gmat_ref, mask_ref,
              og_ref, sout_ref, st_ref, *, c, nchunk, levels, single):
    t = pl.program_id(1)

    if not single:
        @pl.when(t == 0)
        def _():
            for h in range(GLA_HEADS):
                st_ref[h] = jnp.transpose(s0_ref[0, h])

    nrow = GLA_HEADS * c
    gw = _gla_group_heads(c) * c

    def stack(x, width):
        return jnp.concatenate([x[:, h * width:(h + 1) * width] for h in range(GLA_HEADS)], axis=0)

    def diag_blocks(prod):
        return [prod[g:g + gw, g:g + gw] for g in range(0, nrow, gw)]

    def decays(ci):
        rows = pl.ds(0, c) if single else pl.ds(pl.multiple_of(ci * c, c), c)
        qb = stack(q_ref[rows, :], GLA_HK) * (GLA_HK ** -0.5)
        kb = stack(k_ref[rows, :], GLA_HK)
        q = qb.astype(F32)
        k = kb.astype(F32)
        v = stack(v_ref[rows, :], GLA_HV)
        x = stack(_dot(zs_ref[rows, :].astype(BF16), w2_ref[...]) + bg_ref[...], GLA_HK) * LOG2E
        log_a = (jnp.minimum(x, 0.0) - jnp.log2(1.0 + jnp.exp2(-jnp.abs(x)))) * (1.0 / GLA_GATE_TAU)
        hi = log_a.astype(BF16)
        lo = (log_a - hi.astype(F32)).astype(BF16)
        gm = gmat_ref[...]
        pre = _dot(gm, hi) + _dot(gm, lo)
        b = pre[0:nrow]
        b_last = [b[h * c + c - 1:h * c + c, :] for h in range(GLA_HEADS)]
        q_in = (q * jnp.exp2(b)).astype(BF16)
        k_out = (k * jnp.exp2(jnp.concatenate([jnp.broadcast_to(r, (c, GLA_HK)) for r in b_last], axis=0)
                              - b)).astype(BF16)
        att = [mask_ref[0] * blk for blk in diag_blocks(_dot_nt(qb, kb))]
        return dict(rows=rows, qb=qb, kb=kb, v=v, log_a=log_a, hi=hi, lo=lo, b=b, level2=pre[nrow:2 * nrow],
                    b_last=b_last, q_in=q_in, k_out=k_out, att=att)

    def level(ch, l):
        d = jnp.exp2(_gla_level_exponent(ch["b"], ch["log_a"], ch["level2"], c >> (l + 1), nrow)).astype(BF16)
        prod = _dot_nt(ch["qb"] * d, ch["kb"] * d)
        ch["att"] = [a + mask_ref[1 + l] * blk for a, blk in zip(ch["att"], diag_blocks(prod))]

    def intra(ch):
        att = ch["att"]
        if len(att) == 1:
            att_full = att[0].astype(BF16)
        else:
            zero = jnp.zeros((gw, gw), BF16)
            att_full = jnp.concatenate(
                [jnp.concatenate([a.astype(BF16) if i == j else zero for j in range(len(att))], axis=1)
                 for i, a in enumerate(att)], axis=0)
        ch["o_intra"] = _dot(att_full, ch["v"])

    def carried(ch):
        v, q_in, k_out, o_intra = ch["v"], ch["q_in"], ch["k_out"], ch["o_intra"]
        outs = []
        for h in range(GLA_HEADS):
            hr = slice(h * c, (h + 1) * c)
            if single:
                s0 = s0_ref[0, h]
                outs.append(o_intra[hr] + _dot(q_in[hr], s0.astype(BF16)))
                ones = jnp.ones((c, LANES), BF16)
                decay = jnp.exp2(_dot_tn(ch["hi"][hr], ones) + _dot_tn(ch["lo"][hr], ones))
                sout_ref[0, h] = (s0 * jnp.concatenate([decay] * (GLA_HV // LANES), axis=1)
                                  + _dot_tn(k_out[hr], v[hr]))
            else:
                st = st_ref[h]
                outs.append(o_intra[hr] + _dot_nt(q_in[hr], st.astype(BF16)))
                st_ref[h] = st * jnp.exp2(ch["b_last"][h]) + _dot_tn(v[hr], k_out[hr])
        return outs

    def emit(ch, outs):
        for h, o in enumerate(outs):
            cv = slice(h * GLA_HV, (h + 1) * GLA_HV)
            gate = r_ref[ch["rows"], cv].astype(F32)
            og = _row_rms(o) * go_ref[...] * (gate * _sigmoid(gate))
            og_ref[ch["rows"], cv] = og.astype(BF16)

    def chunks(first, count):
        group = [decays(first + i) for i in range(count)]
        for l in range(levels):
            for ch in group:
                level(ch, l)
        for ch in group:
            intra(ch)
        pending = None
        for ch in group:
            outs = carried(ch)
            if pending is not None:
                emit(*pending)
            pending = (ch, outs)
        emit(*pending)

    if single:
        chunks(0, 1)
        return
    per_trip = GLA_CHUNKS_PER_TRIP if nchunk % GLA_CHUNKS_PER_TRIP == 0 else 1

    def trip(gi, carry):
        chunks(gi * per_trip, per_trip)
        return carry

    lax.fori_loop(0, nchunk // per_trip, trip, 0)

    @pl.when(t == pl.num_programs(1) - 1)
    def _():
        for h in range(GLA_HEADS):
            sout_ref[0, h] = jnp.transpose(st_ref[h])


def _gla(zm, zs, w2ext, bg, go, s0, batch, seq, c, tb):
    n = zm.shape[0]
    nt = seq // tb
    gmat, masks, levels = _gla_tables(c)
    dk, dv = GLA_HEADS * GLA_HK, GLA_HEADS * GLA_HV
    const2 = lambda b, t: (0, 0)
    return pl.pallas_call(
        functools.partial(_gla_body, c=c, nchunk=tb // c, levels=levels, single=(seq == c)),
        grid=(batch, nt),
        in_specs=[
            pl.BlockSpec((tb, dk), lambda b, t: (b * nt + t, COL_Q // dk)),
            pl.BlockSpec((tb, dk), lambda b, t: (b * nt + t, COL_K // dk)),
            pl.BlockSpec((tb, dv), lambda b, t: (b * nt + t, COL_V // dv)),
            pl.BlockSpec((tb, dv), lambda b, t: (b * nt + t, COL_R // dv)),
            pl.BlockSpec((tb, SMALL_COLS), lambda b, t: (b * nt + t, 0)),
            pl.BlockSpec((SMALL_COLS, dk), const2),
            pl.BlockSpec((1, dk), const2),
            pl.BlockSpec((1, GLA_HV), const2),
            pl.BlockSpec((1, GLA_HEADS, GLA_HK, GLA_HV), lambda b, t: (b, 0, 0, 0)),
            pl.BlockSpec(gmat.shape, const2),
            pl.BlockSpec(masks.shape, lambda b, t: (0, 0, 0)),
        ],
        out_specs=[
            pl.BlockSpec((tb, dv), lambda b, t: (b * nt + t, 0)),
            pl.BlockSpec((1, GLA_HEADS, GLA_HK, GLA_HV), lambda b, t: (b, 0, 0, 0)),
        ],
        out_shape=[
            jax.ShapeDtypeStruct((n, dv), BF16),
            jax.ShapeDtypeStruct((batch, GLA_HEADS, GLA_HK, GLA_HV), F32),
        ],
        scratch_shapes=[pltpu.VMEM((GLA_HEADS, GLA_HV, GLA_HK), F32)],
        compiler_params=_params(("parallel", "arbitrary"), 40),
        name="gla",
    )(zm, zm, zm, zm, zs, w2ext, bg, go, s0, gmat, masks)


def _mla_proj_body(mq_ref, mkv_ref, zs_ref, wqn_ref, wqp_ref, gql_ref, gkl_ref, gq_ref, gqp_ref, gkp_ref,
                   tab_ref, p_ref, q_ref, ckv_ref, kpe_ref, kp2_ref):
    qlat = (_row_rms(mq_ref[...].astype(F32)) * gql_ref[...]).astype(BF16)
    tab = tab_ref[...]
    tab2 = jnp.concatenate([tab, tab], axis=1)
    starts = list(range(0, MLA_HEADS * LANES, 2 * LANES))

    def project(lo):
        cols = slice(lo, lo + 2 * LANES)
        return _dot(qlat, wqn_ref[:, cols]), _dot(qlat, wqp_ref[:, cols])

    cur = project(starts[0])
    for g, lo in enumerate(starts):
        nxt = project(starts[g + 1]) if g + 1 < len(starts) else None
        nope = (_slab_rms(cur[0], p_ref) * (gq_ref[...] * Q_SCALE)).astype(BF16)
        pe = (_slab_rms(cur[1], p_ref) * (gqp_ref[...] * Q_SCALE) * tab2).astype(BF16)
        for i in range(2):
            dst = 2 * lo + i * 2 * LANES
            q_ref[:, dst:dst + LANES] = nope[:, i * LANES:(i + 1) * LANES]
            q_ref[:, dst + LANES:dst + 2 * LANES] = pe[:, i * LANES:(i + 1) * LANES]
        cur = nxt
    ckv_ref[...] = _row_rms(mkv_ref[...].astype(F32)) * gkl_ref[...]
    slab = zs_ref[:, 0:LANES]
    rot = _row_rms(slab) * gkp_ref[...] * tab
    kp2 = rot + pltpu.roll(rot, MLA_ROPE, axis=1)
    kpe_ref[...] = kp2[:, 0:MLA_ROPE]
    kp2_ref[...] = kp2.astype(BF16)


def _mla_proj(zm, zs, wqn, wqp, gql, gkl, gq, gqp, gkp, tab, p128, tm, tab_blocks):
    n = zm.shape[0]
    const = lambda i: (0, 0)
    return pl.pallas_call(
        _mla_proj_body,
        grid=(n // tm,),
        in_specs=[
            pl.BlockSpec((tm, MLA_LORA), lambda i: (i, COL_MQ // MLA_LORA)),
            pl.BlockSpec((tm, MLA_LORA), lambda i: (i, COL_MKV // MLA_LORA)),
            pl.BlockSpec((tm, SMALL_COLS), lambda i: (i, 0)),
            pl.BlockSpec(wqn.shape, const),
            pl.BlockSpec(wqp.shape, const),
            pl.BlockSpec((1, MLA_LORA), const),
            pl.BlockSpec((1, MLA_LORA), const),
            pl.BlockSpec((1, 2 * LANES), const),
            pl.BlockSpec((1, 2 * LANES), const),
            pl.BlockSpec((1, LANES), const),
            pl.BlockSpec((tm, LANES), lambda i: (i % tab_blocks, 0)),
            pl.BlockSpec((2 * LANES, 2 * LANES), const),
        ],
        out_specs=[
            pl.BlockSpec((tm, 2 * LANES * MLA_HEADS), lambda i: (i, 0)),
            pl.BlockSpec((tm, MLA_LORA), lambda i: (i, 0)),
            pl.BlockSpec((tm, MLA_ROPE), lambda i: (i, 0)),
            pl.BlockSpec((tm, LANES), lambda i: (i, 0)),
        ],
        out_shape=[
            jax.ShapeDtypeStruct((n, 2 * LANES * MLA_HEADS), BF16),
            jax.ShapeDtypeStruct((n, MLA_LORA), F32),
            jax.ShapeDtypeStruct((n, MLA_ROPE), F32),
            jax.ShapeDtypeStruct((n, LANES), BF16),
        ],
        compiler_params=_params(("parallel",), 48),
        name="mla_proj",
    )(zm, zm, zs, wqn, wqp, gql, gkl, gq, gqp, gkp, tab, p128)


def _mla_expand_keys(c_ref, wk_ref, gk_ref, p_ref, kn_ref):
    cb = c_ref[...].astype(BF16)
    starts = list(range(0, MLA_HEADS * LANES, 2 * LANES))
    cur = _dot(cb, wk_ref[:, starts[0]:starts[0] + 2 * LANES])
    for g, lo in enumerate(starts):
        nxt = _dot(cb, wk_ref[:, starts[g + 1]:starts[g + 1] + 2 * LANES]) if g + 1 < len(starts) else None
        kn_ref[:, lo:lo + 2 * LANES] = (_slab_rms(cur, p_ref) * gk_ref[...]).astype(BF16)
        cur = nxt
    return cb


def _mla_expand_k_body(c_ref, wk_ref, gk_ref, p_ref, kn_ref):
    _mla_expand_keys(c_ref, wk_ref, gk_ref, p_ref, kn_ref)


def _mla_expand_kv_body(c_ref, wk_ref, gk_ref, p_ref, wvt_ref, kn_ref, vt_ref):
    cb = _mla_expand_keys(c_ref, wk_ref, gk_ref, p_ref, kn_ref)
    vt = _dot_nt(wvt_ref[...], cb).astype(BF16)
    for h in range(MLA_HEADS):
        vt_ref[h, 0] = vt[h * MLA_VDIM:(h + 1) * MLA_VDIM, :]


def _mla_expand(ckv, wk, gk, p128, tm, wvt=None):
    n = ckv.shape[0]
    const = lambda i: (0, 0)
    width = MLA_HEADS * LANES
    in_specs = [
        pl.BlockSpec((tm, MLA_LORA), lambda i: (i, 0)),
        pl.BlockSpec(wk.shape, const),
        pl.BlockSpec((1, 2 * LANES), const),
        pl.BlockSpec((2 * LANES, 2 * LANES), const),
    ]
    out_specs = [pl.BlockSpec((tm, width), lambda i: (i, 0))]
    out_shape = [jax.ShapeDtypeStruct((n, width), BF16)]
    args = [ckv, wk, gk, p128]
    body = _mla_expand_k_body
    if wvt is not None:
        body = _mla_expand_kv_body
        in_specs.append(pl.BlockSpec(wvt.shape, const))
        out_specs.append(pl.BlockSpec((MLA_HEADS, 1, MLA_VDIM, tm), lambda i: (0, i, 0, 0)))
        out_shape.append(jax.ShapeDtypeStruct((MLA_HEADS, n // tm, MLA_VDIM, tm), BF16))
        args.append(wvt)
    return pl.pallas_call(
        body,
        grid=(n // tm,),
        in_specs=in_specs,
        out_specs=out_specs,
        out_shape=out_shape,
        compiler_params=_params(("parallel",), 40),
        name="mla_expand",
    )(*args)


def _attn_prompt_body(qi_tab, kb_tab, q_ref, kn_ref, kp_ref, vt_ref, o_ref,
                      kext_ref, vx_ref, bias_ref, m_ref, acc_ref, s0, s1, p0, p1, a0, a1, *, tq, nq):
    @pl.when((pl.program_id(0) == 0) & (pl.program_id(1) == 0))
    def _():
        kc = lax.broadcasted_iota(jnp.int32, (tq, tq), 0) // CHUNK
        qc = lax.broadcasted_iota(jnp.int32, (tq, tq), 1) // CHUNK
        bias_ref[...] = jnp.where(kc <= qc, 0.0, NEG_BIG)

    kext_ref[:, 0:LANES] = kn_ref[...]
    kext_ref[:, LANES:2 * LANES] = kp_ref[...]
    vx_ref[:, 0:MLA_VDIM, :] = vt_ref[0]
    vx_ref[:, MLA_VDIM:, :] = jnp.ones((nq, vx_ref.shape[1] - MLA_VDIM, tq), BF16)
    m_ref[...] = jnp.full(m_ref.shape, NEG_BIG, F32)
    acc_ref[...] = jnp.zeros(acc_ref.shape, F32)
    nblk = nq * (nq + 1) // 2
    s_bufs, p_bufs, a_bufs = (s0, s1), (p0, p1), (a0, a1)

    def rows(i):
        return pl.ds(pl.multiple_of(i * tq, tq), tq)

    qw = tq // ATTN_QUERY_SPLIT

    def diagonal(t):
        return isinstance(t, int) and t < nq

    def keys_used(t, lo):
        return lo + qw if diagonal(t) else tq

    def scores(t, par, lo):
        nk = keys_used(t, lo)
        q = q_ref[pl.ds(pl.multiple_of(qi_tab[t] * tq + lo, qw), qw), :]
        k = kext_ref[pl.ds(pl.multiple_of(kb_tab[t] * tq, tq), nk), :]
        s_bufs[par][0:nk, lo:lo + qw] = _dot_nt(k, q)

    def softmax(t, par, lo):
        qi = qi_tab[t]
        nk = keys_used(t, lo)
        cs = slice(lo, lo + qw)
        s = s_bufs[par][0:nk, cs]
        if diagonal(t):
            s = s + bias_ref[0:nk, cs]
        m_old = m_ref[qi, :, cs]
        m_new = jnp.maximum(m_old, jnp.max(s, axis=0, keepdims=True))
        a_bufs[par][:, cs] = jnp.exp2(m_old - m_new)
        m_ref[qi, :, cs] = m_new
        p_bufs[par][0:nk, cs] = jnp.exp2(s - m_new).astype(BF16)

    def values(t, par, lo):
        qi = qi_tab[t]
        nk = keys_used(t, lo)
        cs = slice(lo, lo + qw)
        acc_ref[qi, :, cs] = (a_bufs[par][:, cs] * acc_ref[qi, :, cs]
                              + _dot(vx_ref[kb_tab[t]][:, 0:nk], p_bufs[par][0:nk, cs]))

    slices = range(0, tq, qw)

    def step(u, par):
        for lo in slices:
            softmax(u - 1, 1 - par, lo)
            scores(u, par, lo)
            values(u - 2, par, lo)

    for lo in slices:
        scores(0, 0, lo)
    for lo in slices:
        scores(1, 1, lo)
        softmax(0, 0, lo)
    first_loop_step = nq + 2
    for u in range(2, first_loop_step):
        step(u, u % 2)
    trips = (nblk - first_loop_step) // ATTN_UNROLL

    def trip(j, carry):
        for i in range(ATTN_UNROLL):
            step(first_loop_step + ATTN_UNROLL * j + i, i % 2)
        return carry

    lax.fori_loop(0, trips, trip, 0)
    for u in range(first_loop_step + trips * ATTN_UNROLL, nblk):
        step(u, u % 2)
    for lo in slices:
        softmax(nblk - 1, 1, lo)
        values(nblk - 2, 0, lo)
    for lo in slices:
        values(nblk - 1, 1, lo)
    for qi in range(nq):
        acc = acc_ref[qi]
        o_ref[qi * tq:(qi + 1) * tq, :] = jnp.transpose(
            acc[0:MLA_VDIM] / acc[MLA_VDIM:MLA_VDIM + 1]).astype(BF16)


def _attn_prompt(qext, kn, kp2, vt, batch, seq, tq):
    n = qext.shape[0]
    nq = seq // tq
    assert nq % 2 == 0
    pairs = [(i, i) for i in range(nq)] + [(qi, kb) for qi in range(nq) for kb in range(qi)]
    qi_tab = jnp.asarray([p[0] for p in pairs], jnp.int32)
    kb_tab = jnp.asarray([p[1] for p in pairs], jnp.int32)
    ones_rows = 16
    grid_spec = pltpu.PrefetchScalarGridSpec(
        num_scalar_prefetch=2,
        grid=(batch, MLA_HEADS),
        in_specs=[
            pl.BlockSpec((seq, 2 * LANES), lambda b, h, *_: (b, h)),
            pl.BlockSpec((seq, LANES), lambda b, h, *_: (b, h)),
            pl.BlockSpec((seq, LANES), lambda b, h, *_: (b, 0)),
            pl.BlockSpec((1, nq, MLA_VDIM, tq), lambda b, h, *_: (h, b, 0, 0)),
        ],
        out_specs=pl.BlockSpec((seq, MLA_VDIM), lambda b, h, *_: (b, h)),
        scratch_shapes=[
            pltpu.VMEM((seq, 2 * LANES), BF16),
            pltpu.VMEM((nq, MLA_VDIM + ones_rows, tq), BF16),
            pltpu.VMEM((tq, tq), F32),
            pltpu.VMEM((nq, 1, tq), F32),
            pltpu.VMEM((nq, MLA_VDIM + ones_rows, tq), F32),
            pltpu.VMEM((tq, tq), F32), pltpu.VMEM((tq, tq), F32),
            pltpu.VMEM((tq, tq), BF16), pltpu.VMEM((tq, tq), BF16),
            pltpu.VMEM((1, tq), F32), pltpu.VMEM((1, tq), F32),
        ],
    )
    return pl.pallas_call(
        functools.partial(_attn_prompt_body, tq=tq, nq=nq),
        grid_spec=grid_spec,
        out_shape=jax.ShapeDtypeStruct((n, MLA_HEADS * MLA_VDIM), BF16),
        compiler_params=_params(("arbitrary", "arbitrary"), 40),
        name="attn_prompt",
    )(qi_tab, kb_tab, qext, kn, kp2, vt)


SAMPLE_STREAMS_PER_STEP = 2


def _attn_sample_body(q_ref, hm_ref, kpp_ref, cp_ref, knn_ref, kpn_ref, cn_ref, wk_ref, gk_ref, p_ref, wv_ref,
                      o_ref, kn_sc, *, seq):
    past = cp_ref.shape[0] // SAMPLE_STREAMS_PER_STEP

    def with_ones(cb):
        return jnp.concatenate([cb, jnp.ones((cb.shape[0], LANES), BF16)], axis=1)

    streams = range(SAMPLE_STREAMS_PER_STEP)
    new = [slice(s * seq, (s + 1) * seq) for s in streams]
    old = [slice(s * past, (s + 1) * past) for s in streams]
    starts = list(range(0, MLA_HEADS * LANES, 2 * LANES))

    def project(cb, lo):
        return _dot(cb, wk_ref[:, lo:lo + 2 * LANES])

    def normalise(s, lo, kraw):
        kn_sc[s, :, lo:lo + 2 * LANES] = (_slab_rms(kraw, p_ref) * gk_ref[...]).astype(BF16)

    def scores(s):
        q = q_ref[new[s], :]
        qn = jnp.concatenate([q[:, 2 * h * LANES:(2 * h + 1) * LANES] for h in range(MLA_HEADS)], axis=1)
        qbd = jnp.concatenate([qn] * MLA_HEADS, axis=0) * hm_ref[...]
        qpe = jnp.concatenate([q[:, (2 * h + 1) * LANES:(2 * h + 2) * LANES] for h in range(MLA_HEADS)], axis=0)
        s_past = _dot_nt(kn_sc[s], qbd) + _dot_nt(kpp_ref[old[s], :], qpe)
        s_new = _dot_nt(knn_ref[new[s], :], qbd) + _dot_nt(kpn_ref[new[s], :], qpe)
        return s_past, s_new

    def latents(s, cb, s_past, s_new):
        m = jnp.maximum(jnp.max(s_past, axis=0, keepdims=True), jnp.max(s_new, axis=0, keepdims=True))
        p_past = jnp.exp2(s_past - m).astype(BF16)
        p_new = jnp.exp2(s_new - m).astype(BF16)
        acc = (_dot_tn(p_past, with_ones(cb))
               + _dot_tn(p_new, with_ones(cn_ref[new[s], :].astype(BF16))))
        return (acc[:, 0:MLA_LORA] / acc[:, MLA_LORA:MLA_LORA + 1]).astype(BF16)

    def outputs(s, lat):
        for h in range(MLA_HEADS):
            cols = slice(h * MLA_VDIM, (h + 1) * MLA_VDIM)
            o_ref[new[s], cols] = _dot(lat[h * seq:(h + 1) * seq], wv_ref[:, cols]).astype(BF16)

    cbs = [cp_ref[old[s], :].astype(BF16) for s in streams]
    cur = [project(cbs[s], starts[0]) for s in streams]
    for g, lo in enumerate(starts):
        nxt = [project(cbs[s], starts[g + 1]) for s in streams] if g + 1 < len(starts) else None
        for s in streams:
            normalise(s, lo, cur[s])
        cur = nxt
    sc = [scores(s) for s in streams]
    lat = [latents(s, cbs[s], *sc[s]) for s in streams]
    for s in streams:
        outputs(s, lat[s])


def _attn_sample(qext, kp2_past, ckv_past, kn_new, kp2_new, ckv_new, wk, gk, p128, wv, batch, past, seq):
    n = qext.shape[0]
    width = MLA_HEADS * LANES
    head_mask = jnp.asarray(np.kron(np.eye(MLA_HEADS, dtype=np.float32), np.ones((seq, LANES), np.float32)), BF16)
    group = SAMPLE_STREAMS_PER_STEP
    assert batch % group == 0
    seq, past = group * seq, group * past
    stream = lambda b: (b, 0)
    const = lambda b: (0, 0)
    return pl.pallas_call(
        functools.partial(_attn_sample_body, seq=seq // group),
        grid=(batch // group,),
        in_specs=[
            pl.BlockSpec((seq, 2 * width), stream),
            pl.BlockSpec(head_mask.shape, const),
            pl.BlockSpec((past, LANES), stream),
            pl.BlockSpec((past, MLA_LORA), stream),
            pl.BlockSpec((seq, width), stream),
            pl.BlockSpec((seq, LANES), stream),
            pl.BlockSpec((seq, MLA_LORA), stream),
            pl.BlockSpec(wk.shape, const),
            pl.BlockSpec((1, 2 * LANES), const),
            pl.BlockSpec((2 * LANES, 2 * LANES), const),
            pl.BlockSpec(wv.shape, const),
        ],
        out_specs=pl.BlockSpec((seq, MLA_HEADS * MLA_VDIM), stream),
        out_shape=jax.ShapeDtypeStruct((n, MLA_HEADS * MLA_VDIM), BF16),
        scratch_shapes=[pltpu.VMEM((group, past // group, width), BF16)],
        compiler_params=_params(("parallel",), 48),
        name="attn_sample",
    )(qext, head_mask, kp2_past, ckv_past, kn_new, kp2_new, ckv_new, wk, gk, p128, wv)


MERGE_COL_CHUNK = 256


def _branch_merge_body(og_ref, at_ref, wg_ref, wm_ref, ga_ref, gb_ref, u_ref, *bf16_copies):
    if bf16_copies:
        wgb_ref, wmb_ref = bf16_copies
        wgb_ref[...] = wg_ref[...].astype(BF16)
        wmb_ref[...] = wm_ref[...].astype(BF16)
        wg_ref, wm_ref = wgb_ref, wmb_ref
    og = og_ref[...]
    at = at_ref[...]
    starts = list(range(0, u_ref.shape[1], MERGE_COL_CHUNK))

    def gates(lo):
        cols = slice(lo, lo + MERGE_COL_CHUNK)
        return _sigmoid(ga_ref[:, cols].astype(F32)), _sigmoid(gb_ref[:, cols].astype(F32))

    nxt = gates(starts[0])
    for g, lo in enumerate(starts):
        cols = slice(lo, lo + MERGE_COL_CHUNK)
        sa, sb = nxt
        ya = _dot(og, wg_ref[:, cols])
        yb = _dot(at, wm_ref[:, cols])
        nxt = gates(starts[g + 1]) if g + 1 < len(starts) else None
        u_ref[:, cols] = (sa * ya + sb * yb).astype(BF16)


def _out_proj_body(u_ref, wo_ref, x_ref, g2_ref, x1_ref, h2_ref, *bf16_copies):
    if bf16_copies:
        (wob_ref,) = bf16_copies
        wob_ref[...] = wo_ref[...].astype(BF16)
        wo_ref = wob_ref
    x1 = x_ref[...] + _dot(u_ref[...], wo_ref[...])
    x1_ref[...] = x1
    h2_ref[...] = (_row_rms(x1) * g2_ref[...]).astype(BF16)


def _merge(og, att, wg, wm, zm, wo, x, g2, tm_u, tm_x, emit_bf16_weights=False):
    n = x.shape[0]
    tn = 1024 if tm_u >= 1024 else 512
    assert not emit_bf16_weights or (n == tm_u and n == tm_x)
    w_spec = pl.BlockSpec((D_MODEL, tn), lambda i, j: (0, j))
    w_shape = jax.ShapeDtypeStruct((D_MODEL, D_MODEL), BF16)
    copies = 2 if emit_bf16_weights else 0
    u, *wgm = pl.pallas_call(
        _branch_merge_body,
        grid=(n // tm_u, D_MODEL // tn),
        in_specs=[
            pl.BlockSpec((tm_u, D_MODEL), lambda i, j: (i, 0)),
            pl.BlockSpec((tm_u, D_MODEL), lambda i, j: (i, 0)),
            w_spec,
            w_spec,
            pl.BlockSpec((tm_u, tn), lambda i, j: (i, COL_GA // tn + j)),
            pl.BlockSpec((tm_u, tn), lambda i, j: (i, COL_GB // tn + j)),
        ],
        out_specs=[pl.BlockSpec((tm_u, tn), lambda i, j: (i, j))] + [w_spec] * copies,
        out_shape=[jax.ShapeDtypeStruct((n, D_MODEL), BF16)] + [w_shape] * copies,
        compiler_params=_params(("parallel", "parallel"), 56),
        name="branch_merge",
    )(og, att, wg, wm, zm, zm)
    wo_spec = pl.BlockSpec((D_MODEL, D_MODEL), lambda i: (0, 0))
    copies = 1 if emit_bf16_weights else 0
    x1, h2, *wob = pl.pallas_call(
        _out_proj_body,
        grid=(n // tm_x,),
        in_specs=[
            pl.BlockSpec((tm_x, D_MODEL), lambda i: (i, 0)),
            wo_spec,
            pl.BlockSpec((tm_x, D_MODEL), lambda i: (i, 0)),
            pl.BlockSpec((1, D_MODEL), lambda i: (0, 0)),
        ],
        out_specs=[
            pl.BlockSpec((tm_x, D_MODEL), lambda i: (i, 0)),
            pl.BlockSpec((tm_x, D_MODEL), lambda i: (i, 0)),
        ] + [wo_spec] * copies,
        out_shape=[
            jax.ShapeDtypeStruct((n, D_MODEL), F32),
            jax.ShapeDtypeStruct((n, D_MODEL), BF16),
        ] + [w_shape] * copies,
        compiler_params=_params(("parallel",), 56),
        name="out_proj",
    )(u, wo, x, g2)
    return (x1, h2) + tuple(wgm) + tuple(wob)


FFN_COL_CHUNK = 256


def _gelu_gate(a, a1, a2, cw, gt):
    c = cw[3:4, :] + cw[2:3, :] * a + cw[0:1, :] * a2 + cw[1:2, :] * a1
    return 0.5 * c * (1.0 + lax.erf(c * (2.0 ** -0.5))) * gt


def _ffn_up_seq_body(h_ref, wa_ref, wg_ref, cw_ref, hist_ref, act_ref, tail_ref, carry_ref, *, tm, tiles_per_seq):
    i = pl.program_id(0)
    j = pl.program_id(1)

    @pl.when(i % tiles_per_seq == 0)
    def _():
        carry_ref[j] = hist_ref[0]

    row = lax.broadcasted_iota(jnp.int32, (tm, 1), 0)
    starts = list(range(0, act_ref.shape[1], FFN_COL_CHUNK))

    def project(lo):
        h = h_ref[...]
        return _dot(h, wa_ref[:, lo:lo + FFN_COL_CHUNK]), _dot(h, wg_ref[:, lo:lo + FFN_COL_CHUNK])

    nxt = project(starts[0])
    for g, lo in enumerate(starts):
        cols = slice(lo, lo + FFN_COL_CHUNK)
        a, gt = nxt
        nxt = project(starts[g + 1]) if g + 1 < len(starts) else None
        prev = carry_ref[j, :, cols]
        a1 = jnp.where(row == 0, prev[7:8, :], pltpu.roll(a, 1, axis=0))
        a2 = jnp.where(row == 0, prev[6:7, :], jnp.where(row == 1, prev[7:8, :], pltpu.roll(a, 2, axis=0)))
        act_ref[:, cols] = _gelu_gate(a, a1, a2, cw_ref[:, cols], gt).astype(BF16)
        carry_ref[j, :, cols] = a[tm - 8:tm, :]
        tail_ref[0, :, cols] = a[tm - (CONV_W - 1):tm, :]


def _ffn_up_seq(h2, wa, wg, cw, hist8, seq, tm):
    n = h2.shape[0]
    tn = 512
    nj = D_FF // tn
    tps = seq // tm
    return pl.pallas_call(
        functools.partial(_ffn_up_seq_body, tm=tm, tiles_per_seq=tps),
        grid=(n // tm, nj),
        in_specs=[
            pl.BlockSpec((tm, D_MODEL), lambda i, j: (i, 0)),
            pl.BlockSpec((D_MODEL, tn), lambda i, j: (0, j)),
            pl.BlockSpec((D_MODEL, tn), lambda i, j: (0, j)),
            pl.BlockSpec((8, tn), lambda i, j: (0, j)),
            pl.BlockSpec((1, 8, tn), lambda i, j: (i // tps, 0, j)),
        ],
        out_specs=[
            pl.BlockSpec((tm, tn), lambda i, j: (i, j)),
            pl.BlockSpec((1, CONV_W - 1, tn), lambda i, j: (i, 0, j)),
        ],
        out_shape=[
            jax.ShapeDtypeStruct((n, D_FF), BF16),
            jax.ShapeDtypeStruct((n // tm, CONV_W - 1, D_FF), F32),
        ],
        scratch_shapes=[pltpu.VMEM((nj, 8, tn), F32)],
        compiler_params=_params(("arbitrary", "arbitrary"), 48),
        name="ffn_up_seq",
    )(h2, wa, wg, cw, hist8)


def _ffn_up_multi_body(h_ref, wa_ref, wg_ref, cw_ref, p1_ref, p2_ref, act_ref, a_ref, wab_ref, wgb_ref, *, tm, seq):
    wa = wa_ref[...].astype(BF16)
    wg = wg_ref[...].astype(BF16)
    wab_ref[...] = wa
    wgb_ref[...] = wg
    a = _dot(h_ref[...], wa)
    gt = _dot(h_ref[...], wg)
    pos = lax.broadcasted_iota(jnp.int32, (tm, 1), 0) % seq
    a1 = jnp.where(pos == 0, p1_ref[...], pltpu.roll(a, 1, axis=0))
    a2 = jnp.where(pos <= 1, p2_ref[...], pltpu.roll(a, 2, axis=0))
    act_ref[...] = _gelu_gate(a, a1, a2, cw_ref[...], gt).astype(BF16)
    a_ref[...] = a


def _ffn_up_multi(h2, wup, cw, p1, p2, seq):
    n = h2.shape[0]
    tn = 512
    nj = D_FF // tn
    col = lambda j: (0, j)
    return pl.pallas_call(
        functools.partial(_ffn_up_multi_body, tm=n, seq=seq),
        grid=(nj,),
        in_specs=[
            pl.BlockSpec((n, D_MODEL), lambda j: (0, 0)),
            pl.BlockSpec((D_MODEL, tn), col),
            pl.BlockSpec((D_MODEL, tn), lambda j: (0, nj + j)),
            pl.BlockSpec((8, tn), col),
            pl.BlockSpec((n, tn), col),
            pl.BlockSpec((n, tn), col),
        ],
        out_specs=[pl.BlockSpec((n, tn), col)] * 2 + [pl.BlockSpec((D_MODEL, tn), col)] * 2,
        out_shape=[jax.ShapeDtypeStruct((n, D_FF), BF16), jax.ShapeDtypeStruct((n, D_FF), F32)]
        + [jax.ShapeDtypeStruct((D_MODEL, D_FF), BF16)] * 2,
        compiler_params=_params(("parallel",), 48),
        name="ffn_up_multi",
    )(h2, wup, wup, cw, p1, p2)


def _ffn_down_body(act_ref, wd_ref, x1_ref, o_ref):
    o_ref[...] = x1_ref[...] + _dot(act_ref[...], wd_ref[...])


def _ffn_down_cast_body(act_ref, wd_ref, x1_ref, o_ref, wdb_ref):
    wd = wd_ref[...].astype(BF16)
    wdb_ref[...] = wd
    o_ref[...] = x1_ref[...] + _dot(act_ref[...], wd)


def _ffn_down(act, wd, x1, tm, emit_bf16_weights=False):
    n = act.shape[0]
    tn = 512
    out_specs = [pl.BlockSpec((tm, tn), lambda i, j: (i, j))]
    out_shape = [jax.ShapeDtypeStruct((n, D_MODEL), F32)]
    if emit_bf16_weights:
        assert n == tm
        out_specs.append(pl.BlockSpec((D_FF, tn), lambda i, j: (0, j)))
        out_shape.append(jax.ShapeDtypeStruct((D_FF, D_MODEL), BF16))
    return pl.pallas_call(
        _ffn_down_cast_body if emit_bf16_weights else _ffn_down_body,
        grid=(n // tm, D_MODEL // tn),
        in_specs=[
            pl.BlockSpec((tm, D_FF), lambda i, j: (i, 0)),
            pl.BlockSpec((D_FF, tn), lambda i, j: (0, j)),
            pl.BlockSpec((tm, tn), lambda i, j: (i, j)),
        ],
        out_specs=out_specs,
        out_shape=out_shape,
        compiler_params=_params(("parallel", "parallel"), 56),
        name="ffn_down",
    )(act, wd, x1)


def _swap_halves(w):
    half = w.shape[-1] // 2
    return jnp.concatenate([w[..., half:], w[..., :half]], axis=-1)


def _rope_table(pos):
    half = MLA_ROPE // 2
    inv = ROPE_THETA ** (-np.arange(half, dtype=np.float64) * 2.0 / MLA_ROPE)
    ang = np.asarray(pos, np.float64)[:, None] * inv[None, :]
    cos, sin = np.cos(ang), np.sin(ang)
    return jnp.asarray(np.concatenate([cos, cos, -sin, sin], axis=-1), F32)


def _layer_weights(w_in, g_norm1, gla_w_gate2, gla_b_gate, gla_g_out, w_br_gla, mla_g_qlat, mla_w_uq,
                   mla_g_kvlat, mla_w_ukv, mla_g_q, mla_g_k, mla_g_qpe, mla_g_kpe, w_br_mla, w_out,
                   g_norm2, ffn_w_up, ffn_conv_w, ffn_conv_b, ffn_w_down):
    w2ext = jnp.zeros((SMALL_COLS, GLA_HEADS * GLA_HK), F32).at[SMALL_LR:SMALL_LR + GLA_GATE_RANK].set(
        gla_w_gate2).astype(BF16)
    wq = mla_w_uq.reshape(MLA_LORA, MLA_HEADS, MLA_NOPE + MLA_ROPE)
    wqn = wq[:, :, :MLA_NOPE].reshape(MLA_LORA, MLA_HEADS * MLA_NOPE).astype(BF16)
    wq_pe = wq[:, :, MLA_NOPE:]
    wqp = jnp.concatenate([wq_pe, _swap_halves(wq_pe)], axis=-1).reshape(MLA_LORA, MLA_HEADS * LANES).astype(BF16)
    wkv = mla_w_ukv.reshape(MLA_LORA, MLA_HEADS, MLA_NOPE + MLA_VDIM)
    wk = wkv[:, :, :MLA_NOPE].reshape(MLA_LORA, MLA_HEADS * MLA_NOPE).astype(BF16)
    wv = wkv[:, :, MLA_NOPE:].reshape(MLA_LORA, MLA_HEADS * MLA_VDIM).astype(BF16)
    cw = jnp.concatenate([ffn_conv_w, ffn_conv_b[None, :], jnp.zeros((8 - CONV_W - 1, D_FF), F32)], axis=0)
    return dict(
        w_in_t=w_in.T, g1=g_norm1[None, :], w2ext=w2ext, bg=gla_b_gate[None, :],
        go=gla_g_out[None, :], wg=w_br_gla, wqn=wqn, wqp=wqp, gql=mla_g_qlat[None, :],
        gkl=mla_g_kvlat[None, :], gq=jnp.tile(mla_g_q, 2)[None, :],
        gqp=jnp.tile(jnp.concatenate([mla_g_qpe, _swap_halves(mla_g_qpe)]), 2)[None, :],
        gkp=jnp.concatenate([mla_g_kpe, _swap_halves(mla_g_kpe)])[None, :],
        wk=wk, wv=wv, wvt=wv.T, gk=jnp.tile(mla_g_k, 2)[None, :], wm=w_br_mla,
        wo=w_out,
        g2=g_norm2[None, :], wup=ffn_w_up, cw=cw, wd=ffn_w_down,
        p128=jnp.asarray(np.kron(np.eye(2, dtype=np.float32), np.full((LANES, LANES), 1.0 / LANES, np.float32)), BF16),
    )


def _trunk_front(x, w, s0, batch, seq, chunk, gla_tb, tm_in, tm, tab, tab_blocks, with_values):
    if "w_main" not in w:
        w["w_main"], narrow, zm, zs = _regroup_in_proj(x, w["g1"], w["w_in_t"])
        w["w_small"] = _small_weight_rows(narrow).astype(BF16)
    else:
        zm, zs = _in_proj(x, w["g1"], w["w_main"], w["w_small"], tm_in)
    og, s_new = _gla(zm, zs, w["w2ext"], w["bg"], w["go"], s0, batch, seq, chunk, gla_tb)
    qext, ckv, kpe, kp2 = _mla_proj(zm, zs, w["wqn"], w["wqp"], w["gql"], w["gkl"], w["gq"], w["gqp"],
                                    w["gkp"], tab, w["p128"], tm, tab_blocks)
    kv = _mla_expand(ckv, w["wk"], w["gk"], w["p128"], tm, w["wvt"] if with_values else None)
    return zm, og, s_new, qext, ckv, kpe, kp2, kv


def kernel(x_prompt, x_sample, state_gla, cache_mla_ckv, cache_mla_kpe, cache_ffn_conv, w_in, g_norm1, gla_w_gate2, gla_b_gate, gla_g_out, w_br_gla, mla_g_qlat, mla_w_uq, mla_g_kvlat, mla_w_ukv, mla_g_q, mla_g_k, mla_g_qpe, mla_g_kpe, w_br_mla, w_out, g_norm2, ffn_w_up, ffn_conv_w, ffn_conv_b, ffn_w_down):
    bp, tp, _ = x_prompt.shape
    bs, ts, _ = x_sample.shape
    depth = w_in.shape[0]
    past = cache_mla_ckv.shape[2]
    np_rows, ns_rows = bp * tp, bs * ts
    tm_p = 512
    tm_big = 1024
    tm_ffn = 1024
    tab_p = _rope_table(np.arange(tp))
    tab_s = jnp.tile(_rope_table(past + np.arange(ts)), (bs, 1))
    xp = x_prompt.reshape(np_rows, D_MODEL)
    xs = x_sample.reshape(ns_rows, D_MODEL)
    outs = [[] for _ in range(8)]
    layer_weights = (w_in, g_norm1, gla_w_gate2, gla_b_gate, gla_g_out, w_br_gla, mla_g_qlat, mla_w_uq,
                     mla_g_kvlat, mla_w_ukv, mla_g_q, mla_g_k, mla_g_qpe, mla_g_kpe, w_br_mla, w_out,
                     g_norm2, ffn_w_up, ffn_conv_w, ffn_conv_b, ffn_w_down)
    for l in range(depth):
        w = _layer_weights(*[a[l] for a in layer_weights])

        zm, og, ss, qext, ckv_s, kpe_s, kp2, (kn,) = _trunk_front(
            xs, w, state_gla[l], bs, ts, ts, ts, ns_rows, ns_rows, tab_s, 1, False)
        ckv_past = cache_mla_ckv[l].reshape(bs * past, MLA_LORA)
        kpe_past = cache_mla_kpe[l].reshape(bs * past, MLA_ROPE)
        kp2_past = jnp.concatenate([kpe_past, kpe_past], axis=-1).astype(BF16)
        att = _attn_sample(qext, kp2_past, ckv_past, kn, kp2, ckv_s, w["wk"], w["gk"], w["p128"], w["wv"],
                           bs, past, ts)
        x1, h2, w_br_gla_bf, w_br_mla_bf, w_out_bf = _merge(og, att, w["wg"], w["wm"], zm, w["wo"], xs, w["g2"],
                                                            ns_rows, ns_rows, emit_bf16_weights=True)
        hist = cache_ffn_conv[l]
        zrow = jnp.zeros((bs, ts - 1, D_FF), F32)
        p1 = jnp.concatenate([hist[:, 1:2], zrow], axis=1).reshape(ns_rows, D_FF)
        p2 = jnp.concatenate([hist, zrow[:, 1:]], axis=1).reshape(ns_rows, D_FF)
        act, a_full, wa_bf, wg_bf = _ffn_up_multi(h2, w["wup"], w["cw"], p1, p2, ts)
        fs = a_full.reshape(bs, ts, D_FF)[:, ts - (CONV_W - 1):]
        xs, wd_bf = _ffn_down(act, w["wd"], x1, ns_rows, emit_bf16_weights=True)

        s0 = jnp.zeros((bp, GLA_HEADS, GLA_HK, GLA_HV), F32)
        zm, og, sp, qext, ckv_p, kpe_p, kp2, (kn, vt) = _trunk_front(
            xp, w, s0, bp, tp, CHUNK, 512, tm_big, tm_p, tab_p, tp // tm_p, True)
        att = _attn_prompt(qext, kn, kp2, vt, bp, tp, 512)
        x1, h2 = _merge(og, att, w_br_gla_bf, w_br_mla_bf, zm, w_out_bf, xp, w["g2"], tm_big, tm_p)
        hist8 = jnp.zeros((bp, 8, D_FF), F32)
        act, tails = _ffn_up_seq(h2, wa_bf, wg_bf, w["cw"], hist8, tp, tm_ffn)
        fp = tails.reshape(bp, tp // tm_ffn, CONV_W - 1, D_FF)[:, -1]
        (xp,) = _ffn_down(act, wd_bf, x1, tm_big)

        for lst, val in zip(outs, (sp, ss, ckv_p.reshape(bp, tp, MLA_LORA), ckv_s.reshape(bs, ts, MLA_LORA),
                                   kpe_p.reshape(bp, tp, MLA_ROPE), kpe_s.reshape(bs, ts, MLA_ROPE), fp, fs)):
            lst.append(val)
    return (xp.reshape(bp, tp, D_MODEL), xs.reshape(bs, ts, D_MODEL)) + tuple(jnp.stack(o, 0) for o in outs)
```
